```python
import math
import jax, jax.numpy as jnp
from jax import lax
import numpy as np

D_MODEL = 1024
BATCH = 8
SEQ = 2048
DEPTH = 4
DEC_BATCH = 8
DEC_SEQ = 32
PAST_LEN = 2048

CHUNK = 64
Q_BLOCK = 128
EPS = 1e-6
ROPE_BASE = 10000.0

N_MLA = 8
Q_LORA = 384
KV_LORA = 256
NOPE_DIM = 64
ROPE_DIM = 32
QK_DIM = NOPE_DIM + ROPE_DIM
V_DIM = 64
N_FOX = 8
FOX_DIM = 64
D_MIX = N_MLA * V_DIM + N_FOX * FOX_DIM
IN_SPLITS = (Q_LORA,
             Q_LORA + KV_LORA,
             Q_LORA + KV_LORA + ROPE_DIM,
             Q_LORA + KV_LORA + ROPE_DIM + N_FOX * FOX_DIM,
             Q_LORA + KV_LORA + ROPE_DIM + 2 * N_FOX * FOX_DIM,
             Q_LORA + KV_LORA + ROPE_DIM + 3 * N_FOX * FOX_DIM)
D_IN = Q_LORA + KV_LORA + ROPE_DIM + 3 * N_FOX * FOX_DIM + N_FOX
D_FF = 2816
N_EXPERTS = 8
TOP_K = 2
E_FF = 1792
N_DENSE = (DEPTH + 1) // 2
N_MOE = DEPTH // 2

kernel_name = "hybrid_mla_fox_streaming_step"


def rmsnorm(x, g):
    xf = x.astype(jnp.float32)
    y = xf * lax.rsqrt(jnp.mean(xf * xf, axis=-1, keepdims=True) + EPS)
    return (y * g.astype(jnp.float32)).astype(x.dtype)


def rope(x, pos):
    half = ROPE_DIM // 2
    inv = ROPE_BASE ** (-jnp.arange(half, dtype=jnp.float32) / half)
    ang = pos.astype(jnp.float32)[:, None] * inv[None, :]
    shape = (pos.shape[0],) + (1,) * (x.ndim - 3) + (half,)
    cos = jnp.cos(ang).reshape(shape)
    sin = jnp.sin(ang).reshape(shape)
    xf = x.astype(jnp.float32)
    x1, x2 = xf[..., :half], xf[..., half:]
    return jnp.concatenate([x1 * cos - x2 * sin, x2 * cos + x1 * sin], axis=-1).astype(x.dtype)


def attention(q, k, v, q_pos, k_pos, c_q=None, c_k=None):
    B, Tq, H, dk = q.shape
    dv = v.shape[-1]
    scale = dk ** -0.5
    nb = Tq // Q_BLOCK if (Tq % Q_BLOCK == 0 and Tq > Q_BLOCK) else 1
    bq = Tq // nb

    def block(xs):
        qb, pb = xs[0], xs[1]
        s = jnp.einsum("bqhd,bkhd->bhqk", qb, k, preferred_element_type=jnp.float32) * scale
        if c_k is None:
            allowed = (k_pos[None, :] // CHUNK) <= (pb[:, None] // CHUNK)
        else:
            cb = xs[2]
            s = s + (jnp.transpose(cb, (0, 2, 1))[:, :, :, None]
                     - jnp.transpose(c_k, (0, 2, 1))[:, :, None, :])
            allowed = k_pos[None, :] <= pb[:, None]
        s = jnp.where(allowed[None, None], s, -jnp.inf)
        p = jax.nn.softmax(s, axis=-1).astype(v.dtype)
        return jnp.einsum("bhqk,bkhd->bqhd", p, v)

    qs = jnp.swapaxes(q.reshape(B, nb, bq, H, dk), 0, 1)
    ps = q_pos.reshape(nb, bq)
    if c_k is None:
        xs = (qs, ps)
    else:
        xs = (qs, ps, jnp.swapaxes(c_q.reshape(B, nb, bq, H), 0, 1))
    out = lax.map(block, xs)
    return jnp.swapaxes(out, 0, 1).reshape(B, Tq, H, dv)


def mixer(xn, start, past, w_in, b_f, g_cq, g_ckv, w_uq, w_ukv,
          g_qn_mla, g_kn_mla, g_qn_fox, g_kn_fox, w_out):
    B, T, _ = xn.shape
    Tk = start + T
    q_pos = start + jnp.arange(T, dtype=jnp.int32)
    k_pos = jnp.arange(Tk, dtype=jnp.int32)

    proj = xn @ w_in
    c_q, c_kv, k_rope, fq, fk, fv, f_logit = jnp.split(proj, IN_SPLITS, axis=-1)

    c_kv = rmsnorm(c_kv, g_ckv)
    k_rope = rope(k_rope, q_pos)
    fk = rmsnorm(fk.reshape(B, T, N_FOX, FOX_DIM), g_kn_fox)
    fv = fv.reshape(B, T, N_FOX, FOX_DIM)
    logf = jax.nn.log_sigmoid(f_logit.astype(jnp.float32) + b_f.astype(jnp.float32))
    new_state = (c_kv, k_rope, fk, fv, logf.astype(xn.dtype))
    if past is None:
        ckv_all, krope_all, fk_all, fv_all, logf_all = new_state
    else:
        ckv_all, krope_all, fk_all, fv_all, logf_all = tuple(
            jnp.concatenate([p.astype(n.dtype), n], axis=1) for p, n in zip(past, new_state))

    q = (rmsnorm(c_q, g_cq) @ w_uq).reshape(B, T, N_MLA, QK_DIM)
    q = jnp.concatenate([q[..., :NOPE_DIM], rope(q[..., NOPE_DIM:], q_pos)], axis=-1)
    kv = (ckv_all @ w_ukv).reshape(B, Tk, N_MLA, NOPE_DIM + V_DIM)
    k = jnp.concatenate(
        [kv[..., :NOPE_DIM],
         jnp.broadcast_to(krope_all[:, :, None, :], (B, Tk, N_MLA, ROPE_DIM))], axis=-1)
    v = kv[..., NOPE_DIM:]
    q = rmsnorm(q, g_qn_mla)
    k = rmsnorm(k, g_kn_mla)
    o_mla = attention(q, k, v, q_pos, k_pos)

    c_cum = jnp.cumsum(logf_all.astype(jnp.float32), axis=1)
    fq = rmsnorm(fq.reshape(B, T, N_FOX, FOX_DIM), g_qn_fox)
    o_fox = attention(fq, fk_all, fv_all, q_pos, k_pos, c_cum[:, start:], c_cum)

    o = jnp.concatenate([o_mla.reshape(B, T, N_MLA * V_DIM),
                         o_fox.reshape(B, T, N_FOX * FOX_DIM)], axis=-1)
    return o @ w_out, new_state


def swiglu(x, wg, wu, wd):
    return (jax.nn.silu(x @ wg) * (x @ wu)) @ wd


def moe(x, w_router, b_router, we_gate, we_up, we_down):
    logits = jnp.einsum("btd,de->bte", x, w_router, preferred_element_type=jnp.float32) \
        + b_router.astype(jnp.float32)
    probs = jax.nn.softmax(logits, axis=-1)
    top_p, top_i = lax.top_k(probs, TOP_K)
    top_p = top_p / jnp.sum(top_p, axis=-1, keepdims=True)
    gate = jnp.sum(jnp.where(top_i[..., None] == jnp.arange(N_EXPERTS), top_p[..., None], 0.0),
                   axis=-2).astype(x.dtype)
    y = jnp.zeros_like(x)
    for e in range(N_EXPERTS):
        y = y + gate[..., e:e + 1] * swiglu(x, we_gate[e], we_up[e], we_down[e])
    return y


def trunk(x, start, past, w):
    (g_mix, w_in, b_f, g_cq, g_ckv, w_uq, w_ukv, g_qn_mla, g_kn_mla, g_qn_fox, g_kn_fox,
     w_out, g_ffn, w_gate, w_up, w_down, w_router, b_router, we_gate, we_up, we_down) = w
    new = ([], [], [], [], [])
    for i in range(DEPTH):
        layer_past = None if past is None else tuple(p[i] for p in past)
        m, st = mixer(rmsnorm(x, g_mix[i]), start, layer_past, w_in[i], b_f[i], g_cq[i],
                      g_ckv[i], w_uq[i], w_ukv[i], g_qn_mla[i], g_kn_mla[i],
                      g_qn_fox[i], g_kn_fox[i], w_out[i])
        x = x + m
        xn = rmsnorm(x, g_ffn[i])
        j = i // 2
        if i % 2 == 0:
            x = x + swiglu(xn, w_gate[j], w_up[j], w_down[j])
        else:
            x = x + moe(xn, w_router[j], b_router[j], we_gate[j], we_up[j], we_down[j])
        for lst, s in zip(new, st):
            lst.append(s)
    return x, tuple(jnp.stack(lst) for lst in new)


def setup_inputs(seed: int = 0) -> dict:
    key = jax.random.key(seed)
    keys = jax.random.split(key, 40)
    cnt = [0]

    def nrm(shape, scale=1.0):
        k = keys[cnt[0]]
        cnt[0] += 1
        return jax.random.normal(k, shape, jnp.float32) * scale

    def gain(shape):
        return 1.0 + 0.1 * nrm(shape)

    return {
        "x_prompt": nrm((BATCH, SEQ, D_MODEL)),
        "x_sample": nrm((DEC_BATCH, DEC_SEQ, D_MODEL)),
        "cache_mla_ckv": nrm((DEPTH, DEC_BATCH, PAST_LEN, KV_LORA)),
        "cache_mla_krope": nrm((DEPTH, DEC_BATCH, PAST_LEN, ROPE_DIM)),
        "cache_fox_k": nrm((DEPTH, DEC_BATCH, PAST_LEN, N_FOX, FOX_DIM)),
        "cache_fox_v": nrm((DEPTH, DEC_BATCH, PAST_LEN, N_FOX, FOX_DIM)),
        "cache_fox_logf": jax.nn.log_sigmoid(4.0 + nrm((DEPTH, DEC_BATCH, PAST_LEN, N_FOX))),
        "g_mix": gain((DEPTH, D_MODEL)),
        "w_in": nrm((DEPTH, D_MODEL, D_IN), D_MODEL ** -0.5),
        "b_f": 4.0 + 0.5 * nrm((DEPTH, N_FOX)),
        "g_cq": gain((DEPTH, Q_LORA)),
        "g_ckv": gain((DEPTH, KV_LORA)),
        "w_uq": nrm((DEPTH, Q_LORA, N_MLA * QK_DIM), Q_LORA ** -0.5),
        "w_ukv": nrm((DEPTH, KV_LORA, N_MLA * (NOPE_DIM + V_DIM)), KV_LORA ** -0.5),
        "g_qn_mla": gain((DEPTH, QK_DIM)),
        "g_kn_mla": gain((DEPTH, QK_DIM)),
        "g_qn_fox": gain((DEPTH, FOX_DIM)),
        "g_kn_fox": gain((DEPTH, FOX_DIM)),
        "w_out": nrm((DEPTH, D_MIX, D_MODEL), D_MIX ** -0.5),
        "g_ffn": gain((DEPTH, D_MODEL)),
        "w_gate": nrm((N_DENSE, D_MODEL, D_FF), D_MODEL ** -0.5),
        "w_up": nrm((N_DENSE, D_MODEL, D_FF), D_MODEL ** -0.5),
        "w_down": nrm((N_DENSE, D_FF, D_MODEL), D_FF ** -0.5),
        "w_router": nrm((N_MOE, D_MODEL, N_EXPERTS), D_MODEL ** -0.5),
        "b_router": nrm((N_MOE, N_EXPERTS), 0.01),
        "we_gate": nrm((N_MOE, N_EXPERTS, D_MODEL, E_FF), D_MODEL ** -0.5),
        "we_up": nrm((N_MOE, N_EXPERTS, D_MODEL, E_FF), D_MODEL ** -0.5),
        "we_down": nrm((N_MOE, N_EXPERTS, E_FF, D_MODEL), E_FF ** -0.5),
    }


def reference(x_prompt, x_sample, cache_mla_ckv, cache_mla_krope, cache_fox_k, cache_fox_v,
              cache_fox_logf, g_mix, w_in, b_f, g_cq, g_ckv, w_uq, w_ukv, g_qn_mla, g_kn_mla,
              g_qn_fox, g_kn_fox, w_out, g_ffn, w_gate, w_up, w_down, w_router, b_router,
              we_gate, we_up, we_down):
    w = (g_mix, w_in, b_f, g_cq, g_ckv, w_uq, w_ukv, g_qn_mla, g_kn_mla, g_qn_fox, g_kn_fox,
         w_out, g_ffn, w_gate, w_up, w_down, w_router, b_router, we_gate, we_up, we_down)
    y_prompt, st_p = trunk(x_prompt, 0, None, w)
    past = (cache_mla_ckv, cache_mla_krope, cache_fox_k, cache_fox_v, cache_fox_logf)
    y_sample, st_s = trunk(x_sample, PAST_LEN, past, w)
    p_ckv, p_krope, p_fk, p_fv, p_logf = st_p
    s_ckv, s_krope, s_fk, s_fv, s_logf = st_s
    return (y_prompt, y_sample, p_ckv, p_krope, p_fk, p_fv, p_logf,
            s_ckv, s_krope, s_fk, s_fv, s_logf)
```

```python
import functools

import jax
import jax.numpy as jnp
from jax import lax
from jax.experimental import pallas as pl
from jax.experimental.pallas import tpu as pltpu

F32 = jnp.float32
BF16 = jnp.bfloat16

D_MODEL = 1024
DEPTH = 4
PAST_LEN = 2048
CHUNK_SHIFT = 6
EPS = 1e-6
ROPE_BASE = 10000.0

N_MLA = 8
Q_LORA = 384
KV_LORA = 256
NOPE_DIM = 64
ROPE_DIM = 32
QK_DIM = NOPE_DIM + ROPE_DIM
V_DIM = 64
N_FOX = 8
FOX_DIM = 64
D_FF = 2816
N_EXPERTS = 8
E_FF = 1792

LANE = 128
MXU = 256
HEAD_PAD = 128
ROPE_LO = NOPE_DIM
ROPE_HI = NOPE_DIM + ROPE_DIM
AUG_W = 16

CQ_LO, CQ_HI = 0, Q_LORA
CKV_LO, CKV_HI = CQ_HI, CQ_HI + KV_LORA
FQ_LO, FQ_HI = CKV_HI, CKV_HI + N_FOX * FOX_DIM
FK_LO, FK_HI = FQ_HI, FQ_HI + N_FOX * FOX_DIM
FV_LO, FV_HI = FK_HI, FK_HI + N_FOX * FOX_DIM
MISC_LO, MISC_HI = FV_HI, FV_HI + LANE
D_IN_P = MISC_HI

VMEM_LIMIT = 56 * 1024 * 1024


def _params(*sem):
    return pltpu.CompilerParams(dimension_semantics=sem, vmem_limit_bytes=VMEM_LIMIT)


def _dot(a, b):
    return jnp.dot(a, b, preferred_element_type=F32)


def _dot_nt(a, b):
    return lax.dot_general(a, b, (((1,), (1,)), ((), ())), preferred_element_type=F32)


def _rms(x, g, n):
    return x * lax.rsqrt(jnp.sum(x * x, axis=-1, keepdims=True) * (1.0 / n) + EPS) * g


def _head_norm(x, smat, dim):
    outs = []
    for c in range(0, x.shape[1], MXU):
        xc = x[:, c:c + MXU]
        ss = _dot((xc * xc).astype(BF16), smat)
        outs.append(xc * lax.rsqrt(ss * (1.0 / dim) + EPS))
    return jnp.concatenate(outs, axis=1)


def _full(shape):
    return pl.BlockSpec(shape, lambda *_: (0,) * len(shape))


def _mixer_in_body(x_ref, gmix_ref, win_ref, bf_ref, gcq_ref, gckv_ref, wuq_ref, gq_ref,
                   gfq_ref, gfk_ref, ct_ref, st_ref, s64_ref, s128_ref,
                   ckv_ref, kr_ref, kr128_ref, fk_ref, fv_ref, lf128_ref,
                   q_ref, fq_ref, fkb_ref, fvb_ref):
    tm = x_ref.shape[0]
    xb = _rms(x_ref[...], gmix_ref[...], D_MODEL).astype(BF16)

    def proj(lo, hi):
        return _dot(xb, win_ref[:, lo:hi])

    ct = ct_ref[...]
    st = st_ref[...]
    lane = lax.broadcasted_iota(jnp.int32, (tm, LANE), 1)

    a = proj(MISC_LO, MISC_HI)
    z = a + bf_ref[...]
    lf = jnp.minimum(z, 0.0) - jnp.log1p(jnp.exp(-jnp.abs(z)))
    lf128_ref[...] = jnp.where(lane < N_FOX, lf, 0.0)
    half = ROPE_DIM // 2
    sw = jnp.where(lane < ROPE_LO + half, pltpu.roll(a, LANE - half, 1), pltpu.roll(a, half, 1))
    kr128 = jnp.where((lane >= ROPE_LO) & (lane < ROPE_HI), a * ct + sw * st, 0.0)
    kr128_ref[...] = kr128
    kr_ref[...] = kr128[:, ROPE_LO:ROPE_HI]

    ckv_ref[...] = _rms(proj(CKV_LO, CKV_HI), gckv_ref[...], KV_LORA)

    s64 = s64_ref[...]
    fk = _head_norm(proj(FK_LO, FK_HI), s64, FOX_DIM) * gfk_ref[...]
    fk_ref[...] = fk
    fkb_ref[...] = fk.astype(BF16)
    fv = proj(FV_LO, FV_HI)
    fv_ref[...] = fv
    fvb_ref[...] = fv.astype(BF16)
    fq_ref[...] = (_head_norm(proj(FQ_LO, FQ_HI), s64, FOX_DIM) * gfq_ref[...]).astype(BF16)

    cq = _rms(proj(CQ_LO, CQ_HI), gcq_ref[...], Q_LORA).astype(BF16)
    nq = N_MLA * HEAD_PAD
    qa = _dot(cq, wuq_ref[:, 0:nq])
    qb = _dot(cq, wuq_ref[:, nq:2 * nq])
    qr = jnp.concatenate(
        [qa[:, h * HEAD_PAD:(h + 1) * HEAD_PAD] * ct + qb[:, h * HEAD_PAD:(h + 1) * HEAD_PAD] * st
         for h in range(N_MLA)], axis=1)
    qn = _head_norm(qr, s128_ref[...], QK_DIM)
    gq = gq_ref[...]
    q_ref[...] = jnp.concatenate(
        [qn[:, h * HEAD_PAD:(h + 1) * HEAD_PAD] * gq for h in range(N_MLA)], axis=1).astype(BF16)


def _mixer_in(x, lw, ct, st, consts, tm):
    rows = x.shape[0]
    nt = ct.shape[0] // tm
    grid = (rows // tm,)
    row = lambda w: pl.BlockSpec((tm, w), lambda i: (i, 0))
    tab = pl.BlockSpec((tm, LANE), lambda i: (i % nt, 0))
    nfx = N_FOX * FOX_DIM
    out_shape = (
        jax.ShapeDtypeStruct((rows, KV_LORA), F32),
        jax.ShapeDtypeStruct((rows, ROPE_DIM), F32),
        jax.ShapeDtypeStruct((rows, LANE), F32),
        jax.ShapeDtypeStruct((rows, nfx), F32),
        jax.ShapeDtypeStruct((rows, nfx), F32),
        jax.ShapeDtypeStruct((rows, LANE), F32),
        jax.ShapeDtypeStruct((rows, N_MLA * HEAD_PAD), BF16),
        jax.ShapeDtypeStruct((rows, nfx), BF16),
        jax.ShapeDtypeStruct((rows, nfx), BF16),
        jax.ShapeDtypeStruct((rows, nfx), BF16),
    )
    out_specs = (row(KV_LORA), row(ROPE_DIM), row(LANE), row(nfx), row(nfx), row(LANE),
                 row(N_MLA * HEAD_PAD), row(nfx), row(nfx), row(nfx))
    in_specs = [row(D_MODEL), _full((1, D_MODEL)), _full((D_MODEL, D_IN_P)), _full((1, LANE)),
                _full((1, Q_LORA)), _full((1, KV_LORA)), _full((Q_LORA, 2 * N_MLA * HEAD_PAD)),
                _full((1, LANE)), _full((1, nfx)), _full((1, nfx)), tab, tab,
                _full((MXU, MXU)), _full((MXU, MXU))]
    return pl.pallas_call(
        _mixer_in_body, grid=grid, in_specs=in_specs, out_specs=out_specs, out_shape=out_shape,
        compiler_params=_params("arbitrary"), name="mixer_in",
    )(x, lw["g_mix"], lw["w_in"], lw["b_f"], lw["g_cq"], lw["g_ckv"], lw["w_uq"], lw["gq"],
      lw["gfq"], lw["gfk"], ct, st, consts["s64"], consts["s128"])


def _kv_body(ckv_ref, kr128_ref, wk_ref, wv_ref, gk_ref, s128_ref, k_ref, v_ref):
    cb = ckv_ref[...].astype(BF16)
    kn = _dot(cb, wk_ref[...])
    kr = kr128_ref[...]
    kk = jnp.concatenate(
        [kn[:, h * HEAD_PAD:(h + 1) * HEAD_PAD] + kr for h in range(N_MLA)], axis=1)
    kk = _head_norm(kk, s128_ref[...], QK_DIM)
    gk = gk_ref[...]
    k_ref[...] = jnp.concatenate(
        [kk[:, h * HEAD_PAD:(h + 1) * HEAD_PAD] * gk for h in range(N_MLA)], axis=1).astype(BF16)
    v_ref[...] = _dot(cb, wv_ref[...]).astype(BF16)


def _kv_expand(ckv, kr128, lw, consts, tm):
    rows = ckv.shape[0]
    row = lambda w: pl.BlockSpec((tm, w), lambda i: (i, 0))
    nk = N_MLA * HEAD_PAD
    nv = N_MLA * V_DIM
    return pl.pallas_call(
        _kv_body, grid=(rows // tm,),
        in_specs=[row(KV_LORA), row(LANE), _full((KV_LORA, nk)), _full((KV_LORA, nv)),
                  _full((1, LANE)), _full((MXU, MXU))],
        out_specs=(row(nk), row(nv)),
        out_shape=(jax.ShapeDtypeStruct((rows, nk), BF16), jax.ShapeDtypeStruct((rows, nv), BF16)),
        compiler_params=_params("arbitrary"), name="kv_expand",
    )(ckv, kr128, lw["w_k"], lw["w_v"], lw["gk"], consts["s128"])


def _cum_body(lf_ref, tri_ref, place_ref, ones_ref, qa_ref, ka_ref, *, blk):
    t = lf_ref.shape[1]
    tri = tri_ref[...]
    carry = jnp.zeros((1, LANE), F32)

    def split3(c):
        hi = c.astype(BF16)
        r1 = c - hi.astype(F32)
        mid = r1.astype(BF16)
        lo = (r1 - mid.astype(F32)).astype(BF16)
        return hi, mid, lo

    ones_q = ones_ref[0:1, :]
    ones_k = ones_ref[1:2, :]
    for b0 in range(0, t, blk):
        hi, mid, lo = split3(lf_ref[0, b0:b0 + blk, :])
        c = _dot(tri, hi) + _dot(tri, mid) + _dot(tri, lo) + carry
        carry = c[blk - 1:blk, :]
        hi, mid, lo = split3(c)
        qa = _dot(hi, place_ref[0]) + _dot(mid, place_ref[1]) + _dot(lo, place_ref[2]) + ones_q
        ka = ones_k - (_dot(hi, place_ref[3]) + _dot(mid, place_ref[4]) + _dot(lo, place_ref[5]))
        qa_ref[0, b0:b0 + blk, :] = qa.astype(BF16)
        ka_ref[0, b0:b0 + blk, :] = ka.astype(BF16)


def _fox_bias(lf128, consts):
    b, t, _ = lf128.shape
    blk = MXU if t % MXU == 0 else LANE
    tri = consts["tri256"] if blk == MXU else consts["tri128"]
    blk3 = pl.BlockSpec((1, t, LANE), lambda i: (i, 0, 0))
    return pl.pallas_call(
        functools.partial(_cum_body, blk=blk), grid=(b,),
        in_specs=[blk3, _full((blk, blk)), _full((6, LANE, LANE)), _full((8, LANE))],
        out_specs=(blk3, blk3),
        out_shape=(jax.ShapeDtypeStruct((b, t, LANE), BF16),) * 2,
        compiler_params=_params("arbitrary"), name="fox_bias",
    )(lf128, tri, consts["place"], consts["aug_ones"])


def _attn_operands(refs, fox, pair, lane):
    if fox:
        q_ref, k_ref, _, qa_ref, ka_ref = refs

        def q_op(hh):
            qm = jnp.where((lane >= FOX_DIM * hh) & (lane < FOX_DIM * (hh + 1)),
                           q_ref[0].astype(F32), 0.0)
            qa = jnp.where((lane >> 4) == 2 * pair + hh, qa_ref[0].astype(F32), 0.0)
            return jnp.concatenate([qm, qa], axis=1).astype(BF16)

        def k_op(hh, rows):
            return jnp.concatenate([k_ref[0, rows, :], ka_ref[0, rows, :]], axis=1)
    else:
        q_ref, k_ref, _ = refs

        def q_op(hh):
            return q_ref[0, :, HEAD_PAD * hh:HEAD_PAD * (hh + 1)]

        def k_op(hh, rows):
            return k_ref[0, rows, HEAD_PAD * hh:HEAD_PAD * (hh + 1)]
    return q_op, k_op


def _attn_prompt_body(*refs, fox, tq):
    n_in = 5 if fox else 3
    ins, (o_ref, s_ref, m_ref, l_ref, acc_ref) = refs[:n_in], refs[n_in:]
    v_ref = ins[2]
    pair = pl.program_id(1)
    i = pl.program_id(2)
    lane = lax.broadcasted_iota(jnp.int32, (tq, LANE), 1)
    q_op, k_op = _attn_operands(ins, fox, pair, lane)
    qs = [q_op(0), q_op(1)]
    r = lax.broadcasted_iota(jnp.int32, (tq, tq), 0)
    c = lax.broadcasted_iota(jnp.int32, (tq, tq), 1)
    allowed = (c <= r) if fox else ((c >> CHUNK_SHIFT) <= (r >> CHUNK_SHIFT))

    m_ref[...] = jnp.full(m_ref.shape, -jnp.inf, F32)

    def fold_max(hh, s):
        m_ref[hh] = jnp.maximum(m_ref[hh], jnp.maximum(s[:, :LANE], s[:, LANE:]))

    def pass1(j, carry):
        rows = pl.ds(pl.multiple_of(j * tq, tq), tq)
        for hh in range(2):
            s = _dot_nt(qs[hh], k_op(hh, rows))
            s_ref[hh, j] = s
            fold_max(hh, s)
        return carry

    lax.fori_loop(0, i, pass1, 0)
    rows = pl.ds(pl.multiple_of(i * tq, tq), tq)
    for hh in range(2):
        s = jnp.where(allowed, _dot_nt(qs[hh], k_op(hh, rows)), -jnp.inf)
        s_ref[hh, i] = s
        fold_max(hh, s)

    mrow = [jnp.max(m_ref[hh], axis=1, keepdims=True) for hh in range(2)]
    l_ref[...] = jnp.zeros(l_ref.shape, F32)
    acc_ref[...] = jnp.zeros(acc_ref.shape, F32)

    def pass2(j, carry):
        rows = pl.ds(pl.multiple_of(j * tq, tq), tq)
        vv = v_ref[0, rows, :]
        for hh in range(2):
            p = jnp.exp(s_ref[hh, j] - mrow[hh])
            l_ref[hh] += p[:, :LANE] + p[:, LANE:]
            acc_ref[hh] += _dot(p.astype(BF16), vv)
        return carry

    lax.fori_loop(0, i + 1, pass2, 0)
    outs = [acc_ref[hh] / jnp.sum(l_ref[hh], axis=1, keepdims=True) for hh in range(2)]
    o_ref[0] = jnp.where(lane < V_DIM, outs[0], outs[1]).astype(BF16)


def _attn_prompt(q, k, v, aug, tq):
    fox = aug is not None
    b, t, _ = q.shape
    qw = LANE if fox else 2 * HEAD_PAD
    nq = t // tq
    in_specs = [pl.BlockSpec((1, tq, qw), lambda bi, p, i: (bi, i, p)),
                pl.BlockSpec((1, t, qw), lambda bi, p, i: (bi, 0, p)),
                pl.BlockSpec((1, t, LANE), lambda bi, p, i: (bi, 0, p))]
    args = [q, k, v]
    if fox:
        in_specs += [pl.BlockSpec((1, tq, LANE), lambda bi, p, i: (bi, i, 0)),
                     pl.BlockSpec((1, t, LANE), lambda bi, p, i: (bi, 0, 0))]
        args += list(aug)
    return pl.pallas_call(
        functools.partial(_attn_prompt_body, fox=fox, tq=tq),
        grid=(b, N_MLA // 2, nq), in_specs=in_specs,
        out_specs=pl.BlockSpec((1, tq, LANE), lambda bi, p, i: (bi, i, p)),
        out_shape=jax.ShapeDtypeStruct((b, t, N_MLA * V_DIM), BF16),
        scratch_shapes=[pltpu.VMEM((2, nq, tq, tq), F32), pltpu.VMEM((2, tq, LANE), F32),
                        pltpu.VMEM((2, tq, LANE), F32), pltpu.VMEM((2, tq, LANE), F32)],
        compiler_params=_params("arbitrary", "arbitrary", "arbitrary"),
        name="attn_fox" if fox else "attn_mla",
    )(*args)


def _attn_sample_body(*refs, fox, start, tk_valid):
    n_in = 5 if fox else 3
    ins, o_ref = refs[:n_in], refs[n_in]
    v_ref = ins[2]
    tq = ins[0].shape[1]
    tk = ins[1].shape[1]
    pair = pl.program_id(1)
    lane = lax.broadcasted_iota(jnp.int32, (tq, LANE), 1)
    q_op, k_op = _attn_operands(ins, fox, pair, lane)
    qpos = start + lax.broadcasted_iota(jnp.int32, (tq, tk), 0)
    kpos = lax.broadcasted_iota(jnp.int32, (tq, tk), 1)
    causal = (kpos <= qpos) if fox else ((kpos >> CHUNK_SHIFT) <= (qpos >> CHUNK_SHIFT))
    allowed = causal & (kpos < tk_valid)
    vv = v_ref[0]
    outs = []
    for hh in range(2):
        s = jnp.where(allowed, _dot_nt(q_op(hh), k_op(hh, slice(None))), -jnp.inf)
        p = jnp.exp(s - jnp.max(s, axis=1, keepdims=True))
        outs.append(_dot(p.astype(BF16), vv) / jnp.sum(p, axis=1, keepdims=True))
    o_ref[0] = jnp.where(lane < V_DIM, outs[0], outs[1]).astype(BF16)


def _attn_sample(q, k, v, aug, start, tk_valid):
    fox = aug is not None
    b, tq, _ = q.shape
    tk = k.shape[1]
    qw = LANE if fox else 2 * HEAD_PAD
    in_specs = [pl.BlockSpec((1, tq, qw), lambda bi, p: (bi, 0, p)),
                pl.BlockSpec((1, tk, qw), lambda bi, p: (bi, 0, p)),
                pl.BlockSpec((1, tk, LANE), lambda bi, p: (bi, 0, p))]
    args = [q, k, v]
    if fox:
        qa, ka = aug
        in_specs += [pl.BlockSpec((1, tq, LANE), lambda bi, p: (bi, start // tq, 0)),
                     pl.BlockSpec((1, tk, LANE), lambda bi, p: (bi, 0, 0))]
        args += [qa, ka]
    return pl.pallas_call(
        functools.partial(_attn_sample_body, fox=fox, start=start, tk_valid=tk_valid),
        grid=(b, N_MLA // 2), in_specs=in_specs,
        out_specs=pl.BlockSpec((1, tq, LANE), lambda bi, p: (bi, 0, p)),
        out_shape=jax.ShapeDtypeStruct((b, tq, N_MLA * V_DIM), BF16),
        compiler_params=_params("arbitrary", "arbitrary"),
        name="attn_fox_step" if fox else "attn_mla_step",
    )(*args)


def _post_attn_body(*refs, moe):
    if moe:
        x_ref, om_ref, of_ref, wo_ref, gffn_ref, wr_ref, br_ref, x2_ref, xn_ref, gate_ref = refs
    else:
        x_ref, om_ref, of_ref, wo_ref, gffn_ref, x2_ref, xn_ref = refs
    nm = N_MLA * V_DIM
    x2 = x_ref[...] + _dot(om_ref[...], wo_ref[0:nm, :]) + _dot(of_ref[...], wo_ref[nm:, :])
    x2_ref[...] = x2
    xn = _rms(x2, gffn_ref[...], D_MODEL)
    xh = xn.astype(BF16)
    xn_ref[...] = xh
    if moe:
        xl = (xn - xh.astype(F32)).astype(BF16)
        r = _dot(xh, wr_ref[...]) + _dot(xl, wr_ref[...])
        logits = r + pltpu.roll(r, LANE - N_EXPERTS, 1) + br_ref[...]
        lane = lax.broadcasted_iota(jnp.int32, logits.shape, 1).astype(F32)
        valid = lane < N_EXPERTS
        lg = jnp.where(valid, logits, -jnp.inf)
        e = jnp.exp(lg - jnp.max(lg, axis=1, keepdims=True))
        probs = e / jnp.sum(e, axis=1, keepdims=True)
        p1 = jnp.where(valid, probs, -1.0)
        m1 = jnp.max(p1, axis=1, keepdims=True)
        i1 = jnp.min(jnp.where(p1 == m1, lane, float(LANE)), axis=1, keepdims=True)
        p2 = jnp.where(lane == i1, -1.0, p1)
        m2 = jnp.max(p2, axis=1, keepdims=True)
        i2 = jnp.min(jnp.where(p2 == m2, lane, float(LANE)), axis=1, keepdims=True)
        den = m1 + m2
        gate_ref[...] = jnp.where(lane == i1, m1 / den, jnp.where(lane == i2, m2 / den, 0.0))


def _post_attn(x, om, of, lw, moe, tm):
    rows = x.shape[0]
    row = lambda w: pl.BlockSpec((tm, w), lambda i: (i, 0))
    nm = N_MLA * V_DIM
    in_specs = [row(D_MODEL), row(nm), row(nm), _full((2 * nm, D_MODEL)), _full((1, D_MODEL))]
    args = [x, om, of, lw["w_out"], lw["g_ffn"]]
    out_specs = [row(D_MODEL), row(D_MODEL)]
    out_shape = [jax.ShapeDtypeStruct((rows, D_MODEL), F32),
                 jax.ShapeDtypeStruct((rows, D_MODEL), BF16)]
    if moe:
        in_specs += [_full((D_MODEL, LANE)), _full((1, LANE))]
        args += [lw["w_router"], lw["b_router"]]
        out_specs.append(row(LANE))
        out_shape.append(jax.ShapeDtypeStruct((rows, LANE), F32))
    return pl.pallas_call(
        functools.partial(_post_attn_body, moe=moe), grid=(rows // tm,),
        in_specs=in_specs, out_specs=tuple(out_specs), out_shape=tuple(out_shape),
        compiler_params=_params("arbitrary"), name="post_attn_moe" if moe else "post_attn",
    )(*args)


def _swiglu_acc(xb, wg_ref, wu_ref, wd_ref, d_ff):
    acc = None
    for c in range(0, d_ff, MXU):
        g = _dot(xb, wg_ref[:, c:c + MXU])
        u = _dot(xb, wu_ref[:, c:c + MXU])
        h = (g * jax.nn.sigmoid(g) * u).astype(BF16)
        d = _dot(h, wd_ref[c:c + MXU, :])
        acc = d if acc is None else acc + d
    return acc


def _ffn_body(x2_ref, xn_ref, wg_ref, wu_ref, wd_ref, o_ref):
    o_ref[...] = x2_ref[...] + _swiglu_acc(xn_ref[...], wg_ref, wu_ref, wd_ref, D_FF)


def _ffn(x2, xn, lw, tm):
    rows = x2.shape[0]
    row = pl.BlockSpec((tm, D_MODEL), lambda i: (i, 0))
    return pl.pallas_call(
        _ffn_body, grid=(rows // tm,),
        in_specs=[row, row, _full((D_MODEL, D_FF)), _full((D_MODEL, D_FF)), _full((D_FF, D_MODEL))],
        out_specs=row, out_shape=jax.ShapeDtypeStruct((rows, D_MODEL), F32),
        compiler_params=_params("arbitrary"), name="ffn_dense",
    )(x2, xn, lw["w_gate"], lw["w_up"], lw["w_down"])


def _moe_body(x2_ref, xn_ref, gate_ref, wg_ref, wu_ref, wd_ref, o_ref):
    e = pl.program_id(1)

    @pl.when(e == 0)
    def _():
        o_ref[...] = x2_ref[...]

    gate = gate_ref[...]
    lane = lax.broadcasted_iota(jnp.int32, gate.shape, 1)
    ge = jnp.sum(jnp.where(lane == e, gate, 0.0), axis=1, keepdims=True)
    o_ref[...] += ge * _swiglu_acc(xn_ref[...], wg_ref.at[0], wu_ref.at[0], wd_ref.at[0], E_FF)


def _moe(x2, xn, gate, lw, tm):
    rows = x2.shape[0]
    row = lambda w: pl.BlockSpec((tm, w), lambda i, e: (i, 0))
    wspec = lambda a, b: pl.BlockSpec((1, a, b), lambda i, e: (e, 0, 0))
    return pl.pallas_call(
        _moe_body, grid=(rows // tm, N_EXPERTS),
        in_specs=[row(D_MODEL), row(D_MODEL), row(LANE), wspec(D_MODEL, E_FF),
                  wspec(D_MODEL, E_FF), wspec(E_FF, D_MODEL)],
        out_specs=row(D_MODEL), out_shape=jax.ShapeDtypeStruct((rows, D_MODEL), F32),
        compiler_params=_params("arbitrary", "arbitrary"), name="moe_gated",
    )(x2, xn, gate, lw["we_gate"], lw["we_up"], lw["we_down"])


def _consts():
    i = jnp.arange(MXU)
    blockdiag = lambda w: ((i[:, None] // w) == (i[None, :] // w)).astype(BF16)
    tri = lambda n: (jnp.arange(n)[None, :] <= jnp.arange(n)[:, None]).astype(BF16)
    r = jnp.arange(LANE)
    place = jnp.stack([((r[:, None] < N_FOX) & (r[None, :] == AUG_W * r[:, None] + k)).astype(BF16)
                       for k in range(6)])
    within = r % AUG_W
    headed = r < AUG_W * N_FOX
    ones_q = (headed & (within >= 3) & (within < 6)).astype(F32)
    ones_k = (headed & (within < 3)).astype(F32)
    aug_ones = jnp.zeros((8, LANE), F32).at[0].set(ones_q).at[1].set(ones_k)
    return {"s64": blockdiag(FOX_DIM), "s128": blockdiag(HEAD_PAD), "tri256": tri(MXU),
            "tri128": tri(LANE), "place": place, "aug_ones": aug_ones}


def _rope_tables(pos):
    half = ROPE_DIM // 2
    inv = ROPE_BASE ** (-jnp.arange(half, dtype=F32) / half)
    ang = pos.astype(F32)[:, None] * inv[None, :]
    cos, sin = jnp.cos(ang), jnp.sin(ang)
    t = pos.shape[0]
    ct = jnp.concatenate([jnp.ones((t, ROPE_LO), F32), cos, cos, jnp.zeros((t, LANE - ROPE_HI), F32)], 1)
    st = jnp.concatenate([jnp.zeros((t, ROPE_LO), F32), -sin, sin, jnp.zeros((t, LANE - ROPE_HI), F32)], 1)
    return ct, st


def _pad_lanes(v, width=LANE):
    return jnp.pad(v, [(0, 0)] * (v.ndim - 1) + [(0, width - v.shape[-1])])


def _prep_layer(i, p):
    half = ROPE_DIM // 2
    w_in = p["w_in"][i]
    s = [0, Q_LORA, Q_LORA + KV_LORA, Q_LORA + KV_LORA + ROPE_DIM]
    nfx = N_FOX * FOX_DIM
    s += [s[3] + nfx, s[3] + 2 * nfx, s[3] + 3 * nfx, s[3] + 3 * nfx + N_FOX]
    c_q, c_kv, k_rope = w_in[:, s[0]:s[1]], w_in[:, s[1]:s[2]], w_in[:, s[2]:s[3]]
    fq, fk, fv, f_logit = w_in[:, s[3]:s[4]], w_in[:, s[4]:s[5]], w_in[:, s[5]:s[6]], w_in[:, s[6]:s[7]]
    zc = lambda n: jnp.zeros((D_MODEL, n), F32)
    misc = jnp.concatenate([f_logit, zc(ROPE_LO - N_FOX), k_rope, zc(LANE - ROPE_HI)], axis=1)
    w_in_p = jnp.concatenate([c_q, c_kv, fq, fk, fv, misc], axis=1).astype(BF16)

    w_uq = p["w_uq"][i].reshape(Q_LORA, N_MLA, QK_DIM)
    nope, rope = w_uq[..., :NOPE_DIM], w_uq[..., NOPE_DIM:]
    rope_sw = jnp.concatenate([rope[..., half:], rope[..., :half]], axis=-1)
    zq = lambda n: jnp.zeros((Q_LORA, N_MLA, n), F32)
    qa = jnp.concatenate([nope, rope, zq(HEAD_PAD - QK_DIM)], axis=-1)
    qb = jnp.concatenate([zq(NOPE_DIM), rope_sw, zq(HEAD_PAD - QK_DIM)], axis=-1)
    w_uq_p = jnp.concatenate([qa.reshape(Q_LORA, -1), qb.reshape(Q_LORA, -1)], axis=1).astype(BF16)

    w_ukv = p["w_ukv"][i].reshape(KV_LORA, N_MLA, NOPE_DIM + V_DIM)
    w_k = _pad_lanes(w_ukv[..., :NOPE_DIM], HEAD_PAD).reshape(KV_LORA, -1).astype(BF16)
    w_v = w_ukv[..., NOPE_DIM:].reshape(KV_LORA, -1).astype(BF16)

    gq = _pad_lanes(p["g_qn_mla"][i] * (QK_DIM ** -0.5))[None]
    gk = _pad_lanes(p["g_kn_mla"][i])[None]
    lw = {
        "g_mix": p["g_mix"][i][None], "w_in": w_in_p, "b_f": _pad_lanes(p["b_f"][i])[None],
        "g_cq": p["g_cq"][i][None], "g_ckv": p["g_ckv"][i][None], "w_uq": w_uq_p,
        "gq": gq, "gk": gk,
        "gfq": jnp.tile(p["g_qn_fox"][i] * (FOX_DIM ** -0.5), N_FOX)[None],
        "gfk": jnp.tile(p["g_kn_fox"][i], N_FOX)[None],
        "w_k": w_k, "w_v": w_v,
        "w_out": p["w_out"][i].astype(BF16), "g_ffn": p["g_ffn"][i][None],
    }
    j = i // 2
    if i % 2 == 0:
        lw.update(w_gate=p["w_gate"][j].astype(BF16), w_up=p["w_up"][j].astype(BF16),
                  w_down=p["w_down"][j].astype(BF16))
    else:
        wr = p["w_router"][j]
        wr_hi = wr.astype(BF16)
        wr_lo = (wr - wr_hi.astype(F32)).astype(BF16)
        lw.update(w_router=_pad_lanes(jnp.concatenate([wr_hi, wr_lo], axis=1)),
                  b_router=_pad_lanes(p["b_router"][j])[None],
                  we_gate=p["we_gate"][j].astype(BF16), we_up=p["we_up"][j].astype(BF16),
                  we_down=p["we_down"][j].astype(BF16))
    return lw


def _channel_mixer(i, x, om, of, lw, tm_post, tm_ffn):
    if i % 2 == 0:
        x2, xn = _post_attn(x, om, of, lw, False, tm_post)
        return _ffn(x2, xn, lw, tm_ffn)
    x2, xn, gate = _post_attn(x, om, of, lw, True, tm_post)
    return _moe(x2, xn, gate, lw, tm_ffn)


def _trunk_prompt(x, layers, consts):
    b, t, _ = x.shape
    rows = b * t
    x = x.reshape(rows, D_MODEL)
    ct, st = _rope_tables(jnp.arange(t, dtype=jnp.int32))
    state = ([], [], [], [], [])
    nfx = N_FOX * FOX_DIM
    for i, lw in enumerate(layers):
        ckv, kr, kr128, fk, fv, lf128, q, fq, fkb, fvb = _mixer_in(x, lw, ct, st, consts, 256)
        k, v = _kv_expand(ckv, kr128, lw, consts, 512)
        aug = _fox_bias(lf128.reshape(b, t, LANE), consts)
        om = _attn_prompt(q.reshape(b, t, -1), k.reshape(b, t, -1), v.reshape(b, t, -1), None, 256)
        of = _attn_prompt(fq.reshape(b, t, nfx), fkb.reshape(b, t, nfx), fvb.reshape(b, t, nfx), aug, 256)
        x = _channel_mixer(i, x, om.reshape(rows, -1), of.reshape(rows, -1), lw, 256, 512)
        for lst, s in zip(state, (ckv.reshape(b, t, KV_LORA), kr.reshape(b, t, ROPE_DIM),
                                  fk.reshape(b, t, N_FOX, FOX_DIM), fv.reshape(b, t, N_FOX, FOX_DIM),
                                  lf128[:, :N_FOX].reshape(b, t, N_FOX))):
            lst.append(s)
    return x.reshape(b, t, D_MODEL), tuple(jnp.stack(s) for s in state)


def _trunk_sample(x, past, layers, consts):
    b, t, _ = x.shape
    rows = b * t
    start = PAST_LEN
    tk_valid = start + t
    tk = -(-tk_valid // LANE) * LANE
    pad = tk - tk_valid
    x = x.reshape(rows, D_MODEL)
    ct, st = _rope_tables(start + jnp.arange(t, dtype=jnp.int32))
    ct, st = jnp.tile(ct, (b, 1)), jnp.tile(st, (b, 1))
    p_ckv, p_krope, p_fk, p_fv, p_logf = past
    state = ([], [], [], [], [])
    nfx = N_FOX * FOX_DIM

    def cat(old, new):
        return jnp.concatenate([old, new, jnp.zeros((b, pad, new.shape[-1]), new.dtype)], axis=1)

    for i, lw in enumerate(layers):
        ckv, kr, kr128, fk, fv, lf128, q, fq, fkb, fvb = _mixer_in(x, lw, ct, st, consts, rows)
        ckv_all = cat(p_ckv[i], ckv.reshape(b, t, KV_LORA))
        old_kr128 = jnp.pad(p_krope[i], ((0, 0), (0, 0), (ROPE_LO, LANE - ROPE_HI)))
        kr_all = cat(old_kr128, kr128.reshape(b, t, LANE))
        k, v = _kv_expand(ckv_all.reshape(b * tk, KV_LORA), kr_all.reshape(b * tk, LANE), lw, consts, 512)
        lf_all = cat(_pad_lanes(p_logf[i]), lf128.reshape(b, t, LANE))
        aug = _fox_bias(lf_all, consts)
        fk_all = cat(p_fk[i].reshape(b, start, nfx).astype(BF16), fkb.reshape(b, t, nfx))
        fv_all = cat(p_fv[i].reshape(b, start, nfx).astype(BF16), fvb.reshape(b, t, nfx))
        om = _attn_sample(q.reshape(b, t, -1), k.reshape(b, tk, -1), v.reshape(b, tk, -1), None,
                          start, tk_valid)
        of = _attn_sample(fq.reshape(b, t, nfx), fk_all, fv_all, aug, start, tk_valid)
        x = _channel_mixer(i, x, om.reshape(rows, -1), of.reshape(rows, -1), lw, rows, rows)
        for lst, s in zip(state, (ckv.reshape(b, t, KV_LORA), kr.reshape(b, t, ROPE_DIM),
                                  fk.reshape(b, t, N_FOX, FOX_DIM), fv.reshape(b, t, N_FOX, FOX_DIM),
                                  lf128[:, :N_FOX].reshape(b, t, N_FOX))):
            lst.append(s)
    return x.reshape(b, t, D_MODEL), tuple(jnp.stack(s) for s in state)


def kernel(x_prompt, x_sample, cache_mla_ckv, cache_mla_krope, cache_fox_k, cache_fox_v, cache_fox_logf, g_mix, w_in, b_f, g_cq, g_ckv, w_uq, w_ukv, g_qn_mla, g_kn_mla, g_qn_fox, g_kn_fox, w_out, g_ffn, w_gate, w_up, w_down, w_router, b_router, we_gate, we_up, we_down):
    p = dict(g_mix=g_mix, w_in=w_in, b_f=b_f, g_cq=g_cq, g_ckv=g_ckv, w_uq=w_uq, w_ukv=w_ukv,
             g_qn_mla=g_qn_mla, g_kn_mla=g_kn_mla, g_qn_fox=g_qn_fox, g_kn_fox=g_kn_fox,
             w_out=w_out, g_ffn=g_ffn, w_gate=w_gate, w_up=w_up, w_down=w_down,
             w_router=w_router, b_router=b_router, we_gate=we_gate, we_up=we_up, we_down=we_down)
    layers = [_prep_layer(i, p) for i in range(DEPTH)]
    consts = _consts()
    y_p, st_p = _trunk_prompt(x_prompt, layers, consts)
    past = (cache_mla_ckv, cache_mla_krope, cache_fox_k, cache_fox_v, cache_fox_logf)
    y_s, st_s = _trunk_sample(x_sample, past, layers, consts)
    return (y_p, y_s) + st_p + st_s
```

```python
import functools

import jax
import jax.numpy as jnp
from jax import lax
from jax.experimental import pallas as pl
from jax.experimental.pallas import tpu as pltpu

F32 = jnp.float32
BF16 = jnp.bfloat16

D_MODEL = 1024
DEPTH = 4
PAST_LEN = 2048
CHUNK_SHIFT = 6
EPS = 1e-6
ROPE_BASE = 10000.0

N_MLA = 8
Q_LORA = 384
KV_LORA = 256
NOPE_DIM = 64
ROPE_DIM = 32
QK_DIM = NOPE_DIM + ROPE_DIM
V_DIM = 64
N_FOX = 8
FOX_DIM = 64
D_FF = 2816
N_EXPERTS = 8
E_FF = 1792

LANE = 128
MXU = 256
HEAD_PAD = 128
ROPE_LO = NOPE_DIM
ROPE_HI = NOPE_DIM + ROPE_DIM
AUG_W = 16

CQ_LO, CQ_HI = 0, Q_LORA
CKV_LO, CKV_HI = CQ_HI, CQ_HI + KV_LORA
FQ_LO, FQ_HI = CKV_HI, CKV_HI + N_FOX * FOX_DIM
FK_LO, FK_HI = FQ_HI, FQ_HI + N_FOX * FOX_DIM
FV_LO, FV_HI = FK_HI, FK_HI + N_FOX * FOX_DIM
MISC_LO, MISC_HI = FV_HI, FV_HI + LANE
D_IN_P = MISC_HI

VMEM_LIMIT = 56 * 1024 * 1024


def _params(*sem):
    return pltpu.CompilerParams(dimension_semantics=sem, vmem_limit_bytes=VMEM_LIMIT)


def _dot(a, b):
    return jnp.dot(a, b, preferred_element_type=F32)


def _dot_nt(a, b):
    return lax.dot_general(a, b, (((1,), (1,)), ((), ())), preferred_element_type=F32)


def _rms(x, g, n):
    return x * lax.rsqrt(jnp.sum(x * x, axis=-1, keepdims=True) * (1.0 / n) + EPS) * g


def _head_norm(x, smat, dim):
    outs = []
    for c in range(0, x.shape[1], MXU):
        xc = x[:, c:c + MXU]
        ss = _dot((xc * xc).astype(BF16), smat)
        outs.append(xc * lax.rsqrt(ss * (1.0 / dim) + EPS))
    return jnp.concatenate(outs, axis=1)


def _full(shape):
    return pl.BlockSpec(shape, lambda *_: (0,) * len(shape))


def _mixer_in_body(x_ref, gmix_ref, win_ref, bf_ref, gcq_ref, gckv_ref, wuq_ref, gq_ref,
                   gfq_ref, gfk_ref, ct_ref, st_ref, s64_ref, s128_ref,
                   ckv_ref, kr_ref, kr128_ref, fk_ref, fv_ref, lf128_ref,
                   q_ref, fq_ref, fkb_ref, fvb_ref):
    tm = x_ref.shape[0]
    xb = _rms(x_ref[...], gmix_ref[...], D_MODEL).astype(BF16)

    def proj(lo, hi):
        return _dot(xb, win_ref[:, lo:hi])

    ct = ct_ref[...]
    st = st_ref[...]
    lane = lax.broadcasted_iota(jnp.int32, (tm, LANE), 1)

    a = proj(MISC_LO, MISC_HI)
    z = a + bf_ref[...]
    lf = jnp.minimum(z, 0.0) - jnp.log1p(jnp.exp(-jnp.abs(z)))
    lf128_ref[...] = jnp.where(lane < N_FOX, lf, 0.0)
    half = ROPE_DIM // 2
    sw = jnp.where(lane < ROPE_LO + half, pltpu.roll(a, LANE - half, 1), pltpu.roll(a, half, 1))
    kr128 = jnp.where((lane >= ROPE_LO) & (lane < ROPE_HI), a * ct + sw * st, 0.0)
    kr128_ref[...] = kr128
    kr_ref[...] = kr128[:, ROPE_LO:ROPE_HI]

    ckv_ref[...] = _rms(proj(CKV_LO, CKV_HI), gckv_ref[...], KV_LORA)

    s64 = s64_ref[...]
    fk = _head_norm(proj(FK_LO, FK_HI), s64, FOX_DIM) * gfk_ref[...]
    fk_ref[...] = fk
    fkb_ref[...] = fk.astype(BF16)
    fv = proj(FV_LO, FV_HI)
    fv_ref[...] = fv
    fvb_ref[...] = fv.astype(BF16)
    fq_ref[...] = (_head_norm(proj(FQ_LO, FQ_HI), s64, FOX_DIM) * gfq_ref[...]).astype(BF16)

    cq = _rms(proj(CQ_LO, CQ_HI), gcq_ref[...], Q_LORA).astype(BF16)
    nq = N_MLA * HEAD_PAD
    qa = _dot(cq, wuq_ref[:, 0:nq])
    qb = _dot(cq, wuq_ref[:, nq:2 * nq])
    qr = jnp.concatenate(
        [qa[:, h * HEAD_PAD:(h + 1) * HEAD_PAD] * ct + qb[:, h * HEAD_PAD:(h + 1) * HEAD_PAD] * st
         for h in range(N_MLA)], axis=1)
    qn = _head_norm(qr, s128_ref[...], QK_DIM)
    gq = gq_ref[...]
    q_ref[...] = jnp.concatenate(
        [qn[:, h * HEAD_PAD:(h + 1) * HEAD_PAD] * gq for h in range(N_MLA)], axis=1).astype(BF16)


def _mixer_in(x, lw, ct, st, consts, tm):
    rows = x.shape[0]
    nt = ct.shape[0] // tm
    grid = (rows // tm,)
    row = lambda w: pl.BlockSpec((tm, w), lambda i: (i, 0))
    tab = pl.BlockSpec((tm, LANE), lambda i: (i % nt, 0))
    nfx = N_FOX * FOX_DIM
    out_shape = (
        jax.ShapeDtypeStruct((rows, KV_LORA), F32),
        jax.ShapeDtypeStruct((rows, ROPE_DIM), F32),
        jax.ShapeDtypeStruct((rows, LANE), F32),
        jax.ShapeDtypeStruct((rows, nfx), F32),
        jax.ShapeDtypeStruct((rows, nfx), F32),
        jax.ShapeDtypeStruct((rows, LANE), F32),
        jax.ShapeDtypeStruct((rows, N_MLA * HEAD_PAD), BF16),
        jax.ShapeDtypeStruct((rows, nfx), BF16),
        jax.ShapeDtypeStruct((rows, nfx), BF16),
        jax.ShapeDtypeStruct((rows, nfx), BF16),
    )
    out_specs = (row(KV_LORA), row(ROPE_DIM), row(LANE), row(nfx), row(nfx), row(LANE),
                 row(N_MLA * HEAD_PAD), row(nfx), row(nfx), row(nfx))
    in_specs = [row(D_MODEL), _full((1, D_MODEL)), _full((D_MODEL, D_IN_P)), _full((1, LANE)),
                _full((1, Q_LORA)), _full((1, KV_LORA)), _full((Q_LORA, 2 * N_MLA * HEAD_PAD)),
                _full((1, LANE)), _full((1, nfx)), _full((1, nfx)), tab, tab,
                _full((MXU, MXU)), _full((MXU, MXU))]
    return pl.pallas_call(
        _mixer_in_body, grid=grid, in_specs=in_specs, out_specs=out_specs, out_shape=out_shape,
        compiler_params=_params("arbitrary"), name="mixer_in",
    )(x, lw["g_mix"], lw["w_in"], lw["b_f"], lw["g_cq"], lw["g_ckv"], lw["w_uq"], lw["gq"],
      lw["gfq"], lw["gfk"], ct, st, consts["s64"], consts["s128"])


def _kv_body(ckv_ref, kr128_ref, wk_ref, wv_ref, gk_ref, s128_ref, k_ref, v_ref):
    cb = ckv_ref[...].astype(BF16)
    kn = _dot(cb, wk_ref[...])
    kr = kr128_ref[...]
    kk = jnp.concatenate(
        [kn[:, h * HEAD_PAD:(h + 1) * HEAD_PAD] + kr for h in range(N_MLA)], axis=1)
    kk = _head_norm(kk, s128_ref[...], QK_DIM)
    gk = gk_ref[...]
    k_ref[...] = jnp.concatenate(
        [kk[:, h * HEAD_PAD:(h + 1) * HEAD_PAD] * gk for h in range(N_MLA)], axis=1).astype(BF16)
    v_ref[...] = _dot(cb, wv_ref[...]).astype(BF16)


def _kv_expand(ckv, kr128, lw, consts, tm):
    rows = ckv.shape[0]
    row = lambda w: pl.BlockSpec((tm, w), lambda i: (i, 0))
    nk = N_MLA * HEAD_PAD
    nv = N_MLA * V_DIM
    return pl.pallas_call(
        _kv_body, grid=(rows // tm,),
        in_specs=[row(KV_LORA), row(LANE), _full((KV_LORA, nk)), _full((KV_LORA, nv)),
                  _full((1, LANE)), _full((MXU, MXU))],
        out_specs=(row(nk), row(nv)),
        out_shape=(jax.ShapeDtypeStruct((rows, nk), BF16), jax.ShapeDtypeStruct((rows, nv), BF16)),
        compiler_params=_params("arbitrary"), name="kv_expand",
    )(ckv, kr128, lw["w_k"], lw["w_v"], lw["gk"], consts["s128"])


def _cum_body(lf_ref, tri_ref, place_ref, ones_ref, qa_ref, ka_ref, *, blk):
    t = lf_ref.shape[1]
    tri = tri_ref[...]
    carry = jnp.zeros((1, LANE), F32)

    def split3(c):
        hi = c.astype(BF16)
        r1 = c - hi.astype(F32)
        mid = r1.astype(BF16)
        lo = (r1 - mid.astype(F32)).astype(BF16)
        return hi, mid, lo

    ones_q = ones_ref[0:1, :]
    ones_k = ones_ref[1:2, :]
    for b0 in range(0, t, blk):
        hi, mid, lo = split3(lf_ref[0, b0:b0 + blk, :])
        c = _dot(tri, hi) + _dot(tri, mid) + _dot(tri, lo) + carry
        carry = c[blk - 1:blk, :]
        hi, mid, lo = split3(c)
        qa = _dot(hi, place_ref[0]) + _dot(mid, place_ref[1]) + _dot(lo, place_ref[2]) + ones_q
        ka = ones_k - (_dot(hi, place_ref[3]) + _dot(mid, place_ref[4]) + _dot(lo, place_ref[5]))
        qa_ref[0, b0:b0 + blk, :] = qa.astype(BF16)
        ka_ref[0, b0:b0 + blk, :] = ka.astype(BF16)


def _fox_bias(lf128, consts):
    b, t, _ = lf128.shape
    blk = MXU if t % MXU == 0 else LANE
    tri = consts["tri256"] if blk == MXU else consts["tri128"]
    blk3 = pl.BlockSpec((1, t, LANE), lambda i: (i, 0, 0))
    return pl.pallas_call(
        functools.partial(_cum_body, blk=blk), grid=(b,),
        in_specs=[blk3, _full((blk, blk)), _full((6, LANE, LANE)), _full((8, LANE))],
        out_specs=(blk3, blk3),
        out_shape=(jax.ShapeDtypeStruct((b, t, LANE), BF16),) * 2,
        compiler_params=_params("arbitrary"), name="fox_bias",
    )(lf128, tri, consts["place"], consts["aug_ones"])


def _attn_operands(refs, fox, pair, lane):
    if fox:
        q_ref, k_ref, _, qa_ref, ka_ref = refs

        def q_op(hh):
            qm = jnp.where((lane >= FOX_DIM * hh) & (lane < FOX_DIM * (hh + 1)),
                           q_ref[0].astype(F32), 0.0)
            qa = jnp.where((lane >> 4) == 2 * pair + hh, qa_ref[0].astype(F32), 0.0)
            return jnp.concatenate([qm, qa], axis=1).astype(BF16)

        def k_op(hh, rows):
            return jnp.concatenate([k_ref[0, rows, :], ka_ref[0, rows, :]], axis=1)
    else:
        q_ref, k_ref, _ = refs

        def q_op(hh):
            return q_ref[0, :, HEAD_PAD * hh:HEAD_PAD * (hh + 1)]

        def k_op(hh, rows):
            return k_ref[0, rows, HEAD_PAD * hh:HEAD_PAD * (hh + 1)]
    return q_op, k_op


def _attn_prompt_body(*refs, fox, tq):
    n_in = 5 if fox else 3
    ins, (o_ref, o_scr) = refs[:n_in], refs[n_in:]
    q_ref, k_ref, v_ref = ins[:3]
    h = pl.program_id(1)
    hh = h % 2
    t = k_ref.shape[1]
    lane = lax.broadcasted_iota(jnp.int32, (tq, LANE), 1)
    r = lax.broadcasted_iota(jnp.int32, (tq, tq), 0)
    c = lax.broadcasted_iota(jnp.int32, (tq, tq), 1)
    allowed = (c <= r) if fox else ((c >> CHUNK_SHIFT) <= (r >> CHUNK_SHIFT))
    for i in range(t // tq):
        lo, hi = i * tq, (i + 1) * tq
        if fox:
            qa_ref, ka_ref = ins[3:]
            qm = jnp.where((lane >> 6) == hh, q_ref[0, lo:hi, :].astype(F32), 0.0)
            qa = jnp.where((lane >> 4) == h, qa_ref[0, lo:hi, :].astype(F32), 0.0)
            q = jnp.concatenate([qm, qa], axis=1).astype(BF16)
            k = jnp.concatenate([k_ref[0, 0:hi, :], ka_ref[0, 0:hi, :]], axis=1)
        else:
            q = q_ref[0, lo:hi, :]
            k = k_ref[0, 0:hi, :]
        s = _dot_nt(q, k)
        sd = jnp.where(allowed, s[:, lo:hi], -jnp.inf)
        s = sd if i == 0 else jnp.concatenate([s[:, 0:lo], sd], axis=1)
        p = jnp.exp(s - jnp.max(s, axis=1, keepdims=True))
        o_scr[lo:hi, :] = _dot(p.astype(BF16), v_ref[0, 0:hi, :]) / jnp.sum(p, axis=1, keepdims=True)

    @pl.when(hh == 0)
    def _():
        o_ref[0] = o_scr[...].astype(BF16)

    @pl.when(hh == 1)
    def _():
        lane_t = lax.broadcasted_iota(jnp.int32, (t, LANE), 1)
        o_ref[0] = jnp.where(lane_t < V_DIM, o_ref[0].astype(F32), o_scr[...]).astype(BF16)


def _attn_prompt(q, k, v, aug, tq):
    fox = aug is not None
    b, t, _ = q.shape
    blk = lambda f: pl.BlockSpec((1, t, LANE), f)
    per_head = lambda bi, h: (bi, 0, h)
    per_pair = lambda bi, h: (bi, 0, h // 2)
    shared = lambda bi, h: (bi, 0, 0)
    qk = per_pair if fox else per_head
    in_specs = [blk(qk), blk(qk), blk(per_pair)]
    args = [q, k, v]
    if fox:
        in_specs += [blk(shared), blk(shared)]
        args += list(aug)
    return pl.pallas_call(
        functools.partial(_attn_prompt_body, fox=fox, tq=tq),
        grid=(b, N_MLA), in_specs=in_specs, out_specs=blk(per_pair),
        out_shape=jax.ShapeDtypeStruct((b, t, N_MLA * V_DIM), BF16),
        scratch_shapes=[pltpu.VMEM((t, LANE), F32)],
        compiler_params=_params("arbitrary", "arbitrary"),
        name="attn_fox" if fox else "attn_mla",
    )(*args)


def _attn_sample_body(*refs, fox, start, tk_valid):
    n_in = 5 if fox else 3
    ins, o_ref = refs[:n_in], refs[n_in]
    v_ref = ins[2]
    tq = ins[0].shape[1]
    tk = ins[1].shape[1]
    pair = pl.program_id(1)
    lane = lax.broadcasted_iota(jnp.int32, (tq, LANE), 1)
    q_op, k_op = _attn_operands(ins, fox, pair, lane)
    qpos = start + lax.broadcasted_iota(jnp.int32, (tq, tk), 0)
    kpos = lax.broadcasted_iota(jnp.int32, (tq, tk), 1)
    causal = (kpos <= qpos) if fox else ((kpos >> CHUNK_SHIFT) <= (qpos >> CHUNK_SHIFT))
    allowed = causal & (kpos < tk_valid)
    vv = v_ref[0]
    outs = []
    for hh in range(2):
        s = jnp.where(allowed, _dot_nt(q_op(hh), k_op(hh, slice(None))), -jnp.inf)
        p = jnp.exp(s - jnp.max(s, axis=1, keepdims=True))
        outs.append(_dot(p.astype(BF16), vv) / jnp.sum(p, axis=1, keepdims=True))
    o_ref[0] = jnp.where(lane < V_DIM, outs[0], outs[1]).astype(BF16)


def _attn_sample(q, k, v, aug, start, tk_valid):
    fox = aug is not None
    b, tq, _ = q.shape
    tk = k.shape[1]
    qw = LANE if fox else 2 * HEAD_PAD
    in_specs = [pl.BlockSpec((1, tq, qw), lambda bi, p: (bi, 0, p)),
                pl.BlockSpec((1, tk, qw), lambda bi, p: (bi, 0, p)),
                pl.BlockSpec((1, tk, LANE), lambda bi, p: (bi, 0, p))]
    args = [q, k, v]
    if fox:
        qa, ka = aug
        in_specs += [pl.BlockSpec((1, tq, LANE), lambda bi, p: (bi, start // tq, 0)),
                     pl.BlockSpec((1, tk, LANE), lambda bi, p: (bi, 0, 0))]
        args += [qa, ka]
    return pl.pallas_call(
        functools.partial(_attn_sample_body, fox=fox, start=start, tk_valid=tk_valid),
        grid=(b, N_MLA // 2), in_specs=in_specs,
        out_specs=pl.BlockSpec((1, tq, LANE), lambda bi, p: (bi, 0, p)),
        out_shape=jax.ShapeDtypeStruct((b, tq, N_MLA * V_DIM), BF16),
        compiler_params=_params("arbitrary", "arbitrary"),
        name="attn_fox_step" if fox else "attn_mla_step",
    )(*args)


def _post_attn_body(*refs, moe):
    if moe:
        (x_ref, om_ref, of_ref, wo_ref, gffn_ref, wr_ref, br_ref, tri_ref,
         x2_ref, xn_ref, route_ref, cnt_ref, carry_ref) = refs
    else:
        x_ref, om_ref, of_ref, wo_ref, gffn_ref, x2_ref, xn_ref = refs
    nm = N_MLA * V_DIM
    x2 = x_ref[...] + _dot(om_ref[...], wo_ref[0:nm, :]) + _dot(of_ref[...], wo_ref[nm:, :])
    x2_ref[...] = x2
    xn = _rms(x2, gffn_ref[...], D_MODEL)
    xh = xn.astype(BF16)
    xn_ref[...] = xn if moe else xh
    if moe:
        xl = (xn - xh.astype(F32)).astype(BF16)
        r = _dot(xh, wr_ref[...]) + _dot(xl, wr_ref[...])
        logits = r + pltpu.roll(r, LANE - N_EXPERTS, 1) + br_ref[...]
        lane = lax.broadcasted_iota(jnp.int32, logits.shape, 1).astype(F32)
        valid = lane < N_EXPERTS
        lg = jnp.where(valid, logits, -jnp.inf)
        e = jnp.exp(lg - jnp.max(lg, axis=1, keepdims=True))
        probs = e / jnp.sum(e, axis=1, keepdims=True)
        p1 = jnp.where(valid, probs, -1.0)
        m1 = jnp.max(p1, axis=1, keepdims=True)
        i1 = jnp.min(jnp.where(p1 == m1, lane, float(LANE)), axis=1, keepdims=True)
        p2 = jnp.where(lane == i1, -1.0, p1)
        m2 = jnp.max(p2, axis=1, keepdims=True)
        i2 = jnp.min(jnp.where(p2 == m2, lane, float(LANE)), axis=1, keepdims=True)
        den = m1 + m2
        @pl.when(pl.program_id(0) == 0)
        def _():
            carry_ref[...] = jnp.zeros(carry_ref.shape, F32)

        sel = jnp.where((lane == i1) | (lane == i2), 1.0, 0.0)
        incl = _dot(tri_ref[...], sel.astype(BF16)) + carry_ref[0:1, :]
        excl = incl - sel
        rank1 = jnp.sum(jnp.where(lane == i1, excl, 0.0), axis=1, keepdims=True)
        rank2 = jnp.sum(jnp.where(lane == i2, excl, 0.0), axis=1, keepdims=True)
        total = incl[incl.shape[0] - 1:, :]
        carry_ref[...] = jnp.broadcast_to(total, carry_ref.shape)
        cnt_ref[...] = jnp.broadcast_to(total, cnt_ref.shape)
        cols = (i1, i2, rank1, rank2, m1 / den, m2 / den)
        route = jnp.zeros(logits.shape, F32)
        for n, col in enumerate(cols):
            route = jnp.where(lane == n, col, route)
        route_ref[...] = route


def _post_attn(x, om, of, lw, moe, tm, tri=None):
    rows = x.shape[0]
    row = lambda w: pl.BlockSpec((tm, w), lambda i: (i, 0))
    nm = N_MLA * V_DIM
    in_specs = [row(D_MODEL), row(nm), row(nm), _full((2 * nm, D_MODEL)), _full((1, D_MODEL))]
    args = [x, om, of, lw["w_out"], lw["g_ffn"]]
    out_specs = [row(D_MODEL), row(D_MODEL)]
    out_shape = [jax.ShapeDtypeStruct((rows, D_MODEL), F32),
                 jax.ShapeDtypeStruct((rows, D_MODEL), F32 if moe else BF16)]
    scratch = []
    if moe:
        assert tm == MXU
        in_specs += [_full((D_MODEL, LANE)), _full((1, LANE)), _full((MXU, MXU))]
        args += [lw["w_router"], lw["b_router"], tri]
        out_specs += [row(LANE), _full((8, LANE))]
        out_shape += [jax.ShapeDtypeStruct((rows, LANE), F32), jax.ShapeDtypeStruct((8, LANE), F32)]
        scratch = [pltpu.VMEM((8, LANE), F32)]
    return pl.pallas_call(
        functools.partial(_post_attn_body, moe=moe), grid=(rows // tm,),
        in_specs=in_specs, out_specs=tuple(out_specs), out_shape=tuple(out_shape),
        scratch_shapes=scratch,
        compiler_params=_params("arbitrary"), name="post_attn_moe" if moe else "post_attn",
    )(*args)


def _swiglu_acc(xb, wg_ref, wu_ref, wd_ref, d_ff):
    acc = None
    for c in range(0, d_ff, MXU):
        g = _dot(xb, wg_ref[:, c:c + MXU])
        u = _dot(xb, wu_ref[:, c:c + MXU])
        h = (g * jax.nn.sigmoid(g) * u).astype(BF16)
        d = _dot(h, wd_ref[c:c + MXU, :])
        acc = d if acc is None else acc + d
    return acc


def _ffn_body(x2_ref, xn_ref, wg_ref, wu_ref, wd_ref, o_ref):
    o_ref[...] = x2_ref[...] + _swiglu_acc(xn_ref[...], wg_ref, wu_ref, wd_ref, D_FF)


def _ffn(x2, xn, lw, tm):
    rows = x2.shape[0]
    row = pl.BlockSpec((tm, D_MODEL), lambda i: (i, 0))
    return pl.pallas_call(
        _ffn_body, grid=(rows // tm,),
        in_specs=[row, row, _full((D_MODEL, D_FF)), _full((D_MODEL, D_FF)), _full((D_FF, D_MODEL))],
        out_specs=row, out_shape=jax.ShapeDtypeStruct((rows, D_MODEL), F32),
        compiler_params=_params("arbitrary"), name="ffn_dense",
    )(x2, xn, lw["w_gate"], lw["w_up"], lw["w_down"])


SCATTER_UNROLL = 8


def _inv_body(pos_ref, inv_ref, *, two_n, tmg, ext):
    def fill(i, c):
        for u in range(SCATTER_UNROLL):
            k = i * SCATTER_UNROLL + u
            inv_ref[k] = two_n + (k & (tmg - 1))
        return c

    lax.fori_loop(0, ext // SCATTER_UNROLL, fill, 0)

    def put(i, c):
        for u in range(SCATTER_UNROLL):
            a = i * SCATTER_UNROLL + u
            inv_ref[tmg + pos_ref[a]] = a
        return c

    lax.fori_loop(0, two_n // SCATTER_UNROLL, put, 0)


def _route_inverse(pos_flat, tmg, n_tiles):
    ext = (n_tiles + 3) * tmg
    return pl.pallas_call(
        functools.partial(_inv_body, two_n=pos_flat.shape[0], tmg=tmg, ext=ext),
        in_specs=[pl.BlockSpec(memory_space=pltpu.SMEM)],
        out_specs=pl.BlockSpec(memory_space=pltpu.SMEM),
        out_shape=jax.ShapeDtypeStruct((ext,), jnp.int32), name="route_inverse",
    )(pos_flat)


def _moe_routed_body(te_ref, inv_ref, xn_hbm, wg_ref, wu_ref, wd_ref, z_hbm,
                     gbuf, obuf, gsem, ssem, *, n, tmg, n_tiles):
    del te_ref
    j = pl.program_id(0)
    cur = j % 2
    nxt = 1 - cur

    def row_copy_in(entry, slot, r):
        d = inv_ref[entry]
        tok = jnp.where(d < 2 * n, d & (n - 1), 0)
        return pltpu.make_async_copy(xn_hbm.at[pl.ds(tok, 1), :], gbuf.at[slot, pl.ds(r, 1), :],
                                     gsem.at[slot])

    def row_copy_out(entry, slot, r):
        return pltpu.make_async_copy(obuf.at[slot, pl.ds(r, 1), :],
                                     z_hbm.at[pl.ds(inv_ref[entry], 1), :], ssem.at[0])

    def tile_in(slot):
        return pltpu.make_async_copy(xn_hbm.at[pl.ds(0, tmg), :], gbuf.at[slot], gsem.at[slot])

    def tile_out(slot):
        return pltpu.make_async_copy(obuf.at[slot], z_hbm.at[pl.ds(0, tmg), :], ssem.at[0])

    @pl.when(j == 0)
    def _():
        obuf[1] = jnp.zeros(obuf.shape[1:], F32)
        for r in range(tmg):
            row_copy_in(tmg + r, 0, r).start()

    @pl.when(j >= 1)
    def _():
        tile_out(cur).wait()

    tile_in(cur).wait()
    for r in range(tmg):
        row_copy_in((j + 2) * tmg + r, nxt, r).start()
        row_copy_out(j * tmg + r, nxt, r).start()
    xb = gbuf[cur].astype(BF16)
    obuf[cur] = _swiglu_acc(xb, wg_ref.at[0], wu_ref.at[0], wd_ref.at[0], E_FF)

    @pl.when(j == n_tiles)
    def _():
        tile_out(nxt).wait()
        tile_in(nxt).wait()


def _moe_routed(xn, tile_expert, inv, lw, tmg, n_tiles):
    n = xn.shape[0]
    assert n & (n - 1) == 0
    wspec = lambda a, b: pl.BlockSpec((1, a, b), lambda j, te, iv: (te[j], 0, 0))
    grid_spec = pltpu.PrefetchScalarGridSpec(
        num_scalar_prefetch=2, grid=(n_tiles + 1,),
        in_specs=[pl.BlockSpec(memory_space=pl.ANY), wspec(D_MODEL, E_FF), wspec(D_MODEL, E_FF),
                  wspec(E_FF, D_MODEL)],
        out_specs=pl.BlockSpec(memory_space=pl.ANY),
        scratch_shapes=[pltpu.VMEM((2, tmg, D_MODEL), F32), pltpu.VMEM((2, tmg, D_MODEL), F32),
                        pltpu.SemaphoreType.DMA((2,)), pltpu.SemaphoreType.DMA((1,))])
    return pl.pallas_call(
        functools.partial(_moe_routed_body, n=n, tmg=tmg, n_tiles=n_tiles),
        grid_spec=grid_spec, out_shape=jax.ShapeDtypeStruct((2 * n + tmg, D_MODEL), F32),
        compiler_params=_params("arbitrary"), name="moe_routed",
    )(tile_expert, inv, xn, lw["we_gate"], lw["we_up"], lw["we_down"])


def _combine_body(x2_ref, route_ref, z0_ref, z1_ref, o_ref):
    route = route_ref[...]
    lane = lax.broadcasted_iota(jnp.int32, route.shape, 1)
    g1 = jnp.sum(jnp.where(lane == 4, route, 0.0), axis=1, keepdims=True)
    g2 = jnp.sum(jnp.where(lane == 5, route, 0.0), axis=1, keepdims=True)
    o_ref[...] = x2_ref[...] + g1 * z0_ref[...] + g2 * z1_ref[...]


def _moe_combine(x2, route, z, tm):
    n = x2.shape[0]
    row = lambda w: pl.BlockSpec((tm, w), lambda i: (i, 0))
    return pl.pallas_call(
        _combine_body, grid=(n // tm,),
        in_specs=[row(D_MODEL), row(LANE), row(D_MODEL),
                  pl.BlockSpec((tm, D_MODEL), lambda i: (i + n // tm, 0))],
        out_specs=row(D_MODEL), out_shape=jax.ShapeDtypeStruct((n, D_MODEL), F32),
        compiler_params=_params("arbitrary"), name="moe_combine",
    )(x2, route, z, z)


def _moe(x2, xn, route, counts, lw, tmg):
    n = x2.shape[0]
    n_tiles = 2 * n // tmg + N_EXPERTS
    e = route[:, 0:2].astype(jnp.int32)
    rank = route[:, 2:4].astype(jnp.int32)
    cnt = counts[0, :N_EXPERTS].astype(jnp.int32)
    tiles = (cnt + tmg - 1) // tmg
    tile_end = jnp.cumsum(tiles)
    pos = (tile_end - tiles)[e] * tmg + rank
    steps = jnp.minimum(jnp.arange(n_tiles + 1, dtype=jnp.int32), tile_end[-1] - 1)
    tile_expert = jnp.minimum(jnp.searchsorted(tile_end, steps, side="right"),
                              N_EXPERTS - 1).astype(jnp.int32)
    inv = _route_inverse(pos.T.reshape(-1), tmg, n_tiles)
    z = _moe_routed(xn, tile_expert, inv, lw, tmg, n_tiles)
    return _moe_combine(x2, route, z, tmg)


def _consts():
    i = jnp.arange(MXU)
    blockdiag = lambda w: ((i[:, None] // w) == (i[None, :] // w)).astype(BF16)
    tri = lambda n: (jnp.arange(n)[None, :] <= jnp.arange(n)[:, None]).astype(BF16)
    r = jnp.arange(LANE)
    place = jnp.stack([((r[:, None] < N_FOX) & (r[None, :] == AUG_W * r[:, None] + k)).astype(BF16)
                       for k in range(6)])
    within = r % AUG_W
    headed = r < AUG_W * N_FOX
    ones_q = (headed & (within >= 3) & (within < 6)).astype(F32)
    ones_k = (headed & (within < 3)).astype(F32)
    aug_ones = jnp.zeros((8, LANE), F32).at[0].set(ones_q).at[1].set(ones_k)
    return {"s64": blockdiag(FOX_DIM), "s128": blockdiag(HEAD_PAD), "tri256": tri(MXU),
            "tri128": tri(LANE), "place": place, "aug_ones": aug_ones}


def _rope_tables(pos):
    half = ROPE_DIM // 2
    inv = ROPE_BASE ** (-jnp.arange(half, dtype=F32) / half)
    ang = pos.astype(F32)[:, None] * inv[None, :]
    cos, sin = jnp.cos(ang), jnp.sin(ang)
    t = pos.shape[0]
    ct = jnp.concatenate([jnp.ones((t, ROPE_LO), F32), cos, cos, jnp.zeros((t, LANE - ROPE_HI), F32)], 1)
    st = jnp.concatenate([jnp.zeros((t, ROPE_LO), F32), -sin, sin, jnp.zeros((t, LANE - ROPE_HI), F32)], 1)
    return ct, st


def _pad_lanes(v, width=LANE):
    return jnp.pad(v, [(0, 0)] * (v.ndim - 1) + [(0, width - v.shape[-1])])


def _prep_layer(i, p):
    half = ROPE_DIM // 2
    w_in = p["w_in"][i]
    s = [0, Q_LORA, Q_LORA + KV_LORA, Q_LORA + KV_LORA + ROPE_DIM]
    nfx = N_FOX * FOX_DIM
    s += [s[3] + nfx, s[3] + 2 * nfx, s[3] + 3 * nfx, s[3] + 3 * nfx + N_FOX]
    c_q, c_kv, k_rope = w_in[:, s[0]:s[1]], w_in[:, s[1]:s[2]], w_in[:, s[2]:s[3]]
    fq, fk, fv, f_logit = w_in[:, s[3]:s[4]], w_in[:, s[4]:s[5]], w_in[:, s[5]:s[6]], w_in[:, s[6]:s[7]]
    zc = lambda n: jnp.zeros((D_MODEL, n), F32)
    misc = jnp.concatenate([f_logit, zc(ROPE_LO - N_FOX), k_rope, zc(LANE - ROPE_HI)], axis=1)
    w_in_p = jnp.concatenate([c_q, c_kv, fq, fk, fv, misc], axis=1).astype(BF16)

    w_uq = p["w_uq"][i].reshape(Q_LORA, N_MLA, QK_DIM)
    nope, rope = w_uq[..., :NOPE_DIM], w_uq[..., NOPE_DIM:]
    rope_sw = jnp.concatenate([rope[..., half:], rope[..., :half]], axis=-1)
    zq = lambda n: jnp.zeros((Q_LORA, N_MLA, n), F32)
    qa = jnp.concatenate([nope, rope, zq(HEAD_PAD - QK_DIM)], axis=-1)
    qb = jnp.concatenate([zq(NOPE_DIM), rope_sw, zq(HEAD_PAD - QK_DIM)], axis=-1)
    w_uq_p = jnp.concatenate([qa.reshape(Q_LORA, -1), qb.reshape(Q_LORA, -1)], axis=1).astype(BF16)

    w_ukv = p["w_ukv"][i].reshape(KV_LORA, N_MLA, NOPE_DIM + V_DIM)
    w_k = _pad_lanes(w_ukv[..., :NOPE_DIM], HEAD_PAD).reshape(KV_LORA, -1).astype(BF16)
    w_v = w_ukv[..., NOPE_DIM:].reshape(KV_LORA, -1).astype(BF16)

    gq = _pad_lanes(p["g_qn_mla"][i] * (QK_DIM ** -0.5))[None]
    gk = _pad_lanes(p["g_kn_mla"][i])[None]
    lw = {
        "g_mix": p["g_mix"][i][None], "w_in": w_in_p, "b_f": _pad_lanes(p["b_f"][i])[None],
        "g_cq": p["g_cq"][i][None], "g_ckv": p["g_ckv"][i][None], "w_uq": w_uq_p,
        "gq": gq, "gk": gk,
        "gfq": jnp.tile(p["g_qn_fox"][i] * (FOX_DIM ** -0.5), N_FOX)[None],
        "gfk": jnp.tile(p["g_kn_fox"][i], N_FOX)[None],
        "w_k": w_k, "w_v": w_v,
        "w_out": p["w_out"][i].astype(BF16), "g_ffn": p["g_ffn"][i][None],
    }
    j = i // 2
    if i % 2 == 0:
        lw.update(w_gate=p["w_gate"][j].astype(BF16), w_up=p["w_up"][j].astype(BF16),
                  w_down=p["w_down"][j].astype(BF16))
    else:
        wr = p["w_router"][j]
        wr_hi = wr.astype(BF16)
        wr_lo = (wr - wr_hi.astype(F32)).astype(BF16)
        lw.update(w_router=_pad_lanes(jnp.concatenate([wr_hi, wr_lo], axis=1)),
                  b_router=_pad_lanes(p["b_router"][j])[None],
                  we_gate=p["we_gate"][j].astype(BF16), we_up=p["we_up"][j].astype(BF16),
                  we_down=p["we_down"][j].astype(BF16))
    return lw


def _channel_mixer(i, x, om, of, lw, consts, tm_post, tm_ffn):
    if i % 2 == 0:
        x2, xn = _post_attn(x, om, of, lw, False, tm_post)
        return _ffn(x2, xn, lw, tm_ffn)
    x2, xn, route, counts = _post_attn(x, om, of, lw, True, MXU, consts["tri256"])
    return _moe(x2, xn, route, counts, lw, MXU)


def _trunk_prompt(x, layers, consts):
    b, t, _ = x.shape
    rows = b * t
    x = x.reshape(rows, D_MODEL)
    ct, st = _rope_tables(jnp.arange(t, dtype=jnp.int32))
    state = ([], [], [], [], [])
    nfx = N_FOX * FOX_DIM
    for i, lw in enumerate(layers):
        ckv, kr, kr128, fk, fv, lf128, q, fq, fkb, fvb = _mixer_in(x, lw, ct, st, consts, 256)
        k, v = _kv_expand(ckv, kr128, lw, consts, 512)
        aug = _fox_bias(lf128.reshape(b, t, LANE), consts)
        om = _attn_prompt(q.reshape(b, t, -1), k.reshape(b, t, -1), v.reshape(b, t, -1), None, 256)
        of = _attn_prompt(fq.reshape(b, t, nfx), fkb.reshape(b, t, nfx), fvb.reshape(b, t, nfx), aug, 256)
        x = _channel_mixer(i, x, om.reshape(rows, -1), of.reshape(rows, -1), lw, consts, 256, 512)
        for lst, s in zip(state, (ckv.reshape(b, t, KV_LORA), kr.reshape(b, t, ROPE_DIM),
                                  fk.reshape(b, t, N_FOX, FOX_DIM), fv.reshape(b, t, N_FOX, FOX_DIM),
                                  lf128[:, :N_FOX].reshape(b, t, N_FOX))):
            lst.append(s)
    return x.reshape(b, t, D_MODEL), tuple(jnp.stack(s) for s in state)


def _trunk_sample(x, past, layers, consts):
    b, t, _ = x.shape
    rows = b * t
    start = PAST_LEN
    tk_valid = start + t
    tk = -(-tk_valid // LANE) * LANE
    pad = tk - tk_valid
    x = x.reshape(rows, D_MODEL)
    ct, st = _rope_tables(start + jnp.arange(t, dtype=jnp.int32))
    ct, st = jnp.tile(ct, (b, 1)), jnp.tile(st, (b, 1))
    p_ckv, p_krope, p_fk, p_fv, p_logf = past
    state = ([], [], [], [], [])
    nfx = N_FOX * FOX_DIM

    def cat(old, new):
        return jnp.concatenate([old, new, jnp.zeros((b, pad, new.shape[-1]), new.dtype)], axis=1)

    for i, lw in enumerate(layers):
        ckv, kr, kr128, fk, fv, lf128, q, fq, fkb, fvb = _mixer_in(x, lw, ct, st, consts, rows)
        ckv_all = cat(p_ckv[i], ckv.reshape(b, t, KV_LORA))
        old_kr128 = jnp.pad(p_krope[i], ((0, 0), (0, 0), (ROPE_LO, LANE - ROPE_HI)))
        kr_all = cat(old_kr128, kr128.reshape(b, t, LANE))
        k, v = _kv_expand(ckv_all.reshape(b * tk, KV_LORA), kr_all.reshape(b * tk, LANE), lw, consts, 512)
        lf_all = cat(_pad_lanes(p_logf[i]), lf128.reshape(b, t, LANE))
        aug = _fox_bias(lf_all, consts)
        fk_all = cat(p_fk[i].reshape(b, start, nfx).astype(BF16), fkb.reshape(b, t, nfx))
        fv_all = cat(p_fv[i].reshape(b, start, nfx).astype(BF16), fvb.reshape(b, t, nfx))
        om = _attn_sample(q.reshape(b, t, -1), k.reshape(b, tk, -1), v.reshape(b, tk, -1), None,
                          start, tk_valid)
        of = _attn_sample(fq.reshape(b, t, nfx), fk_all, fv_all, aug, start, tk_valid)
        x = _channel_mixer(i, x, om.reshape(rows, -1), of.reshape(rows, -1), lw, consts, rows, rows)
        for lst, s in zip(state, (ckv.reshape(b, t, KV_LORA), kr.reshape(b, t, ROPE_DIM),
                                  fk.reshape(b, t, N_FOX, FOX_DIM), fv.reshape(b, t, N_FOX, FOX_DIM),
                                  lf128[:, :N_FOX].reshape(b, t, N_FOX))):
            lst.append(s)
    return x.reshape(b, t, D_MODEL), tuple(jnp.stack(s) for s in state)


def kernel(x_prompt, x_sample, cache_mla_ckv, cache_mla_krope, cache_fox_k, cache_fox_v, cache_fox_logf, g_mix, w_in, b_f, g_cq, g_ckv, w_uq, w_ukv, g_qn_mla, g_kn_mla, g_qn_fox, g_kn_fox, w_out, g_ffn, w_gate, w_up, w_down, w_router, b_router, we_gate, we_up, we_down):
    p = dict(g_mix=g_mix, w_in=w_in, b_f=b_f, g_cq=g_cq, g_ckv=g_ckv, w_uq=w_uq, w_ukv=w_ukv,
             g_qn_mla=g_qn_mla, g_kn_mla=g_kn_mla, g_qn_fox=g_qn_fox, g_kn_fox=g_kn_fox,
             w_out=w_out, g_ffn=g_ffn, w_gate=w_gate, w_up=w_up, w_down=w_down,
             w_router=w_router, b_router=b_router, we_gate=we_gate, we_up=we_up, we_down=we_down)
    layers = [_prep_layer(i, p) for i in range(DEPTH)]
    consts = _consts()
    y_p, st_p = _trunk_prompt(x_prompt, layers, consts)
    past = (cache_mla_ckv, cache_mla_krope, cache_fox_k, cache_fox_v, cache_fox_logf)
    y_s, st_s = _trunk_sample(x_sample, past, layers, consts)
    return (y_p, y_s) + st_p + st_s
```

```python
import functools

import jax
import jax.numpy as jnp
from jax import lax
from jax.experimental import pallas as pl
from jax.experimental.pallas import tpu as pltpu

F32 = jnp.float32
BF16 = jnp.bfloat16

D_MODEL = 1024
DEPTH = 4
PAST_LEN = 2048
CHUNK_SHIFT = 6
EPS = 1e-6
ROPE_BASE = 10000.0

N_MLA = 8
Q_LORA = 384
KV_LORA = 256
NOPE_DIM = 64
ROPE_DIM = 32
QK_DIM = NOPE_DIM + ROPE_DIM
V_DIM = 64
N_FOX = 8
FOX_DIM = 64
D_FF = 2816
N_EXPERTS = 8
E_FF = 1792

LANE = 128
MXU = 256
HEAD_PAD = 128
ROPE_LO = NOPE_DIM
ROPE_HI = NOPE_DIM + ROPE_DIM
AUG_W = 16

CQ_LO, CQ_HI = 0, Q_LORA
CKV_LO, CKV_HI = CQ_HI, CQ_HI + KV_LORA
FQ_LO, FQ_HI = CKV_HI, CKV_HI + N_FOX * FOX_DIM
FK_LO, FK_HI = FQ_HI, FQ_HI + N_FOX * FOX_DIM
FV_LO, FV_HI = FK_HI, FK_HI + N_FOX * FOX_DIM
MISC_LO, MISC_HI = FV_HI, FV_HI + LANE
D_IN_P = MISC_HI

VMEM_LIMIT = 56 * 1024 * 1024


def _params(*sem):
    return pltpu.CompilerParams(dimension_semantics=sem, vmem_limit_bytes=VMEM_LIMIT)


def _dot(a, b):
    return jnp.dot(a, b, preferred_element_type=F32)


def _dot_nt(a, b):
    return lax.dot_general(a, b, (((1,), (1,)), ((), ())), preferred_element_type=F32)


def _rms(x, g, n):
    return x * lax.rsqrt(jnp.sum(x * x, axis=-1, keepdims=True) * (1.0 / n) + EPS) * g


def _head_norm(x, smat, dim):
    outs = []
    for c in range(0, x.shape[1], MXU):
        xc = x[:, c:c + MXU]
        ss = _dot((xc * xc).astype(BF16), smat)
        outs.append(xc * lax.rsqrt(ss * (1.0 / dim) + EPS))
    return jnp.concatenate(outs, axis=1)


def _full(shape):
    return pl.BlockSpec(shape, lambda *_: (0,) * len(shape))


def _mixer_in_body(x_ref, gmix_ref, win_ref, bf_ref, gcq_ref, gckv_ref, wuq_ref, gq_ref,
                   gfq_ref, gfk_ref, ct_ref, st_ref, s64_ref, s128_ref,
                   ckv_ref, kr_ref, kr128_ref, fk_ref, fv_ref, lf128_ref,
                   q_ref, fq_ref, fkb_ref, fvb_ref):
    tm = x_ref.shape[0]
    xb = _rms(x_ref[...], gmix_ref[...], D_MODEL).astype(BF16)

    def proj(lo, hi):
        return _dot(xb, win_ref[:, lo:hi])

    ct = ct_ref[...]
    st = st_ref[...]
    lane = lax.broadcasted_iota(jnp.int32, (tm, LANE), 1)

    a = proj(MISC_LO, MISC_HI)
    z = a + bf_ref[...]
    lf = jnp.minimum(z, 0.0) - jnp.log1p(jnp.exp(-jnp.abs(z)))
    lf128_ref[...] = jnp.where(lane < N_FOX, lf, 0.0)
    half = ROPE_DIM // 2
    sw = jnp.where(lane < ROPE_LO + half, pltpu.roll(a, LANE - half, 1), pltpu.roll(a, half, 1))
    kr128 = jnp.where((lane >= ROPE_LO) & (lane < ROPE_HI), a * ct + sw * st, 0.0)
    kr128_ref[...] = kr128
    kr_ref[...] = kr128[:, ROPE_LO:ROPE_HI]

    ckv_ref[...] = _rms(proj(CKV_LO, CKV_HI), gckv_ref[...], KV_LORA)

    s64 = s64_ref[...]
    fk = _head_norm(proj(FK_LO, FK_HI), s64, FOX_DIM) * gfk_ref[...]
    fk_ref[...] = fk
    fkb_ref[...] = fk.astype(BF16)
    fv = proj(FV_LO, FV_HI)
    fv_ref[...] = fv
    fvb_ref[...] = fv.astype(BF16)
    fq_ref[...] = (_head_norm(proj(FQ_LO, FQ_HI), s64, FOX_DIM) * gfq_ref[...]).astype(BF16)

    cq = _rms(proj(CQ_LO, CQ_HI), gcq_ref[...], Q_LORA).astype(BF16)
    nq = N_MLA * HEAD_PAD
    qa = _dot(cq, wuq_ref[:, 0:nq])
    qb = _dot(cq, wuq_ref[:, nq:2 * nq])
    qr = jnp.concatenate(
        [qa[:, h * HEAD_PAD:(h + 1) * HEAD_PAD] * ct + qb[:, h * HEAD_PAD:(h + 1) * HEAD_PAD] * st
         for h in range(N_MLA)], axis=1)
    qn = _head_norm(qr, s128_ref[...], QK_DIM)
    gq = gq_ref[...]
    q_ref[...] = jnp.concatenate(
        [qn[:, h * HEAD_PAD:(h + 1) * HEAD_PAD] * gq for h in range(N_MLA)], axis=1).astype(BF16)


def _mixer_in(x, lw, ct, st, consts, tm):
    rows = x.shape[0]
    nt = ct.shape[0] // tm
    grid = (rows // tm,)
    row = lambda w: pl.BlockSpec((tm, w), lambda i: (i, 0))
    tab = pl.BlockSpec((tm, LANE), lambda i: (i % nt, 0))
    nfx = N_FOX * FOX_DIM
    out_shape = (
        jax.ShapeDtypeStruct((rows, KV_LORA), F32),
        jax.ShapeDtypeStruct((rows, ROPE_DIM), F32),
        jax.ShapeDtypeStruct((rows, LANE), F32),
        jax.ShapeDtypeStruct((rows, nfx), F32),
        jax.ShapeDtypeStruct((rows, nfx), F32),
        jax.ShapeDtypeStruct((rows, LANE), F32),
        jax.ShapeDtypeStruct((rows, N_MLA * HEAD_PAD), BF16),
        jax.ShapeDtypeStruct((rows, nfx), BF16),
        jax.ShapeDtypeStruct((rows, nfx), BF16),
        jax.ShapeDtypeStruct((rows, nfx), BF16),
    )
    out_specs = (row(KV_LORA), row(ROPE_DIM), row(LANE), row(nfx), row(nfx), row(LANE),
                 row(N_MLA * HEAD_PAD), row(nfx), row(nfx), row(nfx))
    in_specs = [row(D_MODEL), _full((1, D_MODEL)), _full((D_MODEL, D_IN_P)), _full((1, LANE)),
                _full((1, Q_LORA)), _full((1, KV_LORA)), _full((Q_LORA, 2 * N_MLA * HEAD_PAD)),
                _full((1, LANE)), _full((1, nfx)), _full((1, nfx)), tab, tab,
                _full((MXU, MXU)), _full((MXU, MXU))]
    return pl.pallas_call(
        _mixer_in_body, grid=grid, in_specs=in_specs, out_specs=out_specs, out_shape=out_shape,
        compiler_params=_params("arbitrary"), name="mixer_in",
    )(x, lw["g_mix"], lw["w_in"], lw["b_f"], lw["g_cq"], lw["g_ckv"], lw["w_uq"], lw["gq"],
      lw["gfq"], lw["gfk"], ct, st, consts["s64"], consts["s128"])


def _kv_body(ckv_ref, kr128_ref, wk_ref, wv_ref, gk_ref, s128_ref, k_ref, v_ref):
    cb = ckv_ref[...].astype(BF16)
    kn = _dot(cb, wk_ref[...])
    kr = kr128_ref[...]
    kk = jnp.concatenate(
        [kn[:, h * HEAD_PAD:(h + 1) * HEAD_PAD] + kr for h in range(N_MLA)], axis=1)
    kk = _head_norm(kk, s128_ref[...], QK_DIM)
    gk = gk_ref[...]
    k_ref[...] = jnp.concatenate(
        [kk[:, h * HEAD_PAD:(h + 1) * HEAD_PAD] * gk for h in range(N_MLA)], axis=1).astype(BF16)
    v_ref[...] = _dot(cb, wv_ref[...]).astype(BF16)


def _kv_expand(ckv, kr128, lw, consts, tm):
    rows = ckv.shape[0]
    row = lambda w: pl.BlockSpec((tm, w), lambda i: (i, 0))
    nk = N_MLA * HEAD_PAD
    nv = N_MLA * V_DIM
    return pl.pallas_call(
        _kv_body, grid=(rows // tm,),
        in_specs=[row(KV_LORA), row(LANE), _full((KV_LORA, nk)), _full((KV_LORA, nv)),
                  _full((1, LANE)), _full((MXU, MXU))],
        out_specs=(row(nk), row(nv)),
        out_shape=(jax.ShapeDtypeStruct((rows, nk), BF16), jax.ShapeDtypeStruct((rows, nv), BF16)),
        compiler_params=_params("arbitrary"), name="kv_expand",
    )(ckv, kr128, lw["w_k"], lw["w_v"], lw["gk"], consts["s128"])


def _cum_body(lf_ref, tri_ref, place_ref, ones_ref, qa_ref, ka_ref, *, blk):
    t = lf_ref.shape[1]
    tri = tri_ref[...]
    carry = jnp.zeros((1, LANE), F32)

    def split3(c):
        hi = c.astype(BF16)
        r1 = c - hi.astype(F32)
        mid = r1.astype(BF16)
        lo = (r1 - mid.astype(F32)).astype(BF16)
        return hi, mid, lo

    ones_q = ones_ref[0:1, :]
    ones_k = ones_ref[1:2, :]
    for b0 in range(0, t, blk):
        hi, mid, lo = split3(lf_ref[0, b0:b0 + blk, :])
        c = _dot(tri, hi) + _dot(tri, mid) + _dot(tri, lo) + carry
        carry = c[blk - 1:blk, :]
        hi, mid, lo = split3(c)
        qa = _dot(hi, place_ref[0]) + _dot(mid, place_ref[1]) + _dot(lo, place_ref[2]) + ones_q
        ka = ones_k - (_dot(hi, place_ref[3]) + _dot(mid, place_ref[4]) + _dot(lo, place_ref[5]))
        qa_ref[0, b0:b0 + blk, :] = qa.astype(BF16)
        ka_ref[0, b0:b0 + blk, :] = ka.astype(BF16)


def _fox_bias(lf128, consts):
    b, t, _ = lf128.shape
    blk = MXU if t % MXU == 0 else LANE
    tri = consts["tri256"] if blk == MXU else consts["tri128"]
    blk3 = pl.BlockSpec((1, t, LANE), lambda i: (i, 0, 0))
    return pl.pallas_call(
        functools.partial(_cum_body, blk=blk), grid=(b,),
        in_specs=[blk3, _full((blk, blk)), _full((6, LANE, LANE)), _full((8, LANE))],
        out_specs=(blk3, blk3),
        out_shape=(jax.ShapeDtypeStruct((b, t, LANE), BF16),) * 2,
        compiler_params=_params("arbitrary"), name="fox_bias",
    )(lf128, tri, consts["place"], consts["aug_ones"])


def _attn_operands(refs, fox, pair, lane):
    if fox:
        q_ref, k_ref, _, qa_ref, ka_ref = refs

        def q_op(hh):
            qm = jnp.where((lane >= FOX_DIM * hh) & (lane < FOX_DIM * (hh + 1)),
                           q_ref[0].astype(F32), 0.0)
            qa = jnp.where((lane >> 4) == 2 * pair + hh, qa_ref[0].astype(F32), 0.0)
            return jnp.concatenate([qm, qa], axis=1).astype(BF16)

        def k_op(hh, rows):
            return jnp.concatenate([k_ref[0, rows, :], ka_ref[0, rows, :]], axis=1)
    else:
        q_ref, k_ref, _ = refs

        def q_op(hh):
            return q_ref[0, :, HEAD_PAD * hh:HEAD_PAD * (hh + 1)]

        def k_op(hh, rows):
            return k_ref[0, rows, HEAD_PAD * hh:HEAD_PAD * (hh + 1)]
    return q_op, k_op


def _attn_prompt_body(*refs, fox, tq):
    n_in = 5 if fox else 3
    ins, (o_ref, o_scr) = refs[:n_in], refs[n_in:]
    q_ref, k_ref, v_ref = ins[:3]
    h = pl.program_id(1)
    hh = h % 2
    t = k_ref.shape[1]
    lane = lax.broadcasted_iota(jnp.int32, (tq, LANE), 1)
    r = lax.broadcasted_iota(jnp.int32, (tq, tq), 0)
    c = lax.broadcasted_iota(jnp.int32, (tq, tq), 1)
    allowed = (c <= r) if fox else ((c >> CHUNK_SHIFT) <= (r >> CHUNK_SHIFT))
    for i in range(t // tq):
        lo, hi = i * tq, (i + 1) * tq
        if fox:
            qa_ref, ka_ref = ins[3:]
            qm = jnp.where((lane >> 6) == hh, q_ref[0, lo:hi, :].astype(F32), 0.0)
            qa = jnp.where((lane >> 4) == h, qa_ref[0, lo:hi, :].astype(F32), 0.0)
            q = jnp.concatenate([qm, qa], axis=1).astype(BF16)
            k = jnp.concatenate([k_ref[0, 0:hi, :], ka_ref[0, 0:hi, :]], axis=1)
        else:
            q = q_ref[0, lo:hi, :]
            k = k_ref[0, 0:hi, :]
        s = _dot_nt(q, k)
        sd = jnp.where(allowed, s[:, lo:hi], -jnp.inf)
        s = sd if i == 0 else jnp.concatenate([s[:, 0:lo], sd], axis=1)
        p = jnp.exp(s - jnp.max(s, axis=1, keepdims=True))
        o_scr[lo:hi, :] = _dot(p.astype(BF16), v_ref[0, 0:hi, :]) / jnp.sum(p, axis=1, keepdims=True)

    @pl.when(hh == 0)
    def _():
        o_ref[0] = o_scr[...].astype(BF16)

    @pl.when(hh == 1)
    def _():
        lane_t = lax.broadcasted_iota(jnp.int32, (t, LANE), 1)
        o_ref[0] = jnp.where(lane_t < V_DIM, o_ref[0].astype(F32), o_scr[...]).astype(BF16)


def _attn_prompt(q, k, v, aug, tq):
    fox = aug is not None
    b, t, _ = q.shape
    blk = lambda f: pl.BlockSpec((1, t, LANE), f)
    per_head = lambda bi, h: (bi, 0, h)
    per_pair = lambda bi, h: (bi, 0, h // 2)
    shared = lambda bi, h: (bi, 0, 0)
    qk = per_pair if fox else per_head
    in_specs = [blk(qk), blk(qk), blk(per_pair)]
    args = [q, k, v]
    if fox:
        in_specs += [blk(shared), blk(shared)]
        args += list(aug)
    return pl.pallas_call(
        functools.partial(_attn_prompt_body, fox=fox, tq=tq),
        grid=(b, N_MLA), in_specs=in_specs, out_specs=blk(per_pair),
        out_shape=jax.ShapeDtypeStruct((b, t, N_MLA * V_DIM), BF16),
        scratch_shapes=[pltpu.VMEM((t, LANE), F32)],
        compiler_params=_params("arbitrary", "arbitrary"),
        name="attn_fox" if fox else "attn_mla",
    )(*args)


def _attn_sample_body(*refs, fox, start, tk_valid):
    n_in = 5 if fox else 3
    ins, o_ref = refs[:n_in], refs[n_in]
    v_ref = ins[2]
    tq = ins[0].shape[1]
    tk = ins[1].shape[1]
    pair = pl.program_id(1)
    lane = lax.broadcasted_iota(jnp.int32, (tq, LANE), 1)
    q_op, k_op = _attn_operands(ins, fox, pair, lane)
    qpos = start + lax.broadcasted_iota(jnp.int32, (tq, tk), 0)
    kpos = lax.broadcasted_iota(jnp.int32, (tq, tk), 1)
    causal = (kpos <= qpos) if fox else ((kpos >> CHUNK_SHIFT) <= (qpos >> CHUNK_SHIFT))
    allowed = causal & (kpos < tk_valid)
    vv = v_ref[0]
    outs = []
    for hh in range(2):
        s = jnp.where(allowed, _dot_nt(q_op(hh), k_op(hh, slice(None))), -jnp.inf)
        p = jnp.exp(s - jnp.max(s, axis=1, keepdims=True))
        outs.append(_dot(p.astype(BF16), vv) / jnp.sum(p, axis=1, keepdims=True))
    o_ref[0] = jnp.where(lane < V_DIM, outs[0], outs[1]).astype(BF16)


def _attn_sample(q, k, v, aug, start, tk_valid):
    fox = aug is not None
    b, tq, _ = q.shape
    tk = k.shape[1]
    qw = LANE if fox else 2 * HEAD_PAD
    in_specs = [pl.BlockSpec((1, tq, qw), lambda bi, p: (bi, 0, p)),
                pl.BlockSpec((1, tk, qw), lambda bi, p: (bi, 0, p)),
                pl.BlockSpec((1, tk, LANE), lambda bi, p: (bi, 0, p))]
    args = [q, k, v]
    if fox:
        qa, ka = aug
        in_specs += [pl.BlockSpec((1, tq, LANE), lambda bi, p: (bi, start // tq, 0)),
                     pl.BlockSpec((1, tk, LANE), lambda bi, p: (bi, 0, 0))]
        args += [qa, ka]
    return pl.pallas_call(
        functools.partial(_attn_sample_body, fox=fox, start=start, tk_valid=tk_valid),
        grid=(b, N_MLA // 2), in_specs=in_specs,
        out_specs=pl.BlockSpec((1, tq, LANE), lambda bi, p: (bi, 0, p)),
        out_shape=jax.ShapeDtypeStruct((b, tq, N_MLA * V_DIM), BF16),
        compiler_params=_params("arbitrary", "arbitrary"),
        name="attn_fox_step" if fox else "attn_mla_step",
    )(*args)


def _post_attn_body(*refs, moe):
    if moe:
        (x_ref, om_ref, of_ref, wo_ref, gffn_ref, wr_ref, br_ref, tri_ref,
         x2_ref, xn_ref, route_ref, routet_ref, cnt_ref, carry_ref) = refs
    else:
        x_ref, om_ref, of_ref, wo_ref, gffn_ref, x2_ref, xn_ref = refs
    nm = N_MLA * V_DIM
    x2 = x_ref[...] + _dot(om_ref[...], wo_ref[0:nm, :]) + _dot(of_ref[...], wo_ref[nm:, :])
    x2_ref[...] = x2
    xn = _rms(x2, gffn_ref[...], D_MODEL)
    xh = xn.astype(BF16)
    xn_ref[...] = xn if moe else xh
    if moe:
        xl = (xn - xh.astype(F32)).astype(BF16)
        r = _dot(xh, wr_ref[...]) + _dot(xl, wr_ref[...])
        logits = r + pltpu.roll(r, LANE - N_EXPERTS, 1) + br_ref[...]
        lane = lax.broadcasted_iota(jnp.int32, logits.shape, 1).astype(F32)
        valid = lane < N_EXPERTS
        lg = jnp.where(valid, logits, -jnp.inf)
        e = jnp.exp(lg - jnp.max(lg, axis=1, keepdims=True))
        probs = e / jnp.sum(e, axis=1, keepdims=True)
        p1 = jnp.where(valid, probs, -1.0)
        m1 = jnp.max(p1, axis=1, keepdims=True)
        i1 = jnp.min(jnp.where(p1 == m1, lane, float(LANE)), axis=1, keepdims=True)
        p2 = jnp.where(lane == i1, -1.0, p1)
        m2 = jnp.max(p2, axis=1, keepdims=True)
        i2 = jnp.min(jnp.where(p2 == m2, lane, float(LANE)), axis=1, keepdims=True)
        den = m1 + m2
        @pl.when(pl.program_id(0) == 0)
        def _():
            carry_ref[...] = jnp.zeros(carry_ref.shape, F32)

        sel = jnp.where((lane == i1) | (lane == i2), 1.0, 0.0)
        incl = _dot(tri_ref[...], sel.astype(BF16)) + carry_ref[0:1, :]
        excl = incl - sel
        rank1 = jnp.sum(jnp.where(lane == i1, excl, 0.0), axis=1, keepdims=True)
        rank2 = jnp.sum(jnp.where(lane == i2, excl, 0.0), axis=1, keepdims=True)
        total = incl[incl.shape[0] - 1:, :]
        carry_ref[...] = jnp.broadcast_to(total, carry_ref.shape)
        cnt_ref[...] = jnp.broadcast_to(total, cnt_ref.shape)
        cols = (i1, i2, rank1, rank2, m1 / den, m2 / den)
        route = jnp.zeros(logits.shape, F32)
        for n, col in enumerate(cols):
            route = jnp.where(lane == n, col, route)
        route_ref[...] = route
        routet_ref[...] = route.T[0:8, :]


def _post_attn(x, om, of, lw, moe, tm, tri=None):
    rows = x.shape[0]
    row = lambda w: pl.BlockSpec((tm, w), lambda i: (i, 0))
    nm = N_MLA * V_DIM
    in_specs = [row(D_MODEL), row(nm), row(nm), _full((2 * nm, D_MODEL)), _full((1, D_MODEL))]
    args = [x, om, of, lw["w_out"], lw["g_ffn"]]
    out_specs = [row(D_MODEL), row(D_MODEL)]
    out_shape = [jax.ShapeDtypeStruct((rows, D_MODEL), F32),
                 jax.ShapeDtypeStruct((rows, D_MODEL), F32 if moe else BF16)]
    scratch = []
    if moe:
        assert tm == MXU
        in_specs += [_full((D_MODEL, LANE)), _full((1, LANE)), _full((MXU, MXU))]
        args += [lw["w_router"], lw["b_router"], tri]
        out_specs += [row(LANE), pl.BlockSpec((8, tm), lambda i: (0, i)), _full((8, LANE))]
        out_shape += [jax.ShapeDtypeStruct((rows, LANE), F32), jax.ShapeDtypeStruct((8, rows), F32),
                      jax.ShapeDtypeStruct((8, LANE), F32)]
        scratch = [pltpu.VMEM((8, LANE), F32)]
    return pl.pallas_call(
        functools.partial(_post_attn_body, moe=moe), grid=(rows // tm,),
        in_specs=in_specs, out_specs=tuple(out_specs), out_shape=tuple(out_shape),
        scratch_shapes=scratch,
        compiler_params=_params("arbitrary"), name="post_attn_moe" if moe else "post_attn",
    )(*args)


def _swiglu_acc(xb, wg_ref, wu_ref, wd_ref, d_ff, before_chunk=None):
    acc = None
    for c in range(0, d_ff, MXU):
        if before_chunk is not None:
            before_chunk(c // MXU, d_ff // MXU)
        g = _dot(xb, wg_ref[:, c:c + MXU])
        u = _dot(xb, wu_ref[:, c:c + MXU])
        h = (g * jax.nn.sigmoid(g) * u).astype(BF16)
        d = _dot(h, wd_ref[c:c + MXU, :])
        acc = d if acc is None else acc + d
    return acc


def _ffn_body(x2_ref, xn_ref, wg_ref, wu_ref, wd_ref, o_ref):
    o_ref[...] = x2_ref[...] + _swiglu_acc(xn_ref[...], wg_ref.at[0], wu_ref.at[0], wd_ref.at[0], D_FF)


def _ffn(x2, xn, lw, tm):
    rows = x2.shape[0]
    row = pl.BlockSpec((tm, D_MODEL), lambda i: (i, 0))
    jl = lw["mixer_idx"]
    wspec = lambda a, b: pl.BlockSpec((1, a, b), lambda i: (jl, 0, 0))
    return pl.pallas_call(
        _ffn_body, grid=(rows // tm,),
        in_specs=[row, row, wspec(D_MODEL, D_FF), wspec(D_MODEL, D_FF), wspec(D_FF, D_MODEL)],
        out_specs=row, out_shape=jax.ShapeDtypeStruct((rows, D_MODEL), F32),
        compiler_params=_params("arbitrary"), name="ffn_dense",
    )(x2, xn, lw["w_gate"], lw["w_up"], lw["w_down"])


SCATTER_UNROLL = 8


def _inv_body(pos_ref, base_hbm, inv_ref, sem, *, tmg):
    fill = pltpu.make_async_copy(base_hbm, inv_ref, sem.at[0])
    fill.start()
    fill.wait()

    def put(i, c):
        for u in range(SCATTER_UNROLL):
            a = i * SCATTER_UNROLL + u
            inv_ref[tmg + pos_ref[a]] = a
        return c

    lax.fori_loop(0, pos_ref.shape[0] // SCATTER_UNROLL, put, 0)


def _route_inverse(pos_flat, base, tmg):
    return pl.pallas_call(
        functools.partial(_inv_body, tmg=tmg),
        in_specs=[pl.BlockSpec(memory_space=pltpu.SMEM), pl.BlockSpec(memory_space=pl.ANY)],
        out_specs=pl.BlockSpec(memory_space=pltpu.SMEM),
        out_shape=jax.ShapeDtypeStruct(base.shape, jnp.int32),
        scratch_shapes=[pltpu.SemaphoreType.DMA((1,))], name="route_inverse",
    )(pos_flat, base)


def _moe_routed_body(te_ref, src_ref, dst_ref, xn_hbm, wg_ref, wu_ref, wd_ref, z_hbm,
                     gbuf, obuf, gsem, ssem, *, tmg, n_tiles):
    del te_ref
    j = pl.program_id(0)

    def row_in(entry, slot, r):
        return pltpu.make_async_copy(xn_hbm.at[pl.ds(src_ref[entry], 1), :],
                                     gbuf.at[slot, pl.ds(r, 1), :], gsem.at[slot])

    def row_out(entry, slot, r):
        return pltpu.make_async_copy(obuf.at[slot, pl.ds(r, 1), :],
                                     z_hbm.at[pl.ds(dst_ref[entry], 1), :], ssem.at[0])

    def tile_in(slot):
        return pltpu.make_async_copy(xn_hbm.at[pl.ds(0, tmg), :], gbuf.at[slot], gsem.at[slot])

    def tile_out(slot):
        return pltpu.make_async_copy(obuf.at[slot], z_hbm.at[pl.ds(0, tmg), :], ssem.at[0])

    @pl.when(j == 0)
    def _():
        obuf[1] = jnp.zeros(obuf.shape[1:], F32)
        for r in range(tmg):
            row_in(tmg + r, 0, r).start()

    def step(cur):
        nxt = 1 - cur

        @pl.when(j >= 1)
        def _():
            tile_out(cur).wait()

        tile_in(cur).wait()
        xb = gbuf[cur].astype(BF16)

        def issue(k, n_chunks):
            for r in range(k * tmg // n_chunks, (k + 1) * tmg // n_chunks):
                row_in((j + 2) * tmg + r, nxt, r).start()
                row_out(j * tmg + r, nxt, r).start()

        obuf[cur] = _swiglu_acc(xb, wg_ref.at[0, 0], wu_ref.at[0, 0], wd_ref.at[0, 0], E_FF, issue)

    for parity in range(2):
        pl.when(j % 2 == parity)(functools.partial(step, parity))

    assert n_tiles % 2 == 0

    @pl.when(j == n_tiles)
    def _():
        tile_out(1).wait()
        tile_in(1).wait()


def _moe_routed(xn, tile_expert, src, dst, lw, tmg, n_tiles):
    n = xn.shape[0]
    jl = lw["mixer_idx"]
    wspec = lambda a, b: pl.BlockSpec((1, 1, a, b), lambda j, te, s, d: (jl, te[j], 0, 0))
    grid_spec = pltpu.PrefetchScalarGridSpec(
        num_scalar_prefetch=3, grid=(n_tiles + 1,),
        in_specs=[pl.BlockSpec(memory_space=pl.ANY), wspec(D_MODEL, E_FF), wspec(D_MODEL, E_FF),
                  wspec(E_FF, D_MODEL)],
        out_specs=pl.BlockSpec(memory_space=pl.ANY),
        scratch_shapes=[pltpu.VMEM((2, tmg, D_MODEL), F32), pltpu.VMEM((2, tmg, D_MODEL), F32),
                        pltpu.SemaphoreType.DMA((2,)), pltpu.SemaphoreType.DMA((1,))])
    return pl.pallas_call(
        functools.partial(_moe_routed_body, tmg=tmg, n_tiles=n_tiles),
        grid_spec=grid_spec, out_shape=jax.ShapeDtypeStruct((2 * n + tmg, D_MODEL), F32),
        compiler_params=_params("arbitrary"), name="moe_routed",
    )(tile_expert, src, dst, xn, lw["we_gate"], lw["we_up"], lw["we_down"])


def _combine_body(x2_ref, route_ref, z0_ref, z1_ref, o_ref):
    route = route_ref[...]
    lane = lax.broadcasted_iota(jnp.int32, route.shape, 1)
    g1 = jnp.sum(jnp.where(lane == 4, route, 0.0), axis=1, keepdims=True)
    g2 = jnp.sum(jnp.where(lane == 5, route, 0.0), axis=1, keepdims=True)
    o_ref[...] = x2_ref[...] + g1 * z0_ref[...] + g2 * z1_ref[...]


def _moe_combine(x2, route, z, tm):
    n = x2.shape[0]
    row = lambda w: pl.BlockSpec((tm, w), lambda i: (i, 0))
    return pl.pallas_call(
        _combine_body, grid=(n // tm,),
        in_specs=[row(D_MODEL), row(LANE), row(D_MODEL),
                  pl.BlockSpec((tm, D_MODEL), lambda i: (i + n // tm, 0))],
        out_specs=row(D_MODEL), out_shape=jax.ShapeDtypeStruct((n, D_MODEL), F32),
        compiler_params=_params("arbitrary"), name="moe_combine",
    )(x2, route, z, z)


def _moe(x2, xn, route, route_t, counts, lw, tmg):
    n = x2.shape[0]
    assert n & (n - 1) == 0
    n_tiles = 2 * n // tmg + N_EXPERTS
    ext = (n_tiles + 3) * tmg
    e = route_t[0:2].astype(jnp.int32)
    rank = route_t[2:4].astype(jnp.int32)
    cnt = counts[0, :N_EXPERTS].astype(jnp.int32)
    tiles = (cnt + tmg - 1) // tmg
    tile_end = jnp.cumsum(tiles)
    row_start = (tile_end - tiles) * tmg
    start_of = jnp.sum(jnp.where(e[..., None] == jnp.arange(N_EXPERTS), row_start, 0), axis=-1)
    pos_flat = (start_of + rank).reshape(-1)
    steps = jnp.minimum(jnp.arange(n_tiles + 1, dtype=jnp.int32), tile_end[-1] - 1)
    tile_expert = jnp.minimum(jnp.sum((steps[:, None] >= tile_end[None, :]).astype(jnp.int32), axis=1),
                              N_EXPERTS - 1)
    idx = jnp.arange(ext, dtype=jnp.int32)
    dst = _route_inverse(pos_flat, 2 * n + (idx & (tmg - 1)), tmg)
    src = jnp.where(dst < 2 * n, dst, idx) & (n - 1)
    z = _moe_routed(xn, tile_expert, src, dst, lw, tmg, n_tiles)
    return _moe_combine(x2, route, z, tmg)


def _consts():
    i = jnp.arange(MXU)
    blockdiag = lambda w: ((i[:, None] // w) == (i[None, :] // w)).astype(BF16)
    tri = lambda n: (jnp.arange(n)[None, :] <= jnp.arange(n)[:, None]).astype(BF16)
    r = jnp.arange(LANE)
    place = jnp.stack([((r[:, None] < N_FOX) & (r[None, :] == AUG_W * r[:, None] + k)).astype(BF16)
                       for k in range(6)])
    within = r % AUG_W
    headed = r < AUG_W * N_FOX
    ones_q = (headed & (within >= 3) & (within < 6)).astype(F32)
    ones_k = (headed & (within < 3)).astype(F32)
    aug_ones = jnp.zeros((8, LANE), F32).at[0].set(ones_q).at[1].set(ones_k)
    return {"s64": blockdiag(FOX_DIM), "s128": blockdiag(HEAD_PAD), "tri256": tri(MXU),
            "tri128": tri(LANE), "place": place, "aug_ones": aug_ones}


def _rope_tables(pos):
    half = ROPE_DIM // 2
    inv = ROPE_BASE ** (-jnp.arange(half, dtype=F32) / half)
    ang = pos.astype(F32)[:, None] * inv[None, :]
    cos, sin = jnp.cos(ang), jnp.sin(ang)
    t = pos.shape[0]
    ct = jnp.concatenate([jnp.ones((t, ROPE_LO), F32), cos, cos, jnp.zeros((t, LANE - ROPE_HI), F32)], 1)
    st = jnp.concatenate([jnp.zeros((t, ROPE_LO), F32), -sin, sin, jnp.zeros((t, LANE - ROPE_HI), F32)], 1)
    return ct, st


def _pad_lanes(v, width=LANE):
    return jnp.pad(v, [(0, 0)] * (v.ndim - 1) + [(0, width - v.shape[-1])])


def _prep_layer(i, p):
    half = ROPE_DIM // 2
    w_in = p["w_in"][i]
    s = [0, Q_LORA, Q_LORA + KV_LORA, Q_LORA + KV_LORA + ROPE_DIM]
    nfx = N_FOX * FOX_DIM
    s += [s[3] + nfx, s[3] + 2 * nfx, s[3] + 3 * nfx, s[3] + 3 * nfx + N_FOX]
    c_q, c_kv, k_rope = w_in[:, s[0]:s[1]], w_in[:, s[1]:s[2]], w_in[:, s[2]:s[3]]
    fq, fk, fv, f_logit = w_in[:, s[3]:s[4]], w_in[:, s[4]:s[5]], w_in[:, s[5]:s[6]], w_in[:, s[6]:s[7]]
    zc = lambda n: jnp.zeros((D_MODEL, n), F32)
    misc = jnp.concatenate([f_logit, zc(ROPE_LO - N_FOX), k_rope, zc(LANE - ROPE_HI)], axis=1)
    w_in_p = jnp.concatenate([c_q, c_kv, fq, fk, fv, misc], axis=1).astype(BF16)

    w_uq = p["w_uq"][i].reshape(Q_LORA, N_MLA, QK_DIM)
    nope, rope = w_uq[..., :NOPE_DIM], w_uq[..., NOPE_DIM:]
    rope_sw = jnp.concatenate([rope[..., half:], rope[..., :half]], axis=-1)
    zq = lambda n: jnp.zeros((Q_LORA, N_MLA, n), F32)
    qa = jnp.concatenate([nope, rope, zq(HEAD_PAD - QK_DIM)], axis=-1)
    qb = jnp.concatenate([zq(NOPE_DIM), rope_sw, zq(HEAD_PAD - QK_DIM)], axis=-1)
    w_uq_p = jnp.concatenate([qa.reshape(Q_LORA, -1), qb.reshape(Q_LORA, -1)], axis=1).astype(BF16)

    w_ukv = p["w_ukv"][i].reshape(KV_LORA, N_MLA, NOPE_DIM + V_DIM)
    w_k = _pad_lanes(w_ukv[..., :NOPE_DIM], HEAD_PAD).reshape(KV_LORA, -1).astype(BF16)
    w_v = w_ukv[..., NOPE_DIM:].reshape(KV_LORA, -1).astype(BF16)

    gq = _pad_lanes(p["g_qn_mla"][i] * (QK_DIM ** -0.5))[None]
    gk = _pad_lanes(p["g_kn_mla"][i])[None]
    lw = {
        "g_mix": p["g_mix"][i][None], "w_in": w_in_p, "b_f": _pad_lanes(p["b_f"][i])[None],
        "g_cq": p["g_cq"][i][None], "g_ckv": p["g_ckv"][i][None], "w_uq": w_uq_p,
        "gq": gq, "gk": gk,
        "gfq": jnp.tile(p["g_qn_fox"][i] * (FOX_DIM ** -0.5), N_FOX)[None],
        "gfk": jnp.tile(p["g_kn_fox"][i], N_FOX)[None],
        "w_k": w_k, "w_v": w_v,
        "w_out": p["w_out"][i].astype(BF16), "g_ffn": p["g_ffn"][i][None],
    }
    j = i // 2
    lw["mixer_idx"] = j
    if i % 2 == 0:
        lw.update(w_gate=p["w_gate_b"], w_up=p["w_up_b"], w_down=p["w_down_b"])
    else:
        wr = p["w_router"][j]
        wr_hi = wr.astype(BF16)
        wr_lo = (wr - wr_hi.astype(F32)).astype(BF16)
        lw.update(w_router=_pad_lanes(jnp.concatenate([wr_hi, wr_lo], axis=1)),
                  b_router=_pad_lanes(p["b_router"][j])[None],
                  we_gate=p["we_gate_b"], we_up=p["we_up_b"], we_down=p["we_down_b"])
    return lw


def _channel_mixer(i, x, om, of, lw, consts, tm_post, tm_ffn):
    if i % 2 == 0:
        x2, xn = _post_attn(x, om, of, lw, False, tm_post)
        return _ffn(x2, xn, lw, tm_ffn)
    x2, xn, route, route_t, counts = _post_attn(x, om, of, lw, True, MXU, consts["tri256"])
    return _moe(x2, xn, route, route_t, counts, lw, MXU)


def _trunk_prompt(x, layers, consts):
    b, t, _ = x.shape
    rows = b * t
    x = x.reshape(rows, D_MODEL)
    ct, st = _rope_tables(jnp.arange(t, dtype=jnp.int32))
    state = ([], [], [], [], [])
    nfx = N_FOX * FOX_DIM
    for i, lw in enumerate(layers):
        ckv, kr, kr128, fk, fv, lf128, q, fq, fkb, fvb = _mixer_in(x, lw, ct, st, consts, 256)
        k, v = _kv_expand(ckv, kr128, lw, consts, 512)
        aug = _fox_bias(lf128.reshape(b, t, LANE), consts)
        om = _attn_prompt(q.reshape(b, t, -1), k.reshape(b, t, -1), v.reshape(b, t, -1), None, 256)
        of = _attn_prompt(fq.reshape(b, t, nfx), fkb.reshape(b, t, nfx), fvb.reshape(b, t, nfx), aug, 256)
        x = _channel_mixer(i, x, om.reshape(rows, -1), of.reshape(rows, -1), lw, consts, 256, 512)
        for lst, s in zip(state, (ckv.reshape(b, t, KV_LORA), kr.reshape(b, t, ROPE_DIM),
                                  fk.reshape(b, t, N_FOX, FOX_DIM), fv.reshape(b, t, N_FOX, FOX_DIM),
                                  lf128[:, :N_FOX].reshape(b, t, N_FOX))):
            lst.append(s)
    return x.reshape(b, t, D_MODEL), tuple(jnp.stack(s) for s in state)


def _trunk_sample(x, past, layers, consts):
    b, t, _ = x.shape
    rows = b * t
    start = PAST_LEN
    tk_valid = start + t
    tk = -(-tk_valid // LANE) * LANE
    pad = tk - tk_valid
    x = x.reshape(rows, D_MODEL)
    ct, st = _rope_tables(start + jnp.arange(t, dtype=jnp.int32))
    ct, st = jnp.tile(ct, (b, 1)), jnp.tile(st, (b, 1))
    p_ckv, p_krope, p_fk, p_fv, p_logf = past
    state = ([], [], [], [], [])
    nfx = N_FOX * FOX_DIM

    def cat(old, new):
        return jnp.concatenate([old, new, jnp.zeros((b, pad, new.shape[-1]), new.dtype)], axis=1)

    for i, lw in enumerate(layers):
        ckv, kr, kr128, fk, fv, lf128, q, fq, fkb, fvb = _mixer_in(x, lw, ct, st, consts, rows)
        ckv_all = cat(p_ckv[i], ckv.reshape(b, t, KV_LORA))
        old_kr128 = jnp.pad(p_krope[i], ((0, 0), (0, 0), (ROPE_LO, LANE - ROPE_HI)))
        kr_all = cat(old_kr128, kr128.reshape(b, t, LANE))
        k, v = _kv_expand(ckv_all.reshape(b * tk, KV_LORA), kr_all.reshape(b * tk, LANE), lw, consts, 512)
        lf_all = cat(_pad_lanes(p_logf[i]), lf128.reshape(b, t, LANE))
        aug = _fox_bias(lf_all, consts)
        fk_all = cat(p_fk[i].reshape(b, start, nfx).astype(BF16), fkb.reshape(b, t, nfx))
        fv_all = cat(p_fv[i].reshape(b, start, nfx).astype(BF16), fvb.reshape(b, t, nfx))
        om = _attn_sample(q.reshape(b, t, -1), k.reshape(b, tk, -1), v.reshape(b, tk, -1), None,
                          start, tk_valid)
        of = _attn_sample(fq.reshape(b, t, nfx), fk_all, fv_all, aug, start, tk_valid)
        x = _channel_mixer(i, x, om.reshape(rows, -1), of.reshape(rows, -1), lw, consts, rows, rows)
        for lst, s in zip(state, (ckv.reshape(b, t, KV_LORA), kr.reshape(b, t, ROPE_DIM),
                                  fk.reshape(b, t, N_FOX, FOX_DIM), fv.reshape(b, t, N_FOX, FOX_DIM),
                                  lf128[:, :N_FOX].reshape(b, t, N_FOX))):
            lst.append(s)
    return x.reshape(b, t, D_MODEL), tuple(jnp.stack(s) for s in state)


def kernel(x_prompt, x_sample, cache_mla_ckv, cache_mla_krope, cache_fox_k, cache_fox_v, cache_fox_logf, g_mix, w_in, b_f, g_cq, g_ckv, w_uq, w_ukv, g_qn_mla, g_kn_mla, g_qn_fox, g_kn_fox, w_out, g_ffn, w_gate, w_up, w_down, w_router, b_router, we_gate, we_up, we_down):
    p = dict(g_mix=g_mix, w_in=w_in, b_f=b_f, g_cq=g_cq, g_ckv=g_ckv, w_uq=w_uq, w_ukv=w_ukv,
             g_qn_mla=g_qn_mla, g_kn_mla=g_kn_mla, g_qn_fox=g_qn_fox, g_kn_fox=g_kn_fox,
             w_out=w_out, g_ffn=g_ffn, w_gate=w_gate, w_up=w_up, w_down=w_down,
             w_router=w_router, b_router=b_router, we_gate=we_gate, we_up=we_up, we_down=we_down)
    for name in ("w_gate", "w_up", "w_down", "we_gate", "we_up", "we_down"):
        p[name + "_b"] = p[name].astype(BF16)
    layers = [_prep_layer(i, p) for i in range(DEPTH)]
    consts = _consts()
    y_p, st_p = _trunk_prompt(x_prompt, layers, consts)
    past = (cache_mla_ckv, cache_mla_krope, cache_fox_k, cache_fox_v, cache_fox_logf)
    y_s, st_s = _trunk_sample(x_sample, past, layers, consts)
    return (y_p, y_s) + st_p + st_s
```

```python
import functools

import jax
import jax.numpy as jnp
from jax import lax
from jax.experimental import pallas as pl
from jax.experimental.pallas import tpu as pltpu

F32 = jnp.float32
BF16 = jnp.bfloat16

D_MODEL = 1024
DEPTH = 4
PAST_LEN = 2048
CHUNK_SHIFT = 6
EPS = 1e-6
ROPE_BASE = 10000.0

N_MLA = 8
Q_LORA = 384
KV_LORA = 256
NOPE_DIM = 64
ROPE_DIM = 32
QK_DIM = NOPE_DIM + ROPE_DIM
V_DIM = 64
N_FOX = 8
FOX_DIM = 64
D_FF = 2816
N_EXPERTS = 8
E_FF = 1792

LOG2E = 1.4426950408889634
LANE = 128
MXU = 256
HEAD_PAD = 128
ROPE_LO = NOPE_DIM
ROPE_HI = NOPE_DIM + ROPE_DIM
AUG_W = 16
GROUP = 4

CQ_LO, CQ_HI = 0, Q_LORA
CKV_LO, CKV_HI = CQ_HI, CQ_HI + KV_LORA
FQ_LO, FQ_HI = CKV_HI, CKV_HI + N_FOX * FOX_DIM
FK_LO, FK_HI = FQ_HI, FQ_HI + N_FOX * FOX_DIM
FV_LO, FV_HI = FK_HI, FK_HI + N_FOX * FOX_DIM
MISC_LO, MISC_HI = FV_HI, FV_HI + LANE
D_IN_P = MISC_HI

VMEM_LIMIT = 56 * 1024 * 1024


def _params(*sem):
    return pltpu.CompilerParams(dimension_semantics=sem, vmem_limit_bytes=VMEM_LIMIT)


def _dot(a, b):
    return jnp.dot(a, b, preferred_element_type=F32)


def _dot_nt(a, b):
    return lax.dot_general(a, b, (((1,), (1,)), ((), ())), preferred_element_type=F32)


def _rms(x, g, n):
    return x * lax.rsqrt(jnp.sum(x * x, axis=-1, keepdims=True) * (1.0 / n) + EPS) * g


def _head_norm(x, smat, dim):
    outs = []
    for c in range(0, x.shape[1], MXU):
        xc = x[:, c:c + MXU]
        ss = _dot((xc * xc).astype(BF16), smat)
        outs.append(xc * lax.rsqrt(ss * (1.0 / dim) + EPS))
    return jnp.concatenate(outs, axis=1)


def _full(shape):
    return pl.BlockSpec(shape, lambda *_: (0,) * len(shape))


N_MIXER_IN = 14
STACKED_OUT = (0, 1, 3, 4, 6)


def _mixer_in_body(*refs, stacked, first):
    (x_ref, gmix_ref, win_ref, bf_ref, gcq_ref, gckv_ref, wuq_ref, gq_ref,
     gfq_ref, gfk_ref, ct_ref, st_ref, s64_ref, s128_ref) = refs[:N_MIXER_IN]
    (ckv_ref, kr_ref, kr128_ref, fk_ref, fv_ref, lf128_ref, lf_ref,
     q_ref, fq_ref, fkb_ref, fvb_ref) = refs[len(refs) - 11:]
    tm = x_ref.shape[0]
    xb = _rms(x_ref[...], gmix_ref[...], D_MODEL).astype(BF16)

    def proj(lo, hi):
        return _dot(xb, win_ref[:, lo:hi])

    ct = ct_ref[...]
    st = st_ref[...]
    lane = lax.broadcasted_iota(jnp.int32, (tm, LANE), 1)

    a = proj(MISC_LO, MISC_HI)
    z = a + bf_ref[...]
    lf = jnp.minimum(z, 0.0) - jnp.log1p(jnp.exp(-jnp.abs(z)))
    lf128 = jnp.where(lane < N_FOX, lf, 0.0)
    lf128_ref[...] = lf128
    half = ROPE_DIM // 2
    sw = jnp.where(lane < ROPE_LO + half, pltpu.roll(a, LANE - half, 1), pltpu.roll(a, half, 1))
    kr128 = jnp.where((lane >= ROPE_LO) & (lane < ROPE_HI), a * ct + sw * st, 0.0)
    kr128_ref[...] = kr128
    ckv = _rms(proj(CKV_LO, CKV_HI), gckv_ref[...], KV_LORA)

    s64 = s64_ref[...]
    fk = _head_norm(proj(FK_LO, FK_HI), s64, FOX_DIM) * gfk_ref[...]
    fkb_ref[...] = fk.astype(BF16)
    fv = proj(FV_LO, FV_HI)
    fvb_ref[...] = fv.astype(BF16)
    fq_ref[...] = (_head_norm(proj(FQ_LO, FQ_HI), s64, FOX_DIM) * gfq_ref[...]).astype(BF16)

    if stacked:
        ckv_ref[0] = ckv
        kr_ref[0, 0] = kr128.T[ROPE_LO:ROPE_HI, :]
        lf_ref[0, 0] = lf128.T[0:N_FOX, :]
        fk_ref[0, 0] = fk.T
        fv_ref[0, 0] = fv.T
        if first:
            for ref in (ckv_ref, kr_ref, lf_ref, fk_ref, fv_ref):
                ref[1:] = jnp.zeros((ref.shape[0] - 1,) + ref.shape[1:], F32)
    else:
        ckv_ref[...] = ckv
        kr_ref[...] = kr128[:, ROPE_LO:ROPE_HI]
        lf_ref[...] = lf128[:, 0:N_FOX]
        fk_ref[...] = fk
        fv_ref[...] = fv

    cq = _rms(proj(CQ_LO, CQ_HI), gcq_ref[...], Q_LORA).astype(BF16)
    nq = N_MLA * HEAD_PAD
    qa = _dot(cq, wuq_ref[:, 0:nq])
    qb = _dot(cq, wuq_ref[:, nq:2 * nq])
    qr = jnp.concatenate(
        [qa[:, h * HEAD_PAD:(h + 1) * HEAD_PAD] * ct + qb[:, h * HEAD_PAD:(h + 1) * HEAD_PAD] * st
         for h in range(N_MLA)], axis=1)
    qn = _head_norm(qr, s128_ref[...], QK_DIM)
    gq = gq_ref[...]
    q_ref[...] = jnp.concatenate(
        [qn[:, h * HEAD_PAD:(h + 1) * HEAD_PAD] * gq for h in range(N_MLA)], axis=1).astype(BF16)


def _mixer_in(x, lw, ct, st, consts, tm, stack=None, layer=0, batch=1):
    rows = x.shape[0]
    nt = ct.shape[0] // tm
    grid = (rows // tm,)
    row = lambda w: pl.BlockSpec((tm, w), lambda i: (i, 0))
    tab = pl.BlockSpec((tm, LANE), lambda i: (i % nt, 0))
    nfx = N_FOX * FOX_DIM
    sds = jax.ShapeDtypeStruct
    stacked = stack is not None
    first = stacked and not stack
    if stacked:
        assert first == (layer == 0)
        t = rows // batch
        nd = DEPTH if first else 1
        tmaj = lambda w: pl.BlockSpec((nd, 1, w, tm), lambda i: (layer, i // nt, 0, i % nt))
        state = [(sds((DEPTH, rows, KV_LORA), F32), pl.BlockSpec((nd, tm, KV_LORA), lambda i: (layer, i, 0))),
                 (sds((DEPTH, batch, ROPE_DIM, t), F32), tmaj(ROPE_DIM)),
                 (sds((DEPTH, batch, nfx, t), F32), tmaj(nfx)),
                 (sds((DEPTH, batch, nfx, t), F32), tmaj(nfx)),
                 (sds((DEPTH, batch, N_FOX, t), F32), tmaj(N_FOX))]
    else:
        state = [(sds((rows, KV_LORA), F32), row(KV_LORA)), (sds((rows, ROPE_DIM), F32), row(ROPE_DIM)),
                 (sds((rows, nfx), F32), row(nfx)), (sds((rows, nfx), F32), row(nfx)),
                 (sds((rows, N_FOX), F32), row(N_FOX))]
    ckv_o, kr_o, fk_o, fv_o, lf_o = state
    outs = [ckv_o, kr_o, (sds((rows, LANE), F32), row(LANE)), fk_o, fv_o,
            (sds((rows, LANE), F32), row(LANE)), lf_o,
            (sds((rows, N_MLA * HEAD_PAD), BF16), row(N_MLA * HEAD_PAD)),
            (sds((rows, nfx), BF16), row(nfx)), (sds((rows, nfx), BF16), row(nfx)),
            (sds((rows, nfx), BF16), row(nfx))]
    in_specs = [row(D_MODEL), _full((1, D_MODEL)), _full((D_MODEL, D_IN_P)), _full((1, LANE)),
                _full((1, Q_LORA)), _full((1, KV_LORA)), _full((Q_LORA, 2 * N_MLA * HEAD_PAD)),
                _full((1, LANE)), _full((1, nfx)), _full((1, nfx)), tab, tab,
                _full((MXU, MXU)), _full((MXU, MXU))]
    args = [x, lw["g_mix"], lw["w_in"], lw["b_f"], lw["g_cq"], lw["g_ckv"], lw["w_uq"], lw["gq"],
            lw["gfq"], lw["gfk"], ct, st, consts["s64"], consts["s128"]]
    assert len(args) == N_MIXER_IN
    aliases = {}
    if stack:
        in_specs += [pl.BlockSpec(memory_space=pl.ANY)] * len(stack)
        args += list(stack)
        aliases = {N_MIXER_IN + n: o for n, o in enumerate(STACKED_OUT)}
    return pl.pallas_call(
        functools.partial(_mixer_in_body, stacked=stacked, first=first), grid=grid, in_specs=in_specs,
        out_specs=tuple(o[1] for o in outs), out_shape=tuple(o[0] for o in outs),
        input_output_aliases=aliases, compiler_params=_params("arbitrary"), name="mixer_in",
    )(*args)


def _kv_body(ckv_ref, kr128_ref, wk_ref, wv_ref, gk_ref, s128_ref, k_ref, v_ref):
    cb = (ckv_ref[0] if len(ckv_ref.shape) == 3 else ckv_ref[...]).astype(BF16)
    kn = _dot(cb, wk_ref[...])
    kr = kr128_ref[...]
    kk = jnp.concatenate(
        [kn[:, h * HEAD_PAD:(h + 1) * HEAD_PAD] + kr for h in range(N_MLA)], axis=1)
    kk = _head_norm(kk, s128_ref[...], QK_DIM)
    gk = gk_ref[...]
    k_ref[...] = jnp.concatenate(
        [kk[:, h * HEAD_PAD:(h + 1) * HEAD_PAD] * gk for h in range(N_MLA)], axis=1).astype(BF16)
    v_ref[...] = _dot(cb, wv_ref[...]).astype(BF16)


def _kv_expand(ckv, kr128, lw, consts, tm, layer=None):
    rows = kr128.shape[0]
    row = lambda w: pl.BlockSpec((tm, w), lambda i: (i, 0))
    ckv_spec = row(KV_LORA) if layer is None else pl.BlockSpec((1, tm, KV_LORA), lambda i: (layer, i, 0))
    nk = N_MLA * HEAD_PAD
    nv = N_MLA * V_DIM
    return pl.pallas_call(
        _kv_body, grid=(rows // tm,),
        in_specs=[ckv_spec, row(LANE), _full((KV_LORA, nk)), _full((KV_LORA, nv)),
                  _full((1, LANE)), _full((MXU, MXU))],
        out_specs=(row(nk), row(nv)),
        out_shape=(jax.ShapeDtypeStruct((rows, nk), BF16), jax.ShapeDtypeStruct((rows, nv), BF16)),
        compiler_params=_params("arbitrary"), name="kv_expand",
    )(ckv, kr128, lw["w_k"], lw["w_v"], lw["gk"], consts["s128"])


def _cum_body(lf_ref, tri_ref, place_ref, ones_ref, qa_ref, ka_ref, *, blk):
    t = lf_ref.shape[1]
    tri = tri_ref[...]
    carry = jnp.zeros((1, LANE), F32)

    def split3(c):
        hi = c.astype(BF16)
        r1 = c - hi.astype(F32)
        mid = r1.astype(BF16)
        lo = (r1 - mid.astype(F32)).astype(BF16)
        return hi, mid, lo

    ones_q = ones_ref[0:1, :]
    ones_k = ones_ref[1:2, :]
    for b0 in range(0, t, blk):
        hi, mid, lo = split3(lf_ref[0, b0:b0 + blk, :])
        c = _dot(tri, hi) + _dot(tri, mid) + _dot(tri, lo) + carry
        carry = c[blk - 1:blk, :]
        hi, mid, lo = split3(c * LOG2E)
        qa = _dot(hi, place_ref[0]) + _dot(mid, place_ref[1]) + _dot(lo, place_ref[2]) + ones_q
        ka = ones_k - (_dot(hi, place_ref[3]) + _dot(mid, place_ref[4]) + _dot(lo, place_ref[5]))
        qa_ref[0, b0:b0 + blk, :] = qa.astype(BF16)
        ka_ref[0, b0:b0 + blk, :] = ka.astype(BF16)


def _fox_bias(lf128, consts):
    b, t, _ = lf128.shape
    blk = MXU if t % MXU == 0 else LANE
    tri = consts["tri256"] if blk == MXU else consts["tri128"]
    blk3 = pl.BlockSpec((1, t, LANE), lambda i: (i, 0, 0))
    return pl.pallas_call(
        functools.partial(_cum_body, blk=blk), grid=(b,),
        in_specs=[blk3, _full((blk, blk)), _full((6, LANE, LANE)), _full((8, LANE))],
        out_specs=(blk3, blk3),
        out_shape=(jax.ShapeDtypeStruct((b, t, LANE), BF16),) * 2,
        compiler_params=_params("arbitrary"), name="fox_bias",
    )(lf128, tri, consts["place"], consts["aug_ones"])


def _attn_operands(refs, fox, pair, lane):
    if fox:
        q_ref, k_ref, _, qa_ref, ka_ref = refs

        def q_op(hh):
            qm = jnp.where((lane >= FOX_DIM * hh) & (lane < FOX_DIM * (hh + 1)),
                           q_ref[0].astype(F32), 0.0)
            qa = jnp.where((lane >> 4) == 2 * pair + hh, qa_ref[0].astype(F32), 0.0)
            return jnp.concatenate([qm, qa], axis=1).astype(BF16)

        def k_op(hh, rows):
            return jnp.concatenate([k_ref[0, rows, :], ka_ref[0, rows, :]], axis=1)
    else:
        q_ref, k_ref, _ = refs

        def q_op(hh):
            return q_ref[0, :, HEAD_PAD * hh:HEAD_PAD * (hh + 1)]

        def k_op(hh, rows):
            return k_ref[0, rows, HEAD_PAD * hh:HEAD_PAD * (hh + 1)]
    return q_op, k_op


def _attn_prompt_body(*refs, fox, tq):
    n_in = 5 if fox else 3
    ins, (o_ref, o_scr) = refs[:n_in], refs[n_in:]
    q_ref, k_ref, v_ref = ins[:3]
    h = pl.program_id(1)
    hh = h % 2
    t = k_ref.shape[1]
    lane = lax.broadcasted_iota(jnp.int32, (tq, LANE), 1)
    r = lax.broadcasted_iota(jnp.int32, (tq, tq), 0)
    c = lax.broadcasted_iota(jnp.int32, (tq, tq), 1)
    allowed = (c <= r) if fox else ((c >> CHUNK_SHIFT) <= (r >> CHUNK_SHIFT))
    def scores(i):
        lo, hi = i * tq, (i + 1) * tq
        if fox:
            qa_ref, ka_ref = ins[3:]
            qm = jnp.where((lane >> 6) == hh, q_ref[0, lo:hi, :].astype(F32), 0.0)
            qa = jnp.where((lane >> 4) == h, qa_ref[0, lo:hi, :].astype(F32), 0.0)
            q = jnp.concatenate([qm, qa], axis=1).astype(BF16)
            k = jnp.concatenate([k_ref[0, 0:hi, :], ka_ref[0, 0:hi, :]], axis=1)
        else:
            q = q_ref[0, lo:hi, :]
            k = k_ref[0, 0:hi, :]
        s = _dot_nt(q, k)
        sd = jnp.where(allowed, s[:, lo:hi], -jnp.inf)
        return sd if i == 0 else jnp.concatenate([s[:, 0:lo], sd], axis=1)

    def softmax(s):
        p = jnp.exp2(s - jnp.max(s, axis=1, keepdims=True))
        return p.astype(BF16), jnp.sum(p, axis=1, keepdims=True)

    def values(i, p, l):
        lo, hi = i * tq, (i + 1) * tq
        o_scr[lo:hi, :] = _dot(p, v_ref[0, 0:hi, :]) / l

    order = list(reversed(range(t // tq)))
    for g in range(0, len(order), GROUP):
        blocks = order[g:g + GROUP]
        ss = [scores(i) for i in blocks]
        pl_ = [softmax(s) for s in ss]
        for i, (p, l) in zip(blocks, pl_):
            values(i, p, l)

    @pl.when(hh == 0)
    def _():
        o_ref[0] = o_scr[...].astype(BF16)

    @pl.when(hh == 1)
    def _():
        lane_t = lax.broadcasted_iota(jnp.int32, (t, LANE), 1)
        o_ref[0] = jnp.where(lane_t < V_DIM, o_ref[0].astype(F32), o_scr[...]).astype(BF16)


def _attn_prompt(q, k, v, aug, tq):
    fox = aug is not None
    b, t, _ = q.shape
    blk = lambda f: pl.BlockSpec((1, t, LANE), f)
    per_head = lambda bi, h: (bi, 0, h)
    per_pair = lambda bi, h: (bi, 0, h // 2)
    shared = lambda bi, h: (bi, 0, 0)
    qk = per_pair if fox else per_head
    in_specs = [blk(qk), blk(qk), blk(per_pair)]
    args = [q, k, v]
    if fox:
        in_specs += [blk(shared), blk(shared)]
        args += list(aug)
    return pl.pallas_call(
        functools.partial(_attn_prompt_body, fox=fox, tq=tq),
        grid=(b, N_MLA), in_specs=in_specs, out_specs=blk(per_pair),
        out_shape=jax.ShapeDtypeStruct((b, t, N_MLA * V_DIM), BF16),
        scratch_shapes=[pltpu.VMEM((t, LANE), F32)],
        compiler_params=_params("arbitrary", "arbitrary"),
        name="attn_fox" if fox else "attn_mla",
    )(*args)


def _attn_sample_body(*refs, fox, start, tk_valid):
    n_in = 5 if fox else 3
    ins, o_ref = refs[:n_in], refs[n_in]
    v_ref = ins[2]
    tq = ins[0].shape[1]
    tk = ins[1].shape[1]
    pair = pl.program_id(1)
    lane = lax.broadcasted_iota(jnp.int32, (tq, LANE), 1)
    q_op, k_op = _attn_operands(ins, fox, pair, lane)
    qpos = start + lax.broadcasted_iota(jnp.int32, (tq, tk), 0)
    kpos = lax.broadcasted_iota(jnp.int32, (tq, tk), 1)
    causal = (kpos <= qpos) if fox else ((kpos >> CHUNK_SHIFT) <= (qpos >> CHUNK_SHIFT))
    allowed = causal & (kpos < tk_valid)
    vv = v_ref[0]
    outs = []
    for hh in range(2):
        s = jnp.where(allowed, _dot_nt(q_op(hh), k_op(hh, slice(None))), -jnp.inf)
        p = jnp.exp2(s - jnp.max(s, axis=1, keepdims=True))
        outs.append(_dot(p.astype(BF16), vv) / jnp.sum(p, axis=1, keepdims=True))
    o_ref[0] = jnp.where(lane < V_DIM, outs[0], outs[1]).astype(BF16)


def _attn_sample(q, k, v, aug, start, tk_valid):
    fox = aug is not None
    b, tq, _ = q.shape
    tk = k.shape[1]
    qw = LANE if fox else 2 * HEAD_PAD
    in_specs = [pl.BlockSpec((1, tq, qw), lambda bi, p: (bi, 0, p)),
                pl.BlockSpec((1, tk, qw), lambda bi, p: (bi, 0, p)),
                pl.BlockSpec((1, tk, LANE), lambda bi, p: (bi, 0, p))]
    args = [q, k, v]
    if fox:
        qa, ka = aug
        in_specs += [pl.BlockSpec((1, tq, LANE), lambda bi, p: (bi, start // tq, 0)),
                     pl.BlockSpec((1, tk, LANE), lambda bi, p: (bi, 0, 0))]
        args += [qa, ka]
    return pl.pallas_call(
        functools.partial(_attn_sample_body, fox=fox, start=start, tk_valid=tk_valid),
        grid=(b, N_MLA // 2), in_specs=in_specs,
        out_specs=pl.BlockSpec((1, tq, LANE), lambda bi, p: (bi, 0, p)),
        out_shape=jax.ShapeDtypeStruct((b, tq, N_MLA * V_DIM), BF16),
        compiler_params=_params("arbitrary", "arbitrary"),
        name="attn_fox_step" if fox else "attn_mla_step",
    )(*args)


def _post_attn_body(*refs, moe):
    if moe:
        (x_ref, om_ref, of_ref, wo_ref, gffn_ref, wr_ref, br_ref, tri_ref,
         x2_ref, xn_ref, route_ref, routet_ref, cnt_ref, carry_ref) = refs
    else:
        x_ref, om_ref, of_ref, wo_ref, gffn_ref, x2_ref, xn_ref = refs
    nm = N_MLA * V_DIM
    x2 = x_ref[...] + _dot(om_ref[...], wo_ref[0:nm, :]) + _dot(of_ref[...], wo_ref[nm:, :])
    x2_ref[...] = x2
    xn = _rms(x2, gffn_ref[...], D_MODEL)
    xh = xn.astype(BF16)
    xn_ref[...] = xn if moe else xh
    if moe:
        xl = (xn - xh.astype(F32)).astype(BF16)
        r = _dot(xh, wr_ref[...]) + _dot(xl, wr_ref[...])
        logits = r + pltpu.roll(r, LANE - N_EXPERTS, 1) + br_ref[...]
        lane = lax.broadcasted_iota(jnp.int32, logits.shape, 1).astype(F32)
        valid = lane < N_EXPERTS
        lg = jnp.where(valid, logits, -jnp.inf)
        e = jnp.exp(lg - jnp.max(lg, axis=1, keepdims=True))
        probs = e / jnp.sum(e, axis=1, keepdims=True)
        p1 = jnp.where(valid, probs, -1.0)
        m1 = jnp.max(p1, axis=1, keepdims=True)
        i1 = jnp.min(jnp.where(p1 == m1, lane, float(LANE)), axis=1, keepdims=True)
        p2 = jnp.where(lane == i1, -1.0, p1)
        m2 = jnp.max(p2, axis=1, keepdims=True)
        i2 = jnp.min(jnp.where(p2 == m2, lane, float(LANE)), axis=1, keepdims=True)
        den = m1 + m2
        @pl.when(pl.program_id(0) == 0)
        def _():
            carry_ref[...] = jnp.zeros(carry_ref.shape, F32)

        sel = jnp.where((lane == i1) | (lane == i2), 1.0, 0.0)
        incl = _dot(tri_ref[...], sel.astype(BF16)) + carry_ref[0:1, :]
        excl = incl - sel
        rank1 = jnp.sum(jnp.where(lane == i1, excl, 0.0), axis=1, keepdims=True)
        rank2 = jnp.sum(jnp.where(lane == i2, excl, 0.0), axis=1, keepdims=True)
        total = incl[incl.shape[0] - 1:, :]
        carry_ref[...] = jnp.broadcast_to(total, carry_ref.shape)
        cnt_ref[...] = jnp.broadcast_to(total, cnt_ref.shape)
        cols = (i1, i2, rank1, rank2, m1 / den, m2 / den)
        route = jnp.zeros(logits.shape, F32)
        for n, col in enumerate(cols):
            route = jnp.where(lane == n, col, route)
        route_ref[...] = route
        routet_ref[...] = route.T[0:8, :]


def _post_attn(x, om, of, lw, moe, tm, tri=None):
    rows = x.shape[0]
    row = lambda w: pl.BlockSpec((tm, w), lambda i: (i, 0))
    nm = N_MLA * V_DIM
    in_specs = [row(D_MODEL), row(nm), row(nm), _full((2 * nm, D_MODEL)), _full((1, D_MODEL))]
    args = [x, om, of, lw["w_out"], lw["g_ffn"]]
    out_specs = [row(D_MODEL), row(D_MODEL)]
    out_shape = [jax.ShapeDtypeStruct((rows, D_MODEL), F32),
                 jax.ShapeDtypeStruct((rows, D_MODEL), F32 if moe else BF16)]
    scratch = []
    if moe:
        assert tm == MXU
        in_specs += [_full((D_MODEL, LANE)), _full((1, LANE)), _full((MXU, MXU))]
        args += [lw["w_router"], lw["b_router"], tri]
        out_specs += [row(LANE), pl.BlockSpec((8, tm), lambda i: (0, i)), _full((8, LANE))]
        out_shape += [jax.ShapeDtypeStruct((rows, LANE), F32), jax.ShapeDtypeStruct((8, rows), F32),
                      jax.ShapeDtypeStruct((8, LANE), F32)]
        scratch = [pltpu.VMEM((8, LANE), F32)]
    return pl.pallas_call(
        functools.partial(_post_attn_body, moe=moe), grid=(rows // tm,),
        in_specs=in_specs, out_specs=tuple(out_specs), out_shape=tuple(out_shape),
        scratch_shapes=scratch,
        compiler_params=_params("arbitrary"), name="post_attn_moe" if moe else "post_attn",
    )(*args)


def _swiglu_acc(xb, wg_ref, wu_ref, wd_ref, d_ff, before_chunk=None):
    acc = None
    for c in range(0, d_ff, MXU):
        if before_chunk is not None:
            before_chunk(c // MXU, d_ff // MXU)
        g = _dot(xb, wg_ref[:, c:c + MXU])
        u = _dot(xb, wu_ref[:, c:c + MXU])
        h = (g * jax.nn.sigmoid(g) * u).astype(BF16)
        d = _dot(h, wd_ref[c:c + MXU, :])
        acc = d if acc is None else acc + d
    return acc


def _ffn_body(x2_ref, xn_ref, wg_ref, wu_ref, wd_ref, o_ref):
    o_ref[...] = x2_ref[...] + _swiglu_acc(xn_ref[...], wg_ref.at[0], wu_ref.at[0], wd_ref.at[0], D_FF)


def _ffn(x2, xn, lw, tm):
    rows = x2.shape[0]
    row = pl.BlockSpec((tm, D_MODEL), lambda i: (i, 0))
    jl = lw["mixer_idx"]
    wspec = lambda a, b: pl.BlockSpec((1, a, b), lambda i: (jl, 0, 0))
    return pl.pallas_call(
        _ffn_body, grid=(rows // tm,),
        in_specs=[row, row, wspec(D_MODEL, D_FF), wspec(D_MODEL, D_FF), wspec(D_FF, D_MODEL)],
        out_specs=row, out_shape=jax.ShapeDtypeStruct((rows, D_MODEL), F32),
        compiler_params=_params("arbitrary"), name="ffn_dense",
    )(x2, xn, lw["w_gate"], lw["w_up"], lw["w_down"])


SCATTER_UNROLL = 8


def _inv_body(pos_ref, base_hbm, inv_ref, sem, *, tmg):
    fill = pltpu.make_async_copy(base_hbm, inv_ref, sem.at[0])
    fill.start()
    fill.wait()

    def put(i, c):
        for u in range(SCATTER_UNROLL):
            a = i * SCATTER_UNROLL + u
            inv_ref[tmg + pos_ref[a]] = a
        return c

    lax.fori_loop(0, pos_ref.shape[0] // SCATTER_UNROLL, put, 0)


def _route_inverse(pos_flat, base, tmg):
    return pl.pallas_call(
        functools.partial(_inv_body, tmg=tmg),
        in_specs=[pl.BlockSpec(memory_space=pltpu.SMEM), pl.BlockSpec(memory_space=pl.ANY)],
        out_specs=pl.BlockSpec(memory_space=pltpu.SMEM),
        out_shape=jax.ShapeDtypeStruct(base.shape, jnp.int32),
        scratch_shapes=[pltpu.SemaphoreType.DMA((1,))], name="route_inverse",
    )(pos_flat, base)


def _moe_routed_body(te_ref, src_ref, dst_ref, xn_hbm, wg_ref, wu_ref, wd_ref, z_hbm,
                     gbuf, obuf, gsem, ssem, *, tmg, n_tiles):
    del te_ref
    j = pl.program_id(0)

    def row_in(entry, slot, r):
        return pltpu.make_async_copy(xn_hbm.at[pl.ds(src_ref[entry], 1), :],
                                     gbuf.at[slot, pl.ds(r, 1), :], gsem.at[slot])

    def row_out(entry, slot, r):
        return pltpu.make_async_copy(obuf.at[slot, pl.ds(r, 1), :],
                                     z_hbm.at[pl.ds(dst_ref[entry], 1), :], ssem.at[0])

    def tile_in(slot):
        return pltpu.make_async_copy(xn_hbm.at[pl.ds(0, tmg), :], gbuf.at[slot], gsem.at[slot])

    def tile_out(slot):
        return pltpu.make_async_copy(obuf.at[slot], z_hbm.at[pl.ds(0, tmg), :], ssem.at[0])

    @pl.when(j == 0)
    def _():
        obuf[1] = jnp.zeros(obuf.shape[1:], F32)
        for r in range(tmg):
            row_in(tmg + r, 0, r).start()

    def step(cur):
        nxt = 1 - cur

        @pl.when(j >= 1)
        def _():
            tile_out(cur).wait()

        tile_in(cur).wait()
        xb = gbuf[cur].astype(BF16)

        def issue(k, n_chunks):
            for r in range(k * tmg // n_chunks, (k + 1) * tmg // n_chunks):
                row_in((j + 2) * tmg + r, nxt, r).start()
                row_out(j * tmg + r, nxt, r).start()

        obuf[cur] = _swiglu_acc(xb, wg_ref.at[0, 0], wu_ref.at[0, 0], wd_ref.at[0, 0], E_FF, issue)

    for parity in range(2):
        pl.when(j % 2 == parity)(functools.partial(step, parity))

    assert n_tiles % 2 == 0

    @pl.when(j == n_tiles)
    def _():
        tile_out(1).wait()
        tile_in(1).wait()


def _moe_routed(xn, tile_expert, src, dst, lw, tmg, n_tiles):
    n = xn.shape[0]
    jl = lw["mixer_idx"]
    wspec = lambda a, b: pl.BlockSpec((1, 1, a, b), lambda j, te, s, d: (jl, te[j], 0, 0))
    grid_spec = pltpu.PrefetchScalarGridSpec(
        num_scalar_prefetch=3, grid=(n_tiles + 1,),
        in_specs=[pl.BlockSpec(memory_space=pl.ANY), wspec(D_MODEL, E_FF), wspec(D_MODEL, E_FF),
                  wspec(E_FF, D_MODEL)],
        out_specs=pl.BlockSpec(memory_space=pl.ANY),
        scratch_shapes=[pltpu.VMEM((2, tmg, D_MODEL), F32), pltpu.VMEM((2, tmg, D_MODEL), F32),
                        pltpu.SemaphoreType.DMA((2,)), pltpu.SemaphoreType.DMA((1,))])
    return pl.pallas_call(
        functools.partial(_moe_routed_body, tmg=tmg, n_tiles=n_tiles),
        grid_spec=grid_spec, out_shape=jax.ShapeDtypeStruct((2 * n + tmg, D_MODEL), F32),
        compiler_params=_params("arbitrary"), name="moe_routed",
    )(tile_expert, src, dst, xn, lw["we_gate"], lw["we_up"], lw["we_down"])


def _combine_body(x2_ref, route_ref, z0_ref, z1_ref, o_ref):
    route = route_ref[...]
    lane = lax.broadcasted_iota(jnp.int32, route.shape, 1)
    g1 = jnp.sum(jnp.where(lane == 4, route, 0.0), axis=1, keepdims=True)
    g2 = jnp.sum(jnp.where(lane == 5, route, 0.0), axis=1, keepdims=True)
    o_ref[...] = x2_ref[...] + g1 * z0_ref[...] + g2 * z1_ref[...]


def _moe_combine(x2, route, z, tm):
    n = x2.shape[0]
    row = lambda w: pl.BlockSpec((tm, w), lambda i: (i, 0))
    return pl.pallas_call(
        _combine_body, grid=(n // tm,),
        in_specs=[row(D_MODEL), row(LANE), row(D_MODEL),
                  pl.BlockSpec((tm, D_MODEL), lambda i: (i + n // tm, 0))],
        out_specs=row(D_MODEL), out_shape=jax.ShapeDtypeStruct((n, D_MODEL), F32),
        compiler_params=_params("arbitrary"), name="moe_combine",
    )(x2, route, z, z)


def _moe(x2, xn, route, route_t, counts, lw, tmg):
    n = x2.shape[0]
    assert n & (n - 1) == 0
    n_tiles = 2 * n // tmg + N_EXPERTS
    ext = (n_tiles + 3) * tmg
    e = route_t[0:2].astype(jnp.int32)
    rank = route_t[2:4].astype(jnp.int32)
    cnt = counts[0, :N_EXPERTS].astype(jnp.int32)
    tiles = (cnt + tmg - 1) // tmg
    tile_end = jnp.cumsum(tiles)
    row_start = (tile_end - tiles) * tmg
    start_of = jnp.sum(jnp.where(e[..., None] == jnp.arange(N_EXPERTS), row_start, 0), axis=-1)
    pos_flat = (start_of + rank).reshape(-1)
    steps = jnp.minimum(jnp.arange(n_tiles + 1, dtype=jnp.int32), tile_end[-1] - 1)
    tile_expert = jnp.minimum(jnp.sum((steps[:, None] >= tile_end[None, :]).astype(jnp.int32), axis=1),
                              N_EXPERTS - 1)
    idx = jnp.arange(ext, dtype=jnp.int32)
    dst = _route_inverse(pos_flat, 2 * n + (idx & (tmg - 1)), tmg)
    src = jnp.where(dst < 2 * n, dst, idx) & (n - 1)
    z = _moe_routed(xn, tile_expert, src, dst, lw, tmg, n_tiles)
    return _moe_combine(x2, route, z, tmg)


def _consts():
    i = jnp.arange(MXU)
    blockdiag = lambda w: ((i[:, None] // w) == (i[None, :] // w)).astype(BF16)
    tri = lambda n: (jnp.arange(n)[None, :] <= jnp.arange(n)[:, None]).astype(BF16)
    r = jnp.arange(LANE)
    place = jnp.stack([((r[:, None] < N_FOX) & (r[None, :] == AUG_W * r[:, None] + k)).astype(BF16)
                       for k in range(6)])
    within = r % AUG_W
    headed = r < AUG_W * N_FOX
    ones_q = (headed & (within >= 3) & (within < 6)).astype(F32)
    ones_k = (headed & (within < 3)).astype(F32)
    aug_ones = jnp.zeros((8, LANE), F32).at[0].set(ones_q).at[1].set(ones_k)
    return {"s64": blockdiag(FOX_DIM), "s128": blockdiag(HEAD_PAD), "tri256": tri(MXU),
            "tri128": tri(LANE), "place": place, "aug_ones": aug_ones}


def _rope_tables(pos):
    half = ROPE_DIM // 2
    inv = ROPE_BASE ** (-jnp.arange(half, dtype=F32) / half)
    ang = pos.astype(F32)[:, None] * inv[None, :]
    cos, sin = jnp.cos(ang), jnp.sin(ang)
    t = pos.shape[0]
    ct = jnp.concatenate([jnp.ones((t, ROPE_LO), F32), cos, cos, jnp.zeros((t, LANE - ROPE_HI), F32)], 1)
    st = jnp.concatenate([jnp.zeros((t, ROPE_LO), F32), -sin, sin, jnp.zeros((t, LANE - ROPE_HI), F32)], 1)
    return ct, st


def _pad_lanes(v, width=LANE):
    return jnp.pad(v, [(0, 0)] * (v.ndim - 1) + [(0, width - v.shape[-1])])


def _prep_layer(i, p):
    half = ROPE_DIM // 2
    w_in = p["w_in"][i]
    s = [0, Q_LORA, Q_LORA + KV_LORA, Q_LORA + KV_LORA + ROPE_DIM]
    nfx = N_FOX * FOX_DIM
    s += [s[3] + nfx, s[3] + 2 * nfx, s[3] + 3 * nfx, s[3] + 3 * nfx + N_FOX]
    c_q, c_kv, k_rope = w_in[:, s[0]:s[1]], w_in[:, s[1]:s[2]], w_in[:, s[2]:s[3]]
    fq, fk, fv, f_logit = w_in[:, s[3]:s[4]], w_in[:, s[4]:s[5]], w_in[:, s[5]:s[6]], w_in[:, s[6]:s[7]]
    zc = lambda n: jnp.zeros((D_MODEL, n), F32)
    misc = jnp.concatenate([f_logit, zc(ROPE_LO - N_FOX), k_rope, zc(LANE - ROPE_HI)], axis=1)
    w_in_p = jnp.concatenate([c_q, c_kv, fq, fk, fv, misc], axis=1).astype(BF16)

    w_uq = p["w_uq"][i].reshape(Q_LORA, N_MLA, QK_DIM)
    nope, rope = w_uq[..., :NOPE_DIM], w_uq[..., NOPE_DIM:]
    rope_sw = jnp.concatenate([rope[..., half:], rope[..., :half]], axis=-1)
    zq = lambda n: jnp.zeros((Q_LORA, N_MLA, n), F32)
    qa = jnp.concatenate([nope, rope, zq(HEAD_PAD - QK_DIM)], axis=-1)
    qb = jnp.concatenate([zq(NOPE_DIM), rope_sw, zq(HEAD_PAD - QK_DIM)], axis=-1)
    w_uq_p = jnp.concatenate([qa.reshape(Q_LORA, -1), qb.reshape(Q_LORA, -1)], axis=1).astype(BF16)

    w_ukv = p["w_ukv"][i].reshape(KV_LORA, N_MLA, NOPE_DIM + V_DIM)
    w_k = _pad_lanes(w_ukv[..., :NOPE_DIM], HEAD_PAD).reshape(KV_LORA, -1).astype(BF16)
    w_v = w_ukv[..., NOPE_DIM:].reshape(KV_LORA, -1).astype(BF16)

    gq = _pad_lanes(p["g_qn_mla"][i] * (QK_DIM ** -0.5 * LOG2E))[None]
    gk = _pad_lanes(p["g_kn_mla"][i])[None]
    lw = {
        "g_mix": p["g_mix"][i][None], "w_in": w_in_p, "b_f": _pad_lanes(p["b_f"][i])[None],
        "g_cq": p["g_cq"][i][None], "g_ckv": p["g_ckv"][i][None], "w_uq": w_uq_p,
        "gq": gq, "gk": gk,
        "gfq": jnp.tile(p["g_qn_fox"][i] * (FOX_DIM ** -0.5 * LOG2E), N_FOX)[None],
        "gfk": jnp.tile(p["g_kn_fox"][i], N_FOX)[None],
        "w_k": w_k, "w_v": w_v,
        "w_out": p["w_out"][i].astype(BF16), "g_ffn": p["g_ffn"][i][None],
    }
    j = i // 2
    lw["mixer_idx"] = j
    if i % 2 == 0:
        lw.update(w_gate=p["w_gate_b"], w_up=p["w_up_b"], w_down=p["w_down_b"])
    else:
        wr = p["w_router"][j]
        wr_hi = wr.astype(BF16)
        wr_lo = (wr - wr_hi.astype(F32)).astype(BF16)
        lw.update(w_router=_pad_lanes(jnp.concatenate([wr_hi, wr_lo], axis=1)),
                  b_router=_pad_lanes(p["b_router"][j])[None],
                  we_gate=p["we_gate_b"], we_up=p["we_up_b"], we_down=p["we_down_b"])
    return lw


def _channel_mixer(i, x, om, of, lw, consts, tm_post, tm_ffn):
    if i % 2 == 0:
        x2, xn = _post_attn(x, om, of, lw, False, tm_post)
        return _ffn(x2, xn, lw, tm_ffn)
    x2, xn, route, route_t, counts = _post_attn(x, om, of, lw, True, MXU, consts["tri256"])
    return _moe(x2, xn, route, route_t, counts, lw, MXU)


def _trunk_prompt(x, layers, consts):
    b, t, _ = x.shape
    rows = b * t
    x = x.reshape(rows, D_MODEL)
    ct, st = _rope_tables(jnp.arange(t, dtype=jnp.int32))
    nfx = N_FOX * FOX_DIM
    stack = ()
    for i, lw in enumerate(layers):
        (ckv_s, krt_s, kr128, fkt_s, fvt_s, lf128, lft_s, q, fq, fkb, fvb) = _mixer_in(
            x, lw, ct, st, consts, 256, stack=stack, layer=i, batch=b)
        stack = (ckv_s, krt_s, fkt_s, fvt_s, lft_s)
        k, v = _kv_expand(ckv_s, kr128, lw, consts, 512, layer=i)
        aug = _fox_bias(lf128.reshape(b, t, LANE), consts)
        om = _attn_prompt(q.reshape(b, t, -1), k.reshape(b, t, -1), v.reshape(b, t, -1), None, 256)
        of = _attn_prompt(fq.reshape(b, t, nfx), fkb.reshape(b, t, nfx), fvb.reshape(b, t, nfx), aug, 256)
        x = _channel_mixer(i, x, om.reshape(rows, -1), of.reshape(rows, -1), lw, consts, 256, 512)
    ckv_s, krt_s, fkt_s, fvt_s, lft_s = stack
    heads = lambda a: jnp.transpose(a.reshape(DEPTH, b, N_FOX, FOX_DIM, t), (0, 1, 4, 2, 3))
    state = (ckv_s.reshape(DEPTH, b, t, KV_LORA), jnp.swapaxes(krt_s, 2, 3), heads(fkt_s), heads(fvt_s),
             jnp.swapaxes(lft_s, 2, 3))
    return x.reshape(b, t, D_MODEL), state


def _trunk_sample(x, past, layers, consts):
    b, t, _ = x.shape
    rows = b * t
    start = PAST_LEN
    tk_valid = start + t
    tk = -(-tk_valid // LANE) * LANE
    pad = tk - tk_valid
    x = x.reshape(rows, D_MODEL)
    ct, st = _rope_tables(start + jnp.arange(t, dtype=jnp.int32))
    ct, st = jnp.tile(ct, (b, 1)), jnp.tile(st, (b, 1))
    p_ckv, p_krope, p_fk, p_fv, p_logf = past
    state = ([], [], [], [], [])
    nfx = N_FOX * FOX_DIM

    def cat(old, new):
        return jnp.concatenate([old, new, jnp.zeros((b, pad, new.shape[-1]), new.dtype)], axis=1)

    for i, lw in enumerate(layers):
        ckv, kr, kr128, fk, fv, lf128, lf, q, fq, fkb, fvb = _mixer_in(x, lw, ct, st, consts, rows)
        ckv_all = cat(p_ckv[i], ckv.reshape(b, t, KV_LORA))
        old_kr128 = jnp.pad(p_krope[i], ((0, 0), (0, 0), (ROPE_LO, LANE - ROPE_HI)))
        kr_all = cat(old_kr128, kr128.reshape(b, t, LANE))
        k, v = _kv_expand(ckv_all.reshape(b * tk, KV_LORA), kr_all.reshape(b * tk, LANE), lw, consts, 512)
        lf_all = cat(_pad_lanes(p_logf[i]), lf128.reshape(b, t, LANE))
        aug = _fox_bias(lf_all, consts)
        fk_all = cat(p_fk[i].reshape(b, start, nfx).astype(BF16), fkb.reshape(b, t, nfx))
        fv_all = cat(p_fv[i].reshape(b, start, nfx).astype(BF16), fvb.reshape(b, t, nfx))
        om = _attn_sample(q.reshape(b, t, -1), k.reshape(b, tk, -1), v.reshape(b, tk, -1), None,
                          start, tk_valid)
        of = _attn_sample(fq.reshape(b, t, nfx), fk_all, fv_all, aug, start, tk_valid)
        x = _channel_mixer(i, x, om.reshape(rows, -1), of.reshape(rows, -1), lw, consts, rows, rows)
        for lst, s in zip(state, (ckv.reshape(b, t, KV_LORA), kr.reshape(b, t, ROPE_DIM),
                                  fk.reshape(b, t, N_FOX, FOX_DIM), fv.reshape(b, t, N_FOX, FOX_DIM),
                                  lf.reshape(b, t, N_FOX))):
            lst.append(s)
    return x.reshape(b, t, D_MODEL), tuple(jnp.stack(s) for s in state)


def kernel(x_prompt, x_sample, cache_mla_ckv, cache_mla_krope, cache_fox_k, cache_fox_v, cache_fox_logf, g_mix, w_in, b_f, g_cq, g_ckv, w_uq, w_ukv, g_qn_mla, g_kn_mla, g_qn_fox, g_kn_fox, w_out, g_ffn, w_gate, w_up, w_down, w_router, b_router, we_gate, we_up, we_down):
    p = dict(g_mix=g_mix, w_in=w_in, b_f=b_f, g_cq=g_cq, g_ckv=g_ckv, w_uq=w_uq, w_ukv=w_ukv,
             g_qn_mla=g_qn_mla, g_kn_mla=g_kn_mla, g_qn_fox=g_qn_fox, g_kn_fox=g_kn_fox,
             w_out=w_out, g_ffn=g_ffn, w_gate=w_gate, w_up=w_up, w_down=w_down,
             w_router=w_router, b_router=b_router, we_gate=we_gate, we_up=we_up, we_down=we_down)
    for name in ("w_gate", "w_up", "w_down", "we_gate", "we_up", "we_down"):
        p[name + "_b"] = p[name].astype(BF16)
    layers = [_prep_layer(i, p) for i in range(DEPTH)]
    consts = _consts()
    y_p, st_p = _trunk_prompt(x_prompt, layers, consts)
    past = (cache_mla_ckv, cache_mla_krope, cache_fox_k, cache_fox_v, cache_fox_logf)
    y_s, st_s = _trunk_sample(x_sample, past, layers, consts)
    return (y_p, y_s) + st_p + st_s
```

```python
import functools

import jax
import jax.numpy as jnp
from jax import lax
from jax.experimental import pallas as pl
from jax.experimental.pallas import tpu as pltpu

F32 = jnp.float32
BF16 = jnp.bfloat16

D_MODEL = 1024
DEPTH = 4
PAST_LEN = 2048
CHUNK_SHIFT = 6
EPS = 1e-6
ROPE_BASE = 10000.0

N_MLA = 8
Q_LORA = 384
KV_LORA = 256
NOPE_DIM = 64
ROPE_DIM = 32
QK_DIM = NOPE_DIM + ROPE_DIM
V_DIM = 64
N_FOX = 8
FOX_DIM = 64
D_FF = 2816
N_EXPERTS = 8
E_FF = 1792

LOG2E = 1.4426950408889634
LANE = 128
MXU = 256
HEAD_PAD = 128
ROPE_LO = NOPE_DIM
ROPE_HI = NOPE_DIM + ROPE_DIM
AUG_W = 16
GROUP = 4

CQ_LO, CQ_HI = 0, Q_LORA
CKV_LO, CKV_HI = CQ_HI, CQ_HI + KV_LORA
FQ_LO, FQ_HI = CKV_HI, CKV_HI + N_FOX * FOX_DIM
FK_LO, FK_HI = FQ_HI, FQ_HI + N_FOX * FOX_DIM
FV_LO, FV_HI = FK_HI, FK_HI + N_FOX * FOX_DIM
MISC_LO, MISC_HI = FV_HI, FV_HI + LANE
D_IN_P = MISC_HI

VMEM_LIMIT = 56 * 1024 * 1024


def _params(*sem):
    return pltpu.CompilerParams(dimension_semantics=sem, vmem_limit_bytes=VMEM_LIMIT)


def _dot(a, b):
    return jnp.dot(a, b, preferred_element_type=F32)


def _dot_nt(a, b):
    return lax.dot_general(a, b, (((1,), (1,)), ((), ())), preferred_element_type=F32)


def _rms(x, g, n):
    return x * lax.rsqrt(jnp.sum(x * x, axis=-1, keepdims=True) * (1.0 / n) + EPS) * g


def _head_norm(x, smat, dim):
    outs = []
    for c in range(0, x.shape[1], MXU):
        xc = x[:, c:c + MXU]
        ss = _dot((xc * xc).astype(BF16), smat)
        outs.append(xc * lax.rsqrt(ss * (1.0 / dim) + EPS))
    return jnp.concatenate(outs, axis=1)


def _full(shape):
    return pl.BlockSpec(shape, lambda *_: (0,) * len(shape))


N_MIXER_IN = 14
STACKED_OUT = (0, 1, 3, 4, 6)


def _mixer_in_body(*refs, stacked, first):
    (x_ref, gmix_ref, win_ref, bf_ref, gcq_ref, gckv_ref, wuq_ref, gq_ref,
     gfq_ref, gfk_ref, ct_ref, st_ref, s64_ref, s128_ref) = refs[:N_MIXER_IN]
    (ckv_ref, kr_ref, kr128_ref, fk_ref, fv_ref, lf128_ref, lf_ref,
     q_ref, fq_ref, fkb_ref, fvb_ref) = refs[len(refs) - 11:]
    tm = x_ref.shape[0]
    xb = _rms(x_ref[...], gmix_ref[...], D_MODEL).astype(BF16)

    def proj(lo, hi):
        return _dot(xb, win_ref[:, lo:hi])

    ct = ct_ref[...]
    st = st_ref[...]
    lane = lax.broadcasted_iota(jnp.int32, (tm, LANE), 1)

    a = proj(MISC_LO, MISC_HI)
    z = a + bf_ref[...]
    lf = jnp.minimum(z, 0.0) - jnp.log1p(jnp.exp(-jnp.abs(z)))
    lf128 = jnp.where(lane < N_FOX, lf, 0.0)
    lf128_ref[...] = lf128
    half = ROPE_DIM // 2
    sw = jnp.where(lane < ROPE_LO + half, pltpu.roll(a, LANE - half, 1), pltpu.roll(a, half, 1))
    kr128 = jnp.where((lane >= ROPE_LO) & (lane < ROPE_HI), a * ct + sw * st, 0.0)
    kr128_ref[...] = kr128
    ckv = _rms(proj(CKV_LO, CKV_HI), gckv_ref[...], KV_LORA)

    s64 = s64_ref[...]
    fk = _head_norm(proj(FK_LO, FK_HI), s64, FOX_DIM) * gfk_ref[...]
    fkb_ref[...] = fk.astype(BF16)
    fv = proj(FV_LO, FV_HI)
    fvb_ref[...] = fv.astype(BF16)
    fq_ref[...] = (_head_norm(proj(FQ_LO, FQ_HI), s64, FOX_DIM) * gfq_ref[...]).astype(BF16)

    if stacked:
        ckv_ref[0] = ckv
        kr_ref[0, 0] = kr128.T[ROPE_LO:ROPE_HI, :]
        lf_ref[0, 0] = lf128.T[0:N_FOX, :]
        fk_ref[0, 0] = fk.T
        fv_ref[0, 0] = fv.T
        if first:
            for ref in (ckv_ref, kr_ref, lf_ref, fk_ref, fv_ref):
                ref[1:] = jnp.zeros((ref.shape[0] - 1,) + ref.shape[1:], F32)
    else:
        ckv_ref[...] = ckv
        kr_ref[...] = kr128[:, ROPE_LO:ROPE_HI]
        lf_ref[...] = lf128[:, 0:N_FOX]
        fk_ref[...] = fk
        fv_ref[...] = fv

    cq = _rms(proj(CQ_LO, CQ_HI), gcq_ref[...], Q_LORA).astype(BF16)
    nq = N_MLA * HEAD_PAD
    qa = _dot(cq, wuq_ref[:, 0:nq])
    qb = _dot(cq, wuq_ref[:, nq:2 * nq])
    qr = jnp.concatenate(
        [qa[:, h * HEAD_PAD:(h + 1) * HEAD_PAD] * ct + qb[:, h * HEAD_PAD:(h + 1) * HEAD_PAD] * st
         for h in range(N_MLA)], axis=1)
    qn = _head_norm(qr, s128_ref[...], QK_DIM)
    gq = gq_ref[...]
    q_ref[...] = jnp.concatenate(
        [qn[:, h * HEAD_PAD:(h + 1) * HEAD_PAD] * gq for h in range(N_MLA)], axis=1).astype(BF16)


def _mixer_in(x, lw, ct, st, consts, tm, stack=None, layer=0, batch=1):
    rows = x.shape[0]
    nt = ct.shape[0] // tm
    grid = (rows // tm,)
    row = lambda w: pl.BlockSpec((tm, w), lambda i: (i, 0))
    tab = pl.BlockSpec((tm, LANE), lambda i: (i % nt, 0))
    nfx = N_FOX * FOX_DIM
    sds = jax.ShapeDtypeStruct
    stacked = stack is not None
    first = stacked and not stack
    if stacked:
        assert first == (layer == 0)
        t = rows // batch
        nd = DEPTH if first else 1
        tmaj = lambda w: pl.BlockSpec((nd, 1, w, tm), lambda i: (layer, i // nt, 0, i % nt))
        state = [(sds((DEPTH, rows, KV_LORA), F32), pl.BlockSpec((nd, tm, KV_LORA), lambda i: (layer, i, 0))),
                 (sds((DEPTH, batch, ROPE_DIM, t), F32), tmaj(ROPE_DIM)),
                 (sds((DEPTH, batch, nfx, t), F32), tmaj(nfx)),
                 (sds((DEPTH, batch, nfx, t), F32), tmaj(nfx)),
                 (sds((DEPTH, batch, N_FOX, t), F32), tmaj(N_FOX))]
    else:
        state = [(sds((rows, KV_LORA), F32), row(KV_LORA)), (sds((rows, ROPE_DIM), F32), row(ROPE_DIM)),
                 (sds((rows, nfx), F32), row(nfx)), (sds((rows, nfx), F32), row(nfx)),
                 (sds((rows, N_FOX), F32), row(N_FOX))]
    ckv_o, kr_o, fk_o, fv_o, lf_o = state
    outs = [ckv_o, kr_o, (sds((rows, LANE), F32), row(LANE)), fk_o, fv_o,
            (sds((rows, LANE), F32), row(LANE)), lf_o,
            (sds((rows, N_MLA * HEAD_PAD), BF16), row(N_MLA * HEAD_PAD)),
            (sds((rows, nfx), BF16), row(nfx)), (sds((rows, nfx), BF16), row(nfx)),
            (sds((rows, nfx), BF16), row(nfx))]
    in_specs = [row(D_MODEL), _full((1, D_MODEL)), _full((D_MODEL, D_IN_P)), _full((1, LANE)),
                _full((1, Q_LORA)), _full((1, KV_LORA)), _full((Q_LORA, 2 * N_MLA * HEAD_PAD)),
                _full((1, LANE)), _full((1, nfx)), _full((1, nfx)), tab, tab,
                _full((MXU, MXU)), _full((MXU, MXU))]
    args = [x, lw["g_mix"], lw["w_in"], lw["b_f"], lw["g_cq"], lw["g_ckv"], lw["w_uq"], lw["gq"],
            lw["gfq"], lw["gfk"], ct, st, consts["s64"], consts["s128"]]
    assert len(args) == N_MIXER_IN
    aliases = {}
    if stack:
        in_specs += [pl.BlockSpec(memory_space=pl.ANY)] * len(stack)
        args += list(stack)
        aliases = {N_MIXER_IN + n: o for n, o in enumerate(STACKED_OUT)}
    return pl.pallas_call(
        functools.partial(_mixer_in_body, stacked=stacked, first=first), grid=grid, in_specs=in_specs,
        out_specs=tuple(o[1] for o in outs), out_shape=tuple(o[0] for o in outs),
        input_output_aliases=aliases, compiler_params=_params("arbitrary"), name="mixer_in",
    )(*args)


def _kv_emit(ckv, kr, wk_ref, wv_ref, gk_ref, s128_ref, k_ref, v_ref):
    cb = ckv.astype(BF16)
    kn = _dot(cb, wk_ref[...])
    kk = jnp.concatenate(
        [kn[:, h * HEAD_PAD:(h + 1) * HEAD_PAD] + kr for h in range(N_MLA)], axis=1)
    kk = _head_norm(kk, s128_ref[...], QK_DIM)
    gk = gk_ref[...]
    k_ref[...] = jnp.concatenate(
        [kk[:, h * HEAD_PAD:(h + 1) * HEAD_PAD] * gk for h in range(N_MLA)], axis=1).astype(BF16)
    v_ref[...] = _dot(cb, wv_ref[...]).astype(BF16)


def _kv_body(ckv_ref, kr128_ref, *rest):
    _kv_emit(ckv_ref[0] if len(ckv_ref.shape) == 3 else ckv_ref[...], kr128_ref[...], *rest)


def _kv_cache_body(ckv_ref, krt_ref, *rest):
    tm = krt_ref.shape[-1]
    pad = lambda n: jnp.zeros((n, tm), F32)
    kr128 = jnp.concatenate([pad(ROPE_LO), krt_ref[0, 0], pad(LANE - ROPE_HI)], axis=0).T
    _kv_emit(ckv_ref[0, 0], kr128, *rest)


def _kv_expand_cache(ckv_all, krope_t, layer, lw, consts, tm):
    _, b, t, _ = ckv_all.shape
    nt = t // tm
    row = lambda w: pl.BlockSpec((tm, w), lambda bi, ti: (bi * nt + ti, 0))
    const = lambda shape: pl.BlockSpec(shape, lambda bi, ti: (0,) * len(shape))
    nk = N_MLA * HEAD_PAD
    nv = N_MLA * V_DIM
    return pl.pallas_call(
        _kv_cache_body, grid=(b, nt),
        in_specs=[pl.BlockSpec((1, 1, tm, KV_LORA), lambda bi, ti: (layer, bi, ti, 0)),
                  pl.BlockSpec((1, 1, ROPE_DIM, tm), lambda bi, ti: (layer, bi, 0, ti)),
                  const((KV_LORA, nk)), const((KV_LORA, nv)), const((1, LANE)), const((MXU, MXU))],
        out_specs=(row(nk), row(nv)),
        out_shape=(jax.ShapeDtypeStruct((b * t, nk), BF16), jax.ShapeDtypeStruct((b * t, nv), BF16)),
        compiler_params=_params("arbitrary", "arbitrary"), name="kv_expand_cache",
    )(ckv_all, krope_t, lw["w_k"], lw["w_v"], lw["gk"], consts["s128"])


def _kv_expand(ckv, kr128, lw, consts, tm, layer=None):
    rows = kr128.shape[0]
    row = lambda w: pl.BlockSpec((tm, w), lambda i: (i, 0))
    ckv_spec = row(KV_LORA) if layer is None else pl.BlockSpec((1, tm, KV_LORA), lambda i: (layer, i, 0))
    nk = N_MLA * HEAD_PAD
    nv = N_MLA * V_DIM
    return pl.pallas_call(
        _kv_body, grid=(rows // tm,),
        in_specs=[ckv_spec, row(LANE), _full((KV_LORA, nk)), _full((KV_LORA, nv)),
                  _full((1, LANE)), _full((MXU, MXU))],
        out_specs=(row(nk), row(nv)),
        out_shape=(jax.ShapeDtypeStruct((rows, nk), BF16), jax.ShapeDtypeStruct((rows, nv), BF16)),
        compiler_params=_params("arbitrary"), name="kv_expand",
    )(ckv, kr128, lw["w_k"], lw["w_v"], lw["gk"], consts["s128"])


def _split3(c):
    hi = c.astype(BF16)
    r1 = c - hi.astype(F32)
    mid = r1.astype(BF16)
    lo = (r1 - mid.astype(F32)).astype(BF16)
    return hi, mid, lo


def _cum_body(lf_ref, tri_ref, place_ref, ones_ref, qa_ref, ka_ref, *, blk):
    t = lf_ref.shape[1]
    tri = tri_ref[...]
    carry = jnp.zeros((1, LANE), F32)
    split3 = _split3
    ones_q = ones_ref[0:1, :]
    ones_k = ones_ref[1:2, :]
    for b0 in range(0, t, blk):
        hi, mid, lo = split3(lf_ref[0, b0:b0 + blk, :])
        c = _dot(tri, hi) + _dot(tri, mid) + _dot(tri, lo) + carry
        carry = c[blk - 1:blk, :]
        hi, mid, lo = split3(c * LOG2E)
        qa = _dot(hi, place_ref[0]) + _dot(mid, place_ref[1]) + _dot(lo, place_ref[2]) + ones_q
        ka = ones_k - (_dot(hi, place_ref[3]) + _dot(mid, place_ref[4]) + _dot(lo, place_ref[5]))
        qa_ref[0, b0:b0 + blk, :] = qa.astype(BF16)
        ka_ref[0, b0:b0 + blk, :] = ka.astype(BF16)


def _fox_bias(lf128, consts):
    b, t, _ = lf128.shape
    blk = MXU if t % MXU == 0 else LANE
    tri = consts["tri256"] if blk == MXU else consts["tri128"]
    blk3 = pl.BlockSpec((1, t, LANE), lambda i: (i, 0, 0))
    return pl.pallas_call(
        functools.partial(_cum_body, blk=blk), grid=(b,),
        in_specs=[blk3, _full((blk, blk)), _full((6, LANE, LANE)), _full((8, LANE))],
        out_specs=(blk3, blk3),
        out_shape=(jax.ShapeDtypeStruct((b, t, LANE), BF16),) * 2,
        compiler_params=_params("arbitrary"), name="fox_bias",
    )(lf128, tri, consts["place"], consts["aug_ones"])


def _attn_prompt_body(*refs, fox, tq):
    n_in = 5 if fox else 3
    ins, (o_ref, o_scr) = refs[:n_in], refs[n_in:]
    q_ref, k_ref, v_ref = ins[:3]
    h = pl.program_id(1)
    hh = h % 2
    t = k_ref.shape[1]
    lane = lax.broadcasted_iota(jnp.int32, (tq, LANE), 1)
    r = lax.broadcasted_iota(jnp.int32, (tq, tq), 0)
    c = lax.broadcasted_iota(jnp.int32, (tq, tq), 1)
    allowed = (c <= r) if fox else ((c >> CHUNK_SHIFT) <= (r >> CHUNK_SHIFT))
    def scores(i):
        lo, hi = i * tq, (i + 1) * tq
        if fox:
            qa_ref, ka_ref = ins[3:]
            qm = jnp.where((lane >> 6) == hh, q_ref[0, lo:hi, :].astype(F32), 0.0)
            qa = jnp.where((lane >> 4) == h, qa_ref[0, lo:hi, :].astype(F32), 0.0)
            q = jnp.concatenate([qm, qa], axis=1).astype(BF16)
            k = jnp.concatenate([k_ref[0, 0:hi, :], ka_ref[0, 0:hi, :]], axis=1)
        else:
            q = q_ref[0, lo:hi, :]
            k = k_ref[0, 0:hi, :]
        s = _dot_nt(q, k)
        sd = jnp.where(allowed, s[:, lo:hi], -jnp.inf)
        return sd if i == 0 else jnp.concatenate([s[:, 0:lo], sd], axis=1)

    def softmax(s):
        p = jnp.exp2(s - jnp.max(s, axis=1, keepdims=True))
        return p.astype(BF16), jnp.sum(p, axis=1, keepdims=True)

    def values(i, p, l):
        lo, hi = i * tq, (i + 1) * tq
        o_scr[lo:hi, :] = _dot(p, v_ref[0, 0:hi, :]) / l

    order = list(reversed(range(t // tq)))
    for g in range(0, len(order), GROUP):
        blocks = order[g:g + GROUP]
        ss = [scores(i) for i in blocks]
        pl_ = [softmax(s) for s in ss]
        for i, (p, l) in zip(blocks, pl_):
            values(i, p, l)

    @pl.when(hh == 0)
    def _():
        o_ref[0] = o_scr[...].astype(BF16)

    @pl.when(hh == 1)
    def _():
        lane_t = lax.broadcasted_iota(jnp.int32, (t, LANE), 1)
        o_ref[0] = jnp.where(lane_t < V_DIM, o_ref[0].astype(F32), o_scr[...]).astype(BF16)


def _attn_prompt(q, k, v, aug, tq):
    fox = aug is not None
    b, t, _ = q.shape
    blk = lambda f: pl.BlockSpec((1, t, LANE), f)
    per_head = lambda bi, h: (bi, 0, h)
    per_pair = lambda bi, h: (bi, 0, h // 2)
    shared = lambda bi, h: (bi, 0, 0)
    qk = per_pair if fox else per_head
    in_specs = [blk(qk), blk(qk), blk(per_pair)]
    args = [q, k, v]
    if fox:
        in_specs += [blk(shared), blk(shared)]
        args += list(aug)
    return pl.pallas_call(
        functools.partial(_attn_prompt_body, fox=fox, tq=tq),
        grid=(b, N_MLA), in_specs=in_specs, out_specs=blk(per_pair),
        out_shape=jax.ShapeDtypeStruct((b, t, N_MLA * V_DIM), BF16),
        scratch_shapes=[pltpu.VMEM((t, LANE), F32)],
        compiler_params=_params("arbitrary", "arbitrary"),
        name="attn_fox" if fox else "attn_mla",
    )(*args)


def _pad_rows(x, rows):
    return jnp.concatenate([x, jnp.zeros((rows - x.shape[0],) + x.shape[1:], x.dtype)], axis=0)


def _step_softmax(s_c, s_n):
    m = jnp.maximum(jnp.max(s_c, axis=1, keepdims=True), jnp.max(s_n, axis=1, keepdims=True))
    p_c = jnp.exp2(s_c - m)
    p_n = jnp.exp2(s_n - m)
    l = jnp.sum(p_c, axis=1, keepdims=True) + jnp.sum(p_n, axis=1, keepdims=True)
    return p_c.astype(BF16), p_n.astype(BF16), l


def _attn_mla_step_body(q_ref, kc_ref, vc_ref, kn_ref, vn_ref, o_ref, *, start):
    tq = q_ref.shape[1]
    tc = kc_ref.shape[1]
    lane = lax.broadcasted_iota(jnp.int32, (tq, LANE), 1)
    chunk = lambda pos: pos >> CHUNK_SHIFT
    qc = chunk(start + lax.broadcasted_iota(jnp.int32, (tq, 1), 0))
    ok_c = chunk(lax.broadcasted_iota(jnp.int32, (tq, tc), 1)) <= qc
    ok_n = (lane < tq) & (chunk(start + lane) <= qc)
    for p in range(N_MLA // 2):
        pair = slice(LANE * p, LANE * (p + 1))
        vc = vc_ref[0, :, pair]
        vn = _pad_rows(vn_ref[0, :, pair], LANE)
        outs = []
        for h in (2 * p, 2 * p + 1):
            head = slice(HEAD_PAD * h, HEAD_PAD * (h + 1))
            q = q_ref[0, :, head]
            s_c = jnp.where(ok_c, _dot_nt(q, kc_ref[0, :, head]), -jnp.inf)
            s_n = jnp.where(ok_n, _dot_nt(q, _pad_rows(kn_ref[0, :, head], LANE)), -jnp.inf)
            p_c, p_n, l = _step_softmax(s_c, s_n)
            outs.append((_dot(p_c, vc) + _dot(p_n, vn)) / l)
        o_ref[0, :, pair] = jnp.where(lane < V_DIM, outs[0], outs[1]).astype(BF16)


def _attn_mla_step(q, kc, vc, kn, vn, start):
    b, tq, _ = q.shape
    blk = lambda a: pl.BlockSpec((1,) + a.shape[1:], lambda bi: (bi, 0, 0))
    return pl.pallas_call(
        functools.partial(_attn_mla_step_body, start=start), grid=(b,),
        in_specs=[blk(q), blk(kc), blk(vc), blk(kn), blk(vn)],
        out_specs=pl.BlockSpec((1, tq, N_MLA * V_DIM), lambda bi: (bi, 0, 0)),
        out_shape=jax.ShapeDtypeStruct((b, tq, N_MLA * V_DIM), BF16),
        compiler_params=_params("arbitrary"), name="attn_mla_step",
    )(q, kc, vc, kn, vn)


def _lane_cumsum(x, triu):
    rows, blk = x.shape[0], triu.shape[0]
    carry = jnp.zeros((rows, 1), F32)
    out = []
    for b0 in range(0, x.shape[1], blk):
        hi, mid, lo = _split3(_pad_rows(x[:, b0:b0 + blk], 16))
        c = (_dot(hi, triu) + _dot(mid, triu) + _dot(lo, triu))[0:rows] + carry
        carry = c[:, blk - 1:blk]
        out.append(c)
    return jnp.concatenate(out, axis=1), carry


def _attn_fox_step_body(q_ref, kt_ref, vt_ref, kn_ref, vn_ref, lft_ref, lfn_ref,
                        triu_ref, tri_ref, triu_s_ref, o_ref):
    tq = q_ref.shape[1]
    lane = lax.broadcasted_iota(jnp.int32, (tq, LANE), 1)
    causal = lane <= lax.broadcasted_iota(jnp.int32, (tq, LANE), 0)
    cc, total = _lane_cumsum(lft_ref[0, 0], triu_ref[...])
    lfn = _pad_rows(lfn_ref[0], LANE)
    hi, mid, lo = _split3(lfn)
    tri = tri_ref[...]
    cn = (_dot(tri, hi) + _dot(tri, mid) + _dot(tri, lo))[0:tq]
    cnt, _ = _lane_cumsum(lfn.T[0:N_FOX, :], triu_s_ref[...])
    for p in range(N_FOX // 2):
        pair = slice(LANE * p, LANE * (p + 1))
        kt = kt_ref[0, 0, pair, :].astype(BF16)
        vt = vt_ref[0, 0, pair, :].astype(BF16)
        kn = _pad_rows(kn_ref[0, :, pair], LANE)
        vn = _pad_rows(vn_ref[0, :, pair], LANE)
        qf = q_ref[0, :, pair].astype(F32)
        outs = []
        for hh in range(2):
            h = 2 * p + hh
            q = jnp.where((lane >> 6) == hh, qf, 0.0).astype(BF16)
            col = jnp.sum(jnp.where(lane == h, cn, 0.0), axis=1, keepdims=True)
            bias_c = ((total[h:h + 1, :] + col) - cc[h:h + 1, :]) * LOG2E
            bias_n = (col - cnt[h:h + 1, :]) * LOG2E
            s_c = _dot(q, kt) + bias_c
            s_n = jnp.where(causal, _dot_nt(q, kn) + bias_n, -jnp.inf)
            p_c, p_n, l = _step_softmax(s_c, s_n)
            outs.append((_dot_nt(p_c, vt) + _dot(p_n, vn)) / l)
        o_ref[0, :, pair] = jnp.where(lane < V_DIM, outs[0], outs[1]).astype(BF16)


def _attn_fox_step(q, kt_all, vt_all, kn, vn, lft_all, lfn, layer, consts):
    b, tq, _ = q.shape
    new = lambda a: pl.BlockSpec((1,) + a.shape[1:], lambda bi: (bi, 0, 0))
    old = lambda a: pl.BlockSpec((1, 1) + a.shape[2:], lambda bi: (layer, bi, 0, 0))
    return pl.pallas_call(
        _attn_fox_step_body, grid=(b,),
        in_specs=[new(q), old(kt_all), old(vt_all), new(kn), new(vn), old(lft_all), new(lfn),
                  _full((MXU, MXU)), _full((LANE, LANE)), _full((LANE, LANE))],
        out_specs=pl.BlockSpec((1, tq, N_FOX * FOX_DIM), lambda bi: (bi, 0, 0)),
        out_shape=jax.ShapeDtypeStruct((b, tq, N_FOX * FOX_DIM), BF16),
        compiler_params=_params("arbitrary"), name="attn_fox_step",
    )(q, kt_all, vt_all, kn, vn, lft_all, lfn, consts["triu256"], consts["tri128"], consts["triu128"])


def _post_attn_body(*refs, moe):
    if moe:
        (x_ref, om_ref, of_ref, wo_ref, gffn_ref, wr_ref, br_ref, tri_ref,
         x2_ref, xn_ref, route_ref, routet_ref, cnt_ref, carry_ref) = refs
    else:
        x_ref, om_ref, of_ref, wo_ref, gffn_ref, x2_ref, xn_ref = refs
    nm = N_MLA * V_DIM
    x2 = x_ref[...] + _dot(om_ref[...], wo_ref[0:nm, :]) + _dot(of_ref[...], wo_ref[nm:, :])
    x2_ref[...] = x2
    xn = _rms(x2, gffn_ref[...], D_MODEL)
    xh = xn.astype(BF16)
    xn_ref[...] = xn if moe else xh
    if moe:
        xl = (xn - xh.astype(F32)).astype(BF16)
        r = _dot(xh, wr_ref[...]) + _dot(xl, wr_ref[...])
        logits = r + pltpu.roll(r, LANE - N_EXPERTS, 1) + br_ref[...]
        lane = lax.broadcasted_iota(jnp.int32, logits.shape, 1).astype(F32)
        valid = lane < N_EXPERTS
        lg = jnp.where(valid, logits, -jnp.inf)
        e = jnp.exp(lg - jnp.max(lg, axis=1, keepdims=True))
        probs = e / jnp.sum(e, axis=1, keepdims=True)
        p1 = jnp.where(valid, probs, -1.0)
        m1 = jnp.max(p1, axis=1, keepdims=True)
        i1 = jnp.min(jnp.where(p1 == m1, lane, float(LANE)), axis=1, keepdims=True)
        p2 = jnp.where(lane == i1, -1.0, p1)
        m2 = jnp.max(p2, axis=1, keepdims=True)
        i2 = jnp.min(jnp.where(p2 == m2, lane, float(LANE)), axis=1, keepdims=True)
        den = m1 + m2
        @pl.when(pl.program_id(0) == 0)
        def _():
            carry_ref[...] = jnp.zeros(carry_ref.shape, F32)

        sel = jnp.where((lane == i1) | (lane == i2), 1.0, 0.0)
        incl = _dot(tri_ref[...], sel.astype(BF16)) + carry_ref[0:1, :]
        excl = incl - sel
        rank1 = jnp.sum(jnp.where(lane == i1, excl, 0.0), axis=1, keepdims=True)
        rank2 = jnp.sum(jnp.where(lane == i2, excl, 0.0), axis=1, keepdims=True)
        total = incl[incl.shape[0] - 1:, :]
        carry_ref[...] = jnp.broadcast_to(total, carry_ref.shape)
        cnt_ref[...] = jnp.broadcast_to(total, cnt_ref.shape)
        cols = (i1, i2, rank1, rank2, m1 / den, m2 / den)
        route = jnp.zeros(logits.shape, F32)
        for n, col in enumerate(cols):
            route = jnp.where(lane == n, col, route)
        route_ref[...] = route
        routet_ref[...] = route.T[0:8, :]


def _post_attn(x, om, of, lw, moe, tm, tri=None):
    rows = x.shape[0]
    row = lambda w: pl.BlockSpec((tm, w), lambda i: (i, 0))
    nm = N_MLA * V_DIM
    in_specs = [row(D_MODEL), row(nm), row(nm), _full((2 * nm, D_MODEL)), _full((1, D_MODEL))]
    args = [x, om, of, lw["w_out"], lw["g_ffn"]]
    out_specs = [row(D_MODEL), row(D_MODEL)]
    out_shape = [jax.ShapeDtypeStruct((rows, D_MODEL), F32),
                 jax.ShapeDtypeStruct((rows, D_MODEL), F32 if moe else BF16)]
    scratch = []
    if moe:
        assert tm == MXU
        in_specs += [_full((D_MODEL, LANE)), _full((1, LANE)), _full((MXU, MXU))]
        args += [lw["w_router"], lw["b_router"], tri]
        out_specs += [row(LANE), pl.BlockSpec((8, tm), lambda i: (0, i)), _full((8, LANE))]
        out_shape += [jax.ShapeDtypeStruct((rows, LANE), F32), jax.ShapeDtypeStruct((8, rows), F32),
                      jax.ShapeDtypeStruct((8, LANE), F32)]
        scratch = [pltpu.VMEM((8, LANE), F32)]
    return pl.pallas_call(
        functools.partial(_post_attn_body, moe=moe), grid=(rows // tm,),
        in_specs=in_specs, out_specs=tuple(out_specs), out_shape=tuple(out_shape),
        scratch_shapes=scratch,
        compiler_params=_params("arbitrary"), name="post_attn_moe" if moe else "post_attn",
    )(*args)


def _swiglu_acc(xb, wg_ref, wu_ref, wd_ref, d_ff, before_chunk=None):
    acc = None
    for c in range(0, d_ff, MXU):
        if before_chunk is not None:
            before_chunk(c // MXU, d_ff // MXU)
        g = _dot(xb, wg_ref[:, c:c + MXU])
        u = _dot(xb, wu_ref[:, c:c + MXU])
        h = (g * jax.nn.sigmoid(g) * u).astype(BF16)
        d = _dot(h, wd_ref[c:c + MXU, :])
        acc = d if acc is None else acc + d
    return acc


def _ffn_body(x2_ref, xn_ref, wg_ref, wu_ref, wd_ref, o_ref):
    o_ref[...] = x2_ref[...] + _swiglu_acc(xn_ref[...], wg_ref.at[0], wu_ref.at[0], wd_ref.at[0], D_FF)


def _ffn(x2, xn, lw, tm):
    rows = x2.shape[0]
    row = pl.BlockSpec((tm, D_MODEL), lambda i: (i, 0))
    jl = lw["mixer_idx"]
    wspec = lambda a, b: pl.BlockSpec((1, a, b), lambda i: (jl, 0, 0))
    return pl.pallas_call(
        _ffn_body, grid=(rows // tm,),
        in_specs=[row, row, wspec(D_MODEL, D_FF), wspec(D_MODEL, D_FF), wspec(D_FF, D_MODEL)],
        out_specs=row, out_shape=jax.ShapeDtypeStruct((rows, D_MODEL), F32),
        compiler_params=_params("arbitrary"), name="ffn_dense",
    )(x2, xn, lw["w_gate"], lw["w_up"], lw["w_down"])


SCATTER_UNROLL = 8


def _inv_body(pos_ref, base_hbm, inv_ref, sem, *, tmg):
    fill = pltpu.make_async_copy(base_hbm, inv_ref, sem.at[0])
    fill.start()
    fill.wait()

    def put(i, c):
        for u in range(SCATTER_UNROLL):
            a = i * SCATTER_UNROLL + u
            inv_ref[tmg + pos_ref[a]] = a
        return c

    lax.fori_loop(0, pos_ref.shape[0] // SCATTER_UNROLL, put, 0)


def _route_inverse(pos_flat, base, tmg):
    return pl.pallas_call(
        functools.partial(_inv_body, tmg=tmg),
        in_specs=[pl.BlockSpec(memory_space=pltpu.SMEM), pl.BlockSpec(memory_space=pl.ANY)],
        out_specs=pl.BlockSpec(memory_space=pltpu.SMEM),
        out_shape=jax.ShapeDtypeStruct(base.shape, jnp.int32),
        scratch_shapes=[pltpu.SemaphoreType.DMA((1,))], name="route_inverse",
    )(pos_flat, base)


def _moe_routed_body(te_ref, src_ref, dst_ref, xn_hbm, wg_ref, wu_ref, wd_ref, z_hbm,
                     gbuf, obuf, gsem, ssem, *, tmg, n_tiles):
    del te_ref
    j = pl.program_id(0)

    def row_in(entry, slot, r):
        return pltpu.make_async_copy(xn_hbm.at[pl.ds(src_ref[entry], 1), :],
                                     gbuf.at[slot, pl.ds(r, 1), :], gsem.at[slot])

    def row_out(entry, slot, r):
        return pltpu.make_async_copy(obuf.at[slot, pl.ds(r, 1), :],
                                     z_hbm.at[pl.ds(dst_ref[entry], 1), :], ssem.at[0])

    def tile_in(slot):
        return pltpu.make_async_copy(xn_hbm.at[pl.ds(0, tmg), :], gbuf.at[slot], gsem.at[slot])

    def tile_out(slot):
        return pltpu.make_async_copy(obuf.at[slot], z_hbm.at[pl.ds(0, tmg), :], ssem.at[0])

    @pl.when(j == 0)
    def _():
        obuf[1] = jnp.zeros(obuf.shape[1:], F32)
        for r in range(tmg):
            row_in(tmg + r, 0, r).start()

    def step(cur):
        nxt = 1 - cur

        @pl.when(j >= 1)
        def _():
            tile_out(cur).wait()

        tile_in(cur).wait()
        xb = gbuf[cur].astype(BF16)

        def issue(k, n_chunks):
            for r in range(k * tmg // n_chunks, (k + 1) * tmg // n_chunks):
                row_in((j + 2) * tmg + r, nxt, r).start()
                row_out(j * tmg + r, nxt, r).start()

        obuf[cur] = _swiglu_acc(xb, wg_ref.at[0, 0], wu_ref.at[0, 0], wd_ref.at[0, 0], E_FF, issue)

    for parity in range(2):
        pl.when(j % 2 == parity)(functools.partial(step, parity))

    assert n_tiles % 2 == 0

    @pl.when(j == n_tiles)
    def _():
        tile_out(1).wait()
        tile_in(1).wait()


def _moe_routed(xn, tile_expert, src, dst, lw, tmg, n_tiles):
    n = xn.shape[0]
    jl = lw["mixer_idx"]
    wspec = lambda a, b: pl.BlockSpec((1, 1, a, b), lambda j, te, s, d: (jl, te[j], 0, 0))
    grid_spec = pltpu.PrefetchScalarGridSpec(
        num_scalar_prefetch=3, grid=(n_tiles + 1,),
        in_specs=[pl.BlockSpec(memory_space=pl.ANY), wspec(D_MODEL, E_FF), wspec(D_MODEL, E_FF),
                  wspec(E_FF, D_MODEL)],
        out_specs=pl.BlockSpec(memory_space=pl.ANY),
        scratch_shapes=[pltpu.VMEM((2, tmg, D_MODEL), F32), pltpu.VMEM((2, tmg, D_MODEL), F32),
                        pltpu.SemaphoreType.DMA((2,)), pltpu.SemaphoreType.DMA((1,))])
    return pl.pallas_call(
        functools.partial(_moe_routed_body, tmg=tmg, n_tiles=n_tiles),
        grid_spec=grid_spec, out_shape=jax.ShapeDtypeStruct((2 * n + tmg, D_MODEL), F32),
        compiler_params=_params("arbitrary"), name="moe_routed",
    )(tile_expert, src, dst, xn, lw["we_gate"], lw["we_up"], lw["we_down"])


def _combine_body(x2_ref, route_ref, z0_ref, z1_ref, o_ref):
    route = route_ref[...]
    lane = lax.broadcasted_iota(jnp.int32, route.shape, 1)
    g1 = jnp.sum(jnp.where(lane == 4, route, 0.0), axis=1, keepdims=True)
    g2 = jnp.sum(jnp.where(lane == 5, route, 0.0), axis=1, keepdims=True)
    o_ref[...] = x2_ref[...] + g1 * z0_ref[...] + g2 * z1_ref[...]


def _moe_combine(x2, route, z, tm):
    n = x2.shape[0]
    row = lambda w: pl.BlockSpec((tm, w), lambda i: (i, 0))
    return pl.pallas_call(
        _combine_body, grid=(n // tm,),
        in_specs=[row(D_MODEL), row(LANE), row(D_MODEL),
                  pl.BlockSpec((tm, D_MODEL), lambda i: (i + n // tm, 0))],
        out_specs=row(D_MODEL), out_shape=jax.ShapeDtypeStruct((n, D_MODEL), F32),
        compiler_params=_params("arbitrary"), name="moe_combine",
    )(x2, route, z, z)


def _moe(x2, xn, route, route_t, counts, lw, tmg):
    n = x2.shape[0]
    assert n & (n - 1) == 0
    n_tiles = 2 * n // tmg + N_EXPERTS
    ext = (n_tiles + 3) * tmg
    e = route_t[0:2].astype(jnp.int32)
    rank = route_t[2:4].astype(jnp.int32)
    cnt = counts[0, :N_EXPERTS].astype(jnp.int32)
    tiles = (cnt + tmg - 1) // tmg
    tile_end = jnp.cumsum(tiles)
    row_start = (tile_end - tiles) * tmg
    start_of = jnp.sum(jnp.where(e[..., None] == jnp.arange(N_EXPERTS), row_start, 0), axis=-1)
    pos_flat = (start_of + rank).reshape(-1)
    steps = jnp.minimum(jnp.arange(n_tiles + 1, dtype=jnp.int32), tile_end[-1] - 1)
    tile_expert = jnp.minimum(jnp.sum((steps[:, None] >= tile_end[None, :]).astype(jnp.int32), axis=1),
                              N_EXPERTS - 1)
    idx = jnp.arange(ext, dtype=jnp.int32)
    dst = _route_inverse(pos_flat, 2 * n + (idx & (tmg - 1)), tmg)
    src = jnp.where(dst < 2 * n, dst, idx) & (n - 1)
    z = _moe_routed(xn, tile_expert, src, dst, lw, tmg, n_tiles)
    return _moe_combine(x2, route, z, tmg)


def _consts():
    i = jnp.arange(MXU)
    blockdiag = lambda w: ((i[:, None] // w) == (i[None, :] // w)).astype(BF16)
    tri = lambda n: (jnp.arange(n)[None, :] <= jnp.arange(n)[:, None]).astype(BF16)
    r = jnp.arange(LANE)
    place = jnp.stack([((r[:, None] < N_FOX) & (r[None, :] == AUG_W * r[:, None] + k)).astype(BF16)
                       for k in range(6)])
    within = r % AUG_W
    headed = r < AUG_W * N_FOX
    ones_q = (headed & (within >= 3) & (within < 6)).astype(F32)
    ones_k = (headed & (within < 3)).astype(F32)
    aug_ones = jnp.zeros((8, LANE), F32).at[0].set(ones_q).at[1].set(ones_k)
    return {"s64": blockdiag(FOX_DIM), "s128": blockdiag(HEAD_PAD), "tri256": tri(MXU),
            "tri128": tri(LANE), "triu256": tri(MXU).T, "triu128": tri(LANE).T,
            "place": place, "aug_ones": aug_ones}


def _rope_tables(pos):
    half = ROPE_DIM // 2
    inv = ROPE_BASE ** (-jnp.arange(half, dtype=F32) / half)
    ang = pos.astype(F32)[:, None] * inv[None, :]
    cos, sin = jnp.cos(ang), jnp.sin(ang)
    t = pos.shape[0]
    ct = jnp.concatenate([jnp.ones((t, ROPE_LO), F32), cos, cos, jnp.zeros((t, LANE - ROPE_HI), F32)], 1)
    st = jnp.concatenate([jnp.zeros((t, ROPE_LO), F32), -sin, sin, jnp.zeros((t, LANE - ROPE_HI), F32)], 1)
    return ct, st


def _pad_lanes(v, width=LANE):
    return jnp.pad(v, [(0, 0)] * (v.ndim - 1) + [(0, width - v.shape[-1])])


def _prep_layer(i, p):
    half = ROPE_DIM // 2
    w_in = p["w_in"][i]
    s = [0, Q_LORA, Q_LORA + KV_LORA, Q_LORA + KV_LORA + ROPE_DIM]
    nfx = N_FOX * FOX_DIM
    s += [s[3] + nfx, s[3] + 2 * nfx, s[3] + 3 * nfx, s[3] + 3 * nfx + N_FOX]
    c_q, c_kv, k_rope = w_in[:, s[0]:s[1]], w_in[:, s[1]:s[2]], w_in[:, s[2]:s[3]]
    fq, fk, fv, f_logit = w_in[:, s[3]:s[4]], w_in[:, s[4]:s[5]], w_in[:, s[5]:s[6]], w_in[:, s[6]:s[7]]
    zc = lambda n: jnp.zeros((D_MODEL, n), F32)
    misc = jnp.concatenate([f_logit, zc(ROPE_LO - N_FOX), k_rope, zc(LANE - ROPE_HI)], axis=1)
    w_in_p = jnp.concatenate([c_q, c_kv, fq, fk, fv, misc], axis=1).astype(BF16)

    w_uq = p["w_uq"][i].reshape(Q_LORA, N_MLA, QK_DIM)
    nope, rope = w_uq[..., :NOPE_DIM], w_uq[..., NOPE_DIM:]
    rope_sw = jnp.concatenate([rope[..., half:], rope[..., :half]], axis=-1)
    zq = lambda n: jnp.zeros((Q_LORA, N_MLA, n), F32)
    qa = jnp.concatenate([nope, rope, zq(HEAD_PAD - QK_DIM)], axis=-1)
    qb = jnp.concatenate([zq(NOPE_DIM), rope_sw, zq(HEAD_PAD - QK_DIM)], axis=-1)
    w_uq_p = jnp.concatenate([qa.reshape(Q_LORA, -1), qb.reshape(Q_LORA, -1)], axis=1).astype(BF16)

    w_ukv = p["w_ukv"][i].reshape(KV_LORA, N_MLA, NOPE_DIM + V_DIM)
    w_k = _pad_lanes(w_ukv[..., :NOPE_DIM], HEAD_PAD).reshape(KV_LORA, -1).astype(BF16)
    w_v = w_ukv[..., NOPE_DIM:].reshape(KV_LORA, -1).astype(BF16)

    gq = _pad_lanes(p["g_qn_mla"][i] * (QK_DIM ** -0.5 * LOG2E))[None]
    gk = _pad_lanes(p["g_kn_mla"][i])[None]
    lw = {
        "g_mix": p["g_mix"][i][None], "w_in": w_in_p, "b_f": _pad_lanes(p["b_f"][i])[None],
        "g_cq": p["g_cq"][i][None], "g_ckv": p["g_ckv"][i][None], "w_uq": w_uq_p,
        "gq": gq, "gk": gk,
        "gfq": jnp.tile(p["g_qn_fox"][i] * (FOX_DIM ** -0.5 * LOG2E), N_FOX)[None],
        "gfk": jnp.tile(p["g_kn_fox"][i], N_FOX)[None],
        "w_k": w_k, "w_v": w_v,
        "w_out": p["w_out"][i].astype(BF16), "g_ffn": p["g_ffn"][i][None],
    }
    j = i // 2
    lw["mixer_idx"] = j
    if i % 2 == 0:
        lw.update(w_gate=p["w_gate_b"], w_up=p["w_up_b"], w_down=p["w_down_b"])
    else:
        wr = p["w_router"][j]
        wr_hi = wr.astype(BF16)
        wr_lo = (wr - wr_hi.astype(F32)).astype(BF16)
        lw.update(w_router=_pad_lanes(jnp.concatenate([wr_hi, wr_lo], axis=1)),
                  b_router=_pad_lanes(p["b_router"][j])[None],
                  we_gate=p["we_gate_b"], we_up=p["we_up_b"], we_down=p["we_down_b"])
    return lw


def _channel_mixer(i, x, om, of, lw, consts, tm_post, tm_ffn):
    if i % 2 == 0:
        x2, xn = _post_attn(x, om, of, lw, False, tm_post)
        return _ffn(x2, xn, lw, tm_ffn)
    x2, xn, route, route_t, counts = _post_attn(x, om, of, lw, True, MXU, consts["tri256"])
    return _moe(x2, xn, route, route_t, counts, lw, MXU)


def _trunk_prompt(x, layers, consts):
    b, t, _ = x.shape
    rows = b * t
    x = x.reshape(rows, D_MODEL)
    ct, st = _rope_tables(jnp.arange(t, dtype=jnp.int32))
    nfx = N_FOX * FOX_DIM
    stack = ()
    for i, lw in enumerate(layers):
        (ckv_s, krt_s, kr128, fkt_s, fvt_s, lf128, lft_s, q, fq, fkb, fvb) = _mixer_in(
            x, lw, ct, st, consts, 256, stack=stack, layer=i, batch=b)
        stack = (ckv_s, krt_s, fkt_s, fvt_s, lft_s)
        k, v = _kv_expand(ckv_s, kr128, lw, consts, 512, layer=i)
        aug = _fox_bias(lf128.reshape(b, t, LANE), consts)
        om = _attn_prompt(q.reshape(b, t, -1), k.reshape(b, t, -1), v.reshape(b, t, -1), None, 256)
        of = _attn_prompt(fq.reshape(b, t, nfx), fkb.reshape(b, t, nfx), fvb.reshape(b, t, nfx), aug, 256)
        x = _channel_mixer(i, x, om.reshape(rows, -1), of.reshape(rows, -1), lw, consts, 256, 512)
    ckv_s, krt_s, fkt_s, fvt_s, lft_s = stack
    heads = lambda a: jnp.transpose(a.reshape(DEPTH, b, N_FOX, FOX_DIM, t), (0, 1, 4, 2, 3))
    state = (ckv_s.reshape(DEPTH, b, t, KV_LORA), jnp.swapaxes(krt_s, 2, 3), heads(fkt_s), heads(fvt_s),
             jnp.swapaxes(lft_s, 2, 3))
    return x.reshape(b, t, D_MODEL), state


def _trunk_sample(x, past, layers, consts):
    b, t, _ = x.shape
    rows = b * t
    start = PAST_LEN
    x = x.reshape(rows, D_MODEL)
    ct, st = _rope_tables(start + jnp.arange(t, dtype=jnp.int32))
    ct, st = jnp.tile(ct, (b, 1)), jnp.tile(st, (b, 1))
    p_ckv, p_krope, p_fk, p_fv, p_logf = past
    nfx = N_FOX * FOX_DIM
    krope_t = jnp.swapaxes(p_krope, 2, 3)
    fk_t = jnp.transpose(p_fk, (0, 1, 3, 4, 2)).reshape(DEPTH, b, nfx, start)
    fv_t = jnp.transpose(p_fv, (0, 1, 3, 4, 2)).reshape(DEPTH, b, nfx, start)
    lf_t = jnp.swapaxes(p_logf, 2, 3)
    state = ([], [], [], [], [])
    for i, lw in enumerate(layers):
        ckv, kr, kr128, fk, fv, lf128, lf, q, fq, fkb, fvb = _mixer_in(x, lw, ct, st, consts, rows)
        k_c, v_c = _kv_expand_cache(p_ckv, krope_t, i, lw, consts, 512)
        k_n, v_n = _kv_expand(ckv, kr128, lw, consts, rows)
        om = _attn_mla_step(q.reshape(b, t, -1), k_c.reshape(b, start, -1), v_c.reshape(b, start, -1),
                            k_n.reshape(b, t, -1), v_n.reshape(b, t, -1), start)
        of = _attn_fox_step(fq.reshape(b, t, nfx), fk_t, fv_t, fkb.reshape(b, t, nfx),
                            fvb.reshape(b, t, nfx), lf_t, lf128.reshape(b, t, LANE), i, consts)
        x = _channel_mixer(i, x, om.reshape(rows, -1), of.reshape(rows, -1), lw, consts, rows, rows)
        for lst, s_ in zip(state, (ckv.reshape(b, t, KV_LORA), kr.reshape(b, t, ROPE_DIM),
                                   fk.reshape(b, t, N_FOX, FOX_DIM), fv.reshape(b, t, N_FOX, FOX_DIM),
                                   lf.reshape(b, t, N_FOX))):
            lst.append(s_)
    return x.reshape(b, t, D_MODEL), tuple(jnp.stack(s_) for s_ in state)


def kernel(x_prompt, x_sample, cache_mla_ckv, cache_mla_krope, cache_fox_k, cache_fox_v, cache_fox_logf, g_mix, w_in, b_f, g_cq, g_ckv, w_uq, w_ukv, g_qn_mla, g_kn_mla, g_qn_fox, g_kn_fox, w_out, g_ffn, w_gate, w_up, w_down, w_router, b_router, we_gate, we_up, we_down):
    p = dict(g_mix=g_mix, w_in=w_in, b_f=b_f, g_cq=g_cq, g_ckv=g_ckv, w_uq=w_uq, w_ukv=w_ukv,
             g_qn_mla=g_qn_mla, g_kn_mla=g_kn_mla, g_qn_fox=g_qn_fox, g_kn_fox=g_kn_fox,
             w_out=w_out, g_ffn=g_ffn, w_gate=w_gate, w_up=w_up, w_down=w_down,
             w_router=w_router, b_router=b_router, we_gate=we_gate, we_up=we_up, we_down=we_down)
    for name in ("w_gate", "w_up", "w_down", "we_gate", "we_up", "we_down"):
        p[name + "_b"] = p[name].astype(BF16)
    layers = [_prep_layer(i, p) for i in range(DEPTH)]
    consts = _consts()
    y_p, st_p = _trunk_prompt(x_prompt, layers, consts)
    past = (cache_mla_ckv, cache_mla_krope, cache_fox_k, cache_fox_v, cache_fox_logf)
    y_s, st_s = _trunk_sample(x_sample, past, layers, consts)
    return (y_p, y_s) + st_p + st_s
```

```python
import functools

import jax
import jax.numpy as jnp
from jax import lax
from jax.experimental import pallas as pl
from jax.experimental.pallas import tpu as pltpu

F32 = jnp.float32
BF16 = jnp.bfloat16

D_MODEL = 1024
DEPTH = 4
PAST_LEN = 2048
CHUNK_SHIFT = 6
EPS = 1e-6
ROPE_BASE = 10000.0

N_MLA = 8
Q_LORA = 384
KV_LORA = 256
NOPE_DIM = 64
ROPE_DIM = 32
QK_DIM = NOPE_DIM + ROPE_DIM
V_DIM = 64
N_FOX = 8
FOX_DIM = 64
D_FF = 2816
N_EXPERTS = 8
E_FF = 1792

LOG2E = 1.4426950408889634
LANE = 128
MXU = 256
HEAD_PAD = 128
ROPE_LO = NOPE_DIM
ROPE_HI = NOPE_DIM + ROPE_DIM
AUG_W = 16
GROUP = 4

CQ_LO, CQ_HI = 0, Q_LORA
CKV_LO, CKV_HI = CQ_HI, CQ_HI + KV_LORA
FQ_LO, FQ_HI = CKV_HI, CKV_HI + N_FOX * FOX_DIM
FK_LO, FK_HI = FQ_HI, FQ_HI + N_FOX * FOX_DIM
FV_LO, FV_HI = FK_HI, FK_HI + N_FOX * FOX_DIM
MISC_LO, MISC_HI = FV_HI, FV_HI + LANE
D_IN_P = MISC_HI

VMEM_LIMIT = 56 * 1024 * 1024


def _params(*sem):
    return pltpu.CompilerParams(dimension_semantics=sem, vmem_limit_bytes=VMEM_LIMIT)


def _dot(a, b):
    return jnp.dot(a, b, preferred_element_type=F32)


def _dot_nt(a, b):
    return lax.dot_general(a, b, (((1,), (1,)), ((), ())), preferred_element_type=F32)


def _rms(x, g, n):
    return x * lax.rsqrt(jnp.sum(x * x, axis=-1, keepdims=True) * (1.0 / n) + EPS) * g


def _head_norm(x, smat, dim):
    outs = []
    for c in range(0, x.shape[1], MXU):
        xc = x[:, c:c + MXU]
        ss = _dot((xc * xc).astype(BF16), smat)
        outs.append(xc * lax.rsqrt(ss * (1.0 / dim) + EPS))
    return jnp.concatenate(outs, axis=1)


def _full(shape):
    return pl.BlockSpec(shape, lambda *_: (0,) * len(shape))


N_MIXER_IN = 14
STACKED_OUT = (0, 1, 3, 4, 6)


def _mixer_in_body(*refs, stacked, first):
    (x_ref, gmix_ref, win_ref, bf_ref, gcq_ref, gckv_ref, wuq_ref, gq_ref,
     gfq_ref, gfk_ref, ct_ref, st_ref, s64_ref, s128_ref) = refs[:N_MIXER_IN]
    (ckv_ref, kr_ref, kr128_ref, fk_ref, fv_ref, lf128_ref, lf_ref,
     q_ref, fq_ref, fkb_ref, fvb_ref) = refs[len(refs) - 11:]
    tm = x_ref.shape[0]
    xb = _rms(x_ref[...], gmix_ref[...], D_MODEL).astype(BF16)

    def proj(lo, hi):
        return _dot(xb, win_ref[:, lo:hi])

    ct = ct_ref[...]
    st = st_ref[...]
    lane = lax.broadcasted_iota(jnp.int32, (tm, LANE), 1)

    cq_raw = proj(CQ_LO, CQ_HI)
    a = proj(MISC_LO, MISC_HI)
    ckv_raw = proj(CKV_LO, CKV_HI)
    fk_raw = proj(FK_LO, FK_HI)
    fv = proj(FV_LO, FV_HI)
    fq_raw = proj(FQ_LO, FQ_HI)

    cq = _rms(cq_raw, gcq_ref[...], Q_LORA).astype(BF16)
    nq = N_MLA * HEAD_PAD
    qa = _dot(cq, wuq_ref[:, 0:nq])
    qb = _dot(cq, wuq_ref[:, nq:2 * nq])

    z = a + bf_ref[...]
    lf = jnp.minimum(z, 0.0) - jnp.log1p(jnp.exp(-jnp.abs(z)))
    lf128 = jnp.where(lane < N_FOX, lf, 0.0)
    lf128_ref[...] = lf128
    half = ROPE_DIM // 2
    sw = jnp.where(lane < ROPE_LO + half, pltpu.roll(a, LANE - half, 1), pltpu.roll(a, half, 1))
    kr128 = jnp.where((lane >= ROPE_LO) & (lane < ROPE_HI), a * ct + sw * st, 0.0)
    kr128_ref[...] = kr128
    ckv = _rms(ckv_raw, gckv_ref[...], KV_LORA)

    qr = jnp.concatenate(
        [qa[:, h * HEAD_PAD:(h + 1) * HEAD_PAD] * ct + qb[:, h * HEAD_PAD:(h + 1) * HEAD_PAD] * st
         for h in range(N_MLA)], axis=1)
    qn = _head_norm(qr, s128_ref[...], QK_DIM)
    gq = gq_ref[...]
    q_ref[...] = jnp.concatenate(
        [qn[:, h * HEAD_PAD:(h + 1) * HEAD_PAD] * gq for h in range(N_MLA)], axis=1).astype(BF16)

    s64 = s64_ref[...]
    fk = _head_norm(fk_raw, s64, FOX_DIM) * gfk_ref[...]
    fkb_ref[...] = fk.astype(BF16)
    fvb_ref[...] = fv.astype(BF16)
    fq_ref[...] = (_head_norm(fq_raw, s64, FOX_DIM) * gfq_ref[...]).astype(BF16)

    if stacked:
        ckv_ref[0] = ckv
        kr_ref[0, 0] = kr128.T[ROPE_LO:ROPE_HI, :]
        lf_ref[0, 0] = lf128.T[0:N_FOX, :]
        fk_ref[0, 0] = fk.T
        fv_ref[0, 0] = fv.T
        if first:
            for ref in (ckv_ref, kr_ref, lf_ref, fk_ref, fv_ref):
                ref[1:] = jnp.zeros((ref.shape[0] - 1,) + ref.shape[1:], F32)
    else:
        ckv_ref[...] = ckv
        kr_ref[...] = kr128[:, ROPE_LO:ROPE_HI]
        lf_ref[...] = lf128[:, 0:N_FOX]
        fk_ref[...] = fk
        fv_ref[...] = fv


def _mixer_in(x, lw, ct, st, consts, tm, stack=None, layer=0, batch=1):
    rows = x.shape[0]
    nt = ct.shape[0] // tm
    grid = (rows // tm,)
    row = lambda w: pl.BlockSpec((tm, w), lambda i: (i, 0))
    tab = pl.BlockSpec((tm, LANE), lambda i: (i % nt, 0))
    nfx = N_FOX * FOX_DIM
    sds = jax.ShapeDtypeStruct
    stacked = stack is not None
    first = stacked and not stack
    if stacked:
        assert first == (layer == 0)
        t = rows // batch
        nd = DEPTH if first else 1
        tmaj = lambda w: pl.BlockSpec((nd, 1, w, tm), lambda i: (layer, i // nt, 0, i % nt))
        state = [(sds((DEPTH, rows, KV_LORA), F32), pl.BlockSpec((nd, tm, KV_LORA), lambda i: (layer, i, 0))),
                 (sds((DEPTH, batch, ROPE_DIM, t), F32), tmaj(ROPE_DIM)),
                 (sds((DEPTH, batch, nfx, t), F32), tmaj(nfx)),
                 (sds((DEPTH, batch, nfx, t), F32), tmaj(nfx)),
                 (sds((DEPTH, batch, N_FOX, t), F32), tmaj(N_FOX))]
    else:
        state = [(sds((rows, KV_LORA), F32), row(KV_LORA)), (sds((rows, ROPE_DIM), F32), row(ROPE_DIM)),
                 (sds((rows, nfx), F32), row(nfx)), (sds((rows, nfx), F32), row(nfx)),
                 (sds((rows, N_FOX), F32), row(N_FOX))]
    ckv_o, kr_o, fk_o, fv_o, lf_o = state
    outs = [ckv_o, kr_o, (sds((rows, LANE), F32), row(LANE)), fk_o, fv_o,
            (sds((rows, LANE), F32), row(LANE)), lf_o,
            (sds((rows, N_MLA * HEAD_PAD), BF16), row(N_MLA * HEAD_PAD)),
            (sds((rows, nfx), BF16), row(nfx)), (sds((rows, nfx), BF16), row(nfx)),
            (sds((rows, nfx), BF16), row(nfx))]
    in_specs = [row(D_MODEL), _full((1, D_MODEL)), _full((D_MODEL, D_IN_P)), _full((1, LANE)),
                _full((1, Q_LORA)), _full((1, KV_LORA)), _full((Q_LORA, 2 * N_MLA * HEAD_PAD)),
                _full((1, LANE)), _full((1, nfx)), _full((1, nfx)), tab, tab,
                _full((MXU, MXU)), _full((MXU, MXU))]
    args = [x, lw["g_mix"], lw["w_in"], lw["b_f"], lw["g_cq"], lw["g_ckv"], lw["w_uq"], lw["gq"],
            lw["gfq"], lw["gfk"], ct, st, consts["s64"], consts["s128"]]
    assert len(args) == N_MIXER_IN
    aliases = {}
    if stack:
        in_specs += [pl.BlockSpec(memory_space=pl.ANY)] * len(stack)
        args += list(stack)
        aliases = {N_MIXER_IN + n: o for n, o in enumerate(STACKED_OUT)}
    return pl.pallas_call(
        functools.partial(_mixer_in_body, stacked=stacked, first=first), grid=grid, in_specs=in_specs,
        out_specs=tuple(o[1] for o in outs), out_shape=tuple(o[0] for o in outs),
        input_output_aliases=aliases, compiler_params=_params("arbitrary"), name="mixer_in",
    )(*args)


def _kv_emit(ckv, kr, wk_ref, wv_ref, gk_ref, s128_ref, k_ref, v_ref):
    cb = ckv.astype(BF16)
    kn = _dot(cb, wk_ref[...])
    kk = jnp.concatenate(
        [kn[:, h * HEAD_PAD:(h + 1) * HEAD_PAD] + kr for h in range(N_MLA)], axis=1)
    kk = _head_norm(kk, s128_ref[...], QK_DIM)
    gk = gk_ref[...]
    k_ref[...] = jnp.concatenate(
        [kk[:, h * HEAD_PAD:(h + 1) * HEAD_PAD] * gk for h in range(N_MLA)], axis=1).astype(BF16)
    v_ref[...] = _dot(cb, wv_ref[...]).astype(BF16)


def _kv_body(ckv_ref, kr128_ref, *rest):
    _kv_emit(ckv_ref[0] if len(ckv_ref.shape) == 3 else ckv_ref[...], kr128_ref[...], *rest)


def _kv_cache_body(ckv_ref, krt_ref, *rest):
    tm = krt_ref.shape[-1]
    pad = lambda n: jnp.zeros((n, tm), F32)
    kr128 = jnp.concatenate([pad(ROPE_LO), krt_ref[0, 0], pad(LANE - ROPE_HI)], axis=0).T
    _kv_emit(ckv_ref[0, 0], kr128, *rest)


def _kv_expand_cache(ckv_all, krope_t, layer, lw, consts, tm):
    _, b, t, _ = ckv_all.shape
    nt = t // tm
    row = lambda w: pl.BlockSpec((tm, w), lambda bi, ti: (bi * nt + ti, 0))
    const = lambda shape: pl.BlockSpec(shape, lambda bi, ti: (0,) * len(shape))
    nk = N_MLA * HEAD_PAD
    nv = N_MLA * V_DIM
    return pl.pallas_call(
        _kv_cache_body, grid=(b, nt),
        in_specs=[pl.BlockSpec((1, 1, tm, KV_LORA), lambda bi, ti: (layer, bi, ti, 0)),
                  pl.BlockSpec((1, 1, ROPE_DIM, tm), lambda bi, ti: (layer, bi, 0, ti)),
                  const((KV_LORA, nk)), const((KV_LORA, nv)), const((1, LANE)), const((MXU, MXU))],
        out_specs=(row(nk), row(nv)),
        out_shape=(jax.ShapeDtypeStruct((b * t, nk), BF16), jax.ShapeDtypeStruct((b * t, nv), BF16)),
        compiler_params=_params("arbitrary", "arbitrary"), name="kv_expand_cache",
    )(ckv_all, krope_t, lw["w_k"], lw["w_v"], lw["gk"], consts["s128"])


def _kv_expand(ckv, kr128, lw, consts, tm, layer=None):
    rows = kr128.shape[0]
    row = lambda w: pl.BlockSpec((tm, w), lambda i: (i, 0))
    ckv_spec = row(KV_LORA) if layer is None else pl.BlockSpec((1, tm, KV_LORA), lambda i: (layer, i, 0))
    nk = N_MLA * HEAD_PAD
    nv = N_MLA * V_DIM
    return pl.pallas_call(
        _kv_body, grid=(rows // tm,),
        in_specs=[ckv_spec, row(LANE), _full((KV_LORA, nk)), _full((KV_LORA, nv)),
                  _full((1, LANE)), _full((MXU, MXU))],
        out_specs=(row(nk), row(nv)),
        out_shape=(jax.ShapeDtypeStruct((rows, nk), BF16), jax.ShapeDtypeStruct((rows, nv), BF16)),
        compiler_params=_params("arbitrary"), name="kv_expand",
    )(ckv, kr128, lw["w_k"], lw["w_v"], lw["gk"], consts["s128"])


def _split3(c):
    hi = c.astype(BF16)
    r1 = c - hi.astype(F32)
    mid = r1.astype(BF16)
    lo = (r1 - mid.astype(F32)).astype(BF16)
    return hi, mid, lo


def _cum_body(lf_ref, tri_ref, place_ref, ones_ref, qa_ref, ka_ref, *, blk):
    t = lf_ref.shape[1]
    tri = tri_ref[...]
    carry = jnp.zeros((1, LANE), F32)
    split3 = _split3
    ones_q = ones_ref[0:1, :]
    ones_k = ones_ref[1:2, :]
    for b0 in range(0, t, blk):
        hi, mid, lo = split3(lf_ref[0, b0:b0 + blk, :])
        c = _dot(tri, hi) + _dot(tri, mid) + _dot(tri, lo) + carry
        carry = c[blk - 1:blk, :]
        hi, mid, lo = split3(c * LOG2E)
        qa = _dot(hi, place_ref[0]) + _dot(mid, place_ref[1]) + _dot(lo, place_ref[2]) + ones_q
        ka = ones_k - (_dot(hi, place_ref[3]) + _dot(mid, place_ref[4]) + _dot(lo, place_ref[5]))
        qa_ref[0, b0:b0 + blk, :] = qa.astype(BF16)
        ka_ref[0, b0:b0 + blk, :] = ka.astype(BF16)


def _fox_bias(lf128, consts):
    b, t, _ = lf128.shape
    blk = MXU if t % MXU == 0 else LANE
    tri = consts["tri256"] if blk == MXU else consts["tri128"]
    blk3 = pl.BlockSpec((1, t, LANE), lambda i: (i, 0, 0))
    return pl.pallas_call(
        functools.partial(_cum_body, blk=blk), grid=(b,),
        in_specs=[blk3, _full((blk, blk)), _full((6, LANE, LANE)), _full((8, LANE))],
        out_specs=(blk3, blk3),
        out_shape=(jax.ShapeDtypeStruct((b, t, LANE), BF16),) * 2,
        compiler_params=_params("arbitrary"), name="fox_bias",
    )(lf128, tri, consts["place"], consts["aug_ones"])


def _attn_prompt_body(*refs, fox, tq):
    n_in = 5 if fox else 3
    ins, (o_ref, o_scr) = refs[:n_in], refs[n_in:]
    q_ref, k_ref, v_ref = ins[:3]
    h = pl.program_id(1)
    hh = h % 2
    t = k_ref.shape[1]
    lane = lax.broadcasted_iota(jnp.int32, (tq, LANE), 1)
    r = lax.broadcasted_iota(jnp.int32, (tq, tq), 0)
    c = lax.broadcasted_iota(jnp.int32, (tq, tq), 1)
    allowed = (c <= r) if fox else ((c >> CHUNK_SHIFT) <= (r >> CHUNK_SHIFT))
    def scores(i):
        lo, hi = i * tq, (i + 1) * tq
        if fox:
            qa_ref, ka_ref = ins[3:]
            qm = jnp.where((lane >> 6) == hh, q_ref[0, lo:hi, :].astype(F32), 0.0)
            qa = jnp.where((lane >> 4) == h, qa_ref[0, lo:hi, :].astype(F32), 0.0)
            q = jnp.concatenate([qm, qa], axis=1).astype(BF16)
            k = jnp.concatenate([k_ref[0, 0:hi, :], ka_ref[0, 0:hi, :]], axis=1)
        else:
            q = q_ref[0, lo:hi, :]
            k = k_ref[0, 0:hi, :]
        s = _dot_nt(q, k)
        sd = jnp.where(allowed, s[:, lo:hi], -jnp.inf)
        return sd if i == 0 else jnp.concatenate([s[:, 0:lo], sd], axis=1)

    def softmax(s):
        p = jnp.exp2(s - jnp.max(s, axis=1, keepdims=True))
        return p.astype(BF16), jnp.sum(p, axis=1, keepdims=True)

    def values(i, p, l):
        lo, hi = i * tq, (i + 1) * tq
        o_scr[lo:hi, :] = _dot(p, v_ref[0, 0:hi, :]) / l

    order = list(reversed(range(t // tq)))
    for g in range(0, len(order), GROUP):
        blocks = order[g:g + GROUP]
        ss = [scores(i) for i in blocks]
        pl_ = [softmax(s) for s in ss]
        for i, (p, l) in zip(blocks, pl_):
            values(i, p, l)

    @pl.when(hh == 0)
    def _():
        o_ref[0] = o_scr[...].astype(BF16)

    @pl.when(hh == 1)
    def _():
        lane_t = lax.broadcasted_iota(jnp.int32, (t, LANE), 1)
        o_ref[0] = jnp.where(lane_t < V_DIM, o_ref[0].astype(F32), o_scr[...]).astype(BF16)


def _attn_prompt(q, k, v, aug, tq):
    fox = aug is not None
    b, t, _ = q.shape
    blk = lambda f: pl.BlockSpec((1, t, LANE), f)
    per_head = lambda bi, h: (bi, 0, h)
    per_pair = lambda bi, h: (bi, 0, h // 2)
    shared = lambda bi, h: (bi, 0, 0)
    qk = per_pair if fox else per_head
    in_specs = [blk(qk), blk(qk), blk(per_pair)]
    args = [q, k, v]
    if fox:
        in_specs += [blk(shared), blk(shared)]
        args += list(aug)
    return pl.pallas_call(
        functools.partial(_attn_prompt_body, fox=fox, tq=tq),
        grid=(b, N_MLA), in_specs=in_specs, out_specs=blk(per_pair),
        out_shape=jax.ShapeDtypeStruct((b, t, N_MLA * V_DIM), BF16),
        scratch_shapes=[pltpu.VMEM((t, LANE), F32)],
        compiler_params=_params("arbitrary", "arbitrary"),
        name="attn_fox" if fox else "attn_mla",
    )(*args)


def _pad_rows(x, rows):
    return jnp.concatenate([x, jnp.zeros((rows - x.shape[0],) + x.shape[1:], x.dtype)], axis=0)


def _step_softmax(s_c, s_n):
    m = jnp.maximum(jnp.max(s_c, axis=1, keepdims=True), jnp.max(s_n, axis=1, keepdims=True))
    p_c = jnp.exp2(s_c - m)
    p_n = jnp.exp2(s_n - m)
    l = jnp.sum(p_c, axis=1, keepdims=True) + jnp.sum(p_n, axis=1, keepdims=True)
    return p_c.astype(BF16), p_n.astype(BF16), l


def _attn_mla_step_body(q_ref, kc_ref, vc_ref, kn_ref, vn_ref, o_ref, *, start):
    tq = q_ref.shape[1]
    tc = kc_ref.shape[1]
    lane = lax.broadcasted_iota(jnp.int32, (tq, LANE), 1)
    chunk = lambda pos: pos >> CHUNK_SHIFT
    qc = chunk(start + lax.broadcasted_iota(jnp.int32, (tq, 1), 0))
    ok_c = chunk(lax.broadcasted_iota(jnp.int32, (tq, tc), 1)) <= qc
    ok_n = (lane < tq) & (chunk(start + lane) <= qc)
    for p in range(N_MLA // 2):
        pair = slice(LANE * p, LANE * (p + 1))
        vc = vc_ref[0, :, pair]
        vn = _pad_rows(vn_ref[0, :, pair], LANE)
        outs = []
        for h in (2 * p, 2 * p + 1):
            head = slice(HEAD_PAD * h, HEAD_PAD * (h + 1))
            q = q_ref[0, :, head]
            s_c = jnp.where(ok_c, _dot_nt(q, kc_ref[0, :, head]), -jnp.inf)
            s_n = jnp.where(ok_n, _dot_nt(q, _pad_rows(kn_ref[0, :, head], LANE)), -jnp.inf)
            p_c, p_n, l = _step_softmax(s_c, s_n)
            outs.append((_dot(p_c, vc) + _dot(p_n, vn)) / l)
        o_ref[0, :, pair] = jnp.where(lane < V_DIM, outs[0], outs[1]).astype(BF16)


def _attn_mla_step(q, kc, vc, kn, vn, start):
    b, tq, _ = q.shape
    blk = lambda a: pl.BlockSpec((1,) + a.shape[1:], lambda bi: (bi, 0, 0))
    return pl.pallas_call(
        functools.partial(_attn_mla_step_body, start=start), grid=(b,),
        in_specs=[blk(q), blk(kc), blk(vc), blk(kn), blk(vn)],
        out_specs=pl.BlockSpec((1, tq, N_MLA * V_DIM), lambda bi: (bi, 0, 0)),
        out_shape=jax.ShapeDtypeStruct((b, tq, N_MLA * V_DIM), BF16),
        compiler_params=_params("arbitrary"), name="attn_mla_step",
    )(q, kc, vc, kn, vn)


def _lane_cumsum(x, triu):
    rows, blk = x.shape[0], triu.shape[0]
    carry = jnp.zeros((rows, 1), F32)
    out = []
    for b0 in range(0, x.shape[1], blk):
        hi, mid, lo = _split3(_pad_rows(x[:, b0:b0 + blk], 16))
        c = (_dot(hi, triu) + _dot(mid, triu) + _dot(lo, triu))[0:rows] + carry
        carry = c[:, blk - 1:blk]
        out.append(c)
    return jnp.concatenate(out, axis=1), carry


def _attn_fox_step_body(q_ref, kt_ref, vt_ref, kn_ref, vn_ref, lft_ref, lfn_ref,
                        triu_ref, tri_ref, triu_s_ref, o_ref):
    tq = q_ref.shape[1]
    lane = lax.broadcasted_iota(jnp.int32, (tq, LANE), 1)
    causal = lane <= lax.broadcasted_iota(jnp.int32, (tq, LANE), 0)
    cc, total = _lane_cumsum(lft_ref[0, 0], triu_ref[...])
    lfn = _pad_rows(lfn_ref[0], LANE)
    hi, mid, lo = _split3(lfn)
    tri = tri_ref[...]
    cn = (_dot(tri, hi) + _dot(tri, mid) + _dot(tri, lo))[0:tq]
    cnt, _ = _lane_cumsum(lfn.T[0:N_FOX, :], triu_s_ref[...])
    for p in range(N_FOX // 2):
        pair = slice(LANE * p, LANE * (p + 1))
        kt = kt_ref[0, 0, pair, :].astype(BF16)
        vt = vt_ref[0, 0, pair, :].astype(BF16)
        kn = _pad_rows(kn_ref[0, :, pair], LANE)
        vn = _pad_rows(vn_ref[0, :, pair], LANE)
        qf = q_ref[0, :, pair].astype(F32)
        outs = []
        for hh in range(2):
            h = 2 * p + hh
            q = jnp.where((lane >> 6) == hh, qf, 0.0).astype(BF16)
            col = jnp.sum(jnp.where(lane == h, cn, 0.0), axis=1, keepdims=True)
            bias_c = ((total[h:h + 1, :] + col) - cc[h:h + 1, :]) * LOG2E
            bias_n = (col - cnt[h:h + 1, :]) * LOG2E
            s_c = _dot(q, kt) + bias_c
            s_n = jnp.where(causal, _dot_nt(q, kn) + bias_n, -jnp.inf)
            p_c, p_n, l = _step_softmax(s_c, s_n)
            outs.append((_dot_nt(p_c, vt) + _dot(p_n, vn)) / l)
        o_ref[0, :, pair] = jnp.where(lane < V_DIM, outs[0], outs[1]).astype(BF16)


def _attn_fox_step(q, kt_all, vt_all, kn, vn, lft_all, lfn, layer, consts):
    b, tq, _ = q.shape
    new = lambda a: pl.BlockSpec((1,) + a.shape[1:], lambda bi: (bi, 0, 0))
    old = lambda a: pl.BlockSpec((1, 1) + a.shape[2:], lambda bi: (layer, bi, 0, 0))
    return pl.pallas_call(
        _attn_fox_step_body, grid=(b,),
        in_specs=[new(q), old(kt_all), old(vt_all), new(kn), new(vn), old(lft_all), new(lfn),
                  _full((MXU, MXU)), _full((LANE, LANE)), _full((LANE, LANE))],
        out_specs=pl.BlockSpec((1, tq, N_FOX * FOX_DIM), lambda bi: (bi, 0, 0)),
        out_shape=jax.ShapeDtypeStruct((b, tq, N_FOX * FOX_DIM), BF16),
        compiler_params=_params("arbitrary"), name="attn_fox_step",
    )(q, kt_all, vt_all, kn, vn, lft_all, lfn, consts["triu256"], consts["tri128"], consts["triu128"])


def _post_attn_body(*refs, moe):
    if moe:
        (x_ref, om_ref, of_ref, wo_ref, gffn_ref, wr_ref, br_ref, tri_ref,
         x2_ref, xn_ref, route_ref, routet_ref, cnt_ref, carry_ref) = refs
    else:
        x_ref, om_ref, of_ref, wo_ref, gffn_ref, x2_ref, xn_ref = refs
    nm = N_MLA * V_DIM
    x2 = x_ref[...] + _dot(om_ref[...], wo_ref[0:nm, :]) + _dot(of_ref[...], wo_ref[nm:, :])
    x2_ref[...] = x2
    xn = _rms(x2, gffn_ref[...], D_MODEL)
    xh = xn.astype(BF16)
    xn_ref[...] = xn if moe else xh
    if moe:
        xl = (xn - xh.astype(F32)).astype(BF16)
        r = _dot(xh, wr_ref[...]) + _dot(xl, wr_ref[...])
        logits = r + pltpu.roll(r, LANE - N_EXPERTS, 1) + br_ref[...]
        lane = lax.broadcasted_iota(jnp.int32, logits.shape, 1).astype(F32)
        valid = lane < N_EXPERTS
        lg = jnp.where(valid, logits, -jnp.inf)
        e = jnp.exp(lg - jnp.max(lg, axis=1, keepdims=True))
        probs = e / jnp.sum(e, axis=1, keepdims=True)
        p1 = jnp.where(valid, probs, -1.0)
        m1 = jnp.max(p1, axis=1, keepdims=True)
        i1 = jnp.min(jnp.where(p1 == m1, lane, float(LANE)), axis=1, keepdims=True)
        p2 = jnp.where(lane == i1, -1.0, p1)
        m2 = jnp.max(p2, axis=1, keepdims=True)
        i2 = jnp.min(jnp.where(p2 == m2, lane, float(LANE)), axis=1, keepdims=True)
        den = m1 + m2
        @pl.when(pl.program_id(0) == 0)
        def _():
            carry_ref[...] = jnp.zeros(carry_ref.shape, F32)

        sel = jnp.where((lane == i1) | (lane == i2), 1.0, 0.0)
        incl = _dot(tri_ref[...], sel.astype(BF16)) + carry_ref[0:1, :]
        excl = incl - sel
        rank1 = jnp.sum(jnp.where(lane == i1, excl, 0.0), axis=1, keepdims=True)
        rank2 = jnp.sum(jnp.where(lane == i2, excl, 0.0), axis=1, keepdims=True)
        total = incl[incl.shape[0] - 1:, :]
        carry_ref[...] = jnp.broadcast_to(total, carry_ref.shape)
        cnt_ref[...] = jnp.broadcast_to(total, cnt_ref.shape)
        cols = (i1, i2, rank1, rank2, m1 / den, m2 / den)
        route = jnp.zeros(logits.shape, F32)
        for n, col in enumerate(cols):
            route = jnp.where(lane == n, col, route)
        route_ref[...] = route
        routet_ref[...] = route.T[0:8, :]


def _post_attn(x, om, of, lw, moe, tm, tri=None):
    rows = x.shape[0]
    row = lambda w: pl.BlockSpec((tm, w), lambda i: (i, 0))
    nm = N_MLA * V_DIM
    in_specs = [row(D_MODEL), row(nm), row(nm), _full((2 * nm, D_MODEL)), _full((1, D_MODEL))]
    args = [x, om, of, lw["w_out"], lw["g_ffn"]]
    out_specs = [row(D_MODEL), row(D_MODEL)]
    out_shape = [jax.ShapeDtypeStruct((rows, D_MODEL), F32),
                 jax.ShapeDtypeStruct((rows, D_MODEL), F32 if moe else BF16)]
    scratch = []
    if moe:
        assert tm == MXU
        in_specs += [_full((D_MODEL, LANE)), _full((1, LANE)), _full((MXU, MXU))]
        args += [lw["w_router"], lw["b_router"], tri]
        out_specs += [row(LANE), pl.BlockSpec((8, tm), lambda i: (0, i)), _full((8, LANE))]
        out_shape += [jax.ShapeDtypeStruct((rows, LANE), F32), jax.ShapeDtypeStruct((8, rows), F32),
                      jax.ShapeDtypeStruct((8, LANE), F32)]
        scratch = [pltpu.VMEM((8, LANE), F32)]
    return pl.pallas_call(
        functools.partial(_post_attn_body, moe=moe), grid=(rows // tm,),
        in_specs=in_specs, out_specs=tuple(out_specs), out_shape=tuple(out_shape),
        scratch_shapes=scratch,
        compiler_params=_params("arbitrary"), name="post_attn_moe" if moe else "post_attn",
    )(*args)


def _swiglu_acc(xb, wg_ref, wu_ref, wd_ref, d_ff, lo=0, hi=None):
    acc = None
    for c in range(lo * MXU, d_ff if hi is None else hi * MXU, MXU):
        g = _dot(xb, wg_ref[:, c:c + MXU])
        u = _dot(xb, wu_ref[:, c:c + MXU])
        h = (g * jax.nn.sigmoid(g) * u).astype(BF16)
        d = _dot(h, wd_ref[c:c + MXU, :])
        acc = d if acc is None else acc + d
    return acc


def _ffn_body(x2_ref, xn_ref, wg_ref, wu_ref, wd_ref, o_ref):
    o_ref[...] = x2_ref[...] + _swiglu_acc(xn_ref[...], wg_ref.at[0], wu_ref.at[0], wd_ref.at[0], D_FF)


def _ffn(x2, xn, lw, tm):
    rows = x2.shape[0]
    row = pl.BlockSpec((tm, D_MODEL), lambda i: (i, 0))
    jl = lw["mixer_idx"]
    wspec = lambda a, b: pl.BlockSpec((1, a, b), lambda i: (jl, 0, 0))
    return pl.pallas_call(
        _ffn_body, grid=(rows // tm,),
        in_specs=[row, row, wspec(D_MODEL, D_FF), wspec(D_MODEL, D_FF), wspec(D_FF, D_MODEL)],
        out_specs=row, out_shape=jax.ShapeDtypeStruct((rows, D_MODEL), F32),
        compiler_params=_params("arbitrary"), name="ffn_dense",
    )(x2, xn, lw["w_gate"], lw["w_up"], lw["w_down"])


SCATTER_UNROLL = 8
SPLIT_CHUNK = 5


def _inv_body(pos_ref, base_hbm, inv_ref, sem, *, tmg):
    fill = pltpu.make_async_copy(base_hbm, inv_ref, sem.at[0])
    fill.start()
    fill.wait()

    def put(i, c):
        for u in range(SCATTER_UNROLL):
            a = i * SCATTER_UNROLL + u
            inv_ref[tmg + pos_ref[a]] = a
        return c

    lax.fori_loop(0, pos_ref.shape[0] // SCATTER_UNROLL, put, 0)


def _route_inverse(pos_flat, base, tmg):
    return pl.pallas_call(
        functools.partial(_inv_body, tmg=tmg),
        in_specs=[pl.BlockSpec(memory_space=pltpu.SMEM), pl.BlockSpec(memory_space=pl.ANY)],
        out_specs=pl.BlockSpec(memory_space=pltpu.SMEM),
        out_shape=jax.ShapeDtypeStruct(base.shape, jnp.int32),
        scratch_shapes=[pltpu.SemaphoreType.DMA((1,))], name="route_inverse",
    )(pos_flat, base)


GATHER, OUT, STAGE = 0, 2, 4


def _moe_routed_body(te_ref, stage_ref, src_ref, dst_ref, xn_hbm, wg_ref, wu_ref, wd_ref, z_hbm,
                     buf, gsem, ssem, *, tmg, n_tiles):
    del te_ref
    j = pl.program_id(0)

    def row_in(entry, slot, r):
        return pltpu.make_async_copy(xn_hbm.at[pl.ds(src_ref[entry], 1), :],
                                     buf.at[GATHER + slot, pl.ds(r, 1), :], gsem.at[slot])

    def row_out(entry, slot, r):
        return pltpu.make_async_copy(buf.at[OUT + slot, pl.ds(r, 1), :],
                                     z_hbm.at[pl.ds(dst_ref[entry], 1), :], ssem.at[0])

    def tile_in(slot):
        return pltpu.make_async_copy(xn_hbm.at[pl.ds(0, tmg), :], buf.at[GATHER + slot], gsem.at[slot])

    def tile_out(slot):
        return pltpu.make_async_copy(buf.at[OUT + slot], z_hbm.at[pl.ds(0, tmg), :], ssem.at[0])

    @pl.when(j == 0)
    def _():
        buf[OUT + 1] = jnp.zeros(buf.shape[1:], F32)
        for r in range(tmg):
            row_in(tmg + r, 0, r).start()

    def step(cur):
        nxt = 1 - cur

        @pl.when(j >= 1)
        def _():
            tile_out(cur).wait()

        tile_in(cur).wait()
        w = (wg_ref.at[0, 0], wu_ref.at[0, 0], wd_ref.at[0, 0])
        head = _swiglu_acc(buf[GATHER + cur].astype(BF16), *w, E_FF, 0, SPLIT_CHUNK)
        for r in range(tmg):
            row_in((j + 2) * tmg + r, nxt, r).start()
            row_out(j * tmg + r, nxt, r).start()
        stage = stage_ref[0]
        buf[stage] = buf[GATHER + cur]
        tail = _swiglu_acc(buf[stage].astype(BF16), *w, E_FF, SPLIT_CHUNK)
        buf[OUT + cur] = head + tail

    for parity in range(2):
        pl.when(j % 2 == parity)(functools.partial(step, parity))

    assert n_tiles % 2 == 0

    @pl.when(j == n_tiles)
    def _():
        tile_out(1).wait()
        tile_in(1).wait()


def _moe_routed(xn, tile_expert, src, dst, lw, tmg, n_tiles):
    n = xn.shape[0]
    jl = lw["mixer_idx"]
    wspec = lambda a, b: pl.BlockSpec((1, 1, a, b), lambda j, te, st, s, d: (jl, te[j], 0, 0))
    grid_spec = pltpu.PrefetchScalarGridSpec(
        num_scalar_prefetch=4, grid=(n_tiles + 1,),
        in_specs=[pl.BlockSpec(memory_space=pl.ANY), wspec(D_MODEL, E_FF), wspec(D_MODEL, E_FF),
                  wspec(E_FF, D_MODEL)],
        out_specs=pl.BlockSpec(memory_space=pl.ANY),
        scratch_shapes=[pltpu.VMEM((STAGE + 1, tmg, D_MODEL), F32),
                        pltpu.SemaphoreType.DMA((2,)), pltpu.SemaphoreType.DMA((1,))])
    return pl.pallas_call(
        functools.partial(_moe_routed_body, tmg=tmg, n_tiles=n_tiles),
        grid_spec=grid_spec, out_shape=jax.ShapeDtypeStruct((2 * n + tmg, D_MODEL), F32),
        compiler_params=_params("arbitrary"), name="moe_routed",
    )(tile_expert, jnp.full((1,), STAGE, jnp.int32), src, dst, xn,
      lw["we_gate"], lw["we_up"], lw["we_down"])


def _combine_body(x2_ref, route_ref, z0_ref, z1_ref, o_ref):
    route = route_ref[...]
    lane = lax.broadcasted_iota(jnp.int32, route.shape, 1)
    g1 = jnp.sum(jnp.where(lane == 4, route, 0.0), axis=1, keepdims=True)
    g2 = jnp.sum(jnp.where(lane == 5, route, 0.0), axis=1, keepdims=True)
    o_ref[...] = x2_ref[...] + g1 * z0_ref[...] + g2 * z1_ref[...]


def _moe_combine(x2, route, z, tm):
    n = x2.shape[0]
    row = lambda w: pl.BlockSpec((tm, w), lambda i: (i, 0))
    return pl.pallas_call(
        _combine_body, grid=(n // tm,),
        in_specs=[row(D_MODEL), row(LANE), row(D_MODEL),
                  pl.BlockSpec((tm, D_MODEL), lambda i: (i + n // tm, 0))],
        out_specs=row(D_MODEL), out_shape=jax.ShapeDtypeStruct((n, D_MODEL), F32),
        compiler_params=_params("arbitrary"), name="moe_combine",
    )(x2, route, z, z)


def _moe(x2, xn, route, route_t, counts, lw, tmg):
    n = x2.shape[0]
    assert n & (n - 1) == 0
    n_tiles = 2 * n // tmg + N_EXPERTS
    ext = (n_tiles + 3) * tmg
    e = route_t[0:2].astype(jnp.int32)
    rank = route_t[2:4].astype(jnp.int32)
    cnt = counts[0, :N_EXPERTS].astype(jnp.int32)
    tiles = (cnt + tmg - 1) // tmg
    tile_end = jnp.cumsum(tiles)
    row_start = (tile_end - tiles) * tmg
    start_of = jnp.sum(jnp.where(e[..., None] == jnp.arange(N_EXPERTS), row_start, 0), axis=-1)
    pos_flat = (start_of + rank).reshape(-1)
    steps = jnp.minimum(jnp.arange(n_tiles + 1, dtype=jnp.int32), tile_end[-1] - 1)
    tile_expert = jnp.minimum(jnp.sum((steps[:, None] >= tile_end[None, :]).astype(jnp.int32), axis=1),
                              N_EXPERTS - 1)
    idx = jnp.arange(ext, dtype=jnp.int32)
    dst = _route_inverse(pos_flat, 2 * n + (idx & (tmg - 1)), tmg)
    src = jnp.where(dst < 2 * n, dst, idx) & (n - 1)
    z = _moe_routed(xn, tile_expert, src, dst, lw, tmg, n_tiles)
    return _moe_combine(x2, route, z, tmg)


def _consts():
    i = jnp.arange(MXU)
    blockdiag = lambda w: ((i[:, None] // w) == (i[None, :] // w)).astype(BF16)
    tri = lambda n: (jnp.arange(n)[None, :] <= jnp.arange(n)[:, None]).astype(BF16)
    r = jnp.arange(LANE)
    place = jnp.stack([((r[:, None] < N_FOX) & (r[None, :] == AUG_W * r[:, None] + k)).astype(BF16)
                       for k in range(6)])
    within = r % AUG_W
    headed = r < AUG_W * N_FOX
    ones_q = (headed & (within >= 3) & (within < 6)).astype(F32)
    ones_k = (headed & (within < 3)).astype(F32)
    aug_ones = jnp.zeros((8, LANE), F32).at[0].set(ones_q).at[1].set(ones_k)
    return {"s64": blockdiag(FOX_DIM), "s128": blockdiag(HEAD_PAD), "tri256": tri(MXU),
            "tri128": tri(LANE), "triu256": tri(MXU).T, "triu128": tri(LANE).T,
            "place": place, "aug_ones": aug_ones}


def _rope_tables(pos):
    half = ROPE_DIM // 2
    inv = ROPE_BASE ** (-jnp.arange(half, dtype=F32) / half)
    ang = pos.astype(F32)[:, None] * inv[None, :]
    cos, sin = jnp.cos(ang), jnp.sin(ang)
    t = pos.shape[0]
    ct = jnp.concatenate([jnp.ones((t, ROPE_LO), F32), cos, cos, jnp.zeros((t, LANE - ROPE_HI), F32)], 1)
    st = jnp.concatenate([jnp.zeros((t, ROPE_LO), F32), -sin, sin, jnp.zeros((t, LANE - ROPE_HI), F32)], 1)
    return ct, st


def _pad_lanes(v, width=LANE):
    return jnp.pad(v, [(0, 0)] * (v.ndim - 1) + [(0, width - v.shape[-1])])


def _prep_layer(i, p):
    half = ROPE_DIM // 2
    w_in = p["w_in"][i]
    s = [0, Q_LORA, Q_LORA + KV_LORA, Q_LORA + KV_LORA + ROPE_DIM]
    nfx = N_FOX * FOX_DIM
    s += [s[3] + nfx, s[3] + 2 * nfx, s[3] + 3 * nfx, s[3] + 3 * nfx + N_FOX]
    c_q, c_kv, k_rope = w_in[:, s[0]:s[1]], w_in[:, s[1]:s[2]], w_in[:, s[2]:s[3]]
    fq, fk, fv, f_logit = w_in[:, s[3]:s[4]], w_in[:, s[4]:s[5]], w_in[:, s[5]:s[6]], w_in[:, s[6]:s[7]]
    zc = lambda n: jnp.zeros((D_MODEL, n), F32)
    misc = jnp.concatenate([f_logit, zc(ROPE_LO - N_FOX), k_rope, zc(LANE - ROPE_HI)], axis=1)
    w_in_p = jnp.concatenate([c_q, c_kv, fq, fk, fv, misc], axis=1).astype(BF16)

    w_uq = p["w_uq"][i].reshape(Q_LORA, N_MLA, QK_DIM)
    nope, rope = w_uq[..., :NOPE_DIM], w_uq[..., NOPE_DIM:]
    rope_sw = jnp.concatenate([rope[..., half:], rope[..., :half]], axis=-1)
    zq = lambda n: jnp.zeros((Q_LORA, N_MLA, n), F32)
    qa = jnp.concatenate([nope, rope, zq(HEAD_PAD - QK_DIM)], axis=-1)
    qb = jnp.concatenate([zq(NOPE_DIM), rope_sw, zq(HEAD_PAD - QK_DIM)], axis=-1)
    w_uq_p = jnp.concatenate([qa.reshape(Q_LORA, -1), qb.reshape(Q_LORA, -1)], axis=1).astype(BF16)

    w_ukv = p["w_ukv"][i].reshape(KV_LORA, N_MLA, NOPE_DIM + V_DIM)
    w_k = _pad_lanes(w_ukv[..., :NOPE_DIM], HEAD_PAD).reshape(KV_LORA, -1).astype(BF16)
    w_v = w_ukv[..., NOPE_DIM:].reshape(KV_LORA, -1).astype(BF16)

    gq = _pad_lanes(p["g_qn_mla"][i] * (QK_DIM ** -0.5 * LOG2E))[None]
    gk = _pad_lanes(p["g_kn_mla"][i])[None]
    lw = {
        "g_mix": p["g_mix"][i][None], "w_in": w_in_p, "b_f": _pad_lanes(p["b_f"][i])[None],
        "g_cq": p["g_cq"][i][None], "g_ckv": p["g_ckv"][i][None], "w_uq": w_uq_p,
        "gq": gq, "gk": gk,
        "gfq": jnp.tile(p["g_qn_fox"][i] * (FOX_DIM ** -0.5 * LOG2E), N_FOX)[None],
        "gfk": jnp.tile(p["g_kn_fox"][i], N_FOX)[None],
        "w_k": w_k, "w_v": w_v,
        "w_out": p["w_out"][i].astype(BF16), "g_ffn": p["g_ffn"][i][None],
    }
    j = i // 2
    lw["mixer_idx"] = j
    if i % 2 == 0:
        lw.update(w_gate=p["w_gate_b"], w_up=p["w_up_b"], w_down=p["w_down_b"])
    else:
        wr = p["w_router"][j]
        wr_hi = wr.astype(BF16)
        wr_lo = (wr - wr_hi.astype(F32)).astype(BF16)
        lw.update(w_router=_pad_lanes(jnp.concatenate([wr_hi, wr_lo], axis=1)),
                  b_router=_pad_lanes(p["b_router"][j])[None],
                  we_gate=p["we_gate_b"], we_up=p["we_up_b"], we_down=p["we_down_b"])
    return lw


def _channel_mixer(i, x, om, of, lw, consts, tm_post, tm_ffn):
    if i % 2 == 0:
        x2, xn = _post_attn(x, om, of, lw, False, tm_post)
        return _ffn(x2, xn, lw, tm_ffn)
    x2, xn, route, route_t, counts = _post_attn(x, om, of, lw, True, MXU, consts["tri256"])
    return _moe(x2, xn, route, route_t, counts, lw, MXU)


def _trunk_prompt(x, layers, consts):
    b, t, _ = x.shape
    rows = b * t
    x = x.reshape(rows, D_MODEL)
    ct, st = _rope_tables(jnp.arange(t, dtype=jnp.int32))
    nfx = N_FOX * FOX_DIM
    stack = ()
    for i, lw in enumerate(layers):
        (ckv_s, krt_s, kr128, fkt_s, fvt_s, lf128, lft_s, q, fq, fkb, fvb) = _mixer_in(
            x, lw, ct, st, consts, 256, stack=stack, layer=i, batch=b)
        stack = (ckv_s, krt_s, fkt_s, fvt_s, lft_s)
        k, v = _kv_expand(ckv_s, kr128, lw, consts, 512, layer=i)
        aug = _fox_bias(lf128.reshape(b, t, LANE), consts)
        om = _attn_prompt(q.reshape(b, t, -1), k.reshape(b, t, -1), v.reshape(b, t, -1), None, 256)
        of = _attn_prompt(fq.reshape(b, t, nfx), fkb.reshape(b, t, nfx), fvb.reshape(b, t, nfx), aug, 256)
        x = _channel_mixer(i, x, om.reshape(rows, -1), of.reshape(rows, -1), lw, consts, 256, 512)
    ckv_s, krt_s, fkt_s, fvt_s, lft_s = stack
    heads = lambda a: jnp.transpose(a.reshape(DEPTH, b, N_FOX, FOX_DIM, t), (0, 1, 4, 2, 3))
    state = (ckv_s.reshape(DEPTH, b, t, KV_LORA), jnp.swapaxes(krt_s, 2, 3), heads(fkt_s), heads(fvt_s),
             jnp.swapaxes(lft_s, 2, 3))
    return x.reshape(b, t, D_MODEL), state


def _trunk_sample(x, past, layers, consts):
    b, t, _ = x.shape
    rows = b * t
    start = PAST_LEN
    x = x.reshape(rows, D_MODEL)
    ct, st = _rope_tables(start + jnp.arange(t, dtype=jnp.int32))
    ct, st = jnp.tile(ct, (b, 1)), jnp.tile(st, (b, 1))
    p_ckv, p_krope, p_fk, p_fv, p_logf = past
    nfx = N_FOX * FOX_DIM
    krope_t = jnp.swapaxes(p_krope, 2, 3)
    fk_t = jnp.transpose(p_fk, (0, 1, 3, 4, 2)).reshape(DEPTH, b, nfx, start)
    fv_t = jnp.transpose(p_fv, (0, 1, 3, 4, 2)).reshape(DEPTH, b, nfx, start)
    lf_t = jnp.swapaxes(p_logf, 2, 3)
    state = ([], [], [], [], [])
    for i, lw in enumerate(layers):
        ckv, kr, kr128, fk, fv, lf128, lf, q, fq, fkb, fvb = _mixer_in(x, lw, ct, st, consts, rows)
        k_c, v_c = _kv_expand_cache(p_ckv, krope_t, i, lw, consts, 512)
        k_n, v_n = _kv_expand(ckv, kr128, lw, consts, rows)
        om = _attn_mla_step(q.reshape(b, t, -1), k_c.reshape(b, start, -1), v_c.reshape(b, start, -1),
                            k_n.reshape(b, t, -1), v_n.reshape(b, t, -1), start)
        of = _attn_fox_step(fq.reshape(b, t, nfx), fk_t, fv_t, fkb.reshape(b, t, nfx),
                            fvb.reshape(b, t, nfx), lf_t, lf128.reshape(b, t, LANE), i, consts)
        x = _channel_mixer(i, x, om.reshape(rows, -1), of.reshape(rows, -1), lw, consts, rows, rows)
        for lst, s_ in zip(state, (ckv.reshape(b, t, KV_LORA), kr.reshape(b, t, ROPE_DIM),
                                   fk.reshape(b, t, N_FOX, FOX_DIM), fv.reshape(b, t, N_FOX, FOX_DIM),
                                   lf.reshape(b, t, N_FOX))):
            lst.append(s_)
    return x.reshape(b, t, D_MODEL), tuple(jnp.stack(s_) for s_ in state)


def kernel(x_prompt, x_sample, cache_mla_ckv, cache_mla_krope, cache_fox_k, cache_fox_v, cache_fox_logf, g_mix, w_in, b_f, g_cq, g_ckv, w_uq, w_ukv, g_qn_mla, g_kn_mla, g_qn_fox, g_kn_fox, w_out, g_ffn, w_gate, w_up, w_down, w_router, b_router, we_gate, we_up, we_down):
    p = dict(g_mix=g_mix, w_in=w_in, b_f=b_f, g_cq=g_cq, g_ckv=g_ckv, w_uq=w_uq, w_ukv=w_ukv,
             g_qn_mla=g_qn_mla, g_kn_mla=g_kn_mla, g_qn_fox=g_qn_fox, g_kn_fox=g_kn_fox,
             w_out=w_out, g_ffn=g_ffn, w_gate=w_gate, w_up=w_up, w_down=w_down,
             w_router=w_router, b_router=b_router, we_gate=we_gate, we_up=we_up, we_down=we_down)
    for name in ("w_gate", "w_up", "w_down", "we_gate", "we_up", "we_down"):
        p[name + "_b"] = p[name].astype(BF16)
    layers = [_prep_layer(i, p) for i in range(DEPTH)]
    consts = _consts()
    y_p, st_p = _trunk_prompt(x_prompt, layers, consts)
    past = (cache_mla_ckv, cache_mla_krope, cache_fox_k, cache_fox_v, cache_fox_logf)
    y_s, st_s = _trunk_sample(x_sample, past, layers, consts)
    return (y_p, y_s) + st_p + st_s
```

```python
import functools

import jax
import jax.numpy as jnp
from jax import lax
from jax.experimental import pallas as pl
from jax.experimental.pallas import tpu as pltpu

F32 = jnp.float32
BF16 = jnp.bfloat16

D_MODEL = 1024
DEPTH = 4
PAST_LEN = 2048
CHUNK_SHIFT = 6
EPS = 1e-6
ROPE_BASE = 10000.0

N_MLA = 8
Q_LORA = 384
KV_LORA = 256
NOPE_DIM = 64
ROPE_DIM = 32
QK_DIM = NOPE_DIM + ROPE_DIM
V_DIM = 64
N_FOX = 8
FOX_DIM = 64
D_FF = 2816
N_EXPERTS = 8
E_FF = 1792

LOG2E = 1.4426950408889634
LANE = 128
MXU = 256
HEAD_PAD = 128
ROPE_LO = NOPE_DIM
ROPE_HI = NOPE_DIM + ROPE_DIM
AUG_W = 16
GROUP = 4

CQ_LO, CQ_HI = 0, Q_LORA
CKV_LO, CKV_HI = CQ_HI, CQ_HI + KV_LORA
FQ_LO, FQ_HI = CKV_HI, CKV_HI + N_FOX * FOX_DIM
FK_LO, FK_HI = FQ_HI, FQ_HI + N_FOX * FOX_DIM
FV_LO, FV_HI = FK_HI, FK_HI + N_FOX * FOX_DIM
MISC_LO, MISC_HI = FV_HI, FV_HI + LANE
D_IN_P = MISC_HI

VMEM_LIMIT = 56 * 1024 * 1024


def _params(*sem):
    return pltpu.CompilerParams(dimension_semantics=sem, vmem_limit_bytes=VMEM_LIMIT)


def _dot(a, b):
    return jnp.dot(a, b, preferred_element_type=F32)


def _dot_nt(a, b):
    return lax.dot_general(a, b, (((1,), (1,)), ((), ())), preferred_element_type=F32)


def _rms(x, g, n):
    return x * lax.rsqrt(jnp.sum(x * x, axis=-1, keepdims=True) * (1.0 / n) + EPS) * g


def _head_norm(x, smat, dim):
    outs = []
    for c in range(0, x.shape[1], MXU):
        xc = x[:, c:c + MXU]
        ss = _dot((xc * xc).astype(BF16), smat)
        outs.append(xc * lax.rsqrt(ss * (1.0 / dim) + EPS))
    return jnp.concatenate(outs, axis=1)


def _full(shape):
    return pl.BlockSpec(shape, lambda *_: (0,) * len(shape))


def _split3(c):
    hi = c.astype(BF16)
    r1 = c - hi.astype(F32)
    mid = r1.astype(BF16)
    lo = (r1 - mid.astype(F32)).astype(BF16)
    return hi, mid, lo


def _combine(x2_ref, route_ref, z0_ref, z1_ref):
    route = route_ref[...]
    lane = lax.broadcasted_iota(jnp.int32, route.shape, 1)
    g1 = jnp.sum(jnp.where(lane == 4, route, 0.0), axis=1, keepdims=True)
    g2 = jnp.sum(jnp.where(lane == 5, route, 0.0), axis=1, keepdims=True)
    return x2_ref[...] + g1 * z0_ref[...] + g2 * z1_ref[...]


N_MIXER_W = 13


def _mixer_core(x, gmix_ref, win_ref, bf_ref, gcq_ref, gckv_ref, wuq_ref, gq_ref,
                gfq_ref, gfk_ref, ct_ref, st_ref, s64_ref, s128_ref):
    tm = x.shape[0]
    xb = _rms(x, gmix_ref[...], D_MODEL).astype(BF16)

    def proj(lo, hi):
        return _dot(xb, win_ref[:, lo:hi])

    ct = ct_ref[...]
    st = st_ref[...]
    lane = lax.broadcasted_iota(jnp.int32, (tm, LANE), 1)

    cq_raw = proj(CQ_LO, CQ_HI)
    a = proj(MISC_LO, MISC_HI)
    ckv_raw = proj(CKV_LO, CKV_HI)
    fk_raw = proj(FK_LO, FK_HI)
    fv = proj(FV_LO, FV_HI)
    fq_raw = proj(FQ_LO, FQ_HI)

    cq = _rms(cq_raw, gcq_ref[...], Q_LORA).astype(BF16)
    nq = N_MLA * HEAD_PAD
    qa = _dot(cq, wuq_ref[:, 0:nq])
    qb = _dot(cq, wuq_ref[:, nq:2 * nq])

    z = a + bf_ref[...]
    lf = jnp.minimum(z, 0.0) - jnp.log1p(jnp.exp(-jnp.abs(z)))
    lf128 = jnp.where(lane < N_FOX, lf, 0.0)
    half = ROPE_DIM // 2
    sw = jnp.where(lane < ROPE_LO + half, pltpu.roll(a, LANE - half, 1), pltpu.roll(a, half, 1))
    kr128 = jnp.where((lane >= ROPE_LO) & (lane < ROPE_HI), a * ct + sw * st, 0.0)
    ckv = _rms(ckv_raw, gckv_ref[...], KV_LORA)

    qr = jnp.concatenate(
        [qa[:, h * HEAD_PAD:(h + 1) * HEAD_PAD] * ct + qb[:, h * HEAD_PAD:(h + 1) * HEAD_PAD] * st
         for h in range(N_MLA)], axis=1)
    qn = _head_norm(qr, s128_ref[...], QK_DIM)
    gq = gq_ref[...]
    q = jnp.concatenate(
        [qn[:, h * HEAD_PAD:(h + 1) * HEAD_PAD] * gq for h in range(N_MLA)], axis=1).astype(BF16)

    s64 = s64_ref[...]
    fk = _head_norm(fk_raw, s64, FOX_DIM) * gfk_ref[...]
    fq = (_head_norm(fq_raw, s64, FOX_DIM) * gfq_ref[...]).astype(BF16)
    return ckv, kr128, lf128, fk, fv, fq, q


def _mixer_in_body(x_ref, *refs):
    (ckv_ref, kr_ref, kr128_ref, fk_ref, fv_ref, lf128_ref, lf_ref,
     q_ref, fq_ref, fkb_ref, fvb_ref) = refs[N_MIXER_W:]
    ckv, kr128, lf128, fk, fv, fq, q = _mixer_core(x_ref[...], *refs[:N_MIXER_W])
    ckv_ref[...] = ckv
    kr128_ref[...] = kr128
    kr_ref[...] = kr128[:, ROPE_LO:ROPE_HI]
    lf128_ref[...] = lf128
    lf_ref[...] = lf128[:, 0:N_FOX]
    fk_ref[...] = fk
    fv_ref[...] = fv
    fkb_ref[...] = fk.astype(BF16)
    fvb_ref[...] = fv.astype(BF16)
    fq_ref[...] = fq
    q_ref[...] = q


def _fox_aug(c, place_ref, ones_ref):
    hi, mid, lo = _split3(c * LOG2E)
    qa = _dot(hi, place_ref[0]) + _dot(mid, place_ref[1]) + _dot(lo, place_ref[2]) + ones_ref[0:1, :]
    ka = ones_ref[1:2, :] - (_dot(hi, place_ref[3]) + _dot(mid, place_ref[4]) + _dot(lo, place_ref[5]))
    return qa.astype(BF16), ka.astype(BF16)


N_STACKED = 5


def _mixer_prompt_body(*refs, first, combine, n_alias, tiles_per_seq):
    n_x = 4 if combine else 1
    n_in = n_x + N_MIXER_W + 6 + n_alias
    weights = refs[n_x:n_x + N_MIXER_W]
    wk_ref, wv_ref, gk_ref, tri_ref, place_ref, ones_ref = refs[n_x + N_MIXER_W:n_x + N_MIXER_W + 6]
    outs = list(refs[n_in:])
    carry_ref = outs.pop()
    (ckv_ref, kr_ref, fk_ref, fv_ref, lf_ref, q_ref, fq_ref, fkb_ref, fvb_ref,
     k_ref, v_ref, qa_ref, ka_ref) = outs[:13]
    if combine:
        x = _combine(*refs[:4])
        outs[13][...] = x
    else:
        x = refs[0][...]
    ckv, kr128, lf128, fk, fv, fq, q = _mixer_core(x, *weights)
    q_ref[...] = q
    fq_ref[...] = fq
    fkb_ref[...] = fk.astype(BF16)
    fvb_ref[...] = fv.astype(BF16)
    ckv_ref[0] = ckv
    kr_ref[0, 0] = kr128.T[ROPE_LO:ROPE_HI, :]
    lf_ref[0, 0] = lf128.T[0:N_FOX, :]
    fk_ref[0, 0] = fk.T
    fv_ref[0, 0] = fv.T
    if first:
        for ref in (ckv_ref, kr_ref, lf_ref, fk_ref, fv_ref):
            ref[1:] = jnp.zeros((ref.shape[0] - 1,) + ref.shape[1:], F32)

    _kv_emit(ckv, kr128, wk_ref, wv_ref, gk_ref, weights[-1], k_ref, v_ref)

    @pl.when(pl.program_id(0) % tiles_per_seq == 0)
    def _():
        carry_ref[...] = jnp.zeros(carry_ref.shape, F32)

    hi, mid, lo = _split3(lf128)
    tri = tri_ref[...]
    c = _dot(tri, hi) + _dot(tri, mid) + _dot(tri, lo) + carry_ref[0:1, :]
    carry_ref[...] = jnp.broadcast_to(c[c.shape[0] - 1:, :], carry_ref.shape)
    qa, ka = _fox_aug(c, place_ref, ones_ref)
    qa_ref[...] = qa
    ka_ref[...] = ka


def _mixer_specs(tm, nt):
    nfx = N_FOX * FOX_DIM
    tab = pl.BlockSpec((tm, LANE), lambda i: (i % nt, 0))
    return [_full((1, D_MODEL)), _full((D_MODEL, D_IN_P)), _full((1, LANE)),
            _full((1, Q_LORA)), _full((1, KV_LORA)), _full((Q_LORA, 2 * N_MLA * HEAD_PAD)),
            _full((1, LANE)), _full((1, nfx)), _full((1, nfx)), tab, tab,
            _full((MXU, MXU)), _full((MXU, MXU))]


def _mixer_args(lw, ct, st, consts):
    return [lw["g_mix"], lw["w_in"], lw["b_f"], lw["g_cq"], lw["g_ckv"], lw["w_uq"], lw["gq"],
            lw["gfq"], lw["gfk"], ct, st, consts["s64"], consts["s128"]]


def _mixer_in(x, lw, ct, st, consts, tm):
    rows = x.shape[0]
    nt = ct.shape[0] // tm
    row = lambda w: pl.BlockSpec((tm, w), lambda i: (i, 0))
    nfx = N_FOX * FOX_DIM
    sds = jax.ShapeDtypeStruct
    outs = [(sds((rows, KV_LORA), F32), row(KV_LORA)), (sds((rows, ROPE_DIM), F32), row(ROPE_DIM)),
            (sds((rows, LANE), F32), row(LANE)), (sds((rows, nfx), F32), row(nfx)),
            (sds((rows, nfx), F32), row(nfx)), (sds((rows, LANE), F32), row(LANE)),
            (sds((rows, N_FOX), F32), row(N_FOX)),
            (sds((rows, N_MLA * HEAD_PAD), BF16), row(N_MLA * HEAD_PAD)),
            (sds((rows, nfx), BF16), row(nfx)), (sds((rows, nfx), BF16), row(nfx)),
            (sds((rows, nfx), BF16), row(nfx))]
    args = _mixer_args(lw, ct, st, consts)
    assert len(args) == N_MIXER_W
    return pl.pallas_call(
        _mixer_in_body, grid=(rows // tm,), in_specs=[row(D_MODEL)] + _mixer_specs(tm, nt),
        out_specs=tuple(o[1] for o in outs), out_shape=tuple(o[0] for o in outs),
        compiler_params=_params("arbitrary"), name="mixer_in",
    )(x, *args)


def _mixer_in_prompt(x, lw, ct, st, consts, stack, layer, batch, combine=None):
    tm = MXU
    rows = (x if combine is None else combine[0]).shape[0]
    t = rows // batch
    nt = t // tm
    first = not stack
    assert first == (layer == 0)
    row = lambda w: pl.BlockSpec((tm, w), lambda i: (i, 0))
    nfx = N_FOX * FOX_DIM
    nk = N_MLA * HEAD_PAD
    nv = N_MLA * V_DIM
    sds = jax.ShapeDtypeStruct
    nd = DEPTH if first else 1
    tmaj = lambda w: pl.BlockSpec((nd, 1, w, tm), lambda i: (layer, i // nt, 0, i % nt))
    outs = [(sds((DEPTH, rows, KV_LORA), F32), pl.BlockSpec((nd, tm, KV_LORA), lambda i: (layer, i, 0))),
            (sds((DEPTH, batch, ROPE_DIM, t), F32), tmaj(ROPE_DIM)),
            (sds((DEPTH, batch, nfx, t), F32), tmaj(nfx)),
            (sds((DEPTH, batch, nfx, t), F32), tmaj(nfx)),
            (sds((DEPTH, batch, N_FOX, t), F32), tmaj(N_FOX)),
            (sds((rows, nk), BF16), row(nk)), (sds((rows, nfx), BF16), row(nfx)),
            (sds((rows, nfx), BF16), row(nfx)), (sds((rows, nfx), BF16), row(nfx)),
            (sds((rows, nk), BF16), row(nk)), (sds((rows, nv), BF16), row(nv)),
            (sds((rows, LANE), BF16), row(LANE)), (sds((rows, LANE), BF16), row(LANE))]
    if combine is None:
        in_specs, args = [row(D_MODEL)], [x]
    else:
        x2, route, z = combine
        in_specs = [row(D_MODEL), row(LANE), row(D_MODEL),
                    pl.BlockSpec((tm, D_MODEL), lambda i: (i + rows // tm, 0))]
        args = [x2, route, z, z]
        outs.append((sds((rows, D_MODEL), F32), row(D_MODEL)))
    in_specs += _mixer_specs(tm, nt) + [_full((KV_LORA, nk)), _full((KV_LORA, nv)), _full((1, LANE)),
                                        _full((MXU, MXU)), _full((6, LANE, LANE)), _full((8, LANE))]
    args += _mixer_args(lw, ct, st, consts) + [lw["w_k"], lw["w_v"], lw["gk"], consts["tri256"],
                                                consts["place"], consts["aug_ones"]]
    aliases = {}
    if stack:
        aliases = {len(args) + n: n for n in range(N_STACKED)}
        in_specs += [pl.BlockSpec(memory_space=pl.ANY)] * N_STACKED
        args += list(stack)
    return pl.pallas_call(
        functools.partial(_mixer_prompt_body, first=first, combine=combine is not None,
                          n_alias=len(stack), tiles_per_seq=nt),
        grid=(rows // tm,), in_specs=in_specs,
        out_specs=tuple(o[1] for o in outs), out_shape=tuple(o[0] for o in outs),
        scratch_shapes=[pltpu.VMEM((8, LANE), F32)],
        input_output_aliases=aliases, compiler_params=_params("arbitrary"), name="mixer_in_prompt",
    )(*args)


def _kv_emit(ckv, kr, wk_ref, wv_ref, gk_ref, s128_ref, k_ref, v_ref):
    cb = ckv.astype(BF16)
    kn = _dot(cb, wk_ref[...])
    kk = jnp.concatenate(
        [kn[:, h * HEAD_PAD:(h + 1) * HEAD_PAD] + kr for h in range(N_MLA)], axis=1)
    kk = _head_norm(kk, s128_ref[...], QK_DIM)
    gk = gk_ref[...]
    k_ref[...] = jnp.concatenate(
        [kk[:, h * HEAD_PAD:(h + 1) * HEAD_PAD] * gk for h in range(N_MLA)], axis=1).astype(BF16)
    v_ref[...] = _dot(cb, wv_ref[...]).astype(BF16)


def _kv_body(ckv_ref, kr128_ref, *rest):
    _kv_emit(ckv_ref[...], kr128_ref[...], *rest)


def _kv_cache_body(ckv_ref, krt_ref, *rest):
    tm = krt_ref.shape[-1]
    pad = lambda n: jnp.zeros((n, tm), F32)
    kr128 = jnp.concatenate([pad(ROPE_LO), krt_ref[0, 0], pad(LANE - ROPE_HI)], axis=0).T
    _kv_emit(ckv_ref[0, 0], kr128, *rest)


def _kv_expand_cache(ckv_all, krope_t, layer, lw, consts, tm):
    _, b, t, _ = ckv_all.shape
    nt = t // tm
    row = lambda w: pl.BlockSpec((tm, w), lambda bi, ti: (bi * nt + ti, 0))
    const = lambda shape: pl.BlockSpec(shape, lambda bi, ti: (0,) * len(shape))
    nk = N_MLA * HEAD_PAD
    nv = N_MLA * V_DIM
    return pl.pallas_call(
        _kv_cache_body, grid=(b, nt),
        in_specs=[pl.BlockSpec((1, 1, tm, KV_LORA), lambda bi, ti: (layer, bi, ti, 0)),
                  pl.BlockSpec((1, 1, ROPE_DIM, tm), lambda bi, ti: (layer, bi, 0, ti)),
                  const((KV_LORA, nk)), const((KV_LORA, nv)), const((1, LANE)), const((MXU, MXU))],
        out_specs=(row(nk), row(nv)),
        out_shape=(jax.ShapeDtypeStruct((b * t, nk), BF16), jax.ShapeDtypeStruct((b * t, nv), BF16)),
        compiler_params=_params("arbitrary", "arbitrary"), name="kv_expand_cache",
    )(ckv_all, krope_t, lw["w_k"], lw["w_v"], lw["gk"], consts["s128"])


def _kv_expand(ckv, kr128, lw, consts, tm):
    rows = kr128.shape[0]
    row = lambda w: pl.BlockSpec((tm, w), lambda i: (i, 0))
    nk = N_MLA * HEAD_PAD
    nv = N_MLA * V_DIM
    return pl.pallas_call(
        _kv_body, grid=(rows // tm,),
        in_specs=[row(KV_LORA), row(LANE), _full((KV_LORA, nk)), _full((KV_LORA, nv)),
                  _full((1, LANE)), _full((MXU, MXU))],
        out_specs=(row(nk), row(nv)),
        out_shape=(jax.ShapeDtypeStruct((rows, nk), BF16), jax.ShapeDtypeStruct((rows, nv), BF16)),
        compiler_params=_params("arbitrary"), name="kv_expand",
    )(ckv, kr128, lw["w_k"], lw["w_v"], lw["gk"], consts["s128"])


def _attn_prompt_body(*refs, fox, tq):
    n_in = 5 if fox else 3
    ins, (o_ref, o_scr) = refs[:n_in], refs[n_in:]
    q_ref, k_ref, v_ref = ins[:3]
    h = pl.program_id(1)
    hh = h % 2
    t = k_ref.shape[1]
    lane = lax.broadcasted_iota(jnp.int32, (tq, LANE), 1)
    r = lax.broadcasted_iota(jnp.int32, (tq, tq), 0)
    c = lax.broadcasted_iota(jnp.int32, (tq, tq), 1)
    allowed = (c <= r) if fox else ((c >> CHUNK_SHIFT) <= (r >> CHUNK_SHIFT))
    def scores(i):
        lo, hi = i * tq, (i + 1) * tq
        if fox:
            qa_ref, ka_ref = ins[3:]
            qm = jnp.where((lane >> 6) == hh, q_ref[0, lo:hi, :].astype(F32), 0.0)
            qa = jnp.where((lane >> 4) == h, qa_ref[0, lo:hi, :].astype(F32), 0.0)
            q = jnp.concatenate([qm, qa], axis=1).astype(BF16)
            k = jnp.concatenate([k_ref[0, 0:hi, :], ka_ref[0, 0:hi, :]], axis=1)
        else:
            q = q_ref[0, lo:hi, :]
            k = k_ref[0, 0:hi, :]
        s = _dot_nt(q, k)
        sd = jnp.where(allowed, s[:, lo:hi], -jnp.inf)
        return sd if i == 0 else jnp.concatenate([s[:, 0:lo], sd], axis=1)

    def softmax(s):
        p = jnp.exp2(s - jnp.max(s, axis=1, keepdims=True))
        return p.astype(BF16), jnp.sum(p, axis=1, keepdims=True)

    def values(i, p, l):
        lo, hi = i * tq, (i + 1) * tq
        o_scr[lo:hi, :] = _dot(p, v_ref[0, 0:hi, :]) / l

    order = list(reversed(range(t // tq)))
    for g in range(0, len(order), GROUP):
        blocks = order[g:g + GROUP]
        ss = [scores(i) for i in blocks]
        pl_ = [softmax(s) for s in ss]
        for i, (p, l) in zip(blocks, pl_):
            values(i, p, l)

    @pl.when(hh == 0)
    def _():
        o_ref[0] = o_scr[...].astype(BF16)

    @pl.when(hh == 1)
    def _():
        lane_t = lax.broadcasted_iota(jnp.int32, (t, LANE), 1)
        o_ref[0] = jnp.where(lane_t < V_DIM, o_ref[0].astype(F32), o_scr[...]).astype(BF16)


def _attn_prompt(q, k, v, aug, tq):
    fox = aug is not None
    b, t, _ = q.shape
    blk = lambda f: pl.BlockSpec((1, t, LANE), f)
    per_head = lambda bi, h: (bi, 0, h)
    per_pair = lambda bi, h: (bi, 0, h // 2)
    shared = lambda bi, h: (bi, 0, 0)
    qk = per_pair if fox else per_head
    in_specs = [blk(qk), blk(qk), blk(per_pair)]
    args = [q, k, v]
    if fox:
        in_specs += [blk(shared), blk(shared)]
        args += list(aug)
    return pl.pallas_call(
        functools.partial(_attn_prompt_body, fox=fox, tq=tq),
        grid=(b, N_MLA), in_specs=in_specs, out_specs=blk(per_pair),
        out_shape=jax.ShapeDtypeStruct((b, t, N_MLA * V_DIM), BF16),
        scratch_shapes=[pltpu.VMEM((t, LANE), F32)],
        compiler_params=_params("arbitrary", "arbitrary"),
        name="attn_fox" if fox else "attn_mla",
    )(*args)


def _pad_rows(x, rows):
    return jnp.concatenate([x, jnp.zeros((rows - x.shape[0],) + x.shape[1:], x.dtype)], axis=0)


def _step_softmax(s_c, s_n):
    m = jnp.maximum(jnp.max(s_c, axis=1, keepdims=True), jnp.max(s_n, axis=1, keepdims=True))
    p_c = jnp.exp2(s_c - m)
    p_n = jnp.exp2(s_n - m)
    l = jnp.sum(p_c, axis=1, keepdims=True) + jnp.sum(p_n, axis=1, keepdims=True)
    return p_c.astype(BF16), p_n.astype(BF16), l


def _attn_mla_step_body(q_ref, kc_ref, vc_ref, kn_ref, vn_ref, o_ref, *, start):
    tq = q_ref.shape[1]
    tc = kc_ref.shape[1]
    lane = lax.broadcasted_iota(jnp.int32, (tq, LANE), 1)
    chunk = lambda pos: pos >> CHUNK_SHIFT
    qc = chunk(start + lax.broadcasted_iota(jnp.int32, (tq, 1), 0))
    ok_c = chunk(lax.broadcasted_iota(jnp.int32, (tq, tc), 1)) <= qc
    ok_n = (lane < tq) & (chunk(start + lane) <= qc)
    for p in range(N_MLA // 2):
        pair = slice(LANE * p, LANE * (p + 1))
        vc = vc_ref[0, :, pair]
        vn = _pad_rows(vn_ref[0, :, pair], LANE)
        outs = []
        for h in (2 * p, 2 * p + 1):
            head = slice(HEAD_PAD * h, HEAD_PAD * (h + 1))
            q = q_ref[0, :, head]
            s_c = jnp.where(ok_c, _dot_nt(q, kc_ref[0, :, head]), -jnp.inf)
            s_n = jnp.where(ok_n, _dot_nt(q, _pad_rows(kn_ref[0, :, head], LANE)), -jnp.inf)
            p_c, p_n, l = _step_softmax(s_c, s_n)
            outs.append((_dot(p_c, vc) + _dot(p_n, vn)) / l)
        o_ref[0, :, pair] = jnp.where(lane < V_DIM, outs[0], outs[1]).astype(BF16)


def _attn_mla_step(q, kc, vc, kn, vn, start):
    b, tq, _ = q.shape
    blk = lambda a: pl.BlockSpec((1,) + a.shape[1:], lambda bi: (bi, 0, 0))
    return pl.pallas_call(
        functools.partial(_attn_mla_step_body, start=start), grid=(b,),
        in_specs=[blk(q), blk(kc), blk(vc), blk(kn), blk(vn)],
        out_specs=pl.BlockSpec((1, tq, N_MLA * V_DIM), lambda bi: (bi, 0, 0)),
        out_shape=jax.ShapeDtypeStruct((b, tq, N_MLA * V_DIM), BF16),
        compiler_params=_params("arbitrary"), name="attn_mla_step",
    )(q, kc, vc, kn, vn)


def _lane_cumsum(x, triu):
    rows, blk = x.shape[0], triu.shape[0]
    carry = jnp.zeros((rows, 1), F32)
    out = []
    for b0 in range(0, x.shape[1], blk):
        hi, mid, lo = _split3(_pad_rows(x[:, b0:b0 + blk], 16))
        c = (_dot(hi, triu) + _dot(mid, triu) + _dot(lo, triu))[0:rows] + carry
        carry = c[:, blk - 1:blk]
        out.append(c)
    return jnp.concatenate(out, axis=1), carry


def _attn_fox_step_body(q_ref, kt_ref, vt_ref, kn_ref, vn_ref, lft_ref, lfn_ref,
                        triu_ref, tri_ref, triu_s_ref, o_ref):
    tq = q_ref.shape[1]
    lane = lax.broadcasted_iota(jnp.int32, (tq, LANE), 1)
    causal = lane <= lax.broadcasted_iota(jnp.int32, (tq, LANE), 0)
    cc, total = _lane_cumsum(lft_ref[0, 0], triu_ref[...])
    lfn = _pad_rows(lfn_ref[0], LANE)
    hi, mid, lo = _split3(lfn)
    tri = tri_ref[...]
    cn = (_dot(tri, hi) + _dot(tri, mid) + _dot(tri, lo))[0:tq]
    cnt, _ = _lane_cumsum(lfn.T[0:N_FOX, :], triu_s_ref[...])
    for p in range(N_FOX // 2):
        pair = slice(LANE * p, LANE * (p + 1))
        kt = kt_ref[0, 0, pair, :].astype(BF16)
        vt = vt_ref[0, 0, pair, :].astype(BF16)
        kn = _pad_rows(kn_ref[0, :, pair], LANE)
        vn = _pad_rows(vn_ref[0, :, pair], LANE)
        qf = q_ref[0, :, pair].astype(F32)
        outs = []
        for hh in range(2):
            h = 2 * p + hh
            q = jnp.where((lane >> 6) == hh, qf, 0.0).astype(BF16)
            col = jnp.sum(jnp.where(lane == h, cn, 0.0), axis=1, keepdims=True)
            bias_c = ((total[h:h + 1, :] + col) - cc[h:h + 1, :]) * LOG2E
            bias_n = (col - cnt[h:h + 1, :]) * LOG2E
            s_c = _dot(q, kt) + bias_c
            s_n = jnp.where(causal, _dot_nt(q, kn) + bias_n, -jnp.inf)
            p_c, p_n, l = _step_softmax(s_c, s_n)
            outs.append((_dot_nt(p_c, vt) + _dot(p_n, vn)) / l)
        o_ref[0, :, pair] = jnp.where(lane < V_DIM, outs[0], outs[1]).astype(BF16)


def _attn_fox_step(q, kt_all, vt_all, kn, vn, lft_all, lfn, layer, consts):
    b, tq, _ = q.shape
    new = lambda a: pl.BlockSpec((1,) + a.shape[1:], lambda bi: (bi, 0, 0))
    old = lambda a: pl.BlockSpec((1, 1) + a.shape[2:], lambda bi: (layer, bi, 0, 0))
    return pl.pallas_call(
        _attn_fox_step_body, grid=(b,),
        in_specs=[new(q), old(kt_all), old(vt_all), new(kn), new(vn), old(lft_all), new(lfn),
                  _full((MXU, MXU)), _full((LANE, LANE)), _full((LANE, LANE))],
        out_specs=pl.BlockSpec((1, tq, N_FOX * FOX_DIM), lambda bi: (bi, 0, 0)),
        out_shape=jax.ShapeDtypeStruct((b, tq, N_FOX * FOX_DIM), BF16),
        compiler_params=_params("arbitrary"), name="attn_fox_step",
    )(q, kt_all, vt_all, kn, vn, lft_all, lfn, consts["triu256"], consts["tri128"], consts["triu128"])


def _post_attn_body(*refs, moe):
    if moe:
        (x_ref, om_ref, of_ref, wo_ref, gffn_ref, wr_ref, br_ref, tri_ref,
         x2_ref, xn_ref, route_ref, routet_ref, cnt_ref, carry_ref) = refs
    else:
        x_ref, om_ref, of_ref, wo_ref, gffn_ref, x2_ref, xn_ref = refs
    nm = N_MLA * V_DIM
    x2 = x_ref[...] + _dot(om_ref[...], wo_ref[0:nm, :]) + _dot(of_ref[...], wo_ref[nm:, :])
    x2_ref[...] = x2
    xn = _rms(x2, gffn_ref[...], D_MODEL)
    xh = xn.astype(BF16)
    xn_ref[...] = xn if moe else xh
    if moe:
        xl = (xn - xh.astype(F32)).astype(BF16)
        r = _dot(xh, wr_ref[...]) + _dot(xl, wr_ref[...])
        logits = r + pltpu.roll(r, LANE - N_EXPERTS, 1) + br_ref[...]
        lane = lax.broadcasted_iota(jnp.int32, logits.shape, 1).astype(F32)
        valid = lane < N_EXPERTS
        lg = jnp.where(valid, logits, -jnp.inf)
        e = jnp.exp(lg - jnp.max(lg, axis=1, keepdims=True))
        probs = e / jnp.sum(e, axis=1, keepdims=True)
        p1 = jnp.where(valid, probs, -1.0)
        m1 = jnp.max(p1, axis=1, keepdims=True)
        i1 = jnp.min(jnp.where(p1 == m1, lane, float(LANE)), axis=1, keepdims=True)
        p2 = jnp.where(lane == i1, -1.0, p1)
        m2 = jnp.max(p2, axis=1, keepdims=True)
        i2 = jnp.min(jnp.where(p2 == m2, lane, float(LANE)), axis=1, keepdims=True)
        den = m1 + m2
        @pl.when(pl.program_id(0) == 0)
        def _():
            carry_ref[...] = jnp.zeros(carry_ref.shape, F32)

        sel = jnp.where((lane == i1) | (lane == i2), 1.0, 0.0)
        incl = _dot(tri_ref[...], sel.astype(BF16)) + carry_ref[0:1, :]
        excl = incl - sel
        rank1 = jnp.sum(jnp.where(lane == i1, excl, 0.0), axis=1, keepdims=True)
        rank2 = jnp.sum(jnp.where(lane == i2, excl, 0.0), axis=1, keepdims=True)
        total = incl[incl.shape[0] - 1:, :]
        carry_ref[...] = jnp.broadcast_to(total, carry_ref.shape)
        cnt_ref[...] = jnp.broadcast_to(total, cnt_ref.shape)
        cols = (i1, i2, rank1, rank2, m1 / den, m2 / den)
        route = jnp.zeros(logits.shape, F32)
        for n, col in enumerate(cols):
            route = jnp.where(lane == n, col, route)
        route_ref[...] = route
        routet_ref[...] = route.T[0:8, :]


def _post_attn(x, om, of, lw, moe, tm, tri=None):
    rows = x.shape[0]
    row = lambda w: pl.BlockSpec((tm, w), lambda i: (i, 0))
    nm = N_MLA * V_DIM
    in_specs = [row(D_MODEL), row(nm), row(nm), _full((2 * nm, D_MODEL)), _full((1, D_MODEL))]
    args = [x, om, of, lw["w_out"], lw["g_ffn"]]
    out_specs = [row(D_MODEL), row(D_MODEL)]
    out_shape = [jax.ShapeDtypeStruct((rows, D_MODEL), F32),
                 jax.ShapeDtypeStruct((rows, D_MODEL), F32 if moe else BF16)]
    scratch = []
    if moe:
        assert tm == MXU
        in_specs += [_full((D_MODEL, LANE)), _full((1, LANE)), _full((MXU, MXU))]
        args += [lw["w_router"], lw["b_router"], tri]
        out_specs += [row(LANE), pl.BlockSpec((8, tm), lambda i: (0, i)), _full((8, LANE))]
        out_shape += [jax.ShapeDtypeStruct((rows, LANE), F32), jax.ShapeDtypeStruct((8, rows), F32),
                      jax.ShapeDtypeStruct((8, LANE), F32)]
        scratch = [pltpu.VMEM((8, LANE), F32)]
    return pl.pallas_call(
        functools.partial(_post_attn_body, moe=moe), grid=(rows // tm,),
        in_specs=in_specs, out_specs=tuple(out_specs), out_shape=tuple(out_shape),
        scratch_shapes=scratch,
        compiler_params=_params("arbitrary"), name="post_attn_moe" if moe else "post_attn",
    )(*args)


def _swiglu_acc(xb, wg_ref, wu_ref, wd_ref, d_ff, lo=0, hi=None):
    acc = None
    for c in range(lo * MXU, d_ff if hi is None else hi * MXU, MXU):
        g = _dot(xb, wg_ref[:, c:c + MXU])
        u = _dot(xb, wu_ref[:, c:c + MXU])
        h = (g * jax.nn.sigmoid(g) * u).astype(BF16)
        d = _dot(h, wd_ref[c:c + MXU, :])
        acc = d if acc is None else acc + d
    return acc


def _ffn_body(x2_ref, xn_ref, wg_ref, wu_ref, wd_ref, o_ref):
    o_ref[...] = x2_ref[...] + _swiglu_acc(xn_ref[...], wg_ref.at[0], wu_ref.at[0], wd_ref.at[0], D_FF)


def _ffn(x2, xn, lw, tm):
    rows = x2.shape[0]
    row = pl.BlockSpec((tm, D_MODEL), lambda i: (i, 0))
    jl = lw["mixer_idx"]
    wspec = lambda a, b: pl.BlockSpec((1, a, b), lambda i: (jl, 0, 0))
    return pl.pallas_call(
        _ffn_body, grid=(rows // tm,),
        in_specs=[row, row, wspec(D_MODEL, D_FF), wspec(D_MODEL, D_FF), wspec(D_FF, D_MODEL)],
        out_specs=row, out_shape=jax.ShapeDtypeStruct((rows, D_MODEL), F32),
        compiler_params=_params("arbitrary"), name="ffn_dense",
    )(x2, xn, lw["w_gate"], lw["w_up"], lw["w_down"])


SCATTER_UNROLL = 8
SPLIT_CHUNK = 5


def _inv_body(pos_ref, base_hbm, inv_ref, sem, *, tmg):
    fill = pltpu.make_async_copy(base_hbm, inv_ref, sem.at[0])
    fill.start()
    fill.wait()

    def put(i, c):
        for u in range(SCATTER_UNROLL):
            a = i * SCATTER_UNROLL + u
            inv_ref[tmg + pos_ref[a]] = a
        return c

    lax.fori_loop(0, pos_ref.shape[0] // SCATTER_UNROLL, put, 0)


def _route_inverse(pos_flat, base, tmg):
    return pl.pallas_call(
        functools.partial(_inv_body, tmg=tmg),
        in_specs=[pl.BlockSpec(memory_space=pltpu.SMEM), pl.BlockSpec(memory_space=pl.ANY)],
        out_specs=pl.BlockSpec(memory_space=pltpu.SMEM),
        out_shape=jax.ShapeDtypeStruct(base.shape, jnp.int32),
        scratch_shapes=[pltpu.SemaphoreType.DMA((1,))], name="route_inverse",
    )(pos_flat, base)


GATHER, OUT, STAGE = 0, 2, 4


def _moe_routed_body(te_ref, stage_ref, src_ref, dst_ref, xn_hbm, wg_ref, wu_ref, wd_ref, z_hbm,
                     buf, gsem, ssem, *, tmg, n_tiles):
    del te_ref
    j = pl.program_id(0)

    def row_in(entry, slot, r):
        return pltpu.make_async_copy(xn_hbm.at[pl.ds(src_ref[entry], 1), :],
                                     buf.at[GATHER + slot, pl.ds(r, 1), :], gsem.at[slot])

    def row_out(entry, slot, r):
        return pltpu.make_async_copy(buf.at[OUT + slot, pl.ds(r, 1), :],
                                     z_hbm.at[pl.ds(dst_ref[entry], 1), :], ssem.at[0])

    def tile_in(slot):
        return pltpu.make_async_copy(xn_hbm.at[pl.ds(0, tmg), :], buf.at[GATHER + slot], gsem.at[slot])

    def tile_out(slot):
        return pltpu.make_async_copy(buf.at[OUT + slot], z_hbm.at[pl.ds(0, tmg), :], ssem.at[0])

    @pl.when(j == 0)
    def _():
        buf[OUT + 1] = jnp.zeros(buf.shape[1:], F32)
        for r in range(tmg):
            row_in(tmg + r, 0, r).start()

    def step(cur):
        nxt = 1 - cur

        @pl.when(j >= 1)
        def _():
            tile_out(cur).wait()

        tile_in(cur).wait()
        w = (wg_ref.at[0, 0], wu_ref.at[0, 0], wd_ref.at[0, 0])
        head = _swiglu_acc(buf[GATHER + cur].astype(BF16), *w, E_FF, 0, SPLIT_CHUNK)
        for r in range(tmg):
            row_in((j + 2) * tmg + r, nxt, r).start()
            row_out(j * tmg + r, nxt, r).start()
        stage = stage_ref[0]
        buf[stage] = buf[GATHER + cur]
        tail = _swiglu_acc(buf[stage].astype(BF16), *w, E_FF, SPLIT_CHUNK)
        buf[OUT + cur] = head + tail

    for parity in range(2):
        pl.when(j % 2 == parity)(functools.partial(step, parity))

    assert n_tiles % 2 == 0

    @pl.when(j == n_tiles)
    def _():
        tile_out(1).wait()
        tile_in(1).wait()


def _moe_routed(xn, tile_expert, src, dst, lw, tmg, n_tiles):
    n = xn.shape[0]
    jl = lw["mixer_idx"]
    wspec = lambda a, b: pl.BlockSpec((1, 1, a, b), lambda j, te, st, s, d: (jl, te[j], 0, 0))
    grid_spec = pltpu.PrefetchScalarGridSpec(
        num_scalar_prefetch=4, grid=(n_tiles + 1,),
        in_specs=[pl.BlockSpec(memory_space=pl.ANY), wspec(D_MODEL, E_FF), wspec(D_MODEL, E_FF),
                  wspec(E_FF, D_MODEL)],
        out_specs=pl.BlockSpec(memory_space=pl.ANY),
        scratch_shapes=[pltpu.VMEM((STAGE + 1, tmg, D_MODEL), F32),
                        pltpu.SemaphoreType.DMA((2,)), pltpu.SemaphoreType.DMA((1,))])
    return pl.pallas_call(
        functools.partial(_moe_routed_body, tmg=tmg, n_tiles=n_tiles),
        grid_spec=grid_spec, out_shape=jax.ShapeDtypeStruct((2 * n + tmg, D_MODEL), F32),
        compiler_params=_params("arbitrary"), name="moe_routed",
    )(tile_expert, jnp.full((1,), STAGE, jnp.int32), src, dst, xn,
      lw["we_gate"], lw["we_up"], lw["we_down"])


def _combine_body(x2_ref, route_ref, z0_ref, z1_ref, o_ref):
    o_ref[...] = _combine(x2_ref, route_ref, z0_ref, z1_ref)


def _moe_combine(x2, route, z, tm):
    n = x2.shape[0]
    row = lambda w: pl.BlockSpec((tm, w), lambda i: (i, 0))
    return pl.pallas_call(
        _combine_body, grid=(n // tm,),
        in_specs=[row(D_MODEL), row(LANE), row(D_MODEL),
                  pl.BlockSpec((tm, D_MODEL), lambda i: (i + n // tm, 0))],
        out_specs=row(D_MODEL), out_shape=jax.ShapeDtypeStruct((n, D_MODEL), F32),
        compiler_params=_params("arbitrary"), name="moe_combine",
    )(x2, route, z, z)


def _moe(x2, xn, route, route_t, counts, lw, tmg, defer_combine=False):
    n = x2.shape[0]
    assert n & (n - 1) == 0
    n_tiles = 2 * n // tmg + N_EXPERTS
    ext = (n_tiles + 3) * tmg
    e = route_t[0:2].astype(jnp.int32)
    rank = route_t[2:4].astype(jnp.int32)
    cnt = counts[0, :N_EXPERTS].astype(jnp.int32)
    tiles = (cnt + tmg - 1) // tmg
    tile_end = jnp.cumsum(tiles)
    row_start = (tile_end - tiles) * tmg
    start_of = jnp.sum(jnp.where(e[..., None] == jnp.arange(N_EXPERTS), row_start, 0), axis=-1)
    pos_flat = (start_of + rank).reshape(-1)
    steps = jnp.minimum(jnp.arange(n_tiles + 1, dtype=jnp.int32), tile_end[-1] - 1)
    tile_expert = jnp.minimum(jnp.sum((steps[:, None] >= tile_end[None, :]).astype(jnp.int32), axis=1),
                              N_EXPERTS - 1)
    idx = jnp.arange(ext, dtype=jnp.int32)
    dst = _route_inverse(pos_flat, 2 * n + (idx & (tmg - 1)), tmg)
    src = jnp.where(dst < 2 * n, dst, idx) & (n - 1)
    z = _moe_routed(xn, tile_expert, src, dst, lw, tmg, n_tiles)
    return (x2, route, z) if defer_combine else _moe_combine(x2, route, z, tmg)


def _consts():
    i = jnp.arange(MXU)
    blockdiag = lambda w: ((i[:, None] // w) == (i[None, :] // w)).astype(BF16)
    tri = lambda n: (jnp.arange(n)[None, :] <= jnp.arange(n)[:, None]).astype(BF16)
    r = jnp.arange(LANE)
    place = jnp.stack([((r[:, None] < N_FOX) & (r[None, :] == AUG_W * r[:, None] + k)).astype(BF16)
                       for k in range(6)])
    within = r % AUG_W
    headed = r < AUG_W * N_FOX
    ones_q = (headed & (within >= 3) & (within < 6)).astype(F32)
    ones_k = (headed & (within < 3)).astype(F32)
    aug_ones = jnp.zeros((8, LANE), F32).at[0].set(ones_q).at[1].set(ones_k)
    return {"s64": blockdiag(FOX_DIM), "s128": blockdiag(HEAD_PAD), "tri256": tri(MXU),
            "tri128": tri(LANE), "triu256": tri(MXU).T, "triu128": tri(LANE).T,
            "place": place, "aug_ones": aug_ones}


def _rope_tables(pos):
    half = ROPE_DIM // 2
    inv = ROPE_BASE ** (-jnp.arange(half, dtype=F32) / half)
    ang = pos.astype(F32)[:, None] * inv[None, :]
    cos, sin = jnp.cos(ang), jnp.sin(ang)
    t = pos.shape[0]
    ct = jnp.concatenate([jnp.ones((t, ROPE_LO), F32), cos, cos, jnp.zeros((t, LANE - ROPE_HI), F32)], 1)
    st = jnp.concatenate([jnp.zeros((t, ROPE_LO), F32), -sin, sin, jnp.zeros((t, LANE - ROPE_HI), F32)], 1)
    return ct, st


def _pad_lanes(v, width=LANE):
    return jnp.pad(v, [(0, 0)] * (v.ndim - 1) + [(0, width - v.shape[-1])])


def _prep_layer(i, p):
    half = ROPE_DIM // 2
    w_in = p["w_in"][i]
    s = [0, Q_LORA, Q_LORA + KV_LORA, Q_LORA + KV_LORA + ROPE_DIM]
    nfx = N_FOX * FOX_DIM
    s += [s[3] + nfx, s[3] + 2 * nfx, s[3] + 3 * nfx, s[3] + 3 * nfx + N_FOX]
    c_q, c_kv, k_rope = w_in[:, s[0]:s[1]], w_in[:, s[1]:s[2]], w_in[:, s[2]:s[3]]
    fq, fk, fv, f_logit = w_in[:, s[3]:s[4]], w_in[:, s[4]:s[5]], w_in[:, s[5]:s[6]], w_in[:, s[6]:s[7]]
    zc = lambda n: jnp.zeros((D_MODEL, n), F32)
    misc = jnp.concatenate([f_logit, zc(ROPE_LO - N_FOX), k_rope, zc(LANE - ROPE_HI)], axis=1)
    w_in_p = jnp.concatenate([c_q, c_kv, fq, fk, fv, misc], axis=1).astype(BF16)

    w_uq = p["w_uq"][i].reshape(Q_LORA, N_MLA, QK_DIM)
    nope, rope = w_uq[..., :NOPE_DIM], w_uq[..., NOPE_DIM:]
    rope_sw = jnp.concatenate([rope[..., half:], rope[..., :half]], axis=-1)
    zq = lambda n: jnp.zeros((Q_LORA, N_MLA, n), F32)
    qa = jnp.concatenate([nope, rope, zq(HEAD_PAD - QK_DIM)], axis=-1)
    qb = jnp.concatenate([zq(NOPE_DIM), rope_sw, zq(HEAD_PAD - QK_DIM)], axis=-1)
    w_uq_p = jnp.concatenate([qa.reshape(Q_LORA, -1), qb.reshape(Q_LORA, -1)], axis=1).astype(BF16)

    w_ukv = p["w_ukv"][i].reshape(KV_LORA, N_MLA, NOPE_DIM + V_DIM)
    w_k = _pad_lanes(w_ukv[..., :NOPE_DIM], HEAD_PAD).reshape(KV_LORA, -1).astype(BF16)
    w_v = w_ukv[..., NOPE_DIM:].reshape(KV_LORA, -1).astype(BF16)

    gq = _pad_lanes(p["g_qn_mla"][i] * (QK_DIM ** -0.5 * LOG2E))[None]
    gk = _pad_lanes(p["g_kn_mla"][i])[None]
    lw = {
        "g_mix": p["g_mix"][i][None], "w_in": w_in_p, "b_f": _pad_lanes(p["b_f"][i])[None],
        "g_cq": p["g_cq"][i][None], "g_ckv": p["g_ckv"][i][None], "w_uq": w_uq_p,
        "gq": gq, "gk": gk,
        "gfq": jnp.tile(p["g_qn_fox"][i] * (FOX_DIM ** -0.5 * LOG2E), N_FOX)[None],
        "gfk": jnp.tile(p["g_kn_fox"][i], N_FOX)[None],
        "w_k": w_k, "w_v": w_v,
        "w_out": p["w_out"][i].astype(BF16), "g_ffn": p["g_ffn"][i][None],
    }
    j = i // 2
    lw["mixer_idx"] = j
    if i % 2 == 0:
        lw.update(w_gate=p["w_gate_b"], w_up=p["w_up_b"], w_down=p["w_down_b"])
    else:
        wr = p["w_router"][j]
        wr_hi = wr.astype(BF16)
        wr_lo = (wr - wr_hi.astype(F32)).astype(BF16)
        lw.update(w_router=_pad_lanes(jnp.concatenate([wr_hi, wr_lo], axis=1)),
                  b_router=_pad_lanes(p["b_router"][j])[None],
                  we_gate=p["we_gate_b"], we_up=p["we_up_b"], we_down=p["we_down_b"])
    return lw


def _channel_mixer(i, x, om, of, lw, consts, tm_post, tm_ffn, defer_combine=False):
    if i % 2 == 0:
        x2, xn = _post_attn(x, om, of, lw, False, tm_post)
        return _ffn(x2, xn, lw, tm_ffn)
    x2, xn, route, route_t, counts = _post_attn(x, om, of, lw, True, MXU, consts["tri256"])
    return _moe(x2, xn, route, route_t, counts, lw, MXU, defer_combine)


def _trunk_prompt(x, layers, consts):
    b, t, _ = x.shape
    rows = b * t
    x = x.reshape(rows, D_MODEL)
    ct, st = _rope_tables(jnp.arange(t, dtype=jnp.int32))
    nfx = N_FOX * FOX_DIM
    stack = ()
    pending = None
    for i, lw in enumerate(layers):
        outs = _mixer_in_prompt(x, lw, ct, st, consts, stack, i, b, combine=pending)
        stack = outs[:N_STACKED]
        q, fq, fkb, fvb, k, v, qaug, kaug = outs[N_STACKED:N_STACKED + 8]
        if pending is not None:
            x = outs[-1]
        om = _attn_prompt(q.reshape(b, t, -1), k.reshape(b, t, -1), v.reshape(b, t, -1), None, 256)
        of = _attn_prompt(fq.reshape(b, t, nfx), fkb.reshape(b, t, nfx), fvb.reshape(b, t, nfx),
                          (qaug.reshape(b, t, LANE), kaug.reshape(b, t, LANE)), 256)
        defer = i % 2 == 1 and i + 1 < len(layers)
        res = _channel_mixer(i, x, om.reshape(rows, -1), of.reshape(rows, -1), lw, consts, 256, 512, defer)
        x, pending = (None, res) if defer else (res, None)
    ckv_s, krt_s, fkt_s, fvt_s, lft_s = stack
    heads = lambda a: jnp.transpose(a.reshape(DEPTH, b, N_FOX, FOX_DIM, t), (0, 1, 4, 2, 3))
    state = (ckv_s.reshape(DEPTH, b, t, KV_LORA), jnp.swapaxes(krt_s, 2, 3), heads(fkt_s), heads(fvt_s),
             jnp.swapaxes(lft_s, 2, 3))
    return x.reshape(b, t, D_MODEL), state


def _trunk_sample(x, past, layers, consts):
    b, t, _ = x.shape
    rows = b * t
    start = PAST_LEN
    x = x.reshape(rows, D_MODEL)
    ct, st = _rope_tables(start + jnp.arange(t, dtype=jnp.int32))
    ct, st = jnp.tile(ct, (b, 1)), jnp.tile(st, (b, 1))
    p_ckv, p_krope, p_fk, p_fv, p_logf = past
    nfx = N_FOX * FOX_DIM
    krope_t = jnp.swapaxes(p_krope, 2, 3)
    fk_t = jnp.transpose(p_fk, (0, 1, 3, 4, 2)).reshape(DEPTH, b, nfx, start)
    fv_t = jnp.transpose(p_fv, (0, 1, 3, 4, 2)).reshape(DEPTH, b, nfx, start)
    lf_t = jnp.swapaxes(p_logf, 2, 3)
    state = ([], [], [], [], [])
    for i, lw in enumerate(layers):
        ckv, kr, kr128, fk, fv, lf128, lf, q, fq, fkb, fvb = _mixer_in(x, lw, ct, st, consts, rows)
        k_c, v_c = _kv_expand_cache(p_ckv, krope_t, i, lw, consts, 512)
        k_n, v_n = _kv_expand(ckv, kr128, lw, consts, rows)
        om = _attn_mla_step(q.reshape(b, t, -1), k_c.reshape(b, start, -1), v_c.reshape(b, start, -1),
                            k_n.reshape(b, t, -1), v_n.reshape(b, t, -1), start)
        of = _attn_fox_step(fq.reshape(b, t, nfx), fk_t, fv_t, fkb.reshape(b, t, nfx),
                            fvb.reshape(b, t, nfx), lf_t, lf128.reshape(b, t, LANE), i, consts)
        x = _channel_mixer(i, x, om.reshape(rows, -1), of.reshape(rows, -1), lw, consts, rows, rows)
        for lst, s_ in zip(state, (ckv.reshape(b, t, KV_LORA), kr.reshape(b, t, ROPE_DIM),
                                   fk.reshape(b, t, N_FOX, FOX_DIM), fv.reshape(b, t, N_FOX, FOX_DIM),
                                   lf.reshape(b, t, N_FOX))):
            lst.append(s_)
    return x.reshape(b, t, D_MODEL), tuple(jnp.stack(s_) for s_ in state)


def kernel(x_prompt, x_sample, cache_mla_ckv, cache_mla_krope, cache_fox_k, cache_fox_v, cache_fox_logf, g_mix, w_in, b_f, g_cq, g_ckv, w_uq, w_ukv, g_qn_mla, g_kn_mla, g_qn_fox, g_kn_fox, w_out, g_ffn, w_gate, w_up, w_down, w_router, b_router, we_gate, we_up, we_down):
    p = dict(g_mix=g_mix, w_in=w_in, b_f=b_f, g_cq=g_cq, g_ckv=g_ckv, w_uq=w_uq, w_ukv=w_ukv,
             g_qn_mla=g_qn_mla, g_kn_mla=g_kn_mla, g_qn_fox=g_qn_fox, g_kn_fox=g_kn_fox,
             w_out=w_out, g_ffn=g_ffn, w_gate=w_gate, w_up=w_up, w_down=w_down,
             w_router=w_router, b_router=b_router, we_gate=we_gate, we_up=we_up, we_down=we_down)
    for name in ("w_gate", "w_up", "w_down", "we_gate", "we_up", "we_down"):
        p[name + "_b"] = p[name].astype(BF16)
    layers = [_prep_layer(i, p) for i in range(DEPTH)]
    consts = _consts()
    y_p, st_p = _trunk_prompt(x_prompt, layers, consts)
    past = (cache_mla_ckv, cache_mla_krope, cache_fox_k, cache_fox_v, cache_fox_logf)
    y_s, st_s = _trunk_sample(x_sample, past, layers, consts)
    return (y_p, y_s) + st_p + st_s
```

```python
import functools

import jax
import jax.numpy as jnp
from jax import lax
from jax.experimental import pallas as pl
from jax.experimental.pallas import tpu as pltpu

F32 = jnp.float32
BF16 = jnp.bfloat16

D_MODEL = 1024
DEPTH = 4
PAST_LEN = 2048
CHUNK_SHIFT = 6
EPS = 1e-6
ROPE_BASE = 10000.0

N_MLA = 8
Q_LORA = 384
KV_LORA = 256
NOPE_DIM = 64
ROPE_DIM = 32
QK_DIM = NOPE_DIM + ROPE_DIM
V_DIM = 64
N_FOX = 8
FOX_DIM = 64
D_FF = 2816
N_EXPERTS = 8
E_FF = 1792

LOG2E = 1.4426950408889634
LANE = 128
MXU = 256
HEAD_PAD = 128
ROPE_LO = NOPE_DIM
ROPE_HI = NOPE_DIM + ROPE_DIM
AUG_W = 16
GROUP = 2

CQ_LO, CQ_HI = 0, Q_LORA
CKV_LO, CKV_HI = CQ_HI, CQ_HI + KV_LORA
FQ_LO, FQ_HI = CKV_HI, CKV_HI + N_FOX * FOX_DIM
FK_LO, FK_HI = FQ_HI, FQ_HI + N_FOX * FOX_DIM
FV_LO, FV_HI = FK_HI, FK_HI + N_FOX * FOX_DIM
MISC_LO, MISC_HI = FV_HI, FV_HI + LANE
D_IN_P = MISC_HI

VMEM_LIMIT = 56 * 1024 * 1024


def _params(*sem):
    return pltpu.CompilerParams(dimension_semantics=sem, vmem_limit_bytes=VMEM_LIMIT)


def _dot(a, b):
    return jnp.dot(a, b, preferred_element_type=F32)


def _dot_nt(a, b):
    return lax.dot_general(a, b, (((1,), (1,)), ((), ())), preferred_element_type=F32)


def _rms(x, g, n):
    return x * lax.rsqrt(jnp.sum(x * x, axis=-1, keepdims=True) * (1.0 / n) + EPS) * g


def _head_norm(x, smat, dim):
    outs = []
    for c in range(0, x.shape[1], MXU):
        xc = x[:, c:c + MXU]
        ss = _dot((xc * xc).astype(BF16), smat)
        outs.append(xc * lax.rsqrt(ss * (1.0 / dim) + EPS))
    return jnp.concatenate(outs, axis=1)


def _full(shape):
    return pl.BlockSpec(shape, lambda *_: (0,) * len(shape))


def _split3(c):
    hi = c.astype(BF16)
    r1 = c - hi.astype(F32)
    mid = r1.astype(BF16)
    lo = (r1 - mid.astype(F32)).astype(BF16)
    return hi, mid, lo


def _combine(x2_ref, route_ref, z0_ref, z1_ref):
    route = route_ref[...]
    lane = lax.broadcasted_iota(jnp.int32, route.shape, 1)
    g1 = jnp.sum(jnp.where(lane == 4, route, 0.0), axis=1, keepdims=True)
    g2 = jnp.sum(jnp.where(lane == 5, route, 0.0), axis=1, keepdims=True)
    return x2_ref[...] + g1 * z0_ref[...] + g2 * z1_ref[...]


N_MIXER_W = 13


def _mixer_core(x, gmix_ref, win_ref, bf_ref, gcq_ref, gckv_ref, wuq_ref, gq_ref,
                gfq_ref, gfk_ref, ct_ref, st_ref, s64_ref, s128_ref):
    tm = x.shape[0]
    xb = _rms(x, gmix_ref[...], D_MODEL).astype(BF16)

    def proj(lo, hi):
        return _dot(xb, win_ref[:, lo:hi])

    ct = ct_ref[...]
    st = st_ref[...]
    lane = lax.broadcasted_iota(jnp.int32, (tm, LANE), 1)

    cq_raw = proj(CQ_LO, CQ_HI)
    a = proj(MISC_LO, MISC_HI)
    ckv_raw = proj(CKV_LO, CKV_HI)
    fk_raw = proj(FK_LO, FK_HI)
    fv = proj(FV_LO, FV_HI)
    fq_raw = proj(FQ_LO, FQ_HI)

    cq = _rms(cq_raw, gcq_ref[...], Q_LORA).astype(BF16)
    nq = N_MLA * HEAD_PAD
    qa = _dot(cq, wuq_ref[:, 0:nq])
    qb = _dot(cq, wuq_ref[:, nq:2 * nq])

    z = a + bf_ref[...]
    lf = jnp.minimum(z, 0.0) - jnp.log1p(jnp.exp(-jnp.abs(z)))
    lf128 = jnp.where(lane < N_FOX, lf, 0.0)
    half = ROPE_DIM // 2
    sw = jnp.where(lane < ROPE_LO + half, pltpu.roll(a, LANE - half, 1), pltpu.roll(a, half, 1))
    kr128 = jnp.where((lane >= ROPE_LO) & (lane < ROPE_HI), a * ct + sw * st, 0.0)
    ckv = _rms(ckv_raw, gckv_ref[...], KV_LORA)

    qr = jnp.concatenate(
        [qa[:, h * HEAD_PAD:(h + 1) * HEAD_PAD] * ct + qb[:, h * HEAD_PAD:(h + 1) * HEAD_PAD] * st
         for h in range(N_MLA)], axis=1)
    qn = _head_norm(qr, s128_ref[...], QK_DIM)
    gq = gq_ref[...]
    q = jnp.concatenate(
        [qn[:, h * HEAD_PAD:(h + 1) * HEAD_PAD] * gq for h in range(N_MLA)], axis=1).astype(BF16)

    s64 = s64_ref[...]
    fk = _head_norm(fk_raw, s64, FOX_DIM) * gfk_ref[...]
    fq = (_head_norm(fq_raw, s64, FOX_DIM) * gfq_ref[...]).astype(BF16)
    return ckv, kr128, lf128, fk, fv, fq, q


def _mixer_in_body(x_ref, *refs):
    (ckv_ref, kr_ref, kr128_ref, fk_ref, fv_ref, lf128_ref, lf_ref,
     q_ref, fq_ref, fkb_ref, fvb_ref) = refs[N_MIXER_W:]
    ckv, kr128, lf128, fk, fv, fq, q = _mixer_core(x_ref[...], *refs[:N_MIXER_W])
    ckv_ref[...] = ckv
    kr128_ref[...] = kr128
    kr_ref[...] = kr128[:, ROPE_LO:ROPE_HI]
    lf128_ref[...] = lf128
    lf_ref[...] = lf128[:, 0:N_FOX]
    fk_ref[...] = fk
    fv_ref[...] = fv
    fkb_ref[...] = fk.astype(BF16)
    fvb_ref[...] = fv.astype(BF16)
    fq_ref[...] = fq
    q_ref[...] = q


def _fox_aug(c, place_ref, ones_ref):
    hi, mid, lo = _split3(c * LOG2E)
    qa = _dot(hi, place_ref[0]) + _dot(mid, place_ref[1]) + _dot(lo, place_ref[2]) + ones_ref[0:1, :]
    ka = ones_ref[1:2, :] - (_dot(hi, place_ref[3]) + _dot(mid, place_ref[4]) + _dot(lo, place_ref[5]))
    return qa.astype(BF16), ka.astype(BF16)


N_STACKED = 5


def _mixer_prompt_body(*refs, first, combine, n_alias, tiles_per_seq):
    n_x = 4 if combine else 1
    n_in = n_x + N_MIXER_W + 6 + n_alias
    weights = refs[n_x:n_x + N_MIXER_W]
    wk_ref, wv_ref, gk_ref, tri_ref, place_ref, ones_ref = refs[n_x + N_MIXER_W:n_x + N_MIXER_W + 6]
    outs = list(refs[n_in:])
    carry_ref = outs.pop()
    (ckv_ref, kr_ref, fk_ref, fv_ref, lf_ref, q_ref, fq_ref, fkb_ref, fvb_ref,
     k_ref, v_ref, qa_ref, ka_ref) = outs[:13]
    if combine:
        x = _combine(*refs[:4])
        outs[13][...] = x
    else:
        x = refs[0][...]
    ckv, kr128, lf128, fk, fv, fq, q = _mixer_core(x, *weights)
    q_ref[...] = q
    fq_ref[...] = fq
    fkb_ref[...] = fk.astype(BF16)
    fvb_ref[...] = fv.astype(BF16)
    ckv_ref[0] = ckv
    kr_ref[0, 0] = kr128.T[ROPE_LO:ROPE_HI, :]
    lf_ref[0, 0] = lf128.T[0:N_FOX, :]
    fk_ref[0, 0] = fk.T
    fv_ref[0, 0] = fv.T
    if first:
        for ref in (ckv_ref, kr_ref, lf_ref, fk_ref, fv_ref):
            ref[1:] = jnp.zeros((ref.shape[0] - 1,) + ref.shape[1:], F32)

    _kv_emit(ckv, kr128, wk_ref, wv_ref, gk_ref, weights[-1], k_ref, v_ref)

    @pl.when(pl.program_id(0) % tiles_per_seq == 0)
    def _():
        carry_ref[...] = jnp.zeros(carry_ref.shape, F32)

    hi, mid, lo = _split3(lf128)
    tri = tri_ref[...]
    c = _dot(tri, hi) + _dot(tri, mid) + _dot(tri, lo) + carry_ref[0:1, :]
    carry_ref[...] = jnp.broadcast_to(c[c.shape[0] - 1:, :], carry_ref.shape)
    qa, ka = _fox_aug(c, place_ref, ones_ref)
    qa_ref[...] = qa
    ka_ref[...] = ka


def _mixer_specs(tm, nt):
    nfx = N_FOX * FOX_DIM
    tab = pl.BlockSpec((tm, LANE), lambda i: (i % nt, 0))
    return [_full((1, D_MODEL)), _full((D_MODEL, D_IN_P)), _full((1, LANE)),
            _full((1, Q_LORA)), _full((1, KV_LORA)), _full((Q_LORA, 2 * N_MLA * HEAD_PAD)),
            _full((1, LANE)), _full((1, nfx)), _full((1, nfx)), tab, tab,
            _full((MXU, MXU)), _full((MXU, MXU))]


def _mixer_args(lw, ct, st, consts):
    return [lw["g_mix"], lw["w_in"], lw["b_f"], lw["g_cq"], lw["g_ckv"], lw["w_uq"], lw["gq"],
            lw["gfq"], lw["gfk"], ct, st, consts["s64"], consts["s128"]]


def _mixer_in(x, lw, ct, st, consts, tm):
    rows = x.shape[0]
    nt = ct.shape[0] // tm
    row = lambda w: pl.BlockSpec((tm, w), lambda i: (i, 0))
    nfx = N_FOX * FOX_DIM
    sds = jax.ShapeDtypeStruct
    outs = [(sds((rows, KV_LORA), F32), row(KV_LORA)), (sds((rows, ROPE_DIM), F32), row(ROPE_DIM)),
            (sds((rows, LANE), F32), row(LANE)), (sds((rows, nfx), F32), row(nfx)),
            (sds((rows, nfx), F32), row(nfx)), (sds((rows, LANE), F32), row(LANE)),
            (sds((rows, N_FOX), F32), row(N_FOX)),
            (sds((rows, N_MLA * HEAD_PAD), BF16), row(N_MLA * HEAD_PAD)),
            (sds((rows, nfx), BF16), row(nfx)), (sds((rows, nfx), BF16), row(nfx)),
            (sds((rows, nfx), BF16), row(nfx))]
    args = _mixer_args(lw, ct, st, consts)
    assert len(args) == N_MIXER_W
    return pl.pallas_call(
        _mixer_in_body, grid=(rows // tm,), in_specs=[row(D_MODEL)] + _mixer_specs(tm, nt),
        out_specs=tuple(o[1] for o in outs), out_shape=tuple(o[0] for o in outs),
        compiler_params=_params("arbitrary"), name="mixer_in",
    )(x, *args)


def _mixer_in_prompt(x, lw, ct, st, consts, stack, layer, batch, combine=None):
    tm = MXU
    rows = (x if combine is None else combine[0]).shape[0]
    t = rows // batch
    nt = t // tm
    first = not stack
    assert first == (layer == 0)
    row = lambda w: pl.BlockSpec((tm, w), lambda i: (i, 0))
    nfx = N_FOX * FOX_DIM
    nk = N_MLA * HEAD_PAD
    nv = N_MLA * V_DIM
    sds = jax.ShapeDtypeStruct
    nd = DEPTH if first else 1
    tmaj = lambda w: pl.BlockSpec((nd, 1, w, tm), lambda i: (layer, i // nt, 0, i % nt))
    outs = [(sds((DEPTH, rows, KV_LORA), F32), pl.BlockSpec((nd, tm, KV_LORA), lambda i: (layer, i, 0))),
            (sds((DEPTH, batch, ROPE_DIM, t), F32), tmaj(ROPE_DIM)),
            (sds((DEPTH, batch, nfx, t), F32), tmaj(nfx)),
            (sds((DEPTH, batch, nfx, t), F32), tmaj(nfx)),
            (sds((DEPTH, batch, N_FOX, t), F32), tmaj(N_FOX)),
            (sds((rows, nk), BF16), row(nk)), (sds((rows, nfx), BF16), row(nfx)),
            (sds((rows, nfx), BF16), row(nfx)), (sds((rows, nfx), BF16), row(nfx)),
            (sds((rows, nk), BF16), row(nk)), (sds((rows, nv), BF16), row(nv)),
            (sds((rows, LANE), BF16), row(LANE)), (sds((rows, LANE), BF16), row(LANE))]
    if combine is None:
        in_specs, args = [row(D_MODEL)], [x]
    else:
        x2, route, z = combine
        in_specs = [row(D_MODEL), row(LANE), row(D_MODEL),
                    pl.BlockSpec((tm, D_MODEL), lambda i: (i + rows // tm, 0))]
        args = [x2, route, z, z]
        outs.append((sds((rows, D_MODEL), F32), row(D_MODEL)))
    in_specs += _mixer_specs(tm, nt) + [_full((KV_LORA, nk)), _full((KV_LORA, nv)), _full((1, LANE)),
                                        _full((MXU, MXU)), _full((6, LANE, LANE)), _full((8, LANE))]
    args += _mixer_args(lw, ct, st, consts) + [lw["w_k"], lw["w_v"], lw["gk"], consts["tri256"],
                                                consts["place"], consts["aug_ones"]]
    aliases = {}
    if stack:
        aliases = {len(args) + n: n for n in range(N_STACKED)}
        in_specs += [pl.BlockSpec(memory_space=pl.ANY)] * N_STACKED
        args += list(stack)
    return pl.pallas_call(
        functools.partial(_mixer_prompt_body, first=first, combine=combine is not None,
                          n_alias=len(stack), tiles_per_seq=nt),
        grid=(rows // tm,), in_specs=in_specs,
        out_specs=tuple(o[1] for o in outs), out_shape=tuple(o[0] for o in outs),
        scratch_shapes=[pltpu.VMEM((8, LANE), F32)],
        input_output_aliases=aliases, compiler_params=_params("arbitrary"), name="mixer_in_prompt",
    )(*args)


def _kv_emit(ckv, kr, wk_ref, wv_ref, gk_ref, s128_ref, k_ref, v_ref):
    cb = ckv.astype(BF16)
    kn = _dot(cb, wk_ref[...])
    kk = jnp.concatenate(
        [kn[:, h * HEAD_PAD:(h + 1) * HEAD_PAD] + kr for h in range(N_MLA)], axis=1)
    kk = _head_norm(kk, s128_ref[...], QK_DIM)
    gk = gk_ref[...]
    k_ref[...] = jnp.concatenate(
        [kk[:, h * HEAD_PAD:(h + 1) * HEAD_PAD] * gk for h in range(N_MLA)], axis=1).astype(BF16)
    v_ref[...] = _dot(cb, wv_ref[...]).astype(BF16)


def _kv_body(ckv_ref, kr128_ref, *rest):
    _kv_emit(ckv_ref[...], kr128_ref[...], *rest)


def _kv_cache_body(ckv_ref, krt_ref, *rest):
    tm = krt_ref.shape[-1]
    pad = lambda n: jnp.zeros((n, tm), F32)
    kr128 = jnp.concatenate([pad(ROPE_LO), krt_ref[0, 0], pad(LANE - ROPE_HI)], axis=0).T
    _kv_emit(ckv_ref[0, 0], kr128, *rest)


def _kv_expand_cache(ckv_all, krope_t, layer, lw, consts, tm):
    _, b, t, _ = ckv_all.shape
    nt = t // tm
    row = lambda w: pl.BlockSpec((tm, w), lambda bi, ti: (bi * nt + ti, 0))
    const = lambda shape: pl.BlockSpec(shape, lambda bi, ti: (0,) * len(shape))
    nk = N_MLA * HEAD_PAD
    nv = N_MLA * V_DIM
    return pl.pallas_call(
        _kv_cache_body, grid=(b, nt),
        in_specs=[pl.BlockSpec((1, 1, tm, KV_LORA), lambda bi, ti: (layer, bi, ti, 0)),
                  pl.BlockSpec((1, 1, ROPE_DIM, tm), lambda bi, ti: (layer, bi, 0, ti)),
                  const((KV_LORA, nk)), const((KV_LORA, nv)), const((1, LANE)), const((MXU, MXU))],
        out_specs=(row(nk), row(nv)),
        out_shape=(jax.ShapeDtypeStruct((b * t, nk), BF16), jax.ShapeDtypeStruct((b * t, nv), BF16)),
        compiler_params=_params("arbitrary", "arbitrary"), name="kv_expand_cache",
    )(ckv_all, krope_t, lw["w_k"], lw["w_v"], lw["gk"], consts["s128"])


def _kv_expand(ckv, kr128, lw, consts, tm):
    rows = kr128.shape[0]
    row = lambda w: pl.BlockSpec((tm, w), lambda i: (i, 0))
    nk = N_MLA * HEAD_PAD
    nv = N_MLA * V_DIM
    return pl.pallas_call(
        _kv_body, grid=(rows // tm,),
        in_specs=[row(KV_LORA), row(LANE), _full((KV_LORA, nk)), _full((KV_LORA, nv)),
                  _full((1, LANE)), _full((MXU, MXU))],
        out_specs=(row(nk), row(nv)),
        out_shape=(jax.ShapeDtypeStruct((rows, nk), BF16), jax.ShapeDtypeStruct((rows, nv), BF16)),
        compiler_params=_params("arbitrary"), name="kv_expand",
    )(ckv, kr128, lw["w_k"], lw["w_v"], lw["gk"], consts["s128"])


def _attn_prompt_body(*refs, fox, tq):
    n_in = 5 if fox else 3
    ins, (o_ref, o_scr) = refs[:n_in], refs[n_in:]
    q_ref, k_ref, v_ref = ins[:3]
    pair = pl.program_id(1)
    t = k_ref.shape[1]
    lane = lax.broadcasted_iota(jnp.int32, (tq, LANE), 1)
    r = lax.broadcasted_iota(jnp.int32, (tq, tq), 0)
    c = lax.broadcasted_iota(jnp.int32, (tq, tq), 1)
    allowed = (c <= r) if fox else ((c >> CHUNK_SHIFT) <= (r >> CHUNK_SHIFT))

    def scores(hh, i):
        lo, hi = i * tq, (i + 1) * tq
        if fox:
            qa_ref, ka_ref = ins[3:]
            qm = jnp.where((lane >> 6) == hh, q_ref[0, lo:hi, :].astype(F32), 0.0)
            qa = jnp.where((lane >> 4) == 2 * pair + hh, qa_ref[0, lo:hi, :].astype(F32), 0.0)
            q = jnp.concatenate([qm, qa], axis=1).astype(BF16)
            k = jnp.concatenate([k_ref[0, 0:hi, :], ka_ref[0, 0:hi, :]], axis=1)
        else:
            head = slice(HEAD_PAD * hh, HEAD_PAD * (hh + 1))
            q = q_ref[0, lo:hi, head]
            k = k_ref[0, 0:hi, head]
        s = _dot_nt(q, k)
        sd = jnp.where(allowed, s[:, lo:hi], -jnp.inf)
        return sd if i == 0 else jnp.concatenate([s[:, 0:lo], sd], axis=1)

    def softmax(s):
        p = jnp.exp2(s - jnp.max(s, axis=1, keepdims=True))
        return p.astype(BF16), jnp.sum(p, axis=1, keepdims=True)

    def values(hh, i, p, l):
        lo, hi = i * tq, (i + 1) * tq
        o_scr[hh, lo:hi, :] = _dot(p, v_ref[0, 0:hi, :]) / l

    order = list(reversed(range(t // tq)))
    for g in range(0, len(order), GROUP):
        units = [(hh, i) for i in order[g:g + GROUP] for hh in range(2)]
        ss = [scores(hh, i) for hh, i in units]
        pl_ = [softmax(s) for s in ss]
        for (hh, i), (p, l) in zip(units, pl_):
            values(hh, i, p, l)

    lane_t = lax.broadcasted_iota(jnp.int32, (t, LANE), 1)
    o_ref[0] = jnp.where(lane_t < V_DIM, o_scr[0], o_scr[1]).astype(BF16)


def _attn_prompt(q, k, v, aug, tq):
    fox = aug is not None
    b, t, _ = q.shape
    qw = LANE if fox else 2 * HEAD_PAD
    blk = lambda w, f: pl.BlockSpec((1, t, w), f)
    per_pair = lambda bi, p: (bi, 0, p)
    shared = lambda bi, p: (bi, 0, 0)
    in_specs = [blk(qw, per_pair), blk(qw, per_pair), blk(LANE, per_pair)]
    args = [q, k, v]
    if fox:
        in_specs += [blk(LANE, shared), blk(LANE, shared)]
        args += list(aug)
    return pl.pallas_call(
        functools.partial(_attn_prompt_body, fox=fox, tq=tq),
        grid=(b, N_MLA // 2), in_specs=in_specs, out_specs=blk(LANE, per_pair),
        out_shape=jax.ShapeDtypeStruct((b, t, N_MLA * V_DIM), BF16),
        scratch_shapes=[pltpu.VMEM((2, t, LANE), F32)],
        compiler_params=_params("arbitrary", "arbitrary"),
        name="attn_fox" if fox else "attn_mla",
    )(*args)


def _pad_rows(x, rows):
    return jnp.concatenate([x, jnp.zeros((rows - x.shape[0],) + x.shape[1:], x.dtype)], axis=0)


def _step_softmax(s_c, s_n):
    m = jnp.maximum(jnp.max(s_c, axis=1, keepdims=True), jnp.max(s_n, axis=1, keepdims=True))
    p_c = jnp.exp2(s_c - m)
    p_n = jnp.exp2(s_n - m)
    l = jnp.sum(p_c, axis=1, keepdims=True) + jnp.sum(p_n, axis=1, keepdims=True)
    return p_c.astype(BF16), p_n.astype(BF16), l


def _attn_mla_step_body(q_ref, kc_ref, vc_ref, kn_ref, vn_ref, o_ref, *, start):
    tq = q_ref.shape[1]
    tc = kc_ref.shape[1]
    lane = lax.broadcasted_iota(jnp.int32, (tq, LANE), 1)
    chunk = lambda pos: pos >> CHUNK_SHIFT
    qc = chunk(start + lax.broadcasted_iota(jnp.int32, (tq, 1), 0))
    ok_c = chunk(lax.broadcasted_iota(jnp.int32, (tq, tc), 1)) <= qc
    ok_n = (lane < tq) & (chunk(start + lane) <= qc)
    for p in range(N_MLA // 2):
        pair = slice(LANE * p, LANE * (p + 1))
        vc = vc_ref[0, :, pair]
        vn = _pad_rows(vn_ref[0, :, pair], LANE)
        outs = []
        for h in (2 * p, 2 * p + 1):
            head = slice(HEAD_PAD * h, HEAD_PAD * (h + 1))
            q = q_ref[0, :, head]
            s_c = jnp.where(ok_c, _dot_nt(q, kc_ref[0, :, head]), -jnp.inf)
            s_n = jnp.where(ok_n, _dot_nt(q, _pad_rows(kn_ref[0, :, head], LANE)), -jnp.inf)
            p_c, p_n, l = _step_softmax(s_c, s_n)
            outs.append((_dot(p_c, vc) + _dot(p_n, vn)) / l)
        o_ref[0, :, pair] = jnp.where(lane < V_DIM, outs[0], outs[1]).astype(BF16)


def _attn_mla_step(q, kc, vc, kn, vn, start):
    b, tq, _ = q.shape
    blk = lambda a: pl.BlockSpec((1,) + a.shape[1:], lambda bi: (bi, 0, 0))
    return pl.pallas_call(
        functools.partial(_attn_mla_step_body, start=start), grid=(b,),
        in_specs=[blk(q), blk(kc), blk(vc), blk(kn), blk(vn)],
        out_specs=pl.BlockSpec((1, tq, N_MLA * V_DIM), lambda bi: (bi, 0, 0)),
        out_shape=jax.ShapeDtypeStruct((b, tq, N_MLA * V_DIM), BF16),
        compiler_params=_params("arbitrary"), name="attn_mla_step",
    )(q, kc, vc, kn, vn)


def _lane_cumsum(x, triu):
    rows, blk = x.shape[0], triu.shape[0]
    carry = jnp.zeros((rows, 1), F32)
    out = []
    for b0 in range(0, x.shape[1], blk):
        hi, mid, lo = _split3(_pad_rows(x[:, b0:b0 + blk], 16))
        c = (_dot(hi, triu) + _dot(mid, triu) + _dot(lo, triu))[0:rows] + carry
        carry = c[:, blk - 1:blk]
        out.append(c)
    return jnp.concatenate(out, axis=1), carry


def _attn_fox_step_body(q_ref, kt_ref, vt_ref, kn_ref, vn_ref, lft_ref, lfn_ref,
                        triu_ref, tri_ref, triu_s_ref, o_ref):
    tq = q_ref.shape[1]
    lane = lax.broadcasted_iota(jnp.int32, (tq, LANE), 1)
    causal = lane <= lax.broadcasted_iota(jnp.int32, (tq, LANE), 0)
    cc, total = _lane_cumsum(lft_ref[0, 0], triu_ref[...])
    lfn = _pad_rows(lfn_ref[0], LANE)
    hi, mid, lo = _split3(lfn)
    tri = tri_ref[...]
    cn = (_dot(tri, hi) + _dot(tri, mid) + _dot(tri, lo))[0:tq]
    cnt, _ = _lane_cumsum(lfn.T[0:N_FOX, :], triu_s_ref[...])
    for p in range(N_FOX // 2):
        pair = slice(LANE * p, LANE * (p + 1))
        kt = kt_ref[0, 0, pair, :].astype(BF16)
        vt = vt_ref[0, 0, pair, :].astype(BF16)
        kn = _pad_rows(kn_ref[0, :, pair], LANE)
        vn = _pad_rows(vn_ref[0, :, pair], LANE)
        qf = q_ref[0, :, pair].astype(F32)
        outs = []
        for hh in range(2):
            h = 2 * p + hh
            q = jnp.where((lane >> 6) == hh, qf, 0.0).astype(BF16)
            col = jnp.sum(jnp.where(lane == h, cn, 0.0), axis=1, keepdims=True)
            bias_c = ((total[h:h + 1, :] + col) - cc[h:h + 1, :]) * LOG2E
            bias_n = (col - cnt[h:h + 1, :]) * LOG2E
            s_c = _dot(q, kt) + bias_c
            s_n = jnp.where(causal, _dot_nt(q, kn) + bias_n, -jnp.inf)
            p_c, p_n, l = _step_softmax(s_c, s_n)
            outs.append((_dot_nt(p_c, vt) + _dot(p_n, vn)) / l)
        o_ref[0, :, pair] = jnp.where(lane < V_DIM, outs[0], outs[1]).astype(BF16)


def _attn_fox_step(q, kt_all, vt_all, kn, vn, lft_all, lfn, layer, consts):
    b, tq, _ = q.shape
    new = lambda a: pl.BlockSpec((1,) + a.shape[1:], lambda bi: (bi, 0, 0))
    old = lambda a: pl.BlockSpec((1, 1) + a.shape[2:], lambda bi: (layer, bi, 0, 0))
    return pl.pallas_call(
        _attn_fox_step_body, grid=(b,),
        in_specs=[new(q), old(kt_all), old(vt_all), new(kn), new(vn), old(lft_all), new(lfn),
                  _full((MXU, MXU)), _full((LANE, LANE)), _full((LANE, LANE))],
        out_specs=pl.BlockSpec((1, tq, N_FOX * FOX_DIM), lambda bi: (bi, 0, 0)),
        out_shape=jax.ShapeDtypeStruct((b, tq, N_FOX * FOX_DIM), BF16),
        compiler_params=_params("arbitrary"), name="attn_fox_step",
    )(q, kt_all, vt_all, kn, vn, lft_all, lfn, consts["triu256"], consts["tri128"], consts["triu128"])


def _post_attn_body(*refs, moe):
    if moe:
        (x_ref, om_ref, of_ref, wo_ref, gffn_ref, wr_ref, br_ref, tri_ref,
         x2_ref, xn_ref, route_ref, routet_ref, cnt_ref, carry_ref) = refs
    else:
        x_ref, om_ref, of_ref, wo_ref, gffn_ref, x2_ref, xn_ref = refs
    nm = N_MLA * V_DIM
    x2 = x_ref[...] + _dot(om_ref[...], wo_ref[0:nm, :]) + _dot(of_ref[...], wo_ref[nm:, :])
    x2_ref[...] = x2
    xn = _rms(x2, gffn_ref[...], D_MODEL)
    xh = xn.astype(BF16)
    xn_ref[...] = xn if moe else xh
    if moe:
        xl = (xn - xh.astype(F32)).astype(BF16)
        r = _dot(xh, wr_ref[...]) + _dot(xl, wr_ref[...])
        logits = r + pltpu.roll(r, LANE - N_EXPERTS, 1) + br_ref[...]
        lane = lax.broadcasted_iota(jnp.int32, logits.shape, 1).astype(F32)
        valid = lane < N_EXPERTS
        lg = jnp.where(valid, logits, -jnp.inf)
        e = jnp.exp(lg - jnp.max(lg, axis=1, keepdims=True))
        probs = e / jnp.sum(e, axis=1, keepdims=True)
        p1 = jnp.where(valid, probs, -1.0)
        m1 = jnp.max(p1, axis=1, keepdims=True)
        i1 = jnp.min(jnp.where(p1 == m1, lane, float(LANE)), axis=1, keepdims=True)
        p2 = jnp.where(lane == i1, -1.0, p1)
        m2 = jnp.max(p2, axis=1, keepdims=True)
        i2 = jnp.min(jnp.where(p2 == m2, lane, float(LANE)), axis=1, keepdims=True)
        den = m1 + m2
        @pl.when(pl.program_id(0) == 0)
        def _():
            carry_ref[...] = jnp.zeros(carry_ref.shape, F32)

        sel = jnp.where((lane == i1) | (lane == i2), 1.0, 0.0)
        incl = _dot(tri_ref[...], sel.astype(BF16)) + carry_ref[0:1, :]
        excl = incl - sel
        rank1 = jnp.sum(jnp.where(lane == i1, excl, 0.0), axis=1, keepdims=True)
        rank2 = jnp.sum(jnp.where(lane == i2, excl, 0.0), axis=1, keepdims=True)
        total = incl[incl.shape[0] - 1:, :]
        carry_ref[...] = jnp.broadcast_to(total, carry_ref.shape)
        cnt_ref[...] = jnp.broadcast_to(total, cnt_ref.shape)
        cols = (i1, i2, rank1, rank2, m1 / den, m2 / den)
        route = jnp.zeros(logits.shape, F32)
        for n, col in enumerate(cols):
            route = jnp.where(lane == n, col, route)
        route_ref[...] = route
        routet_ref[...] = route.T[0:8, :]


def _post_attn(x, om, of, lw, moe, tm, tri=None):
    rows = x.shape[0]
    row = lambda w: pl.BlockSpec((tm, w), lambda i: (i, 0))
    nm = N_MLA * V_DIM
    in_specs = [row(D_MODEL), row(nm), row(nm), _full((2 * nm, D_MODEL)), _full((1, D_MODEL))]
    args = [x, om, of, lw["w_out"], lw["g_ffn"]]
    out_specs = [row(D_MODEL), row(D_MODEL)]
    out_shape = [jax.ShapeDtypeStruct((rows, D_MODEL), F32),
                 jax.ShapeDtypeStruct((rows, D_MODEL), F32 if moe else BF16)]
    scratch = []
    if moe:
        assert tm == MXU
        in_specs += [_full((D_MODEL, LANE)), _full((1, LANE)), _full((MXU, MXU))]
        args += [lw["w_router"], lw["b_router"], tri]
        out_specs += [row(LANE), pl.BlockSpec((8, tm), lambda i: (0, i)), _full((8, LANE))]
        out_shape += [jax.ShapeDtypeStruct((rows, LANE), F32), jax.ShapeDtypeStruct((8, rows), F32),
                      jax.ShapeDtypeStruct((8, LANE), F32)]
        scratch = [pltpu.VMEM((8, LANE), F32)]
    return pl.pallas_call(
        functools.partial(_post_attn_body, moe=moe), grid=(rows // tm,),
        in_specs=in_specs, out_specs=tuple(out_specs), out_shape=tuple(out_shape),
        scratch_shapes=scratch,
        compiler_params=_params("arbitrary"), name="post_attn_moe" if moe else "post_attn",
    )(*args)


def _swiglu_acc(xb, wg_ref, wu_ref, wd_ref, d_ff, lo=0, hi=None):
    acc = None
    for c in range(lo * MXU, d_ff if hi is None else hi * MXU, MXU):
        g = _dot(xb, wg_ref[:, c:c + MXU])
        u = _dot(xb, wu_ref[:, c:c + MXU])
        h = (g * jax.nn.sigmoid(g) * u).astype(BF16)
        d = _dot(h, wd_ref[c:c + MXU, :])
        acc = d if acc is None else acc + d
    return acc


def _post_ffn_body(x_ref, om_ref, of_ref, wo_ref, gffn_ref, wg_ref, wu_ref, wd_ref, o_ref):
    nm = N_MLA * V_DIM
    x2 = x_ref[...] + _dot(om_ref[...], wo_ref[0:nm, :]) + _dot(of_ref[...], wo_ref[nm:, :])
    xn = _rms(x2, gffn_ref[...], D_MODEL).astype(BF16)
    o_ref[...] = x2 + _swiglu_acc(xn, wg_ref.at[0], wu_ref.at[0], wd_ref.at[0], D_FF)


def _post_ffn(x, om, of, lw, tm):
    rows = x.shape[0]
    row = lambda w: pl.BlockSpec((tm, w), lambda i: (i, 0))
    nm = N_MLA * V_DIM
    jl = lw["mixer_idx"]
    wspec = lambda a, b: pl.BlockSpec((1, a, b), lambda i: (jl, 0, 0))
    return pl.pallas_call(
        _post_ffn_body, grid=(rows // tm,),
        in_specs=[row(D_MODEL), row(nm), row(nm), _full((2 * nm, D_MODEL)), _full((1, D_MODEL)),
                  wspec(D_MODEL, D_FF), wspec(D_MODEL, D_FF), wspec(D_FF, D_MODEL)],
        out_specs=row(D_MODEL), out_shape=jax.ShapeDtypeStruct((rows, D_MODEL), F32),
        compiler_params=_params("arbitrary"), name="post_ffn",
    )(x, om, of, lw["w_out"], lw["g_ffn"], lw["w_gate"], lw["w_up"], lw["w_down"])


SCATTER_UNROLL = 8
SPLIT_CHUNK = 5


def _inv_body(pos_ref, base_hbm, inv_ref, sem, *, tmg):
    fill = pltpu.make_async_copy(base_hbm, inv_ref, sem.at[0])
    fill.start()
    fill.wait()

    def put(i, c):
        for u in range(SCATTER_UNROLL):
            a = i * SCATTER_UNROLL + u
            inv_ref[tmg + pos_ref[a]] = a
        return c

    lax.fori_loop(0, pos_ref.shape[0] // SCATTER_UNROLL, put, 0)


def _route_inverse(pos_flat, base, tmg):
    return pl.pallas_call(
        functools.partial(_inv_body, tmg=tmg),
        in_specs=[pl.BlockSpec(memory_space=pltpu.SMEM), pl.BlockSpec(memory_space=pl.ANY)],
        out_specs=pl.BlockSpec(memory_space=pltpu.SMEM),
        out_shape=jax.ShapeDtypeStruct(base.shape, jnp.int32),
        scratch_shapes=[pltpu.SemaphoreType.DMA((1,))], name="route_inverse",
    )(pos_flat, base)


GATHER, OUT, STAGE = 0, 2, 4


def _moe_routed_body(te_ref, stage_ref, src_ref, dst_ref, xn_hbm, wg_ref, wu_ref, wd_ref, z_hbm,
                     buf, gsem, ssem, *, tmg, n_tiles):
    del te_ref
    j = pl.program_id(0)

    def row_in(entry, slot, r):
        return pltpu.make_async_copy(xn_hbm.at[pl.ds(src_ref[entry], 1), :],
                                     buf.at[GATHER + slot, pl.ds(r, 1), :], gsem.at[slot])

    def row_out(entry, slot, r):
        return pltpu.make_async_copy(buf.at[OUT + slot, pl.ds(r, 1), :],
                                     z_hbm.at[pl.ds(dst_ref[entry], 1), :], ssem.at[0])

    def tile_in(slot):
        return pltpu.make_async_copy(xn_hbm.at[pl.ds(0, tmg), :], buf.at[GATHER + slot], gsem.at[slot])

    def tile_out(slot):
        return pltpu.make_async_copy(buf.at[OUT + slot], z_hbm.at[pl.ds(0, tmg), :], ssem.at[0])

    @pl.when(j == 0)
    def _():
        buf[OUT + 1] = jnp.zeros(buf.shape[1:], F32)
        for r in range(tmg):
            row_in(tmg + r, 0, r).start()

    def step(cur):
        nxt = 1 - cur

        @pl.when(j >= 1)
        def _():
            tile_out(cur).wait()

        tile_in(cur).wait()
        w = (wg_ref.at[0, 0], wu_ref.at[0, 0], wd_ref.at[0, 0])
        head = _swiglu_acc(buf[GATHER + cur].astype(BF16), *w, E_FF, 0, SPLIT_CHUNK)
        for r in range(tmg):
            row_in((j + 2) * tmg + r, nxt, r).start()
            row_out(j * tmg + r, nxt, r).start()
        stage = stage_ref[0]
        buf[stage] = buf[GATHER + cur]
        tail = _swiglu_acc(buf[stage].astype(BF16), *w, E_FF, SPLIT_CHUNK)
        buf[OUT + cur] = head + tail

    for parity in range(2):
        pl.when(j % 2 == parity)(functools.partial(step, parity))

    assert n_tiles % 2 == 0

    @pl.when(j == n_tiles)
    def _():
        tile_out(1).wait()
        tile_in(1).wait()


def _moe_routed(xn, tile_expert, src, dst, lw, tmg, n_tiles):
    n = xn.shape[0]
    jl = lw["mixer_idx"]
    wspec = lambda a, b: pl.BlockSpec((1, 1, a, b), lambda j, te, st, s, d: (jl, te[j], 0, 0))
    grid_spec = pltpu.PrefetchScalarGridSpec(
        num_scalar_prefetch=4, grid=(n_tiles + 1,),
        in_specs=[pl.BlockSpec(memory_space=pl.ANY), wspec(D_MODEL, E_FF), wspec(D_MODEL, E_FF),
                  wspec(E_FF, D_MODEL)],
        out_specs=pl.BlockSpec(memory_space=pl.ANY),
        scratch_shapes=[pltpu.VMEM((STAGE + 1, tmg, D_MODEL), F32),
                        pltpu.SemaphoreType.DMA((2,)), pltpu.SemaphoreType.DMA((1,))])
    return pl.pallas_call(
        functools.partial(_moe_routed_body, tmg=tmg, n_tiles=n_tiles),
        grid_spec=grid_spec, out_shape=jax.ShapeDtypeStruct((2 * n + tmg, D_MODEL), F32),
        compiler_params=_params("arbitrary"), name="moe_routed",
    )(tile_expert, jnp.full((1,), STAGE, jnp.int32), src, dst, xn,
      lw["we_gate"], lw["we_up"], lw["we_down"])


def _combine_body(x2_ref, route_ref, z0_ref, z1_ref, o_ref):
    o_ref[...] = _combine(x2_ref, route_ref, z0_ref, z1_ref)


def _moe_combine(x2, route, z, tm):
    n = x2.shape[0]
    row = lambda w: pl.BlockSpec((tm, w), lambda i: (i, 0))
    return pl.pallas_call(
        _combine_body, grid=(n // tm,),
        in_specs=[row(D_MODEL), row(LANE), row(D_MODEL),
                  pl.BlockSpec((tm, D_MODEL), lambda i: (i + n // tm, 0))],
        out_specs=row(D_MODEL), out_shape=jax.ShapeDtypeStruct((n, D_MODEL), F32),
        compiler_params=_params("arbitrary"), name="moe_combine",
    )(x2, route, z, z)


def _moe(x2, xn, route, route_t, counts, lw, tmg, defer_combine=False):
    n = x2.shape[0]
    assert n & (n - 1) == 0
    n_tiles = 2 * n // tmg + N_EXPERTS
    ext = (n_tiles + 3) * tmg
    e = route_t[0:2].astype(jnp.int32)
    rank = route_t[2:4].astype(jnp.int32)
    cnt = counts[0, :N_EXPERTS].astype(jnp.int32)
    tiles = (cnt + tmg - 1) // tmg
    tile_end = jnp.cumsum(tiles)
    row_start = (tile_end - tiles) * tmg
    start_of = jnp.sum(jnp.where(e[..., None] == jnp.arange(N_EXPERTS), row_start, 0), axis=-1)
    pos_flat = (start_of + rank).reshape(-1)
    steps = jnp.minimum(jnp.arange(n_tiles + 1, dtype=jnp.int32), tile_end[-1] - 1)
    tile_expert = jnp.minimum(jnp.sum((steps[:, None] >= tile_end[None, :]).astype(jnp.int32), axis=1),
                              N_EXPERTS - 1)
    idx = jnp.arange(ext, dtype=jnp.int32)
    dst = _route_inverse(pos_flat, 2 * n + (idx & (tmg - 1)), tmg)
    src = jnp.where(dst < 2 * n, dst, idx) & (n - 1)
    z = _moe_routed(xn, tile_expert, src, dst, lw, tmg, n_tiles)
    return (x2, route, z) if defer_combine else _moe_combine(x2, route, z, tmg)


def _consts():
    i = jnp.arange(MXU)
    blockdiag = lambda w: ((i[:, None] // w) == (i[None, :] // w)).astype(BF16)
    tri = lambda n: (jnp.arange(n)[None, :] <= jnp.arange(n)[:, None]).astype(BF16)
    r = jnp.arange(LANE)
    place = jnp.stack([((r[:, None] < N_FOX) & (r[None, :] == AUG_W * r[:, None] + k)).astype(BF16)
                       for k in range(6)])
    within = r % AUG_W
    headed = r < AUG_W * N_FOX
    ones_q = (headed & (within >= 3) & (within < 6)).astype(F32)
    ones_k = (headed & (within < 3)).astype(F32)
    aug_ones = jnp.zeros((8, LANE), F32).at[0].set(ones_q).at[1].set(ones_k)
    return {"s64": blockdiag(FOX_DIM), "s128": blockdiag(HEAD_PAD), "tri256": tri(MXU),
            "tri128": tri(LANE), "triu256": tri(MXU).T, "triu128": tri(LANE).T,
            "place": place, "aug_ones": aug_ones}


def _rope_tables(pos):
    half = ROPE_DIM // 2
    inv = ROPE_BASE ** (-jnp.arange(half, dtype=F32) / half)
    ang = pos.astype(F32)[:, None] * inv[None, :]
    cos, sin = jnp.cos(ang), jnp.sin(ang)
    t = pos.shape[0]
    ct = jnp.concatenate([jnp.ones((t, ROPE_LO), F32), cos, cos, jnp.zeros((t, LANE - ROPE_HI), F32)], 1)
    st = jnp.concatenate([jnp.zeros((t, ROPE_LO), F32), -sin, sin, jnp.zeros((t, LANE - ROPE_HI), F32)], 1)
    return ct, st


def _pad_lanes(v, width=LANE):
    return jnp.pad(v, [(0, 0)] * (v.ndim - 1) + [(0, width - v.shape[-1])])


def _prep_layer(i, p):
    half = ROPE_DIM // 2
    w_in = p["w_in"][i]
    s = [0, Q_LORA, Q_LORA + KV_LORA, Q_LORA + KV_LORA + ROPE_DIM]
    nfx = N_FOX * FOX_DIM
    s += [s[3] + nfx, s[3] + 2 * nfx, s[3] + 3 * nfx, s[3] + 3 * nfx + N_FOX]
    c_q, c_kv, k_rope = w_in[:, s[0]:s[1]], w_in[:, s[1]:s[2]], w_in[:, s[2]:s[3]]
    fq, fk, fv, f_logit = w_in[:, s[3]:s[4]], w_in[:, s[4]:s[5]], w_in[:, s[5]:s[6]], w_in[:, s[6]:s[7]]
    zc = lambda n: jnp.zeros((D_MODEL, n), F32)
    misc = jnp.concatenate([f_logit, zc(ROPE_LO - N_FOX), k_rope, zc(LANE - ROPE_HI)], axis=1)
    w_in_p = jnp.concatenate([c_q, c_kv, fq, fk, fv, misc], axis=1).astype(BF16)

    w_uq = p["w_uq"][i].reshape(Q_LORA, N_MLA, QK_DIM)
    nope, rope = w_uq[..., :NOPE_DIM], w_uq[..., NOPE_DIM:]
    rope_sw = jnp.concatenate([rope[..., half:], rope[..., :half]], axis=-1)
    zq = lambda n: jnp.zeros((Q_LORA, N_MLA, n), F32)
    qa = jnp.concatenate([nope, rope, zq(HEAD_PAD - QK_DIM)], axis=-1)
    qb = jnp.concatenate([zq(NOPE_DIM), rope_sw, zq(HEAD_PAD - QK_DIM)], axis=-1)
    w_uq_p = jnp.concatenate([qa.reshape(Q_LORA, -1), qb.reshape(Q_LORA, -1)], axis=1).astype(BF16)

    w_ukv = p["w_ukv"][i].reshape(KV_LORA, N_MLA, NOPE_DIM + V_DIM)
    w_k = _pad_lanes(w_ukv[..., :NOPE_DIM], HEAD_PAD).reshape(KV_LORA, -1).astype(BF16)
    w_v = w_ukv[..., NOPE_DIM:].reshape(KV_LORA, -1).astype(BF16)

    gq = _pad_lanes(p["g_qn_mla"][i] * (QK_DIM ** -0.5 * LOG2E))[None]
    gk = _pad_lanes(p["g_kn_mla"][i])[None]
    lw = {
        "g_mix": p["g_mix"][i][None], "w_in": w_in_p, "b_f": _pad_lanes(p["b_f"][i])[None],
        "g_cq": p["g_cq"][i][None], "g_ckv": p["g_ckv"][i][None], "w_uq": w_uq_p,
        "gq": gq, "gk": gk,
        "gfq": jnp.tile(p["g_qn_fox"][i] * (FOX_DIM ** -0.5 * LOG2E), N_FOX)[None],
        "gfk": jnp.tile(p["g_kn_fox"][i], N_FOX)[None],
        "w_k": w_k, "w_v": w_v,
        "w_out": p["w_out"][i].astype(BF16), "g_ffn": p["g_ffn"][i][None],
    }
    j = i // 2
    lw["mixer_idx"] = j
    if i % 2 == 0:
        lw.update(w_gate=p["w_gate_b"], w_up=p["w_up_b"], w_down=p["w_down_b"])
    else:
        wr = p["w_router"][j]
        wr_hi = wr.astype(BF16)
        wr_lo = (wr - wr_hi.astype(F32)).astype(BF16)
        lw.update(w_router=_pad_lanes(jnp.concatenate([wr_hi, wr_lo], axis=1)),
                  b_router=_pad_lanes(p["b_router"][j])[None],
                  we_gate=p["we_gate_b"], we_up=p["we_up_b"], we_down=p["we_down_b"])
    return lw


def _channel_mixer(i, x, om, of, lw, consts, tm_post, tm_ffn, defer_combine=False):
    if i % 2 == 0:
        return _post_ffn(x, om, of, lw, tm_ffn)
    x2, xn, route, route_t, counts = _post_attn(x, om, of, lw, True, MXU, consts["tri256"])
    return _moe(x2, xn, route, route_t, counts, lw, MXU, defer_combine)


def _trunk_prompt(x, layers, consts):
    b, t, _ = x.shape
    rows = b * t
    x = x.reshape(rows, D_MODEL)
    ct, st = _rope_tables(jnp.arange(t, dtype=jnp.int32))
    nfx = N_FOX * FOX_DIM
    stack = ()
    pending = None
    for i, lw in enumerate(layers):
        outs = _mixer_in_prompt(x, lw, ct, st, consts, stack, i, b, combine=pending)
        stack = outs[:N_STACKED]
        q, fq, fkb, fvb, k, v, qaug, kaug = outs[N_STACKED:N_STACKED + 8]
        if pending is not None:
            x = outs[-1]
        om = _attn_prompt(q.reshape(b, t, -1), k.reshape(b, t, -1), v.reshape(b, t, -1), None, 256)
        of = _attn_prompt(fq.reshape(b, t, nfx), fkb.reshape(b, t, nfx), fvb.reshape(b, t, nfx),
                          (qaug.reshape(b, t, LANE), kaug.reshape(b, t, LANE)), 256)
        defer = i % 2 == 1 and i + 1 < len(layers)
        res = _channel_mixer(i, x, om.reshape(rows, -1), of.reshape(rows, -1), lw, consts, 256, 512, defer)
        x, pending = (None, res) if defer else (res, None)
    ckv_s, krt_s, fkt_s, fvt_s, lft_s = stack
    heads = lambda a: jnp.transpose(a.reshape(DEPTH, b, N_FOX, FOX_DIM, t), (0, 1, 4, 2, 3))
    state = (ckv_s.reshape(DEPTH, b, t, KV_LORA), jnp.swapaxes(krt_s, 2, 3), heads(fkt_s), heads(fvt_s),
             jnp.swapaxes(lft_s, 2, 3))
    return x.reshape(b, t, D_MODEL), state


def _trunk_sample(x, past, layers, consts):
    b, t, _ = x.shape
    rows = b * t
    start = PAST_LEN
    x = x.reshape(rows, D_MODEL)
    ct, st = _rope_tables(start + jnp.arange(t, dtype=jnp.int32))
    ct, st = jnp.tile(ct, (b, 1)), jnp.tile(st, (b, 1))
    p_ckv, p_krope, p_fk, p_fv, p_logf = past
    nfx = N_FOX * FOX_DIM
    krope_t = jnp.swapaxes(p_krope, 2, 3)
    fk_t = jnp.transpose(p_fk, (0, 1, 3, 4, 2)).reshape(DEPTH, b, nfx, start)
    fv_t = jnp.transpose(p_fv, (0, 1, 3, 4, 2)).reshape(DEPTH, b, nfx, start)
    lf_t = jnp.swapaxes(p_logf, 2, 3)
    state = ([], [], [], [], [])
    for i, lw in enumerate(layers):
        ckv, kr, kr128, fk, fv, lf128, lf, q, fq, fkb, fvb = _mixer_in(x, lw, ct, st, consts, rows)
        k_c, v_c = _kv_expand_cache(p_ckv, krope_t, i, lw, consts, 512)
        k_n, v_n = _kv_expand(ckv, kr128, lw, consts, rows)
        om = _attn_mla_step(q.reshape(b, t, -1), k_c.reshape(b, start, -1), v_c.reshape(b, start, -1),
                            k_n.reshape(b, t, -1), v_n.reshape(b, t, -1), start)
        of = _attn_fox_step(fq.reshape(b, t, nfx), fk_t, fv_t, fkb.reshape(b, t, nfx),
                            fvb.reshape(b, t, nfx), lf_t, lf128.reshape(b, t, LANE), i, consts)
        x = _channel_mixer(i, x, om.reshape(rows, -1), of.reshape(rows, -1), lw, consts, rows, rows)
        for lst, s_ in zip(state, (ckv.reshape(b, t, KV_LORA), kr.reshape(b, t, ROPE_DIM),
                                   fk.reshape(b, t, N_FOX, FOX_DIM), fv.reshape(b, t, N_FOX, FOX_DIM),
                                   lf.reshape(b, t, N_FOX))):
            lst.append(s_)
    return x.reshape(b, t, D_MODEL), tuple(jnp.stack(s_) for s_ in state)


def kernel(x_prompt, x_sample, cache_mla_ckv, cache_mla_krope, cache_fox_k, cache_fox_v, cache_fox_logf, g_mix, w_in, b_f, g_cq, g_ckv, w_uq, w_ukv, g_qn_mla, g_kn_mla, g_qn_fox, g_kn_fox, w_out, g_ffn, w_gate, w_up, w_down, w_router, b_router, we_gate, we_up, we_down):
    p = dict(g_mix=g_mix, w_in=w_in, b_f=b_f, g_cq=g_cq, g_ckv=g_ckv, w_uq=w_uq, w_ukv=w_ukv,
             g_qn_mla=g_qn_mla, g_kn_mla=g_kn_mla, g_qn_fox=g_qn_fox, g_kn_fox=g_kn_fox,
             w_out=w_out, g_ffn=g_ffn, w_gate=w_gate, w_up=w_up, w_down=w_down,
             w_router=w_router, b_router=b_router, we_gate=we_gate, we_up=we_up, we_down=we_down)
    for name in ("w_gate", "w_up", "w_down", "we_gate", "we_up", "we_down"):
        p[name + "_b"] = p[name].astype(BF16)
    layers = [_prep_layer(i, p) for i in range(DEPTH)]
    consts = _consts()
    y_p, st_p = _trunk_prompt(x_prompt, layers, consts)
    past = (cache_mla_ckv, cache_mla_krope, cache_fox_k, cache_fox_v, cache_fox_logf)
    y_s, st_s = _trunk_sample(x_sample, past, layers, consts)
    return (y_p, y_s) + st_p + st_s
```

```python
import functools

import jax
import jax.numpy as jnp
from jax import lax
from jax.experimental import pallas as pl
from jax.experimental.pallas import tpu as pltpu

F32 = jnp.float32
BF16 = jnp.bfloat16

D_MODEL = 1024
DEPTH = 4
PAST_LEN = 2048
CHUNK_SHIFT = 6
EPS = 1e-6
ROPE_BASE = 10000.0

N_MLA = 8
Q_LORA = 384
KV_LORA = 256
NOPE_DIM = 64
ROPE_DIM = 32
QK_DIM = NOPE_DIM + ROPE_DIM
V_DIM = 64
N_FOX = 8
FOX_DIM = 64
D_FF = 2816
N_EXPERTS = 8
E_FF = 1792

LOG2E = 1.4426950408889634
LANE = 128
MXU = 256
HEAD_PAD = 128
ROPE_LO = NOPE_DIM
ROPE_HI = NOPE_DIM + ROPE_DIM
AUG_W = 16
GROUP = 2

CQ_LO, CQ_HI = 0, Q_LORA
CKV_LO, CKV_HI = CQ_HI, CQ_HI + KV_LORA
FQ_LO, FQ_HI = CKV_HI, CKV_HI + N_FOX * FOX_DIM
FK_LO, FK_HI = FQ_HI, FQ_HI + N_FOX * FOX_DIM
FV_LO, FV_HI = FK_HI, FK_HI + N_FOX * FOX_DIM
MISC_LO, MISC_HI = FV_HI, FV_HI + LANE
D_IN_P = MISC_HI

VMEM_LIMIT = 56 * 1024 * 1024


def _params(*sem):
    return pltpu.CompilerParams(dimension_semantics=sem, vmem_limit_bytes=VMEM_LIMIT)


def _dot(a, b):
    return jnp.dot(a, b, preferred_element_type=F32)


def _dot_nt(a, b):
    return lax.dot_general(a, b, (((1,), (1,)), ((), ())), preferred_element_type=F32)


def _rms(x, g, n):
    return x * lax.rsqrt(jnp.sum(x * x, axis=-1, keepdims=True) * (1.0 / n) + EPS) * g


def _head_norm(x, smat, dim):
    outs = []
    for c in range(0, x.shape[1], MXU):
        xc = x[:, c:c + MXU]
        ss = _dot((xc * xc).astype(BF16), smat)
        outs.append(xc * lax.rsqrt(ss * (1.0 / dim) + EPS))
    return jnp.concatenate(outs, axis=1)


def _full(shape):
    return pl.BlockSpec(shape, lambda *_: (0,) * len(shape))


def _split3(c):
    hi = c.astype(BF16)
    r1 = c - hi.astype(F32)
    mid = r1.astype(BF16)
    lo = (r1 - mid.astype(F32)).astype(BF16)
    return hi, mid, lo


def _combine(x2_ref, route_ref, z0_ref, z1_ref):
    route = route_ref[...]
    lane = lax.broadcasted_iota(jnp.int32, route.shape, 1)
    g1 = jnp.sum(jnp.where(lane == 4, route, 0.0), axis=1, keepdims=True)
    g2 = jnp.sum(jnp.where(lane == 5, route, 0.0), axis=1, keepdims=True)
    return x2_ref[...] + g1 * z0_ref[...] + g2 * z1_ref[...]


N_MIXER_W = 13


def _mixer_core(x, gmix_ref, win_ref, bf_ref, gcq_ref, gckv_ref, wuq_ref, gq_ref,
                gfq_ref, gfk_ref, ct_ref, st_ref, s64_ref, s128_ref):
    tm = x.shape[0]
    xb = _rms(x, gmix_ref[...], D_MODEL).astype(BF16)

    def proj(lo, hi):
        return _dot(xb, win_ref[:, lo:hi])

    ct = ct_ref[...]
    st = st_ref[...]
    lane = lax.broadcasted_iota(jnp.int32, (tm, LANE), 1)

    cq_raw = proj(CQ_LO, CQ_HI)
    a = proj(MISC_LO, MISC_HI)
    ckv_raw = proj(CKV_LO, CKV_HI)
    fk_raw = proj(FK_LO, FK_HI)
    fv = proj(FV_LO, FV_HI)
    fq_raw = proj(FQ_LO, FQ_HI)

    cq = _rms(cq_raw, gcq_ref[...], Q_LORA).astype(BF16)
    nq = N_MLA * HEAD_PAD
    qa = _dot(cq, wuq_ref[:, 0:nq])
    qb = _dot(cq, wuq_ref[:, nq:2 * nq])

    z = a + bf_ref[...]
    lf = jnp.minimum(z, 0.0) - jnp.log1p(jnp.exp(-jnp.abs(z)))
    lf128 = jnp.where(lane < N_FOX, lf, 0.0)
    half = ROPE_DIM // 2
    sw = jnp.where(lane < ROPE_LO + half, pltpu.roll(a, LANE - half, 1), pltpu.roll(a, half, 1))
    kr128 = jnp.where((lane >= ROPE_LO) & (lane < ROPE_HI), a * ct + sw * st, 0.0)
    ckv = _rms(ckv_raw, gckv_ref[...], KV_LORA)

    qr = jnp.concatenate(
        [qa[:, h * HEAD_PAD:(h + 1) * HEAD_PAD] * ct + qb[:, h * HEAD_PAD:(h + 1) * HEAD_PAD] * st
         for h in range(N_MLA)], axis=1)
    qn = _head_norm(qr, s128_ref[...], QK_DIM)
    gq = gq_ref[...]
    q = jnp.concatenate(
        [qn[:, h * HEAD_PAD:(h + 1) * HEAD_PAD] * gq for h in range(N_MLA)], axis=1).astype(BF16)

    s64 = s64_ref[...]
    fk = _head_norm(fk_raw, s64, FOX_DIM) * gfk_ref[...]
    fq = (_head_norm(fq_raw, s64, FOX_DIM) * gfq_ref[...]).astype(BF16)
    return ckv, kr128, lf128, fk, fv, fq, q


def _mixer_in_body(x_ref, *refs):
    (ckv_ref, kr_ref, kr128_ref, fk_ref, fv_ref, lf128_ref, lf_ref,
     q_ref, fq_ref, fkb_ref, fvb_ref) = refs[N_MIXER_W:]
    ckv, kr128, lf128, fk, fv, fq, q = _mixer_core(x_ref[...], *refs[:N_MIXER_W])
    ckv_ref[...] = ckv
    kr128_ref[...] = kr128
    kr_ref[...] = kr128[:, ROPE_LO:ROPE_HI]
    lf128_ref[...] = lf128
    lf_ref[...] = lf128[:, 0:N_FOX]
    fk_ref[...] = fk
    fv_ref[...] = fv
    fkb_ref[...] = fk.astype(BF16)
    fvb_ref[...] = fv.astype(BF16)
    fq_ref[...] = fq
    q_ref[...] = q


def _fox_aug(c, place_ref, ones_ref):
    hi, mid, lo = _split3(c * LOG2E)
    qa = _dot(hi, place_ref[0]) + _dot(mid, place_ref[1]) + _dot(lo, place_ref[2]) + ones_ref[0:1, :]
    ka = ones_ref[1:2, :] - (_dot(hi, place_ref[3]) + _dot(mid, place_ref[4]) + _dot(lo, place_ref[5]))
    return qa.astype(BF16), ka.astype(BF16)


N_STACKED = 5


def _mixer_prompt_body(*refs, first, combine, n_alias, tiles_per_seq):
    n_x = 4 if combine else 1
    n_in = n_x + N_MIXER_W + 6 + n_alias
    weights = refs[n_x:n_x + N_MIXER_W]
    wk_ref, wv_ref, gk_ref, tri_ref, place_ref, ones_ref = refs[n_x + N_MIXER_W:n_x + N_MIXER_W + 6]
    outs = list(refs[n_in:])
    carry_ref = outs.pop()
    (ckv_ref, kr_ref, fk_ref, fv_ref, lf_ref, q_ref, fq_ref, fkb_ref, fvb_ref,
     k_ref, v_ref, qa_ref, ka_ref) = outs[:13]
    if combine:
        x = _combine(*refs[:4])
        outs[13][...] = x
    else:
        x = refs[0][...]
    ckv, kr128, lf128, fk, fv, fq, q = _mixer_core(x, *weights)
    q_ref[...] = q
    fq_ref[...] = fq
    fkb_ref[...] = fk.astype(BF16)
    fvb_ref[...] = fv.astype(BF16)
    ckv_ref[0] = ckv
    kr_ref[0, 0] = kr128.T[ROPE_LO:ROPE_HI, :]
    lf_ref[0, 0] = lf128.T[0:N_FOX, :]
    fk_ref[0, 0] = fk.T
    fv_ref[0, 0] = fv.T
    if first:
        for ref in (ckv_ref, kr_ref, lf_ref, fk_ref, fv_ref):
            ref[1:] = jnp.zeros((ref.shape[0] - 1,) + ref.shape[1:], F32)

    _kv_emit(ckv, kr128, wk_ref, wv_ref, gk_ref, weights[-1], k_ref, v_ref)

    @pl.when(pl.program_id(0) % tiles_per_seq == 0)
    def _():
        carry_ref[...] = jnp.zeros(carry_ref.shape, F32)

    tri = tri_ref[...]
    carry = carry_ref[0:1, :]
    for r0 in range(0, lf128.shape[0], MXU):
        hi, mid, lo = _split3(lf128[r0:r0 + MXU, :])
        c = _dot(tri, hi) + _dot(tri, mid) + _dot(tri, lo) + carry
        carry = c[MXU - 1:, :]
        qa, ka = _fox_aug(c, place_ref, ones_ref)
        qa_ref[r0:r0 + MXU, :] = qa
        ka_ref[r0:r0 + MXU, :] = ka
    carry_ref[...] = jnp.broadcast_to(carry, carry_ref.shape)


def _mixer_specs(tm, nt):
    nfx = N_FOX * FOX_DIM
    tab = pl.BlockSpec((tm, LANE), lambda i: (i % nt, 0))
    return [_full((1, D_MODEL)), _full((D_MODEL, D_IN_P)), _full((1, LANE)),
            _full((1, Q_LORA)), _full((1, KV_LORA)), _full((Q_LORA, 2 * N_MLA * HEAD_PAD)),
            _full((1, LANE)), _full((1, nfx)), _full((1, nfx)), tab, tab,
            _full((MXU, MXU)), _full((MXU, MXU))]


def _mixer_args(lw, ct, st, consts):
    return [lw["g_mix"], lw["w_in"], lw["b_f"], lw["g_cq"], lw["g_ckv"], lw["w_uq"], lw["gq"],
            lw["gfq"], lw["gfk"], ct, st, consts["s64"], consts["s128"]]


def _mixer_in(x, lw, ct, st, consts, tm):
    rows = x.shape[0]
    nt = ct.shape[0] // tm
    row = lambda w: pl.BlockSpec((tm, w), lambda i: (i, 0))
    nfx = N_FOX * FOX_DIM
    sds = jax.ShapeDtypeStruct
    outs = [(sds((rows, KV_LORA), F32), row(KV_LORA)), (sds((rows, ROPE_DIM), F32), row(ROPE_DIM)),
            (sds((rows, LANE), F32), row(LANE)), (sds((rows, nfx), F32), row(nfx)),
            (sds((rows, nfx), F32), row(nfx)), (sds((rows, LANE), F32), row(LANE)),
            (sds((rows, N_FOX), F32), row(N_FOX)),
            (sds((rows, N_MLA * HEAD_PAD), BF16), row(N_MLA * HEAD_PAD)),
            (sds((rows, nfx), BF16), row(nfx)), (sds((rows, nfx), BF16), row(nfx)),
            (sds((rows, nfx), BF16), row(nfx))]
    args = _mixer_args(lw, ct, st, consts)
    assert len(args) == N_MIXER_W
    return pl.pallas_call(
        _mixer_in_body, grid=(rows // tm,), in_specs=[row(D_MODEL)] + _mixer_specs(tm, nt),
        out_specs=tuple(o[1] for o in outs), out_shape=tuple(o[0] for o in outs),
        compiler_params=_params("arbitrary"), name="mixer_in",
    )(x, *args)


def _mixer_in_prompt(x, lw, ct, st, consts, stack, layer, batch, combine=None):
    tm = 2 * MXU
    rows = (x if combine is None else combine[0]).shape[0]
    t = rows // batch
    nt = t // tm
    first = not stack
    assert first == (layer == 0)
    row = lambda w: pl.BlockSpec((tm, w), lambda i: (i, 0))
    nfx = N_FOX * FOX_DIM
    nk = N_MLA * HEAD_PAD
    nv = N_MLA * V_DIM
    sds = jax.ShapeDtypeStruct
    nd = DEPTH if first else 1
    tmaj = lambda w: pl.BlockSpec((nd, 1, w, tm), lambda i: (layer, i // nt, 0, i % nt))
    outs = [(sds((DEPTH, rows, KV_LORA), F32), pl.BlockSpec((nd, tm, KV_LORA), lambda i: (layer, i, 0))),
            (sds((DEPTH, batch, ROPE_DIM, t), F32), tmaj(ROPE_DIM)),
            (sds((DEPTH, batch, nfx, t), F32), tmaj(nfx)),
            (sds((DEPTH, batch, nfx, t), F32), tmaj(nfx)),
            (sds((DEPTH, batch, N_FOX, t), F32), tmaj(N_FOX)),
            (sds((rows, nk), BF16), row(nk)), (sds((rows, nfx), BF16), row(nfx)),
            (sds((rows, nfx), BF16), row(nfx)), (sds((rows, nfx), BF16), row(nfx)),
            (sds((rows, nk), BF16), row(nk)), (sds((rows, nv), BF16), row(nv)),
            (sds((rows, LANE), BF16), row(LANE)), (sds((rows, LANE), BF16), row(LANE))]
    if combine is None:
        in_specs, args = [row(D_MODEL)], [x]
    else:
        x2, route, z = combine
        in_specs = [row(D_MODEL), row(LANE), row(D_MODEL),
                    pl.BlockSpec((tm, D_MODEL), lambda i: (i + rows // tm, 0))]
        args = [x2, route, z, z]
        outs.append((sds((rows, D_MODEL), F32), row(D_MODEL)))
    in_specs += _mixer_specs(tm, nt) + [_full((KV_LORA, nk)), _full((KV_LORA, nv)), _full((1, LANE)),
                                        _full((MXU, MXU)), _full((6, LANE, LANE)), _full((8, LANE))]
    args += _mixer_args(lw, ct, st, consts) + [lw["w_k"], lw["w_v"], lw["gk"], consts["tri256"],
                                                consts["place"], consts["aug_ones"]]
    aliases = {}
    if stack:
        aliases = {len(args) + n: n for n in range(N_STACKED)}
        in_specs += [pl.BlockSpec(memory_space=pl.ANY)] * N_STACKED
        args += list(stack)
    return pl.pallas_call(
        functools.partial(_mixer_prompt_body, first=first, combine=combine is not None,
                          n_alias=len(stack), tiles_per_seq=nt),
        grid=(rows // tm,), in_specs=in_specs,
        out_specs=tuple(o[1] for o in outs), out_shape=tuple(o[0] for o in outs),
        scratch_shapes=[pltpu.VMEM((8, LANE), F32)],
        input_output_aliases=aliases, compiler_params=_params("arbitrary"), name="mixer_in_prompt",
    )(*args)


def _kv_emit(ckv, kr, wk_ref, wv_ref, gk_ref, s128_ref, k_ref, v_ref):
    cb = ckv.astype(BF16)
    kn = _dot(cb, wk_ref[...])
    kk = jnp.concatenate(
        [kn[:, h * HEAD_PAD:(h + 1) * HEAD_PAD] + kr for h in range(N_MLA)], axis=1)
    kk = _head_norm(kk, s128_ref[...], QK_DIM)
    gk = gk_ref[...]
    k_ref[...] = jnp.concatenate(
        [kk[:, h * HEAD_PAD:(h + 1) * HEAD_PAD] * gk for h in range(N_MLA)], axis=1).astype(BF16)
    v_ref[...] = _dot(cb, wv_ref[...]).astype(BF16)


def _kv_body(ckv_ref, kr128_ref, *rest):
    _kv_emit(ckv_ref[...], kr128_ref[...], *rest)


def _kv_cache_body(ckv_ref, krt_ref, *rest):
    tm = krt_ref.shape[-1]
    pad = lambda n: jnp.zeros((n, tm), F32)
    kr128 = jnp.concatenate([pad(ROPE_LO), krt_ref[0, 0], pad(LANE - ROPE_HI)], axis=0).T
    _kv_emit(ckv_ref[0, 0], kr128, *rest)


def _kv_expand_cache(ckv_all, krope_t, layer, lw, consts, tm):
    _, b, t, _ = ckv_all.shape
    nt = t // tm
    row = lambda w: pl.BlockSpec((tm, w), lambda bi, ti: (bi * nt + ti, 0))
    const = lambda shape: pl.BlockSpec(shape, lambda bi, ti: (0,) * len(shape))
    nk = N_MLA * HEAD_PAD
    nv = N_MLA * V_DIM
    return pl.pallas_call(
        _kv_cache_body, grid=(b, nt),
        in_specs=[pl.BlockSpec((1, 1, tm, KV_LORA), lambda bi, ti: (layer, bi, ti, 0)),
                  pl.BlockSpec((1, 1, ROPE_DIM, tm), lambda bi, ti: (layer, bi, 0, ti)),
                  const((KV_LORA, nk)), const((KV_LORA, nv)), const((1, LANE)), const((MXU, MXU))],
        out_specs=(row(nk), row(nv)),
        out_shape=(jax.ShapeDtypeStruct((b * t, nk), BF16), jax.ShapeDtypeStruct((b * t, nv), BF16)),
        compiler_params=_params("arbitrary", "arbitrary"), name="kv_expand_cache",
    )(ckv_all, krope_t, lw["w_k"], lw["w_v"], lw["gk"], consts["s128"])


def _kv_expand(ckv, kr128, lw, consts, tm):
    rows = kr128.shape[0]
    row = lambda w: pl.BlockSpec((tm, w), lambda i: (i, 0))
    nk = N_MLA * HEAD_PAD
    nv = N_MLA * V_DIM
    return pl.pallas_call(
        _kv_body, grid=(rows // tm,),
        in_specs=[row(KV_LORA), row(LANE), _full((KV_LORA, nk)), _full((KV_LORA, nv)),
                  _full((1, LANE)), _full((MXU, MXU))],
        out_specs=(row(nk), row(nv)),
        out_shape=(jax.ShapeDtypeStruct((rows, nk), BF16), jax.ShapeDtypeStruct((rows, nv), BF16)),
        compiler_params=_params("arbitrary"), name="kv_expand",
    )(ckv, kr128, lw["w_k"], lw["w_v"], lw["gk"], consts["s128"])


def _attn_prompt_body(*refs, fox, tq):
    n_in = 5 if fox else 3
    ins, (o_ref, o_scr) = refs[:n_in], refs[n_in:]
    q_ref, k_ref, v_ref = ins[:3]
    pair = pl.program_id(1)
    t = k_ref.shape[1]
    lane = lax.broadcasted_iota(jnp.int32, (tq, LANE), 1)
    r = lax.broadcasted_iota(jnp.int32, (tq, tq), 0)
    c = lax.broadcasted_iota(jnp.int32, (tq, tq), 1)
    allowed = (c <= r) if fox else ((c >> CHUNK_SHIFT) <= (r >> CHUNK_SHIFT))

    def scores(hh, i):
        lo, hi = i * tq, (i + 1) * tq
        if fox:
            qa_ref, ka_ref = ins[3:]
            qm = jnp.where((lane >> 6) == hh, q_ref[0, lo:hi, :].astype(F32), 0.0)
            qa = jnp.where((lane >> 4) == 2 * pair + hh, qa_ref[0, lo:hi, :].astype(F32), 0.0)
            q = jnp.concatenate([qm, qa], axis=1).astype(BF16)
            k = jnp.concatenate([k_ref[0, 0:hi, :], ka_ref[0, 0:hi, :]], axis=1)
        else:
            head = slice(HEAD_PAD * hh, HEAD_PAD * (hh + 1))
            q = q_ref[0, lo:hi, head]
            k = k_ref[0, 0:hi, head]
        s = _dot_nt(q, k)
        sd = jnp.where(allowed, s[:, lo:hi], -jnp.inf)
        return sd if i == 0 else jnp.concatenate([s[:, 0:lo], sd], axis=1)

    def softmax(s):
        p = jnp.exp2(s - jnp.max(s, axis=1, keepdims=True))
        return p.astype(BF16), jnp.sum(p, axis=1, keepdims=True)

    def values(hh, i, p, l):
        lo, hi = i * tq, (i + 1) * tq
        o_scr[hh, lo:hi, :] = _dot(p, v_ref[0, 0:hi, :]) / l

    order = list(reversed(range(t // tq)))
    for g in range(0, len(order), GROUP):
        units = [(hh, i) for i in order[g:g + GROUP] for hh in range(2)]
        ss = [scores(hh, i) for hh, i in units]
        pl_ = [softmax(s) for s in ss]
        for (hh, i), (p, l) in zip(units, pl_):
            values(hh, i, p, l)

    lane_t = lax.broadcasted_iota(jnp.int32, (t, LANE), 1)
    o_ref[0] = jnp.where(lane_t < V_DIM, o_scr[0], o_scr[1]).astype(BF16)


def _attn_prompt(q, k, v, aug, tq):
    fox = aug is not None
    b, t, _ = q.shape
    qw = LANE if fox else 2 * HEAD_PAD
    blk = lambda w, f: pl.BlockSpec((1, t, w), f)
    per_pair = lambda bi, p: (bi, 0, p)
    shared = lambda bi, p: (bi, 0, 0)
    in_specs = [blk(qw, per_pair), blk(qw, per_pair), blk(LANE, per_pair)]
    args = [q, k, v]
    if fox:
        in_specs += [blk(LANE, shared), blk(LANE, shared)]
        args += list(aug)
    return pl.pallas_call(
        functools.partial(_attn_prompt_body, fox=fox, tq=tq),
        grid=(b, N_MLA // 2), in_specs=in_specs, out_specs=blk(LANE, per_pair),
        out_shape=jax.ShapeDtypeStruct((b, t, N_MLA * V_DIM), BF16),
        scratch_shapes=[pltpu.VMEM((2, t, LANE), F32)],
        compiler_params=_params("arbitrary", "arbitrary"),
        name="attn_fox" if fox else "attn_mla",
    )(*args)


def _pad_rows(x, rows):
    return jnp.concatenate([x, jnp.zeros((rows - x.shape[0],) + x.shape[1:], x.dtype)], axis=0)


def _step_softmax(s_c, s_n):
    m = jnp.maximum(jnp.max(s_c, axis=1, keepdims=True), jnp.max(s_n, axis=1, keepdims=True))
    p_c = jnp.exp2(s_c - m)
    p_n = jnp.exp2(s_n - m)
    l = jnp.sum(p_c, axis=1, keepdims=True) + jnp.sum(p_n, axis=1, keepdims=True)
    return p_c.astype(BF16), p_n.astype(BF16), l


def _attn_mla_step_body(q_ref, kc_ref, vc_ref, kn_ref, vn_ref, o_ref, *, start):
    tq = q_ref.shape[1]
    tc = kc_ref.shape[1]
    lane = lax.broadcasted_iota(jnp.int32, (tq, LANE), 1)
    chunk = lambda pos: pos >> CHUNK_SHIFT
    qc = chunk(start + lax.broadcasted_iota(jnp.int32, (tq, 1), 0))
    ok_c = chunk(lax.broadcasted_iota(jnp.int32, (tq, tc), 1)) <= qc
    ok_n = (lane < tq) & (chunk(start + lane) <= qc)
    for p in range(N_MLA // 2):
        pair = slice(LANE * p, LANE * (p + 1))
        vc = vc_ref[0, :, pair]
        vn = _pad_rows(vn_ref[0, :, pair], LANE)
        outs = []
        for h in (2 * p, 2 * p + 1):
            head = slice(HEAD_PAD * h, HEAD_PAD * (h + 1))
            q = q_ref[0, :, head]
            s_c = jnp.where(ok_c, _dot_nt(q, kc_ref[0, :, head]), -jnp.inf)
            s_n = jnp.where(ok_n, _dot_nt(q, _pad_rows(kn_ref[0, :, head], LANE)), -jnp.inf)
            p_c, p_n, l = _step_softmax(s_c, s_n)
            outs.append((_dot(p_c, vc) + _dot(p_n, vn)) / l)
        o_ref[0, :, pair] = jnp.where(lane < V_DIM, outs[0], outs[1]).astype(BF16)


def _attn_mla_step(q, kc, vc, kn, vn, start):
    b, tq, _ = q.shape
    blk = lambda a: pl.BlockSpec((1,) + a.shape[1:], lambda bi: (bi, 0, 0))
    return pl.pallas_call(
        functools.partial(_attn_mla_step_body, start=start), grid=(b,),
        in_specs=[blk(q), blk(kc), blk(vc), blk(kn), blk(vn)],
        out_specs=pl.BlockSpec((1, tq, N_MLA * V_DIM), lambda bi: (bi, 0, 0)),
        out_shape=jax.ShapeDtypeStruct((b, tq, N_MLA * V_DIM), BF16),
        compiler_params=_params("arbitrary"), name="attn_mla_step",
    )(q, kc, vc, kn, vn)


def _lane_cumsum(x, triu):
    rows, blk = x.shape[0], triu.shape[0]
    carry = jnp.zeros((rows, 1), F32)
    out = []
    for b0 in range(0, x.shape[1], blk):
        hi, mid, lo = _split3(_pad_rows(x[:, b0:b0 + blk], 16))
        c = (_dot(hi, triu) + _dot(mid, triu) + _dot(lo, triu))[0:rows] + carry
        carry = c[:, blk - 1:blk]
        out.append(c)
    return jnp.concatenate(out, axis=1), carry


def _attn_fox_step_body(q_ref, kt_ref, vt_ref, kn_ref, vn_ref, lft_ref, lfn_ref,
                        triu_ref, tri_ref, triu_s_ref, o_ref):
    tq = q_ref.shape[1]
    lane = lax.broadcasted_iota(jnp.int32, (tq, LANE), 1)
    causal = lane <= lax.broadcasted_iota(jnp.int32, (tq, LANE), 0)
    cc, total = _lane_cumsum(lft_ref[0, 0], triu_ref[...])
    lfn = _pad_rows(lfn_ref[0], LANE)
    hi, mid, lo = _split3(lfn)
    tri = tri_ref[...]
    cn = (_dot(tri, hi) + _dot(tri, mid) + _dot(tri, lo))[0:tq]
    cnt, _ = _lane_cumsum(lfn.T[0:N_FOX, :], triu_s_ref[...])
    for p in range(N_FOX // 2):
        pair = slice(LANE * p, LANE * (p + 1))
        kt = kt_ref[0, 0, pair, :].astype(BF16)
        vt = vt_ref[0, 0, pair, :].astype(BF16)
        kn = _pad_rows(kn_ref[0, :, pair], LANE)
        vn = _pad_rows(vn_ref[0, :, pair], LANE)
        qf = q_ref[0, :, pair].astype(F32)
        outs = []
        for hh in range(2):
            h = 2 * p + hh
            q = jnp.where((lane >> 6) == hh, qf, 0.0).astype(BF16)
            col = jnp.sum(jnp.where(lane == h, cn, 0.0), axis=1, keepdims=True)
            bias_c = ((total[h:h + 1, :] + col) - cc[h:h + 1, :]) * LOG2E
            bias_n = (col - cnt[h:h + 1, :]) * LOG2E
            s_c = _dot(q, kt) + bias_c
            s_n = jnp.where(causal, _dot_nt(q, kn) + bias_n, -jnp.inf)
            p_c, p_n, l = _step_softmax(s_c, s_n)
            outs.append((_dot_nt(p_c, vt) + _dot(p_n, vn)) / l)
        o_ref[0, :, pair] = jnp.where(lane < V_DIM, outs[0], outs[1]).astype(BF16)


def _attn_fox_step(q, kt_all, vt_all, kn, vn, lft_all, lfn, layer, consts):
    b, tq, _ = q.shape
    new = lambda a: pl.BlockSpec((1,) + a.shape[1:], lambda bi: (bi, 0, 0))
    old = lambda a: pl.BlockSpec((1, 1) + a.shape[2:], lambda bi: (layer, bi, 0, 0))
    return pl.pallas_call(
        _attn_fox_step_body, grid=(b,),
        in_specs=[new(q), old(kt_all), old(vt_all), new(kn), new(vn), old(lft_all), new(lfn),
                  _full((MXU, MXU)), _full((LANE, LANE)), _full((LANE, LANE))],
        out_specs=pl.BlockSpec((1, tq, N_FOX * FOX_DIM), lambda bi: (bi, 0, 0)),
        out_shape=jax.ShapeDtypeStruct((b, tq, N_FOX * FOX_DIM), BF16),
        compiler_params=_params("arbitrary"), name="attn_fox_step",
    )(q, kt_all, vt_all, kn, vn, lft_all, lfn, consts["triu256"], consts["tri128"], consts["triu128"])


def _post_attn_body(*refs, moe):
    if moe:
        (x_ref, om_ref, of_ref, wo_ref, gffn_ref, wr_ref, br_ref, tri_ref,
         x2_ref, xn_ref, route_ref, routet_ref, cnt_ref, carry_ref) = refs
    else:
        x_ref, om_ref, of_ref, wo_ref, gffn_ref, x2_ref, xn_ref = refs
    nm = N_MLA * V_DIM
    x2 = x_ref[...] + _dot(om_ref[...], wo_ref[0:nm, :]) + _dot(of_ref[...], wo_ref[nm:, :])
    x2_ref[...] = x2
    xn = _rms(x2, gffn_ref[...], D_MODEL)
    xh = xn.astype(BF16)
    xn_ref[...] = xn if moe else xh
    if moe:
        xl = (xn - xh.astype(F32)).astype(BF16)
        r = _dot(xh, wr_ref[...]) + _dot(xl, wr_ref[...])
        logits = r + pltpu.roll(r, LANE - N_EXPERTS, 1) + br_ref[...]
        lane = lax.broadcasted_iota(jnp.int32, logits.shape, 1).astype(F32)
        valid = lane < N_EXPERTS
        lg = jnp.where(valid, logits, -jnp.inf)
        e = jnp.exp(lg - jnp.max(lg, axis=1, keepdims=True))
        probs = e / jnp.sum(e, axis=1, keepdims=True)
        p1 = jnp.where(valid, probs, -1.0)
        m1 = jnp.max(p1, axis=1, keepdims=True)
        i1 = jnp.min(jnp.where(p1 == m1, lane, float(LANE)), axis=1, keepdims=True)
        p2 = jnp.where(lane == i1, -1.0, p1)
        m2 = jnp.max(p2, axis=1, keepdims=True)
        i2 = jnp.min(jnp.where(p2 == m2, lane, float(LANE)), axis=1, keepdims=True)
        den = m1 + m2
        @pl.when(pl.program_id(0) == 0)
        def _():
            carry_ref[...] = jnp.zeros(carry_ref.shape, F32)

        sel = jnp.where((lane == i1) | (lane == i2), 1.0, 0.0)
        incl = _dot(tri_ref[...], sel.astype(BF16)) + carry_ref[0:1, :]
        excl = incl - sel
        rank1 = jnp.sum(jnp.where(lane == i1, excl, 0.0), axis=1, keepdims=True)
        rank2 = jnp.sum(jnp.where(lane == i2, excl, 0.0), axis=1, keepdims=True)
        total = incl[incl.shape[0] - 1:, :]
        carry_ref[...] = jnp.broadcast_to(total, carry_ref.shape)
        cnt_ref[...] = jnp.broadcast_to(total, cnt_ref.shape)
        cols = (i1, i2, rank1, rank2, m1 / den, m2 / den)
        route = jnp.zeros(logits.shape, F32)
        for n, col in enumerate(cols):
            route = jnp.where(lane == n, col, route)
        route_ref[...] = route
        routet_ref[...] = route.T[0:8, :]


def _post_attn(x, om, of, lw, moe, tm, tri=None):
    rows = x.shape[0]
    row = lambda w: pl.BlockSpec((tm, w), lambda i: (i, 0))
    nm = N_MLA * V_DIM
    in_specs = [row(D_MODEL), row(nm), row(nm), _full((2 * nm, D_MODEL)), _full((1, D_MODEL))]
    args = [x, om, of, lw["w_out"], lw["g_ffn"]]
    out_specs = [row(D_MODEL), row(D_MODEL)]
    out_shape = [jax.ShapeDtypeStruct((rows, D_MODEL), F32),
                 jax.ShapeDtypeStruct((rows, D_MODEL), F32 if moe else BF16)]
    scratch = []
    if moe:
        assert tm == MXU
        in_specs += [_full((D_MODEL, LANE)), _full((1, LANE)), _full((MXU, MXU))]
        args += [lw["w_router"], lw["b_router"], tri]
        out_specs += [row(LANE), pl.BlockSpec((8, tm), lambda i: (0, i)), _full((8, LANE))]
        out_shape += [jax.ShapeDtypeStruct((rows, LANE), F32), jax.ShapeDtypeStruct((8, rows), F32),
                      jax.ShapeDtypeStruct((8, LANE), F32)]
        scratch = [pltpu.VMEM((8, LANE), F32)]
    return pl.pallas_call(
        functools.partial(_post_attn_body, moe=moe), grid=(rows // tm,),
        in_specs=in_specs, out_specs=tuple(out_specs), out_shape=tuple(out_shape),
        scratch_shapes=scratch,
        compiler_params=_params("arbitrary"), name="post_attn_moe" if moe else "post_attn",
    )(*args)


def _swiglu_acc(xb, wg_ref, wu_ref, wd_ref, d_ff, lo=0, hi=None):
    acc = None
    for c in range(lo * MXU, d_ff if hi is None else hi * MXU, MXU):
        g = _dot(xb, wg_ref[:, c:c + MXU])
        u = _dot(xb, wu_ref[:, c:c + MXU])
        h = (g * jax.nn.sigmoid(g) * u).astype(BF16)
        d = _dot(h, wd_ref[c:c + MXU, :])
        acc = d if acc is None else acc + d
    return acc


def _post_ffn_body(x_ref, om_ref, of_ref, wo_ref, gffn_ref, wg_ref, wu_ref, wd_ref, o_ref):
    nm = N_MLA * V_DIM
    x2 = x_ref[...] + _dot(om_ref[...], wo_ref[0:nm, :]) + _dot(of_ref[...], wo_ref[nm:, :])
    xn = _rms(x2, gffn_ref[...], D_MODEL).astype(BF16)
    o_ref[...] = x2 + _swiglu_acc(xn, wg_ref.at[0], wu_ref.at[0], wd_ref.at[0], D_FF)


def _post_ffn(x, om, of, lw, tm):
    rows = x.shape[0]
    row = lambda w: pl.BlockSpec((tm, w), lambda i: (i, 0))
    nm = N_MLA * V_DIM
    jl = lw["mixer_idx"]
    wspec = lambda a, b: pl.BlockSpec((1, a, b), lambda i: (jl, 0, 0))
    return pl.pallas_call(
        _post_ffn_body, grid=(rows // tm,),
        in_specs=[row(D_MODEL), row(nm), row(nm), _full((2 * nm, D_MODEL)), _full((1, D_MODEL)),
                  wspec(D_MODEL, D_FF), wspec(D_MODEL, D_FF), wspec(D_FF, D_MODEL)],
        out_specs=row(D_MODEL), out_shape=jax.ShapeDtypeStruct((rows, D_MODEL), F32),
        compiler_params=_params("arbitrary"), name="post_ffn",
    )(x, om, of, lw["w_out"], lw["g_ffn"], lw["w_gate"], lw["w_up"], lw["w_down"])


SCATTER_UNROLL = 8
SPLIT_CHUNK = 5


def _inv_body(pos_ref, base_hbm, inv_ref, sem, *, tmg):
    fill = pltpu.make_async_copy(base_hbm, inv_ref, sem.at[0])
    fill.start()
    fill.wait()

    def put(i, c):
        for u in range(SCATTER_UNROLL):
            a = i * SCATTER_UNROLL + u
            inv_ref[tmg + pos_ref[a]] = a
        return c

    lax.fori_loop(0, pos_ref.shape[0] // SCATTER_UNROLL, put, 0)


def _route_inverse(pos_flat, base, tmg):
    return pl.pallas_call(
        functools.partial(_inv_body, tmg=tmg),
        in_specs=[pl.BlockSpec(memory_space=pltpu.SMEM), pl.BlockSpec(memory_space=pl.ANY)],
        out_specs=pl.BlockSpec(memory_space=pltpu.SMEM),
        out_shape=jax.ShapeDtypeStruct(base.shape, jnp.int32),
        scratch_shapes=[pltpu.SemaphoreType.DMA((1,))], name="route_inverse",
    )(pos_flat, base)


GATHER, OUT, STAGE = 0, 2, 4


def _moe_routed_body(te_ref, stage_ref, src_ref, dst_ref, xn_hbm, wg_ref, wu_ref, wd_ref, z_hbm,
                     buf, gsem, ssem, *, tmg, n_tiles):
    del te_ref
    j = pl.program_id(0)

    def row_in(entry, slot, r):
        return pltpu.make_async_copy(xn_hbm.at[pl.ds(src_ref[entry], 1), :],
                                     buf.at[GATHER + slot, pl.ds(r, 1), :], gsem.at[slot])

    def row_out(entry, slot, r):
        return pltpu.make_async_copy(buf.at[OUT + slot, pl.ds(r, 1), :],
                                     z_hbm.at[pl.ds(dst_ref[entry], 1), :], ssem.at[0])

    def tile_in(slot):
        return pltpu.make_async_copy(xn_hbm.at[pl.ds(0, tmg), :], buf.at[GATHER + slot], gsem.at[slot])

    def tile_out(slot):
        return pltpu.make_async_copy(buf.at[OUT + slot], z_hbm.at[pl.ds(0, tmg), :], ssem.at[0])

    @pl.when(j == 0)
    def _():
        buf[OUT + 1] = jnp.zeros(buf.shape[1:], F32)
        for r in range(tmg):
            row_in(tmg + r, 0, r).start()

    def step(cur):
        nxt = 1 - cur

        @pl.when(j >= 1)
        def _():
            tile_out(cur).wait()

        tile_in(cur).wait()
        w = (wg_ref.at[0, 0], wu_ref.at[0, 0], wd_ref.at[0, 0])
        head = _swiglu_acc(buf[GATHER + cur].astype(BF16), *w, E_FF, 0, SPLIT_CHUNK)
        for r in range(tmg):
            row_in((j + 2) * tmg + r, nxt, r).start()
            row_out(j * tmg + r, nxt, r).start()
        stage = stage_ref[0]
        buf[stage] = buf[GATHER + cur]
        tail = _swiglu_acc(buf[stage].astype(BF16), *w, E_FF, SPLIT_CHUNK)
        buf[OUT + cur] = head + tail

    for parity in range(2):
        pl.when(j % 2 == parity)(functools.partial(step, parity))

    assert n_tiles % 2 == 0

    @pl.when(j == n_tiles)
    def _():
        tile_out(1).wait()
        tile_in(1).wait()


def _moe_routed(xn, tile_expert, src, dst, lw, tmg, n_tiles):
    n = xn.shape[0]
    jl = lw["mixer_idx"]
    wspec = lambda a, b: pl.BlockSpec((1, 1, a, b), lambda j, te, st, s, d: (jl, te[j], 0, 0))
    grid_spec = pltpu.PrefetchScalarGridSpec(
        num_scalar_prefetch=4, grid=(n_tiles + 1,),
        in_specs=[pl.BlockSpec(memory_space=pl.ANY), wspec(D_MODEL, E_FF), wspec(D_MODEL, E_FF),
                  wspec(E_FF, D_MODEL)],
        out_specs=pl.BlockSpec(memory_space=pl.ANY),
        scratch_shapes=[pltpu.VMEM((STAGE + 1, tmg, D_MODEL), F32),
                        pltpu.SemaphoreType.DMA((2,)), pltpu.SemaphoreType.DMA((1,))])
    return pl.pallas_call(
        functools.partial(_moe_routed_body, tmg=tmg, n_tiles=n_tiles),
        grid_spec=grid_spec, out_shape=jax.ShapeDtypeStruct((2 * n + tmg, D_MODEL), F32),
        compiler_params=_params("arbitrary"), name="moe_routed",
    )(tile_expert, jnp.full((1,), STAGE, jnp.int32), src, dst, xn,
      lw["we_gate"], lw["we_up"], lw["we_down"])


def _combine_body(x2_ref, route_ref, z0_ref, z1_ref, o_ref):
    o_ref[...] = _combine(x2_ref, route_ref, z0_ref, z1_ref)


def _moe_combine(x2, route, z, tm):
    n = x2.shape[0]
    row = lambda w: pl.BlockSpec((tm, w), lambda i: (i, 0))
    return pl.pallas_call(
        _combine_body, grid=(n // tm,),
        in_specs=[row(D_MODEL), row(LANE), row(D_MODEL),
                  pl.BlockSpec((tm, D_MODEL), lambda i: (i + n // tm, 0))],
        out_specs=row(D_MODEL), out_shape=jax.ShapeDtypeStruct((n, D_MODEL), F32),
        compiler_params=_params("arbitrary"), name="moe_combine",
    )(x2, route, z, z)


def _moe(x2, xn, route, route_t, counts, lw, tmg, defer_combine=False):
    n = x2.shape[0]
    assert n & (n - 1) == 0
    n_tiles = 2 * n // tmg + N_EXPERTS
    ext = (n_tiles + 3) * tmg
    e = route_t[0:2].astype(jnp.int32)
    rank = route_t[2:4].astype(jnp.int32)
    cnt = counts[0, :N_EXPERTS].astype(jnp.int32)
    tiles = (cnt + tmg - 1) // tmg
    tile_end = jnp.cumsum(tiles)
    row_start = (tile_end - tiles) * tmg
    start_of = jnp.sum(jnp.where(e[..., None] == jnp.arange(N_EXPERTS), row_start, 0), axis=-1)
    pos_flat = (start_of + rank).reshape(-1)
    steps = jnp.minimum(jnp.arange(n_tiles + 1, dtype=jnp.int32), tile_end[-1] - 1)
    tile_expert = jnp.minimum(jnp.sum((steps[:, None] >= tile_end[None, :]).astype(jnp.int32), axis=1),
                              N_EXPERTS - 1)
    idx = jnp.arange(ext, dtype=jnp.int32)
    dst = _route_inverse(pos_flat, 2 * n + (idx & (tmg - 1)), tmg)
    src = jnp.where(dst < 2 * n, dst, idx) & (n - 1)
    z = _moe_routed(xn, tile_expert, src, dst, lw, tmg, n_tiles)
    return (x2, route, z) if defer_combine else _moe_combine(x2, route, z, tmg)


def _consts():
    i = jnp.arange(MXU)
    blockdiag = lambda w: ((i[:, None] // w) == (i[None, :] // w)).astype(BF16)
    tri = lambda n: (jnp.arange(n)[None, :] <= jnp.arange(n)[:, None]).astype(BF16)
    r = jnp.arange(LANE)
    place = jnp.stack([((r[:, None] < N_FOX) & (r[None, :] == AUG_W * r[:, None] + k)).astype(BF16)
                       for k in range(6)])
    within = r % AUG_W
    headed = r < AUG_W * N_FOX
    ones_q = (headed & (within >= 3) & (within < 6)).astype(F32)
    ones_k = (headed & (within < 3)).astype(F32)
    aug_ones = jnp.zeros((8, LANE), F32).at[0].set(ones_q).at[1].set(ones_k)
    return {"s64": blockdiag(FOX_DIM), "s128": blockdiag(HEAD_PAD), "tri256": tri(MXU),
            "tri128": tri(LANE), "triu256": tri(MXU).T, "triu128": tri(LANE).T,
            "place": place, "aug_ones": aug_ones}


def _rope_tables(pos):
    half = ROPE_DIM // 2
    inv = ROPE_BASE ** (-jnp.arange(half, dtype=F32) / half)
    ang = pos.astype(F32)[:, None] * inv[None, :]
    cos, sin = jnp.cos(ang), jnp.sin(ang)
    t = pos.shape[0]
    ct = jnp.concatenate([jnp.ones((t, ROPE_LO), F32), cos, cos, jnp.zeros((t, LANE - ROPE_HI), F32)], 1)
    st = jnp.concatenate([jnp.zeros((t, ROPE_LO), F32), -sin, sin, jnp.zeros((t, LANE - ROPE_HI), F32)], 1)
    return ct, st


def _pad_lanes(v, width=LANE):
    return jnp.pad(v, [(0, 0)] * (v.ndim - 1) + [(0, width - v.shape[-1])])


def _prep_layer(i, p):
    half = ROPE_DIM // 2
    w_in = p["w_in"][i]
    s = [0, Q_LORA, Q_LORA + KV_LORA, Q_LORA + KV_LORA + ROPE_DIM]
    nfx = N_FOX * FOX_DIM
    s += [s[3] + nfx, s[3] + 2 * nfx, s[3] + 3 * nfx, s[3] + 3 * nfx + N_FOX]
    c_q, c_kv, k_rope = w_in[:, s[0]:s[1]], w_in[:, s[1]:s[2]], w_in[:, s[2]:s[3]]
    fq, fk, fv, f_logit = w_in[:, s[3]:s[4]], w_in[:, s[4]:s[5]], w_in[:, s[5]:s[6]], w_in[:, s[6]:s[7]]
    zc = lambda n: jnp.zeros((D_MODEL, n), F32)
    misc = jnp.concatenate([f_logit, zc(ROPE_LO - N_FOX), k_rope, zc(LANE - ROPE_HI)], axis=1)
    w_in_p = jnp.concatenate([c_q, c_kv, fq, fk, fv, misc], axis=1).astype(BF16)

    w_uq = p["w_uq"][i].reshape(Q_LORA, N_MLA, QK_DIM)
    nope, rope = w_uq[..., :NOPE_DIM], w_uq[..., NOPE_DIM:]
    rope_sw = jnp.concatenate([rope[..., half:], rope[..., :half]], axis=-1)
    zq = lambda n: jnp.zeros((Q_LORA, N_MLA, n), F32)
    qa = jnp.concatenate([nope, rope, zq(HEAD_PAD - QK_DIM)], axis=-1)
    qb = jnp.concatenate([zq(NOPE_DIM), rope_sw, zq(HEAD_PAD - QK_DIM)], axis=-1)
    w_uq_p = jnp.concatenate([qa.reshape(Q_LORA, -1), qb.reshape(Q_LORA, -1)], axis=1).astype(BF16)

    w_ukv = p["w_ukv"][i].reshape(KV_LORA, N_MLA, NOPE_DIM + V_DIM)
    w_k = _pad_lanes(w_ukv[..., :NOPE_DIM], HEAD_PAD).reshape(KV_LORA, -1).astype(BF16)
    w_v = w_ukv[..., NOPE_DIM:].reshape(KV_LORA, -1).astype(BF16)

    gq = _pad_lanes(p["g_qn_mla"][i] * (QK_DIM ** -0.5 * LOG2E))[None]
    gk = _pad_lanes(p["g_kn_mla"][i])[None]
    lw = {
        "g_mix": p["g_mix"][i][None], "w_in": w_in_p, "b_f": _pad_lanes(p["b_f"][i])[None],
        "g_cq": p["g_cq"][i][None], "g_ckv": p["g_ckv"][i][None], "w_uq": w_uq_p,
        "gq": gq, "gk": gk,
        "gfq": jnp.tile(p["g_qn_fox"][i] * (FOX_DIM ** -0.5 * LOG2E), N_FOX)[None],
        "gfk": jnp.tile(p["g_kn_fox"][i], N_FOX)[None],
        "w_k": w_k, "w_v": w_v,
        "w_out": p["w_out"][i].astype(BF16), "g_ffn": p["g_ffn"][i][None],
    }
    j = i // 2
    lw["mixer_idx"] = j
    if i % 2 == 0:
        lw.update(w_gate=p["w_gate_b"], w_up=p["w_up_b"], w_down=p["w_down_b"])
    else:
        wr = p["w_router"][j]
        wr_hi = wr.astype(BF16)
        wr_lo = (wr - wr_hi.astype(F32)).astype(BF16)
        lw.update(w_router=_pad_lanes(jnp.concatenate([wr_hi, wr_lo], axis=1)),
                  b_router=_pad_lanes(p["b_router"][j])[None],
                  we_gate=p["we_gate_b"], we_up=p["we_up_b"], we_down=p["we_down_b"])
    return lw


def _channel_mixer(i, x, om, of, lw, consts, tm_post, tm_ffn, defer_combine=False):
    if i % 2 == 0:
        return _post_ffn(x, om, of, lw, tm_ffn)
    x2, xn, route, route_t, counts = _post_attn(x, om, of, lw, True, MXU, consts["tri256"])
    return _moe(x2, xn, route, route_t, counts, lw, MXU, defer_combine)


def _trunk_prompt(x, layers, consts):
    b, t, _ = x.shape
    rows = b * t
    x = x.reshape(rows, D_MODEL)
    ct, st = _rope_tables(jnp.arange(t, dtype=jnp.int32))
    nfx = N_FOX * FOX_DIM
    stack = ()
    pending = None
    for i, lw in enumerate(layers):
        outs = _mixer_in_prompt(x, lw, ct, st, consts, stack, i, b, combine=pending)
        stack = outs[:N_STACKED]
        q, fq, fkb, fvb, k, v, qaug, kaug = outs[N_STACKED:N_STACKED + 8]
        if pending is not None:
            x = outs[-1]
        om = _attn_prompt(q.reshape(b, t, -1), k.reshape(b, t, -1), v.reshape(b, t, -1), None, 256)
        of = _attn_prompt(fq.reshape(b, t, nfx), fkb.reshape(b, t, nfx), fvb.reshape(b, t, nfx),
                          (qaug.reshape(b, t, LANE), kaug.reshape(b, t, LANE)), 256)
        defer = i % 2 == 1 and i + 1 < len(layers)
        res = _channel_mixer(i, x, om.reshape(rows, -1), of.reshape(rows, -1), lw, consts, 256, 512, defer)
        x, pending = (None, res) if defer else (res, None)
    ckv_s, krt_s, fkt_s, fvt_s, lft_s = stack
    heads = lambda a: jnp.transpose(a.reshape(DEPTH, b, N_FOX, FOX_DIM, t), (0, 1, 4, 2, 3))
    state = (ckv_s.reshape(DEPTH, b, t, KV_LORA), jnp.swapaxes(krt_s, 2, 3), heads(fkt_s), heads(fvt_s),
             jnp.swapaxes(lft_s, 2, 3))
    return x.reshape(b, t, D_MODEL), state


def _trunk_sample(x, past, layers, consts):
    b, t, _ = x.shape
    rows = b * t
    start = PAST_LEN
    x = x.reshape(rows, D_MODEL)
    ct, st = _rope_tables(start + jnp.arange(t, dtype=jnp.int32))
    ct, st = jnp.tile(ct, (b, 1)), jnp.tile(st, (b, 1))
    p_ckv, p_krope, p_fk, p_fv, p_logf = past
    nfx = N_FOX * FOX_DIM
    krope_t = jnp.swapaxes(p_krope, 2, 3)
    fk_t = jnp.transpose(p_fk, (0, 1, 3, 4, 2)).reshape(DEPTH, b, nfx, start)
    fv_t = jnp.transpose(p_fv, (0, 1, 3, 4, 2)).reshape(DEPTH, b, nfx, start)
    lf_t = jnp.swapaxes(p_logf, 2, 3)
    state = ([], [], [], [], [])
    for i, lw in enumerate(layers):
        ckv, kr, kr128, fk, fv, lf128, lf, q, fq, fkb, fvb = _mixer_in(x, lw, ct, st, consts, rows)
        k_c, v_c = _kv_expand_cache(p_ckv, krope_t, i, lw, consts, 512)
        k_n, v_n = _kv_expand(ckv, kr128, lw, consts, rows)
        om = _attn_mla_step(q.reshape(b, t, -1), k_c.reshape(b, start, -1), v_c.reshape(b, start, -1),
                            k_n.reshape(b, t, -1), v_n.reshape(b, t, -1), start)
        of = _attn_fox_step(fq.reshape(b, t, nfx), fk_t, fv_t, fkb.reshape(b, t, nfx),
                            fvb.reshape(b, t, nfx), lf_t, lf128.reshape(b, t, LANE), i, consts)
        x = _channel_mixer(i, x, om.reshape(rows, -1), of.reshape(rows, -1), lw, consts, rows, rows)
        for lst, s_ in zip(state, (ckv.reshape(b, t, KV_LORA), kr.reshape(b, t, ROPE_DIM),
                                   fk.reshape(b, t, N_FOX, FOX_DIM), fv.reshape(b, t, N_FOX, FOX_DIM),
                                   lf.reshape(b, t, N_FOX))):
            lst.append(s_)
    return x.reshape(b, t, D_MODEL), tuple(jnp.stack(s_) for s_ in state)


def kernel(x_prompt, x_sample, cache_mla_ckv, cache_mla_krope, cache_fox_k, cache_fox_v, cache_fox_logf, g_mix, w_in, b_f, g_cq, g_ckv, w_uq, w_ukv, g_qn_mla, g_kn_mla, g_qn_fox, g_kn_fox, w_out, g_ffn, w_gate, w_up, w_down, w_router, b_router, we_gate, we_up, we_down):
    p = dict(g_mix=g_mix, w_in=w_in, b_f=b_f, g_cq=g_cq, g_ckv=g_ckv, w_uq=w_uq, w_ukv=w_ukv,
             g_qn_mla=g_qn_mla, g_kn_mla=g_kn_mla, g_qn_fox=g_qn_fox, g_kn_fox=g_kn_fox,
             w_out=w_out, g_ffn=g_ffn, w_gate=w_gate, w_up=w_up, w_down=w_down,
             w_router=w_router, b_router=b_router, we_gate=we_gate, we_up=we_up, we_down=we_down)
    for name in ("w_gate", "w_up", "w_down", "we_gate", "we_up", "we_down"):
        p[name + "_b"] = p[name].astype(BF16)
    layers = [_prep_layer(i, p) for i in range(DEPTH)]
    consts = _consts()
    y_p, st_p = _trunk_prompt(x_prompt, layers, consts)
    past = (cache_mla_ckv, cache_mla_krope, cache_fox_k, cache_fox_v, cache_fox_logf)
    y_s, st_s = _trunk_sample(x_sample, past, layers, consts)
    return (y_p, y_s) + st_p + st_s
```

```python
import functools

import jax
import jax.numpy as jnp
from jax import lax
from jax.experimental import pallas as pl
from jax.experimental.pallas import tpu as pltpu

F32 = jnp.float32
BF16 = jnp.bfloat16

D_MODEL = 1024
DEPTH = 4
PAST_LEN = 2048
CHUNK_SHIFT = 6
EPS = 1e-6
ROPE_BASE = 10000.0

N_MLA = 8
Q_LORA = 384
KV_LORA = 256
NOPE_DIM = 64
ROPE_DIM = 32
QK_DIM = NOPE_DIM + ROPE_DIM
V_DIM = 64
N_FOX = 8
FOX_DIM = 64
D_FF = 2816
N_EXPERTS = 8
E_FF = 1792

LOG2E = 1.4426950408889634
LANE = 128
MXU = 256
HEAD_PAD = 128
ROPE_LO = NOPE_DIM
ROPE_HI = NOPE_DIM + ROPE_DIM
AUG_W = 16
GROUP = 2

CQ_LO, CQ_HI = 0, Q_LORA
CKV_LO, CKV_HI = CQ_HI, CQ_HI + KV_LORA
FQ_LO, FQ_HI = CKV_HI, CKV_HI + N_FOX * FOX_DIM
FK_LO, FK_HI = FQ_HI, FQ_HI + N_FOX * FOX_DIM
FV_LO, FV_HI = FK_HI, FK_HI + N_FOX * FOX_DIM
MISC_LO, MISC_HI = FV_HI, FV_HI + LANE
D_IN_P = MISC_HI

VMEM_LIMIT = 56 * 1024 * 1024


def _params(*sem):
    return pltpu.CompilerParams(dimension_semantics=sem, vmem_limit_bytes=VMEM_LIMIT)


def _dot(a, b):
    return jnp.dot(a, b, preferred_element_type=F32)


def _dot_nt(a, b):
    return lax.dot_general(a, b, (((1,), (1,)), ((), ())), preferred_element_type=F32)


def _rms(x, g, n):
    return x * lax.rsqrt(jnp.sum(x * x, axis=-1, keepdims=True) * (1.0 / n) + EPS) * g


def _head_norm(x, smat, dim):
    outs = []
    for c in range(0, x.shape[1], MXU):
        xc = x[:, c:c + MXU]
        ss = _dot((xc * xc).astype(BF16), smat)
        outs.append(xc * lax.rsqrt(ss * (1.0 / dim) + EPS))
    return jnp.concatenate(outs, axis=1)


def _full(shape):
    return pl.BlockSpec(shape, lambda *_: (0,) * len(shape))


def _split3(c):
    hi = c.astype(BF16)
    r1 = c - hi.astype(F32)
    mid = r1.astype(BF16)
    lo = (r1 - mid.astype(F32)).astype(BF16)
    return hi, mid, lo


def _combine(x2_ref, route_ref, z0_ref, z1_ref):
    route = route_ref[...]
    lane = lax.broadcasted_iota(jnp.int32, route.shape, 1)
    g1 = jnp.sum(jnp.where(lane == 4, route, 0.0), axis=1, keepdims=True)
    g2 = jnp.sum(jnp.where(lane == 5, route, 0.0), axis=1, keepdims=True)
    return x2_ref[...] + g1 * z0_ref[...] + g2 * z1_ref[...]


N_MIXER_W = 13


def _mixer_core(x, gmix_ref, win_ref, bf_ref, gcq_ref, gckv_ref, wuq_ref, gq_ref,
                gfq_ref, gfk_ref, ct_ref, st_ref, s64_ref, s128_ref):
    tm = x.shape[0]
    xb = _rms(x, gmix_ref[...], D_MODEL).astype(BF16)

    def proj(lo, hi):
        return _dot(xb, win_ref[:, lo:hi])

    ct = ct_ref[...]
    st = st_ref[...]
    lane = lax.broadcasted_iota(jnp.int32, (tm, LANE), 1)

    cq_raw = proj(CQ_LO, CQ_HI)
    a = proj(MISC_LO, MISC_HI)
    ckv_raw = proj(CKV_LO, CKV_HI)
    fk_raw = proj(FK_LO, FK_HI)
    fv = proj(FV_LO, FV_HI)
    fq_raw = proj(FQ_LO, FQ_HI)

    cq = _rms(cq_raw, gcq_ref[...], Q_LORA).astype(BF16)
    nq = N_MLA * HEAD_PAD
    qa = _dot(cq, wuq_ref[:, 0:nq])
    qb = _dot(cq, wuq_ref[:, nq:2 * nq])

    z = a + bf_ref[...]
    lf = jnp.minimum(z, 0.0) - jnp.log1p(jnp.exp(-jnp.abs(z)))
    lf128 = jnp.where(lane < N_FOX, lf, 0.0)
    half = ROPE_DIM // 2
    sw = jnp.where(lane < ROPE_LO + half, pltpu.roll(a, LANE - half, 1), pltpu.roll(a, half, 1))
    kr128 = jnp.where((lane >= ROPE_LO) & (lane < ROPE_HI), a * ct + sw * st, 0.0)
    ckv = _rms(ckv_raw, gckv_ref[...], KV_LORA)

    qr = jnp.concatenate(
        [qa[:, h * HEAD_PAD:(h + 1) * HEAD_PAD] * ct + qb[:, h * HEAD_PAD:(h + 1) * HEAD_PAD] * st
         for h in range(N_MLA)], axis=1)
    qn = _head_norm(qr, s128_ref[...], QK_DIM)
    gq = gq_ref[...]
    q = jnp.concatenate(
        [qn[:, h * HEAD_PAD:(h + 1) * HEAD_PAD] * gq for h in range(N_MLA)], axis=1).astype(BF16)

    s64 = s64_ref[...]
    fk = _head_norm(fk_raw, s64, FOX_DIM) * gfk_ref[...]
    fq = (_head_norm(fq_raw, s64, FOX_DIM) * gfq_ref[...]).astype(BF16)
    return ckv, kr128, lf128, fk, fv, fq, q


def _mixer_in_body(x_ref, *refs):
    (ckv_ref, kr_ref, kr128_ref, fk_ref, fv_ref, lf128_ref, lf_ref,
     q_ref, fq_ref, fkb_ref, fvb_ref) = refs[N_MIXER_W:]
    ckv, kr128, lf128, fk, fv, fq, q = _mixer_core(x_ref[...], *refs[:N_MIXER_W])
    ckv_ref[...] = ckv
    kr128_ref[...] = kr128
    kr_ref[...] = kr128[:, ROPE_LO:ROPE_HI]
    lf128_ref[...] = lf128
    lf_ref[...] = lf128[:, 0:N_FOX]
    fk_ref[...] = fk
    fv_ref[...] = fv
    fkb_ref[...] = fk.astype(BF16)
    fvb_ref[...] = fv.astype(BF16)
    fq_ref[...] = fq
    q_ref[...] = q


def _fox_aug(c, place_ref, ones_ref):
    hi, mid, lo = _split3(c * LOG2E)
    qa = _dot(hi, place_ref[0]) + _dot(mid, place_ref[1]) + _dot(lo, place_ref[2]) + ones_ref[0:1, :]
    ka = ones_ref[1:2, :] - (_dot(hi, place_ref[3]) + _dot(mid, place_ref[4]) + _dot(lo, place_ref[5]))
    return qa.astype(BF16), ka.astype(BF16)


N_STACKED = 5


def _mixer_prompt_body(*refs, first, combine, n_alias, tiles_per_seq):
    n_x = 4 if combine else 1
    n_in = n_x + N_MIXER_W + 6 + n_alias
    weights = refs[n_x:n_x + N_MIXER_W]
    wk_ref, wv_ref, gk_ref, tri_ref, place_ref, ones_ref = refs[n_x + N_MIXER_W:n_x + N_MIXER_W + 6]
    outs = list(refs[n_in:])
    carry_ref = outs.pop()
    (ckv_ref, kr_ref, fk_ref, fv_ref, lf_ref, q_ref, fq_ref, fkb_ref, fvb_ref,
     k_ref, v_ref, qa_ref, ka_ref) = outs[:13]
    if combine:
        x = _combine(*refs[:4])
        outs[13][...] = x
    else:
        x = refs[0][...]
    ckv, kr128, lf128, fk, fv, fq, q = _mixer_core(x, *weights)
    q_ref[...] = q
    fq_ref[...] = fq
    fkb_ref[...] = fk.astype(BF16)
    fvb_ref[...] = fv.astype(BF16)
    ckv_ref[0] = ckv
    kr_ref[0, 0] = kr128.T[ROPE_LO:ROPE_HI, :]
    lf_ref[0, 0] = lf128.T[0:N_FOX, :]
    fk_ref[0, 0] = fk.T
    fv_ref[0, 0] = fv.T
    if first:
        for ref in (ckv_ref, kr_ref, lf_ref, fk_ref, fv_ref):
            ref[1:] = jnp.zeros((ref.shape[0] - 1,) + ref.shape[1:], F32)

    _kv_emit(ckv, kr128, wk_ref, wv_ref, gk_ref, weights[-1], k_ref, v_ref)

    @pl.when(pl.program_id(0) % tiles_per_seq == 0)
    def _():
        carry_ref[...] = jnp.zeros(carry_ref.shape, F32)

    tri = tri_ref[...]
    carry = carry_ref[0:1, :]
    for r0 in range(0, lf128.shape[0], MXU):
        hi, mid, lo = _split3(lf128[r0:r0 + MXU, :])
        c = _dot(tri, hi) + _dot(tri, mid) + _dot(tri, lo) + carry
        carry = c[MXU - 1:, :]
        qa, ka = _fox_aug(c, place_ref, ones_ref)
        qa_ref[r0:r0 + MXU, :] = qa
        ka_ref[r0:r0 + MXU, :] = ka
    carry_ref[...] = jnp.broadcast_to(carry, carry_ref.shape)


def _mixer_specs(tm, nt):
    nfx = N_FOX * FOX_DIM
    tab = pl.BlockSpec((tm, LANE), lambda i: (i % nt, 0))
    return [_full((1, D_MODEL)), _full((D_MODEL, D_IN_P)), _full((1, LANE)),
            _full((1, Q_LORA)), _full((1, KV_LORA)), _full((Q_LORA, 2 * N_MLA * HEAD_PAD)),
            _full((1, LANE)), _full((1, nfx)), _full((1, nfx)), tab, tab,
            _full((MXU, MXU)), _full((MXU, MXU))]


def _mixer_args(lw, ct, st, consts):
    return [lw["g_mix"], lw["w_in"], lw["b_f"], lw["g_cq"], lw["g_ckv"], lw["w_uq"], lw["gq"],
            lw["gfq"], lw["gfk"], ct, st, consts["s64"], consts["s128"]]


def _mixer_in(x, lw, ct, st, consts, tm):
    rows = x.shape[0]
    nt = ct.shape[0] // tm
    row = lambda w: pl.BlockSpec((tm, w), lambda i: (i, 0))
    nfx = N_FOX * FOX_DIM
    sds = jax.ShapeDtypeStruct
    outs = [(sds((rows, KV_LORA), F32), row(KV_LORA)), (sds((rows, ROPE_DIM), F32), row(ROPE_DIM)),
            (sds((rows, LANE), F32), row(LANE)), (sds((rows, nfx), F32), row(nfx)),
            (sds((rows, nfx), F32), row(nfx)), (sds((rows, LANE), F32), row(LANE)),
            (sds((rows, N_FOX), F32), row(N_FOX)),
            (sds((rows, N_MLA * HEAD_PAD), BF16), row(N_MLA * HEAD_PAD)),
            (sds((rows, nfx), BF16), row(nfx)), (sds((rows, nfx), BF16), row(nfx)),
            (sds((rows, nfx), BF16), row(nfx))]
    args = _mixer_args(lw, ct, st, consts)
    assert len(args) == N_MIXER_W
    return pl.pallas_call(
        _mixer_in_body, grid=(rows // tm,), in_specs=[row(D_MODEL)] + _mixer_specs(tm, nt),
        out_specs=tuple(o[1] for o in outs), out_shape=tuple(o[0] for o in outs),
        compiler_params=_params("arbitrary"), name="mixer_in",
    )(x, *args)


def _mixer_in_prompt(x, lw, ct, st, consts, stack, layer, batch, combine=None):
    tm = 2 * MXU
    rows = (x if combine is None else combine[0]).shape[0]
    t = rows // batch
    nt = t // tm
    first = not stack
    assert first == (layer == 0)
    row = lambda w: pl.BlockSpec((tm, w), lambda i: (i, 0))
    nfx = N_FOX * FOX_DIM
    nk = N_MLA * HEAD_PAD
    nv = N_MLA * V_DIM
    sds = jax.ShapeDtypeStruct
    nd = DEPTH if first else 1
    tmaj = lambda w: pl.BlockSpec((nd, 1, w, tm), lambda i: (layer, i // nt, 0, i % nt))
    outs = [(sds((DEPTH, rows, KV_LORA), F32), pl.BlockSpec((nd, tm, KV_LORA), lambda i: (layer, i, 0))),
            (sds((DEPTH, batch, ROPE_DIM, t), F32), tmaj(ROPE_DIM)),
            (sds((DEPTH, batch, nfx, t), F32), tmaj(nfx)),
            (sds((DEPTH, batch, nfx, t), F32), tmaj(nfx)),
            (sds((DEPTH, batch, N_FOX, t), F32), tmaj(N_FOX)),
            (sds((rows, nk), BF16), row(nk)), (sds((rows, nfx), BF16), row(nfx)),
            (sds((rows, nfx), BF16), row(nfx)), (sds((rows, nfx), BF16), row(nfx)),
            (sds((rows, nk), BF16), row(nk)), (sds((rows, nv), BF16), row(nv)),
            (sds((rows, LANE), BF16), row(LANE)), (sds((rows, LANE), BF16), row(LANE))]
    if combine is None:
        in_specs, args = [row(D_MODEL)], [x]
    else:
        x2, route, z = combine
        in_specs = [row(D_MODEL), row(LANE), row(D_MODEL),
                    pl.BlockSpec((tm, D_MODEL), lambda i: (i + rows // tm, 0))]
        args = [x2, route, z, z]
        outs.append((sds((rows, D_MODEL), F32), row(D_MODEL)))
    in_specs += _mixer_specs(tm, nt) + [_full((KV_LORA, nk)), _full((KV_LORA, nv)), _full((1, LANE)),
                                        _full((MXU, MXU)), _full((6, LANE, LANE)), _full((8, LANE))]
    args += _mixer_args(lw, ct, st, consts) + [lw["w_k"], lw["w_v"], lw["gk"], consts["tri256"],
                                                consts["place"], consts["aug_ones"]]
    aliases = {}
    if stack:
        aliases = {len(args) + n: n for n in range(N_STACKED)}
        in_specs += [pl.BlockSpec(memory_space=pl.ANY)] * N_STACKED
        args += list(stack)
    return pl.pallas_call(
        functools.partial(_mixer_prompt_body, first=first, combine=combine is not None,
                          n_alias=len(stack), tiles_per_seq=nt),
        grid=(rows // tm,), in_specs=in_specs,
        out_specs=tuple(o[1] for o in outs), out_shape=tuple(o[0] for o in outs),
        scratch_shapes=[pltpu.VMEM((8, LANE), F32)],
        input_output_aliases=aliases, compiler_params=_params("arbitrary"), name="mixer_in_prompt",
    )(*args)


def _kv_emit(ckv, kr, wk_ref, wv_ref, gk_ref, s128_ref, k_ref, v_ref):
    cb = ckv.astype(BF16)
    kn = _dot(cb, wk_ref[...])
    kk = jnp.concatenate(
        [kn[:, h * HEAD_PAD:(h + 1) * HEAD_PAD] + kr for h in range(N_MLA)], axis=1)
    kk = _head_norm(kk, s128_ref[...], QK_DIM)
    gk = gk_ref[...]
    k_ref[...] = jnp.concatenate(
        [kk[:, h * HEAD_PAD:(h + 1) * HEAD_PAD] * gk for h in range(N_MLA)], axis=1).astype(BF16)
    v_ref[...] = _dot(cb, wv_ref[...]).astype(BF16)


def _kv_body(ckv_ref, kr128_ref, *rest):
    _kv_emit(ckv_ref[...], kr128_ref[...], *rest)


def _kv_cache_body(ckv_ref, krt_ref, *rest):
    tm = krt_ref.shape[-1]
    pad = lambda n: jnp.zeros((n, tm), F32)
    kr128 = jnp.concatenate([pad(ROPE_LO), krt_ref[0, 0], pad(LANE - ROPE_HI)], axis=0).T
    _kv_emit(ckv_ref[0, 0], kr128, *rest)


def _kv_expand_cache(ckv_all, krope_t, layer, lw, consts, tm):
    _, b, t, _ = ckv_all.shape
    nt = t // tm
    row = lambda w: pl.BlockSpec((tm, w), lambda bi, ti: (bi * nt + ti, 0))
    const = lambda shape: pl.BlockSpec(shape, lambda bi, ti: (0,) * len(shape))
    nk = N_MLA * HEAD_PAD
    nv = N_MLA * V_DIM
    return pl.pallas_call(
        _kv_cache_body, grid=(b, nt),
        in_specs=[pl.BlockSpec((1, 1, tm, KV_LORA), lambda bi, ti: (layer, bi, ti, 0)),
                  pl.BlockSpec((1, 1, ROPE_DIM, tm), lambda bi, ti: (layer, bi, 0, ti)),
                  const((KV_LORA, nk)), const((KV_LORA, nv)), const((1, LANE)), const((MXU, MXU))],
        out_specs=(row(nk), row(nv)),
        out_shape=(jax.ShapeDtypeStruct((b * t, nk), BF16), jax.ShapeDtypeStruct((b * t, nv), BF16)),
        compiler_params=_params("arbitrary", "arbitrary"), name="kv_expand_cache",
    )(ckv_all, krope_t, lw["w_k"], lw["w_v"], lw["gk"], consts["s128"])


def _kv_expand(ckv, kr128, lw, consts, tm):
    rows = kr128.shape[0]
    row = lambda w: pl.BlockSpec((tm, w), lambda i: (i, 0))
    nk = N_MLA * HEAD_PAD
    nv = N_MLA * V_DIM
    return pl.pallas_call(
        _kv_body, grid=(rows // tm,),
        in_specs=[row(KV_LORA), row(LANE), _full((KV_LORA, nk)), _full((KV_LORA, nv)),
                  _full((1, LANE)), _full((MXU, MXU))],
        out_specs=(row(nk), row(nv)),
        out_shape=(jax.ShapeDtypeStruct((rows, nk), BF16), jax.ShapeDtypeStruct((rows, nv), BF16)),
        compiler_params=_params("arbitrary"), name="kv_expand",
    )(ckv, kr128, lw["w_k"], lw["w_v"], lw["gk"], consts["s128"])


def _attn_prompt_body(*refs, fox, tq):
    n_in = 5 if fox else 3
    ins, (o_ref, o_scr) = refs[:n_in], refs[n_in:]
    q_ref, k_ref, v_ref = ins[:3]
    pair = pl.program_id(1)
    t = k_ref.shape[1]
    lane = lax.broadcasted_iota(jnp.int32, (tq, LANE), 1)
    r = lax.broadcasted_iota(jnp.int32, (tq, tq), 0)
    c = lax.broadcasted_iota(jnp.int32, (tq, tq), 1)
    allowed = (c <= r) if fox else ((c >> CHUNK_SHIFT) <= (r >> CHUNK_SHIFT))

    def scores(hh, i):
        lo, hi = i * tq, (i + 1) * tq
        if fox:
            qa_ref, ka_ref = ins[3:]
            qm = jnp.where((lane >> 6) == hh, q_ref[0, lo:hi, :].astype(F32), 0.0)
            qa = jnp.where((lane >> 4) == 2 * pair + hh, qa_ref[0, lo:hi, :].astype(F32), 0.0)
            q = jnp.concatenate([qm, qa], axis=1).astype(BF16)
            k = jnp.concatenate([k_ref[0, 0:hi, :], ka_ref[0, 0:hi, :]], axis=1)
        else:
            head = slice(HEAD_PAD * hh, HEAD_PAD * (hh + 1))
            q = q_ref[0, lo:hi, head]
            k = k_ref[0, 0:hi, head]
        s = _dot_nt(q, k)
        sd = jnp.where(allowed, s[:, lo:hi], -jnp.inf)
        return sd if i == 0 else jnp.concatenate([s[:, 0:lo], sd], axis=1)

    def softmax(s):
        p = jnp.exp2(s - jnp.max(s, axis=1, keepdims=True))
        return p.astype(BF16), jnp.sum(p, axis=1, keepdims=True)

    def values(hh, i, p, l):
        lo, hi = i * tq, (i + 1) * tq
        o_scr[hh, lo:hi, :] = _dot(p, v_ref[0, 0:hi, :]) / l

    order = list(reversed(range(t // tq)))
    for g in range(0, len(order), GROUP):
        units = [(hh, i) for i in order[g:g + GROUP] for hh in range(2)]
        ss = [scores(hh, i) for hh, i in units]
        pl_ = [softmax(s) for s in ss]
        for (hh, i), (p, l) in zip(units, pl_):
            values(hh, i, p, l)

    lane_t = lax.broadcasted_iota(jnp.int32, (t, LANE), 1)
    o_ref[0] = jnp.where(lane_t < V_DIM, o_scr[0], o_scr[1]).astype(BF16)


def _attn_prompt(q, k, v, aug, tq):
    fox = aug is not None
    b, t, _ = q.shape
    qw = LANE if fox else 2 * HEAD_PAD
    blk = lambda w, f: pl.BlockSpec((1, t, w), f)
    per_pair = lambda bi, p: (bi, 0, p)
    shared = lambda bi, p: (bi, 0, 0)
    in_specs = [blk(qw, per_pair), blk(qw, per_pair), blk(LANE, per_pair)]
    args = [q, k, v]
    if fox:
        in_specs += [blk(LANE, shared), blk(LANE, shared)]
        args += list(aug)
    return pl.pallas_call(
        functools.partial(_attn_prompt_body, fox=fox, tq=tq),
        grid=(b, N_MLA // 2), in_specs=in_specs, out_specs=blk(LANE, per_pair),
        out_shape=jax.ShapeDtypeStruct((b, t, N_MLA * V_DIM), BF16),
        scratch_shapes=[pltpu.VMEM((2, t, LANE), F32)],
        compiler_params=_params("arbitrary", "arbitrary"),
        name="attn_fox" if fox else "attn_mla",
    )(*args)


def _pad_rows(x, rows):
    return jnp.concatenate([x, jnp.zeros((rows - x.shape[0],) + x.shape[1:], x.dtype)], axis=0)


def _step_softmax(s_c, s_n):
    m = jnp.maximum(jnp.max(s_c, axis=1, keepdims=True), jnp.max(s_n, axis=1, keepdims=True))
    p_c = jnp.exp2(s_c - m)
    p_n = jnp.exp2(s_n - m)
    l = jnp.sum(p_c, axis=1, keepdims=True) + jnp.sum(p_n, axis=1, keepdims=True)
    return p_c.astype(BF16), p_n.astype(BF16), l


def _attn_mla_step_body(q_ref, kc_ref, vc_ref, kn_ref, vn_ref, o_ref, *, start):
    tq = q_ref.shape[1]
    tc = kc_ref.shape[1]
    lane = lax.broadcasted_iota(jnp.int32, (tq, LANE), 1)
    chunk = lambda pos: pos >> CHUNK_SHIFT
    qc = chunk(start + lax.broadcasted_iota(jnp.int32, (tq, 1), 0))
    ok_c = chunk(lax.broadcasted_iota(jnp.int32, (tq, tc), 1)) <= qc
    ok_n = (lane < tq) & (chunk(start + lane) <= qc)
    for p in range(N_MLA // 2):
        pair = slice(LANE * p, LANE * (p + 1))
        vc = vc_ref[0, :, pair]
        vn = _pad_rows(vn_ref[0, :, pair], LANE)
        outs = []
        for h in (2 * p, 2 * p + 1):
            head = slice(HEAD_PAD * h, HEAD_PAD * (h + 1))
            q = q_ref[0, :, head]
            s_c = jnp.where(ok_c, _dot_nt(q, kc_ref[0, :, head]), -jnp.inf)
            s_n = jnp.where(ok_n, _dot_nt(q, _pad_rows(kn_ref[0, :, head], LANE)), -jnp.inf)
            p_c, p_n, l = _step_softmax(s_c, s_n)
            outs.append((_dot(p_c, vc) + _dot(p_n, vn)) / l)
        o_ref[0, :, pair] = jnp.where(lane < V_DIM, outs[0], outs[1]).astype(BF16)


def _attn_mla_step(q, kc, vc, kn, vn, start):
    b, tq, _ = q.shape
    blk = lambda a: pl.BlockSpec((1,) + a.shape[1:], lambda bi: (bi, 0, 0))
    return pl.pallas_call(
        functools.partial(_attn_mla_step_body, start=start), grid=(b,),
        in_specs=[blk(q), blk(kc), blk(vc), blk(kn), blk(vn)],
        out_specs=pl.BlockSpec((1, tq, N_MLA * V_DIM), lambda bi: (bi, 0, 0)),
        out_shape=jax.ShapeDtypeStruct((b, tq, N_MLA * V_DIM), BF16),
        compiler_params=_params("arbitrary"), name="attn_mla_step",
    )(q, kc, vc, kn, vn)


def _lane_cumsum(x, triu):
    rows, blk = x.shape[0], triu.shape[0]
    carry = jnp.zeros((rows, 1), F32)
    out = []
    for b0 in range(0, x.shape[1], blk):
        hi, mid, lo = _split3(_pad_rows(x[:, b0:b0 + blk], 16))
        c = (_dot(hi, triu) + _dot(mid, triu) + _dot(lo, triu))[0:rows] + carry
        carry = c[:, blk - 1:blk]
        out.append(c)
    return jnp.concatenate(out, axis=1), carry


def _attn_fox_step_body(q_ref, kt_ref, vt_ref, kn_ref, vn_ref, lft_ref, lfn_ref,
                        triu_ref, tri_ref, triu_s_ref, o_ref):
    tq = q_ref.shape[1]
    lane = lax.broadcasted_iota(jnp.int32, (tq, LANE), 1)
    causal = lane <= lax.broadcasted_iota(jnp.int32, (tq, LANE), 0)
    cc, total = _lane_cumsum(lft_ref[0, 0], triu_ref[...])
    lfn = _pad_rows(lfn_ref[0], LANE)
    hi, mid, lo = _split3(lfn)
    tri = tri_ref[...]
    cn = (_dot(tri, hi) + _dot(tri, mid) + _dot(tri, lo))[0:tq]
    cnt, _ = _lane_cumsum(lfn.T[0:N_FOX, :], triu_s_ref[...])
    for p in range(N_FOX // 2):
        pair = slice(LANE * p, LANE * (p + 1))
        kt = kt_ref[0, 0, pair, :].astype(BF16)
        vt = vt_ref[0, 0, pair, :].astype(BF16)
        kn = _pad_rows(kn_ref[0, :, pair], LANE)
        vn = _pad_rows(vn_ref[0, :, pair], LANE)
        qf = q_ref[0, :, pair].astype(F32)
        outs = []
        for hh in range(2):
            h = 2 * p + hh
            q = jnp.where((lane >> 6) == hh, qf, 0.0).astype(BF16)
            col = jnp.sum(jnp.where(lane == h, cn, 0.0), axis=1, keepdims=True)
            bias_c = ((total[h:h + 1, :] + col) - cc[h:h + 1, :]) * LOG2E
            bias_n = (col - cnt[h:h + 1, :]) * LOG2E
            s_c = _dot(q, kt) + bias_c
            s_n = jnp.where(causal, _dot_nt(q, kn) + bias_n, -jnp.inf)
            p_c, p_n, l = _step_softmax(s_c, s_n)
            outs.append((_dot_nt(p_c, vt) + _dot(p_n, vn)) / l)
        o_ref[0, :, pair] = jnp.where(lane < V_DIM, outs[0], outs[1]).astype(BF16)


def _attn_fox_step(q, kt_all, vt_all, kn, vn, lft_all, lfn, layer, consts):
    b, tq, _ = q.shape
    new = lambda a: pl.BlockSpec((1,) + a.shape[1:], lambda bi: (bi, 0, 0))
    old = lambda a: pl.BlockSpec((1, 1) + a.shape[2:], lambda bi: (layer, bi, 0, 0))
    return pl.pallas_call(
        _attn_fox_step_body, grid=(b,),
        in_specs=[new(q), old(kt_all), old(vt_all), new(kn), new(vn), old(lft_all), new(lfn),
                  _full((MXU, MXU)), _full((LANE, LANE)), _full((LANE, LANE))],
        out_specs=pl.BlockSpec((1, tq, N_FOX * FOX_DIM), lambda bi: (bi, 0, 0)),
        out_shape=jax.ShapeDtypeStruct((b, tq, N_FOX * FOX_DIM), BF16),
        compiler_params=_params("arbitrary"), name="attn_fox_step",
    )(q, kt_all, vt_all, kn, vn, lft_all, lfn, consts["triu256"], consts["tri128"], consts["triu128"])


def _post_attn_body(x_ref, om_ref, of_ref, wo_ref, gffn_ref, wr_ref, br_ref, tri_ref,
                    x2_ref, xn_ref, route_ref, routet_ref, cnt_ref, carry_ref):
    nm = N_MLA * V_DIM
    x2 = x_ref[...] + _dot(om_ref[...], wo_ref[0:nm, :]) + _dot(of_ref[...], wo_ref[nm:, :])
    x2_ref[...] = x2
    xn = _rms(x2, gffn_ref[...], D_MODEL)
    xn_ref[...] = xn
    blocks = range(0, xn.shape[0], MXU)
    each = lambda f, *cols: [f(*v) for v in zip(*cols)]
    xs = [xn[r0:r0 + MXU] for r0 in blocks]
    xh = each(lambda v: v.astype(BF16), xs)
    xl = each(lambda v, h: (v - h.astype(F32)).astype(BF16), xs, xh)
    wr = wr_ref[...]
    r = each(lambda h, l: _dot(h, wr) + _dot(l, wr), xh, xl)
    bias = br_ref[...]
    lane = lax.broadcasted_iota(jnp.int32, (MXU, LANE), 1).astype(F32)
    valid = lane < N_EXPERTS
    lg = each(lambda v: jnp.where(valid, v + pltpu.roll(v, LANE - N_EXPERTS, 1) + bias, -jnp.inf), r)
    e = each(lambda v: jnp.exp(v - jnp.max(v, axis=1, keepdims=True)), lg)
    probs = each(lambda v: v / jnp.sum(v, axis=1, keepdims=True), e)
    first_at = lambda p, m: jnp.min(jnp.where(p == m, lane, float(LANE)), axis=1, keepdims=True)
    p1 = each(lambda p: jnp.where(valid, p, -1.0), probs)
    m1 = each(lambda p: jnp.max(p, axis=1, keepdims=True), p1)
    i1 = each(first_at, p1, m1)
    p2 = each(lambda p, i: jnp.where(lane == i, -1.0, p), p1, i1)
    m2 = each(lambda p: jnp.max(p, axis=1, keepdims=True), p2)
    i2 = each(first_at, p2, m2)

    @pl.when(pl.program_id(0) == 0)
    def _():
        carry_ref[...] = jnp.zeros(carry_ref.shape, F32)

    sel = each(lambda a, b: jnp.where((lane == a) | (lane == b), 1.0, 0.0), i1, i2)
    tri = tri_ref[...]
    counts = each(lambda v: _dot(tri, v.astype(BF16)), sel)
    total = carry_ref[0:1, :]
    for n, r0 in enumerate(blocks):
        incl = counts[n] + total
        total = incl[MXU - 1:, :]
        excl = incl - sel[n]
        rank1 = jnp.sum(jnp.where(lane == i1[n], excl, 0.0), axis=1, keepdims=True)
        rank2 = jnp.sum(jnp.where(lane == i2[n], excl, 0.0), axis=1, keepdims=True)
        den = m1[n] + m2[n]
        route = jnp.zeros((MXU, LANE), F32)
        for c, col in enumerate((i1[n], i2[n], rank1, rank2, m1[n] / den, m2[n] / den)):
            route = jnp.where(lane == c, col, route)
        route_ref[r0:r0 + MXU, :] = route
        routet_ref[:, r0:r0 + MXU] = route.T[0:8, :]
    carry_ref[...] = jnp.broadcast_to(total, carry_ref.shape)
    cnt_ref[...] = jnp.broadcast_to(total, cnt_ref.shape)


def _post_attn(x, om, of, lw, consts, tm):
    rows = x.shape[0]
    assert tm % MXU == 0
    row = lambda w: pl.BlockSpec((tm, w), lambda i: (i, 0))
    nm = N_MLA * V_DIM
    sds = jax.ShapeDtypeStruct
    return pl.pallas_call(
        _post_attn_body, grid=(rows // tm,),
        in_specs=[row(D_MODEL), row(nm), row(nm), _full((2 * nm, D_MODEL)), _full((1, D_MODEL)),
                  _full((D_MODEL, LANE)), _full((1, LANE)), _full((MXU, MXU))],
        out_specs=(row(D_MODEL), row(D_MODEL), row(LANE), pl.BlockSpec((8, tm), lambda i: (0, i)),
                   _full((8, LANE))),
        out_shape=(sds((rows, D_MODEL), F32), sds((rows, D_MODEL), F32), sds((rows, LANE), F32),
                   sds((8, rows), F32), sds((8, LANE), F32)),
        scratch_shapes=[pltpu.VMEM((8, LANE), F32)],
        compiler_params=_params("arbitrary"), name="post_attn_moe",
    )(x, om, of, lw["w_out"], lw["g_ffn"], lw["w_router"], lw["b_router"], consts["tri256"])


def _swiglu_acc(xb, wg_ref, wu_ref, wd_ref, d_ff, lo=0, hi=None):
    acc = None
    for c in range(lo * MXU, d_ff if hi is None else hi * MXU, MXU):
        g = _dot(xb, wg_ref[:, c:c + MXU])
        u = _dot(xb, wu_ref[:, c:c + MXU])
        h = (g * jax.nn.sigmoid(g) * u).astype(BF16)
        d = _dot(h, wd_ref[c:c + MXU, :])
        acc = d if acc is None else acc + d
    return acc


def _post_ffn_body(x_ref, om_ref, of_ref, wo_ref, gffn_ref, wg_ref, wu_ref, wd_ref, o_ref):
    nm = N_MLA * V_DIM
    x2 = x_ref[...] + _dot(om_ref[...], wo_ref[0:nm, :]) + _dot(of_ref[...], wo_ref[nm:, :])
    xn = _rms(x2, gffn_ref[...], D_MODEL).astype(BF16)
    o_ref[...] = x2 + _swiglu_acc(xn, wg_ref.at[0], wu_ref.at[0], wd_ref.at[0], D_FF)


def _post_ffn(x, om, of, lw, tm):
    rows = x.shape[0]
    row = lambda w: pl.BlockSpec((tm, w), lambda i: (i, 0))
    nm = N_MLA * V_DIM
    jl = lw["mixer_idx"]
    wspec = lambda a, b: pl.BlockSpec((1, a, b), lambda i: (jl, 0, 0))
    return pl.pallas_call(
        _post_ffn_body, grid=(rows // tm,),
        in_specs=[row(D_MODEL), row(nm), row(nm), _full((2 * nm, D_MODEL)), _full((1, D_MODEL)),
                  wspec(D_MODEL, D_FF), wspec(D_MODEL, D_FF), wspec(D_FF, D_MODEL)],
        out_specs=row(D_MODEL), out_shape=jax.ShapeDtypeStruct((rows, D_MODEL), F32),
        compiler_params=_params("arbitrary"), name="post_ffn",
    )(x, om, of, lw["w_out"], lw["g_ffn"], lw["w_gate"], lw["w_up"], lw["w_down"])


SCATTER_UNROLL = 8
SPLIT_CHUNK = 5


def _inv_body(pos_ref, base_hbm, inv_ref, sem, *, tmg):
    fill = pltpu.make_async_copy(base_hbm, inv_ref, sem.at[0])
    fill.start()
    fill.wait()

    def put(i, c):
        for u in range(SCATTER_UNROLL):
            a = i * SCATTER_UNROLL + u
            inv_ref[tmg + pos_ref[a]] = a
        return c

    lax.fori_loop(0, pos_ref.shape[0] // SCATTER_UNROLL, put, 0)


def _route_inverse(pos_flat, base, tmg):
    return pl.pallas_call(
        functools.partial(_inv_body, tmg=tmg),
        in_specs=[pl.BlockSpec(memory_space=pltpu.SMEM), pl.BlockSpec(memory_space=pl.ANY)],
        out_specs=pl.BlockSpec(memory_space=pltpu.SMEM),
        out_shape=jax.ShapeDtypeStruct(base.shape, jnp.int32),
        scratch_shapes=[pltpu.SemaphoreType.DMA((1,))], name="route_inverse",
    )(pos_flat, base)


GATHER, OUT, STAGE = 0, 2, 4


def _moe_routed_body(te_ref, stage_ref, src_ref, dst_ref, xn_hbm, wg_ref, wu_ref, wd_ref, z_hbm,
                     buf, gsem, ssem, *, tmg, n_tiles):
    del te_ref
    j = pl.program_id(0)

    def row_in(entry, slot, r):
        return pltpu.make_async_copy(xn_hbm.at[pl.ds(src_ref[entry], 1), :],
                                     buf.at[GATHER + slot, pl.ds(r, 1), :], gsem.at[slot])

    def row_out(entry, slot, r):
        return pltpu.make_async_copy(buf.at[OUT + slot, pl.ds(r, 1), :],
                                     z_hbm.at[pl.ds(dst_ref[entry], 1), :], ssem.at[0])

    def tile_in(slot):
        return pltpu.make_async_copy(xn_hbm.at[pl.ds(0, tmg), :], buf.at[GATHER + slot], gsem.at[slot])

    def tile_out(slot):
        return pltpu.make_async_copy(buf.at[OUT + slot], z_hbm.at[pl.ds(0, tmg), :], ssem.at[0])

    @pl.when(j == 0)
    def _():
        buf[OUT + 1] = jnp.zeros(buf.shape[1:], F32)
        for r in range(tmg):
            row_in(tmg + r, 0, r).start()

    def step(cur):
        nxt = 1 - cur

        @pl.when(j >= 1)
        def _():
            tile_out(cur).wait()

        tile_in(cur).wait()
        w = (wg_ref.at[0, 0], wu_ref.at[0, 0], wd_ref.at[0, 0])
        head = _swiglu_acc(buf[GATHER + cur].astype(BF16), *w, E_FF, 0, SPLIT_CHUNK)
        for r in range(tmg):
            row_in((j + 2) * tmg + r, nxt, r).start()
            row_out(j * tmg + r, nxt, r).start()
        stage = stage_ref[0]
        buf[stage] = buf[GATHER + cur]
        tail = _swiglu_acc(buf[stage].astype(BF16), *w, E_FF, SPLIT_CHUNK)
        buf[OUT + cur] = head + tail

    for parity in range(2):
        pl.when(j % 2 == parity)(functools.partial(step, parity))

    assert n_tiles % 2 == 0

    @pl.when(j == n_tiles)
    def _():
        tile_out(1).wait()
        tile_in(1).wait()


def _moe_routed(xn, tile_expert, src, dst, lw, tmg, n_tiles):
    n = xn.shape[0]
    jl = lw["mixer_idx"]
    wspec = lambda a, b: pl.BlockSpec((1, 1, a, b), lambda j, te, st, s, d: (jl, te[j], 0, 0))
    grid_spec = pltpu.PrefetchScalarGridSpec(
        num_scalar_prefetch=4, grid=(n_tiles + 1,),
        in_specs=[pl.BlockSpec(memory_space=pl.ANY), wspec(D_MODEL, E_FF), wspec(D_MODEL, E_FF),
                  wspec(E_FF, D_MODEL)],
        out_specs=pl.BlockSpec(memory_space=pl.ANY),
        scratch_shapes=[pltpu.VMEM((STAGE + 1, tmg, D_MODEL), F32),
                        pltpu.SemaphoreType.DMA((2,)), pltpu.SemaphoreType.DMA((1,))])
    return pl.pallas_call(
        functools.partial(_moe_routed_body, tmg=tmg, n_tiles=n_tiles),
        grid_spec=grid_spec, out_shape=jax.ShapeDtypeStruct((2 * n + tmg, D_MODEL), F32),
        compiler_params=_params("arbitrary"), name="moe_routed",
    )(tile_expert, jnp.full((1,), STAGE, jnp.int32), src, dst, xn,
      lw["we_gate"], lw["we_up"], lw["we_down"])


def _combine_body(x2_ref, route_ref, z0_ref, z1_ref, o_ref):
    o_ref[...] = _combine(x2_ref, route_ref, z0_ref, z1_ref)


def _moe_combine(x2, route, z, tm):
    n = x2.shape[0]
    row = lambda w: pl.BlockSpec((tm, w), lambda i: (i, 0))
    return pl.pallas_call(
        _combine_body, grid=(n // tm,),
        in_specs=[row(D_MODEL), row(LANE), row(D_MODEL),
                  pl.BlockSpec((tm, D_MODEL), lambda i: (i + n // tm, 0))],
        out_specs=row(D_MODEL), out_shape=jax.ShapeDtypeStruct((n, D_MODEL), F32),
        compiler_params=_params("arbitrary"), name="moe_combine",
    )(x2, route, z, z)


def _moe(x2, xn, route, route_t, counts, lw, tmg, defer_combine=False):
    n = x2.shape[0]
    assert n & (n - 1) == 0
    n_tiles = 2 * n // tmg + N_EXPERTS
    ext = (n_tiles + 3) * tmg
    e = route_t[0:2].astype(jnp.int32)
    rank = route_t[2:4].astype(jnp.int32)
    cnt = counts[0, :N_EXPERTS].astype(jnp.int32)
    tiles = (cnt + tmg - 1) // tmg
    tile_end = jnp.cumsum(tiles)
    row_start = (tile_end - tiles) * tmg
    start_of = jnp.sum(jnp.where(e[..., None] == jnp.arange(N_EXPERTS), row_start, 0), axis=-1)
    pos_flat = (start_of + rank).reshape(-1)
    steps = jnp.minimum(jnp.arange(n_tiles + 1, dtype=jnp.int32), tile_end[-1] - 1)
    tile_expert = jnp.minimum(jnp.sum((steps[:, None] >= tile_end[None, :]).astype(jnp.int32), axis=1),
                              N_EXPERTS - 1)
    idx = jnp.arange(ext, dtype=jnp.int32)
    dst = _route_inverse(pos_flat, 2 * n + (idx & (tmg - 1)), tmg)
    src = jnp.where(dst < 2 * n, dst, idx) & (n - 1)
    z = _moe_routed(xn, tile_expert, src, dst, lw, tmg, n_tiles)
    return (x2, route, z) if defer_combine else _moe_combine(x2, route, z, tmg)


def _consts():
    i = jnp.arange(MXU)
    blockdiag = lambda w: ((i[:, None] // w) == (i[None, :] // w)).astype(BF16)
    tri = lambda n: (jnp.arange(n)[None, :] <= jnp.arange(n)[:, None]).astype(BF16)
    r = jnp.arange(LANE)
    place = jnp.stack([((r[:, None] < N_FOX) & (r[None, :] == AUG_W * r[:, None] + k)).astype(BF16)
                       for k in range(6)])
    within = r % AUG_W
    headed = r < AUG_W * N_FOX
    ones_q = (headed & (within >= 3) & (within < 6)).astype(F32)
    ones_k = (headed & (within < 3)).astype(F32)
    aug_ones = jnp.zeros((8, LANE), F32).at[0].set(ones_q).at[1].set(ones_k)
    return {"s64": blockdiag(FOX_DIM), "s128": blockdiag(HEAD_PAD), "tri256": tri(MXU),
            "tri128": tri(LANE), "triu256": tri(MXU).T, "triu128": tri(LANE).T,
            "place": place, "aug_ones": aug_ones}


def _rope_tables(pos):
    half = ROPE_DIM // 2
    inv = ROPE_BASE ** (-jnp.arange(half, dtype=F32) / half)
    ang = pos.astype(F32)[:, None] * inv[None, :]
    cos, sin = jnp.cos(ang), jnp.sin(ang)
    t = pos.shape[0]
    ct = jnp.concatenate([jnp.ones((t, ROPE_LO), F32), cos, cos, jnp.zeros((t, LANE - ROPE_HI), F32)], 1)
    st = jnp.concatenate([jnp.zeros((t, ROPE_LO), F32), -sin, sin, jnp.zeros((t, LANE - ROPE_HI), F32)], 1)
    return ct, st


def _pad_lanes(v, width=LANE):
    return jnp.pad(v, [(0, 0)] * (v.ndim - 1) + [(0, width - v.shape[-1])])


def _prep_layer(i, p):
    half = ROPE_DIM // 2
    w_in = p["w_in"][i]
    s = [0, Q_LORA, Q_LORA + KV_LORA, Q_LORA + KV_LORA + ROPE_DIM]
    nfx = N_FOX * FOX_DIM
    s += [s[3] + nfx, s[3] + 2 * nfx, s[3] + 3 * nfx, s[3] + 3 * nfx + N_FOX]
    c_q, c_kv, k_rope = w_in[:, s[0]:s[1]], w_in[:, s[1]:s[2]], w_in[:, s[2]:s[3]]
    fq, fk, fv, f_logit = w_in[:, s[3]:s[4]], w_in[:, s[4]:s[5]], w_in[:, s[5]:s[6]], w_in[:, s[6]:s[7]]
    zc = lambda n: jnp.zeros((D_MODEL, n), F32)
    misc = jnp.concatenate([f_logit, zc(ROPE_LO - N_FOX), k_rope, zc(LANE - ROPE_HI)], axis=1)
    w_in_p = jnp.concatenate([c_q, c_kv, fq, fk, fv, misc], axis=1).astype(BF16)

    w_uq = p["w_uq"][i].reshape(Q_LORA, N_MLA, QK_DIM)
    nope, rope = w_uq[..., :NOPE_DIM], w_uq[..., NOPE_DIM:]
    rope_sw = jnp.concatenate([rope[..., half:], rope[..., :half]], axis=-1)
    zq = lambda n: jnp.zeros((Q_LORA, N_MLA, n), F32)
    qa = jnp.concatenate([nope, rope, zq(HEAD_PAD - QK_DIM)], axis=-1)
    qb = jnp.concatenate([zq(NOPE_DIM), rope_sw, zq(HEAD_PAD - QK_DIM)], axis=-1)
    w_uq_p = jnp.concatenate([qa.reshape(Q_LORA, -1), qb.reshape(Q_LORA, -1)], axis=1).astype(BF16)

    w_ukv = p["w_ukv"][i].reshape(KV_LORA, N_MLA, NOPE_DIM + V_DIM)
    w_k = _pad_lanes(w_ukv[..., :NOPE_DIM], HEAD_PAD).reshape(KV_LORA, -1).astype(BF16)
    w_v = w_ukv[..., NOPE_DIM:].reshape(KV_LORA, -1).astype(BF16)

    gq = _pad_lanes(p["g_qn_mla"][i] * (QK_DIM ** -0.5 * LOG2E))[None]
    gk = _pad_lanes(p["g_kn_mla"][i])[None]
    lw = {
        "g_mix": p["g_mix"][i][None], "w_in": w_in_p, "b_f": _pad_lanes(p["b_f"][i])[None],
        "g_cq": p["g_cq"][i][None], "g_ckv": p["g_ckv"][i][None], "w_uq": w_uq_p,
        "gq": gq, "gk": gk,
        "gfq": jnp.tile(p["g_qn_fox"][i] * (FOX_DIM ** -0.5 * LOG2E), N_FOX)[None],
        "gfk": jnp.tile(p["g_kn_fox"][i], N_FOX)[None],
        "w_k": w_k, "w_v": w_v,
        "w_out": p["w_out"][i].astype(BF16), "g_ffn": p["g_ffn"][i][None],
    }
    j = i // 2
    lw["mixer_idx"] = j
    if i % 2 == 0:
        lw.update(w_gate=p["w_gate_b"], w_up=p["w_up_b"], w_down=p["w_down_b"])
    else:
        wr = p["w_router"][j]
        wr_hi = wr.astype(BF16)
        wr_lo = (wr - wr_hi.astype(F32)).astype(BF16)
        lw.update(w_router=_pad_lanes(jnp.concatenate([wr_hi, wr_lo], axis=1)),
                  b_router=_pad_lanes(p["b_router"][j])[None],
                  we_gate=p["we_gate_b"], we_up=p["we_up_b"], we_down=p["we_down_b"])
    return lw


def _channel_mixer(i, x, om, of, lw, consts, tm_ffn, defer_combine=False):
    if i % 2 == 0:
        return _post_ffn(x, om, of, lw, tm_ffn)
    x2, xn, route, route_t, counts = _post_attn(x, om, of, lw, consts, min(tm_ffn, x.shape[0]))
    return _moe(x2, xn, route, route_t, counts, lw, MXU, defer_combine)


def _trunk_prompt(x, layers, consts):
    b, t, _ = x.shape
    rows = b * t
    x = x.reshape(rows, D_MODEL)
    ct, st = _rope_tables(jnp.arange(t, dtype=jnp.int32))
    nfx = N_FOX * FOX_DIM
    stack = ()
    pending = None
    for i, lw in enumerate(layers):
        outs = _mixer_in_prompt(x, lw, ct, st, consts, stack, i, b, combine=pending)
        stack = outs[:N_STACKED]
        q, fq, fkb, fvb, k, v, qaug, kaug = outs[N_STACKED:N_STACKED + 8]
        if pending is not None:
            x = outs[-1]
        om = _attn_prompt(q.reshape(b, t, -1), k.reshape(b, t, -1), v.reshape(b, t, -1), None, 256)
        of = _attn_prompt(fq.reshape(b, t, nfx), fkb.reshape(b, t, nfx), fvb.reshape(b, t, nfx),
                          (qaug.reshape(b, t, LANE), kaug.reshape(b, t, LANE)), 256)
        defer = i % 2 == 1 and i + 1 < len(layers)
        res = _channel_mixer(i, x, om.reshape(rows, -1), of.reshape(rows, -1), lw, consts, 512, defer)
        x, pending = (None, res) if defer else (res, None)
    ckv_s, krt_s, fkt_s, fvt_s, lft_s = stack
    heads = lambda a: jnp.transpose(a.reshape(DEPTH, b, N_FOX, FOX_DIM, t), (0, 1, 4, 2, 3))
    state = (ckv_s.reshape(DEPTH, b, t, KV_LORA), jnp.swapaxes(krt_s, 2, 3), heads(fkt_s), heads(fvt_s),
             jnp.swapaxes(lft_s, 2, 3))
    return x.reshape(b, t, D_MODEL), state


def _trunk_sample(x, past, layers, consts):
    b, t, _ = x.shape
    rows = b * t
    start = PAST_LEN
    x = x.reshape(rows, D_MODEL)
    ct, st = _rope_tables(start + jnp.arange(t, dtype=jnp.int32))
    ct, st = jnp.tile(ct, (b, 1)), jnp.tile(st, (b, 1))
    p_ckv, p_krope, p_fk, p_fv, p_logf = past
    nfx = N_FOX * FOX_DIM
    krope_t = jnp.swapaxes(p_krope, 2, 3)
    fk_t = jnp.transpose(p_fk, (0, 1, 3, 4, 2)).reshape(DEPTH, b, nfx, start)
    fv_t = jnp.transpose(p_fv, (0, 1, 3, 4, 2)).reshape(DEPTH, b, nfx, start)
    lf_t = jnp.swapaxes(p_logf, 2, 3)
    state = ([], [], [], [], [])
    for i, lw in enumerate(layers):
        ckv, kr, kr128, fk, fv, lf128, lf, q, fq, fkb, fvb = _mixer_in(x, lw, ct, st, consts, rows)
        k_c, v_c = _kv_expand_cache(p_ckv, krope_t, i, lw, consts, 512)
        k_n, v_n = _kv_expand(ckv, kr128, lw, consts, rows)
        om = _attn_mla_step(q.reshape(b, t, -1), k_c.reshape(b, start, -1), v_c.reshape(b, start, -1),
                            k_n.reshape(b, t, -1), v_n.reshape(b, t, -1), start)
        of = _attn_fox_step(fq.reshape(b, t, nfx), fk_t, fv_t, fkb.reshape(b, t, nfx),
                            fvb.reshape(b, t, nfx), lf_t, lf128.reshape(b, t, LANE), i, consts)
        x = _channel_mixer(i, x, om.reshape(rows, -1), of.reshape(rows, -1), lw, consts, rows)
        for lst, s_ in zip(state, (ckv.reshape(b, t, KV_LORA), kr.reshape(b, t, ROPE_DIM),
                                   fk.reshape(b, t, N_FOX, FOX_DIM), fv.reshape(b, t, N_FOX, FOX_DIM),
                                   lf.reshape(b, t, N_FOX))):
            lst.append(s_)
    return x.reshape(b, t, D_MODEL), tuple(jnp.stack(s_) for s_ in state)


def kernel(x_prompt, x_sample, cache_mla_ckv, cache_mla_krope, cache_fox_k, cache_fox_v, cache_fox_logf, g_mix, w_in, b_f, g_cq, g_ckv, w_uq, w_ukv, g_qn_mla, g_kn_mla, g_qn_fox, g_kn_fox, w_out, g_ffn, w_gate, w_up, w_down, w_router, b_router, we_gate, we_up, we_down):
    p = dict(g_mix=g_mix, w_in=w_in, b_f=b_f, g_cq=g_cq, g_ckv=g_ckv, w_uq=w_uq, w_ukv=w_ukv,
             g_qn_mla=g_qn_mla, g_kn_mla=g_kn_mla, g_qn_fox=g_qn_fox, g_kn_fox=g_kn_fox,
             w_out=w_out, g_ffn=g_ffn, w_gate=w_gate, w_up=w_up, w_down=w_down,
             w_router=w_router, b_router=b_router, we_gate=we_gate, we_up=we_up, we_down=we_down)
    for name in ("w_gate", "w_up", "w_down", "we_gate", "we_up", "we_down"):
        p[name + "_b"] = p[name].astype(BF16)
    layers = [_prep_layer(i, p) for i in range(DEPTH)]
    consts = _consts()
    y_p, st_p = _trunk_prompt(x_prompt, layers, consts)
    past = (cache_mla_ckv, cache_mla_krope, cache_fox_k, cache_fox_v, cache_fox_logf)
    y_s, st_s = _trunk_sample(x_sample, past, layers, consts)
    return (y_p, y_s) + st_p + st_s
```

```python
import functools

import jax
import jax.numpy as jnp
from jax import lax
from jax.experimental import pallas as pl
from jax.experimental.pallas import tpu as pltpu

F32 = jnp.float32
BF16 = jnp.bfloat16

D_MODEL = 1024
DEPTH = 4
PAST_LEN = 2048
CHUNK_SHIFT = 6
EPS = 1e-6
ROPE_BASE = 10000.0

N_MLA = 8
Q_LORA = 384
KV_LORA = 256
NOPE_DIM = 64
ROPE_DIM = 32
QK_DIM = NOPE_DIM + ROPE_DIM
V_DIM = 64
N_FOX = 8
FOX_DIM = 64
D_FF = 2816
N_EXPERTS = 8
E_FF = 1792

LOG2E = 1.4426950408889634
LANE = 128
MXU = 256
HEAD_PAD = 128
ROPE_LO = NOPE_DIM
ROPE_HI = NOPE_DIM + ROPE_DIM
AUG_W = 16
GROUP = 2

CQ_LO, CQ_HI = 0, Q_LORA
CKV_LO, CKV_HI = CQ_HI, CQ_HI + KV_LORA
FQ_LO, FQ_HI = CKV_HI, CKV_HI + N_FOX * FOX_DIM
FK_LO, FK_HI = FQ_HI, FQ_HI + N_FOX * FOX_DIM
FV_LO, FV_HI = FK_HI, FK_HI + N_FOX * FOX_DIM
MISC_LO, MISC_HI = FV_HI, FV_HI + LANE
D_IN_P = MISC_HI

VMEM_LIMIT = 56 * 1024 * 1024


def _params(*sem):
    return pltpu.CompilerParams(dimension_semantics=sem, vmem_limit_bytes=VMEM_LIMIT)


def _dot(a, b):
    return jnp.dot(a, b, preferred_element_type=F32)


def _dot_nt(a, b):
    return lax.dot_general(a, b, (((1,), (1,)), ((), ())), preferred_element_type=F32)


def _rms(x, g, n):
    return x * lax.rsqrt(jnp.sum(x * x, axis=-1, keepdims=True) * (1.0 / n) + EPS) * g


def _head_norm(x, smat, dim):
    outs = []
    for c in range(0, x.shape[1], MXU):
        xc = x[:, c:c + MXU]
        ss = _dot((xc * xc).astype(BF16), smat)
        outs.append(xc * lax.rsqrt(ss * (1.0 / dim) + EPS))
    return jnp.concatenate(outs, axis=1)


def _full(shape):
    return pl.BlockSpec(shape, lambda *_: (0,) * len(shape))


def _split3(c):
    hi = c.astype(BF16)
    r1 = c - hi.astype(F32)
    mid = r1.astype(BF16)
    lo = (r1 - mid.astype(F32)).astype(BF16)
    return hi, mid, lo


def _combine(x2_ref, route_ref, z0_ref, z1_ref):
    route = route_ref[...]
    lane = lax.broadcasted_iota(jnp.int32, route.shape, 1)
    g1 = jnp.sum(jnp.where(lane == 4, route, 0.0), axis=1, keepdims=True)
    g2 = jnp.sum(jnp.where(lane == 5, route, 0.0), axis=1, keepdims=True)
    return x2_ref[...] + g1 * z0_ref[...] + g2 * z1_ref[...]


N_MIXER_W = 13


def _mixer_core(x, gmix_ref, win_ref, bf_ref, gcq_ref, gckv_ref, wuq_ref, gq_ref,
                gfq_ref, gfk_ref, ct_ref, st_ref, s64_ref, s128_ref):
    tm = x.shape[0]
    xb = _rms(x, gmix_ref[...], D_MODEL).astype(BF16)

    def proj(lo, hi):
        return _dot(xb, win_ref[:, lo:hi])

    ct = ct_ref[...]
    st = st_ref[...]
    lane = lax.broadcasted_iota(jnp.int32, (tm, LANE), 1)

    cq_raw = proj(CQ_LO, CQ_HI)
    a = proj(MISC_LO, MISC_HI)
    ckv_raw = proj(CKV_LO, CKV_HI)
    fk_raw = proj(FK_LO, FK_HI)
    fv = proj(FV_LO, FV_HI)
    fq_raw = proj(FQ_LO, FQ_HI)

    cq = _rms(cq_raw, gcq_ref[...], Q_LORA).astype(BF16)
    qa = _dot(cq, wuq_ref[...])
    half = ROPE_DIM // 2

    def rotary(blk):
        sw = jnp.where(lane < ROPE_LO + half, pltpu.roll(blk, LANE - half, 1), pltpu.roll(blk, half, 1))
        return blk * ct + sw * st

    z = a + bf_ref[...]
    lf = jnp.minimum(z, 0.0) - jnp.log1p(jnp.exp(-jnp.abs(z)))
    lf128 = jnp.where(lane < N_FOX, lf, 0.0)
    kr128 = jnp.where((lane >= ROPE_LO) & (lane < ROPE_HI), rotary(a), 0.0)
    ckv = _rms(ckv_raw, gckv_ref[...], KV_LORA)

    qr = jnp.concatenate(
        [rotary(qa[:, h * HEAD_PAD:(h + 1) * HEAD_PAD]) for h in range(N_MLA)], axis=1)
    qn = _head_norm(qr, s128_ref[...], QK_DIM)
    gq = gq_ref[...]
    q = jnp.concatenate(
        [qn[:, h * HEAD_PAD:(h + 1) * HEAD_PAD] * gq for h in range(N_MLA)], axis=1).astype(BF16)

    s64 = s64_ref[...]
    fk = _head_norm(fk_raw, s64, FOX_DIM) * gfk_ref[...]
    fq = (_head_norm(fq_raw, s64, FOX_DIM) * gfq_ref[...]).astype(BF16)
    return ckv, kr128, lf128, fk, fv, fq, q


def _mixer_in_body(x_ref, *refs):
    (ckv_ref, kr_ref, kr128_ref, fk_ref, fv_ref, lf128_ref, lf_ref,
     q_ref, fq_ref, fkb_ref, fvb_ref) = refs[N_MIXER_W:]
    ckv, kr128, lf128, fk, fv, fq, q = _mixer_core(x_ref[...], *refs[:N_MIXER_W])
    ckv_ref[...] = ckv
    kr128_ref[...] = kr128
    kr_ref[...] = kr128[:, ROPE_LO:ROPE_HI]
    lf128_ref[...] = lf128
    lf_ref[...] = lf128[:, 0:N_FOX]
    fk_ref[...] = fk
    fv_ref[...] = fv
    fkb_ref[...] = fk.astype(BF16)
    fvb_ref[...] = fv.astype(BF16)
    fq_ref[...] = fq
    q_ref[...] = q


def _fox_aug(c, place_ref, ones_ref):
    hi, mid, lo = _split3(c * LOG2E)
    qa = _dot(hi, place_ref[0]) + _dot(mid, place_ref[1]) + _dot(lo, place_ref[2]) + ones_ref[0:1, :]
    ka = ones_ref[1:2, :] - (_dot(hi, place_ref[3]) + _dot(mid, place_ref[4]) + _dot(lo, place_ref[5]))
    return qa.astype(BF16), ka.astype(BF16)


N_STACKED = 5


def _mixer_prompt_body(*refs, first, combine, n_alias, tiles_per_seq):
    n_x = 4 if combine else 1
    n_in = n_x + N_MIXER_W + 6 + n_alias
    weights = refs[n_x:n_x + N_MIXER_W]
    wk_ref, wv_ref, gk_ref, tri_ref, place_ref, ones_ref = refs[n_x + N_MIXER_W:n_x + N_MIXER_W + 6]
    outs = list(refs[n_in:])
    carry_ref = outs.pop()
    (ckv_ref, kr_ref, fk_ref, fv_ref, lf_ref, q_ref, fq_ref, fkb_ref, fvb_ref,
     k_ref, v_ref, qa_ref, ka_ref) = outs[:13]
    if combine:
        x = _combine(*refs[:4])
        outs[13][...] = x
    else:
        x = refs[0][...]
    ckv, kr128, lf128, fk, fv, fq, q = _mixer_core(x, *weights)
    q_ref[...] = q
    fq_ref[...] = fq
    fkb_ref[...] = fk.astype(BF16)
    fvb_ref[...] = fv.astype(BF16)
    ckv_ref[0] = ckv
    kr_ref[0, 0] = kr128.T[ROPE_LO:ROPE_HI, :]
    lf_ref[0, 0] = lf128.T[0:N_FOX, :]
    fk_ref[0, 0] = fk.T
    fv_ref[0, 0] = fv.T
    if first:
        for ref in (ckv_ref, kr_ref, lf_ref, fk_ref, fv_ref):
            ref[1:] = jnp.zeros((ref.shape[0] - 1,) + ref.shape[1:], F32)

    _kv_emit(ckv, kr128, wk_ref, wv_ref, gk_ref, weights[-1], k_ref, v_ref)

    @pl.when(pl.program_id(0) % tiles_per_seq == 0)
    def _():
        carry_ref[...] = jnp.zeros(carry_ref.shape, F32)

    tri = tri_ref[...]
    carry = carry_ref[0:1, :]
    for r0 in range(0, lf128.shape[0], MXU):
        hi, mid, lo = _split3(lf128[r0:r0 + MXU, :])
        c = _dot(tri, hi) + _dot(tri, mid) + _dot(tri, lo) + carry
        carry = c[MXU - 1:, :]
        qa, ka = _fox_aug(c, place_ref, ones_ref)
        qa_ref[r0:r0 + MXU, :] = qa
        ka_ref[r0:r0 + MXU, :] = ka
    carry_ref[...] = jnp.broadcast_to(carry, carry_ref.shape)


def _mixer_specs(tm, nt):
    nfx = N_FOX * FOX_DIM
    tab = pl.BlockSpec((tm, LANE), lambda i: (i % nt, 0))
    return [_full((1, D_MODEL)), _full((D_MODEL, D_IN_P)), _full((1, LANE)),
            _full((1, Q_LORA)), _full((1, KV_LORA)), _full((Q_LORA, N_MLA * HEAD_PAD)),
            _full((1, LANE)), _full((1, nfx)), _full((1, nfx)), tab, tab,
            _full((MXU, MXU)), _full((MXU, MXU))]


def _mixer_args(lw, ct, st, consts):
    return [lw["g_mix"], lw["w_in"], lw["b_f"], lw["g_cq"], lw["g_ckv"], lw["w_uq"], lw["gq"],
            lw["gfq"], lw["gfk"], ct, st, consts["s64"], consts["s128"]]


def _mixer_in(x, lw, ct, st, consts, tm):
    rows = x.shape[0]
    nt = ct.shape[0] // tm
    row = lambda w: pl.BlockSpec((tm, w), lambda i: (i, 0))
    nfx = N_FOX * FOX_DIM
    sds = jax.ShapeDtypeStruct
    outs = [(sds((rows, KV_LORA), F32), row(KV_LORA)), (sds((rows, ROPE_DIM), F32), row(ROPE_DIM)),
            (sds((rows, LANE), F32), row(LANE)), (sds((rows, nfx), F32), row(nfx)),
            (sds((rows, nfx), F32), row(nfx)), (sds((rows, LANE), F32), row(LANE)),
            (sds((rows, N_FOX), F32), row(N_FOX)),
            (sds((rows, N_MLA * HEAD_PAD), BF16), row(N_MLA * HEAD_PAD)),
            (sds((rows, nfx), BF16), row(nfx)), (sds((rows, nfx), BF16), row(nfx)),
            (sds((rows, nfx), BF16), row(nfx))]
    args = _mixer_args(lw, ct, st, consts)
    assert len(args) == N_MIXER_W
    return pl.pallas_call(
        _mixer_in_body, grid=(rows // tm,), in_specs=[row(D_MODEL)] + _mixer_specs(tm, nt),
        out_specs=tuple(o[1] for o in outs), out_shape=tuple(o[0] for o in outs),
        compiler_params=_params("arbitrary"), name="mixer_in",
    )(x, *args)


def _mixer_in_prompt(x, lw, ct, st, consts, stack, layer, batch, combine=None):
    tm = 2 * MXU
    rows = (x if combine is None else combine[0]).shape[0]
    t = rows // batch
    nt = t // tm
    first = not stack
    assert first == (layer == 0)
    row = lambda w: pl.BlockSpec((tm, w), lambda i: (i, 0))
    nfx = N_FOX * FOX_DIM
    nk = N_MLA * HEAD_PAD
    nv = N_MLA * V_DIM
    sds = jax.ShapeDtypeStruct
    nd = DEPTH if first else 1
    tmaj = lambda w: pl.BlockSpec((nd, 1, w, tm), lambda i: (layer, i // nt, 0, i % nt))
    outs = [(sds((DEPTH, rows, KV_LORA), F32), pl.BlockSpec((nd, tm, KV_LORA), lambda i: (layer, i, 0))),
            (sds((DEPTH, batch, ROPE_DIM, t), F32), tmaj(ROPE_DIM)),
            (sds((DEPTH, batch, nfx, t), F32), tmaj(nfx)),
            (sds((DEPTH, batch, nfx, t), F32), tmaj(nfx)),
            (sds((DEPTH, batch, N_FOX, t), F32), tmaj(N_FOX)),
            (sds((rows, nk), BF16), row(nk)), (sds((rows, nfx), BF16), row(nfx)),
            (sds((rows, nfx), BF16), row(nfx)), (sds((rows, nfx), BF16), row(nfx)),
            (sds((rows, nk), BF16), row(nk)), (sds((rows, nv), BF16), row(nv)),
            (sds((rows, LANE), BF16), row(LANE)), (sds((rows, LANE), BF16), row(LANE))]
    if combine is None:
        in_specs, args = [row(D_MODEL)], [x]
    else:
        x2, route, z = combine
        in_specs = [row(D_MODEL), row(LANE), row(D_MODEL),
                    pl.BlockSpec((tm, D_MODEL), lambda i: (i + rows // tm, 0))]
        args = [x2, route, z, z]
        outs.append((sds((rows, D_MODEL), F32), row(D_MODEL)))
    in_specs += _mixer_specs(tm, nt) + [_full((KV_LORA, nk)), _full((KV_LORA, nv)), _full((1, LANE)),
                                        _full((MXU, MXU)), _full((6, LANE, LANE)), _full((8, LANE))]
    args += _mixer_args(lw, ct, st, consts) + [lw["w_k"], lw["w_v"], lw["gk"], consts["tri256"],
                                                consts["place"], consts["aug_ones"]]
    aliases = {}
    if stack:
        aliases = {len(args) + n: n for n in range(N_STACKED)}
        in_specs += [pl.BlockSpec(memory_space=pl.ANY)] * N_STACKED
        args += list(stack)
    return pl.pallas_call(
        functools.partial(_mixer_prompt_body, first=first, combine=combine is not None,
                          n_alias=len(stack), tiles_per_seq=nt),
        grid=(rows // tm,), in_specs=in_specs,
        out_specs=tuple(o[1] for o in outs), out_shape=tuple(o[0] for o in outs),
        scratch_shapes=[pltpu.VMEM((8, LANE), F32)],
        input_output_aliases=aliases, compiler_params=_params("arbitrary"), name="mixer_in_prompt",
    )(*args)


def _kv_emit(ckv, kr, wk_ref, wv_ref, gk_ref, s128_ref, k_ref, v_ref):
    cb = ckv.astype(BF16)
    kn = _dot(cb, wk_ref[...])
    kk = jnp.concatenate(
        [kn[:, h * HEAD_PAD:(h + 1) * HEAD_PAD] + kr for h in range(N_MLA)], axis=1)
    kk = _head_norm(kk, s128_ref[...], QK_DIM)
    gk = gk_ref[...]
    k_ref[...] = jnp.concatenate(
        [kk[:, h * HEAD_PAD:(h + 1) * HEAD_PAD] * gk for h in range(N_MLA)], axis=1).astype(BF16)
    v_ref[...] = _dot(cb, wv_ref[...]).astype(BF16)


def _kv_body(ckv_ref, kr128_ref, *rest):
    _kv_emit(ckv_ref[...], kr128_ref[...], *rest)


def _kv_cache_body(ckv_ref, krt_ref, *rest):
    tm = krt_ref.shape[-1]
    pad = lambda n: jnp.zeros((n, tm), F32)
    kr128 = jnp.concatenate([pad(ROPE_LO), krt_ref[0, 0], pad(LANE - ROPE_HI)], axis=0).T
    _kv_emit(ckv_ref[0, 0], kr128, *rest)


def _kv_expand_cache(ckv_all, krope_t, layer, lw, consts, tm):
    _, b, t, _ = ckv_all.shape
    nt = t // tm
    row = lambda w: pl.BlockSpec((tm, w), lambda bi, ti: (bi * nt + ti, 0))
    const = lambda shape: pl.BlockSpec(shape, lambda bi, ti: (0,) * len(shape))
    nk = N_MLA * HEAD_PAD
    nv = N_MLA * V_DIM
    return pl.pallas_call(
        _kv_cache_body, grid=(b, nt),
        in_specs=[pl.BlockSpec((1, 1, tm, KV_LORA), lambda bi, ti: (layer, bi, ti, 0)),
                  pl.BlockSpec((1, 1, ROPE_DIM, tm), lambda bi, ti: (layer, bi, 0, ti)),
                  const((KV_LORA, nk)), const((KV_LORA, nv)), const((1, LANE)), const((MXU, MXU))],
        out_specs=(row(nk), row(nv)),
        out_shape=(jax.ShapeDtypeStruct((b * t, nk), BF16), jax.ShapeDtypeStruct((b * t, nv), BF16)),
        compiler_params=_params("arbitrary", "arbitrary"), name="kv_expand_cache",
    )(ckv_all, krope_t, lw["w_k"], lw["w_v"], lw["gk"], consts["s128"])


def _kv_expand(ckv, kr128, lw, consts, tm):
    rows = kr128.shape[0]
    row = lambda w: pl.BlockSpec((tm, w), lambda i: (i, 0))
    nk = N_MLA * HEAD_PAD
    nv = N_MLA * V_DIM
    return pl.pallas_call(
        _kv_body, grid=(rows // tm,),
        in_specs=[row(KV_LORA), row(LANE), _full((KV_LORA, nk)), _full((KV_LORA, nv)),
                  _full((1, LANE)), _full((MXU, MXU))],
        out_specs=(row(nk), row(nv)),
        out_shape=(jax.ShapeDtypeStruct((rows, nk), BF16), jax.ShapeDtypeStruct((rows, nv), BF16)),
        compiler_params=_params("arbitrary"), name="kv_expand",
    )(ckv, kr128, lw["w_k"], lw["w_v"], lw["gk"], consts["s128"])


def _attn_prompt_body(*refs, fox, tq):
    n_in = 5 if fox else 3
    ins, (o_ref, o_scr) = refs[:n_in], refs[n_in:]
    q_ref, k_ref, v_ref = ins[:3]
    pair = pl.program_id(1)
    t = k_ref.shape[1]
    lane = lax.broadcasted_iota(jnp.int32, (tq, LANE), 1)
    r = lax.broadcasted_iota(jnp.int32, (tq, tq), 0)
    c = lax.broadcasted_iota(jnp.int32, (tq, tq), 1)
    allowed = (c <= r) if fox else ((c >> CHUNK_SHIFT) <= (r >> CHUNK_SHIFT))

    def scores(hh, i):
        lo, hi = i * tq, (i + 1) * tq
        if fox:
            qa_ref, ka_ref = ins[3:]
            qm = jnp.where((lane >> 6) == hh, q_ref[0, lo:hi, :].astype(F32), 0.0)
            qa = jnp.where((lane >> 4) == 2 * pair + hh, qa_ref[0, lo:hi, :].astype(F32), 0.0)
            q = jnp.concatenate([qm, qa], axis=1).astype(BF16)
            k = jnp.concatenate([k_ref[0, 0:hi, :], ka_ref[0, 0:hi, :]], axis=1)
        else:
            head = slice(HEAD_PAD * hh, HEAD_PAD * (hh + 1))
            q = q_ref[0, lo:hi, head]
            k = k_ref[0, 0:hi, head]
        s = _dot_nt(q, k)
        sd = jnp.where(allowed, s[:, lo:hi], -jnp.inf)
        return sd if i == 0 else jnp.concatenate([s[:, 0:lo], sd], axis=1)

    def softmax(s):
        p = jnp.exp2(s - jnp.max(s, axis=1, keepdims=True))
        return p.astype(BF16), jnp.sum(p, axis=1, keepdims=True)

    def values(hh, i, p, l):
        lo, hi = i * tq, (i + 1) * tq
        o_scr[hh, lo:hi, :] = _dot(p, v_ref[0, 0:hi, :]) / l

    order = list(reversed(range(t // tq)))
    for g in range(0, len(order), GROUP):
        units = [(hh, i) for i in order[g:g + GROUP] for hh in range(2)]
        ss = [scores(hh, i) for hh, i in units]
        pl_ = [softmax(s) for s in ss]
        for (hh, i), (p, l) in zip(units, pl_):
            values(hh, i, p, l)

    lane_t = lax.broadcasted_iota(jnp.int32, (t, LANE), 1)
    o_ref[0] = jnp.where(lane_t < V_DIM, o_scr[0], o_scr[1]).astype(BF16)


def _attn_prompt(q, k, v, aug, tq):
    fox = aug is not None
    b, t, _ = q.shape
    qw = LANE if fox else 2 * HEAD_PAD
    blk = lambda w, f: pl.BlockSpec((1, t, w), f)
    per_pair = lambda bi, p: (bi, 0, p)
    shared = lambda bi, p: (bi, 0, 0)
    in_specs = [blk(qw, per_pair), blk(qw, per_pair), blk(LANE, per_pair)]
    args = [q, k, v]
    if fox:
        in_specs += [blk(LANE, shared), blk(LANE, shared)]
        args += list(aug)
    return pl.pallas_call(
        functools.partial(_attn_prompt_body, fox=fox, tq=tq),
        grid=(b, N_MLA // 2), in_specs=in_specs, out_specs=blk(LANE, per_pair),
        out_shape=jax.ShapeDtypeStruct((b, t, N_MLA * V_DIM), BF16),
        scratch_shapes=[pltpu.VMEM((2, t, LANE), F32)],
        compiler_params=_params("arbitrary", "arbitrary"),
        name="attn_fox" if fox else "attn_mla",
    )(*args)


def _pad_rows(x, rows):
    return jnp.concatenate([x, jnp.zeros((rows - x.shape[0],) + x.shape[1:], x.dtype)], axis=0)


def _step_softmax(s_c, s_n):
    m = jnp.maximum(jnp.max(s_c, axis=1, keepdims=True), jnp.max(s_n, axis=1, keepdims=True))
    p_c = jnp.exp2(s_c - m)
    p_n = jnp.exp2(s_n - m)
    l = jnp.sum(p_c, axis=1, keepdims=True) + jnp.sum(p_n, axis=1, keepdims=True)
    return p_c.astype(BF16), p_n.astype(BF16), l


def _attn_mla_step_body(q_ref, kc_ref, vc_ref, kn_ref, vn_ref, o_ref, *, start):
    tq = q_ref.shape[1]
    tc = kc_ref.shape[1]
    lane = lax.broadcasted_iota(jnp.int32, (tq, LANE), 1)
    chunk = lambda pos: pos >> CHUNK_SHIFT
    qc = chunk(start + lax.broadcasted_iota(jnp.int32, (tq, 1), 0))
    ok_c = chunk(lax.broadcasted_iota(jnp.int32, (tq, tc), 1)) <= qc
    ok_n = (lane < tq) & (chunk(start + lane) <= qc)
    for p in range(N_MLA // 2):
        pair = slice(LANE * p, LANE * (p + 1))
        vc = vc_ref[0, :, pair]
        vn = _pad_rows(vn_ref[0, :, pair], LANE)
        outs = []
        for h in (2 * p, 2 * p + 1):
            head = slice(HEAD_PAD * h, HEAD_PAD * (h + 1))
            q = q_ref[0, :, head]
            s_c = jnp.where(ok_c, _dot_nt(q, kc_ref[0, :, head]), -jnp.inf)
            s_n = jnp.where(ok_n, _dot_nt(q, _pad_rows(kn_ref[0, :, head], LANE)), -jnp.inf)
            p_c, p_n, l = _step_softmax(s_c, s_n)
            outs.append((_dot(p_c, vc) + _dot(p_n, vn)) / l)
        o_ref[0, :, pair] = jnp.where(lane < V_DIM, outs[0], outs[1]).astype(BF16)


def _attn_mla_step(q, kc, vc, kn, vn, start):
    b, tq, _ = q.shape
    blk = lambda a: pl.BlockSpec((1,) + a.shape[1:], lambda bi: (bi, 0, 0))
    return pl.pallas_call(
        functools.partial(_attn_mla_step_body, start=start), grid=(b,),
        in_specs=[blk(q), blk(kc), blk(vc), blk(kn), blk(vn)],
        out_specs=pl.BlockSpec((1, tq, N_MLA * V_DIM), lambda bi: (bi, 0, 0)),
        out_shape=jax.ShapeDtypeStruct((b, tq, N_MLA * V_DIM), BF16),
        compiler_params=_params("arbitrary"), name="attn_mla_step",
    )(q, kc, vc, kn, vn)


def _lane_cumsum(x, triu):
    rows, blk = x.shape[0], triu.shape[0]
    carry = jnp.zeros((rows, 1), F32)
    out = []
    for b0 in range(0, x.shape[1], blk):
        hi, mid, lo = _split3(_pad_rows(x[:, b0:b0 + blk], 16))
        c = (_dot(hi, triu) + _dot(mid, triu) + _dot(lo, triu))[0:rows] + carry
        carry = c[:, blk - 1:blk]
        out.append(c)
    return jnp.concatenate(out, axis=1), carry


def _attn_fox_step_body(q_ref, kt_ref, vt_ref, kn_ref, vn_ref, lft_ref, lfn_ref,
                        triu_ref, tri_ref, triu_s_ref, o_ref):
    tq = q_ref.shape[1]
    lane = lax.broadcasted_iota(jnp.int32, (tq, LANE), 1)
    causal = lane <= lax.broadcasted_iota(jnp.int32, (tq, LANE), 0)
    cc, total = _lane_cumsum(lft_ref[0, 0], triu_ref[...])
    lfn = _pad_rows(lfn_ref[0], LANE)
    hi, mid, lo = _split3(lfn)
    tri = tri_ref[...]
    cn = (_dot(tri, hi) + _dot(tri, mid) + _dot(tri, lo))[0:tq]
    cnt, _ = _lane_cumsum(lfn.T[0:N_FOX, :], triu_s_ref[...])
    for p in range(N_FOX // 2):
        pair = slice(LANE * p, LANE * (p + 1))
        kt = kt_ref[0, 0, pair, :].astype(BF16)
        vt = vt_ref[0, 0, pair, :].astype(BF16)
        kn = _pad_rows(kn_ref[0, :, pair], LANE)
        vn = _pad_rows(vn_ref[0, :, pair], LANE)
        qf = q_ref[0, :, pair].astype(F32)
        outs = []
        for hh in range(2):
            h = 2 * p + hh
            q = jnp.where((lane >> 6) == hh, qf, 0.0).astype(BF16)
            col = jnp.sum(jnp.where(lane == h, cn, 0.0), axis=1, keepdims=True)
            bias_c = ((total[h:h + 1, :] + col) - cc[h:h + 1, :]) * LOG2E
            bias_n = (col - cnt[h:h + 1, :]) * LOG2E
            s_c = _dot(q, kt) + bias_c
            s_n = jnp.where(causal, _dot_nt(q, kn) + bias_n, -jnp.inf)
            p_c, p_n, l = _step_softmax(s_c, s_n)
            outs.append((_dot_nt(p_c, vt) + _dot(p_n, vn)) / l)
        o_ref[0, :, pair] = jnp.where(lane < V_DIM, outs[0], outs[1]).astype(BF16)


def _attn_fox_step(q, kt_all, vt_all, kn, vn, lft_all, lfn, layer, consts):
    b, tq, _ = q.shape
    new = lambda a: pl.BlockSpec((1,) + a.shape[1:], lambda bi: (bi, 0, 0))
    old = lambda a: pl.BlockSpec((1, 1) + a.shape[2:], lambda bi: (layer, bi, 0, 0))
    return pl.pallas_call(
        _attn_fox_step_body, grid=(b,),
        in_specs=[new(q), old(kt_all), old(vt_all), new(kn), new(vn), old(lft_all), new(lfn),
                  _full((MXU, MXU)), _full((LANE, LANE)), _full((LANE, LANE))],
        out_specs=pl.BlockSpec((1, tq, N_FOX * FOX_DIM), lambda bi: (bi, 0, 0)),
        out_shape=jax.ShapeDtypeStruct((b, tq, N_FOX * FOX_DIM), BF16),
        compiler_params=_params("arbitrary"), name="attn_fox_step",
    )(q, kt_all, vt_all, kn, vn, lft_all, lfn, consts["triu256"], consts["tri128"], consts["triu128"])


def _post_attn_body(x_ref, om_ref, of_ref, wo_ref, gffn_ref, wr_ref, br_ref, tri_ref,
                    x2_ref, xn_ref, route_ref, routet_ref, cnt_ref, carry_ref):
    nm = N_MLA * V_DIM
    x2 = x_ref[...] + _dot(om_ref[...], wo_ref[0:nm, :]) + _dot(of_ref[...], wo_ref[nm:, :])
    x2_ref[...] = x2
    xn = _rms(x2, gffn_ref[...], D_MODEL)
    xn_ref[...] = xn
    blocks = range(0, xn.shape[0], MXU)
    each = lambda f, *cols: [f(*v) for v in zip(*cols)]
    xs = [xn[r0:r0 + MXU] for r0 in blocks]
    xh = each(lambda v: v.astype(BF16), xs)
    xl = each(lambda v, h: (v - h.astype(F32)).astype(BF16), xs, xh)
    wr = wr_ref[...]
    r = each(lambda h, l: _dot(h, wr) + _dot(l, wr), xh, xl)
    bias = br_ref[...]
    lane = lax.broadcasted_iota(jnp.int32, (MXU, LANE), 1).astype(F32)
    valid = lane < N_EXPERTS
    lg = each(lambda v: jnp.where(valid, v + pltpu.roll(v, LANE - N_EXPERTS, 1) + bias, -jnp.inf), r)
    e = each(lambda v: jnp.exp(v - jnp.max(v, axis=1, keepdims=True)), lg)
    probs = each(lambda v: v / jnp.sum(v, axis=1, keepdims=True), e)
    first_at = lambda p, m: jnp.min(jnp.where(p == m, lane, float(LANE)), axis=1, keepdims=True)
    p1 = each(lambda p: jnp.where(valid, p, -1.0), probs)
    m1 = each(lambda p: jnp.max(p, axis=1, keepdims=True), p1)
    i1 = each(first_at, p1, m1)
    p2 = each(lambda p, i: jnp.where(lane == i, -1.0, p), p1, i1)
    m2 = each(lambda p: jnp.max(p, axis=1, keepdims=True), p2)
    i2 = each(first_at, p2, m2)

    @pl.when(pl.program_id(0) == 0)
    def _():
        carry_ref[...] = jnp.zeros(carry_ref.shape, F32)

    sel = each(lambda a, b: jnp.where((lane == a) | (lane == b), 1.0, 0.0), i1, i2)
    tri = tri_ref[...]
    counts = each(lambda v: _dot(tri, v.astype(BF16)), sel)
    total = carry_ref[0:1, :]
    for n, r0 in enumerate(blocks):
        incl = counts[n] + total
        total = incl[MXU - 1:, :]
        excl = incl - sel[n]
        rank1 = jnp.sum(jnp.where(lane == i1[n], excl, 0.0), axis=1, keepdims=True)
        rank2 = jnp.sum(jnp.where(lane == i2[n], excl, 0.0), axis=1, keepdims=True)
        den = m1[n] + m2[n]
        route = jnp.zeros((MXU, LANE), F32)
        for c, col in enumerate((i1[n], i2[n], rank1, rank2, m1[n] / den, m2[n] / den)):
            route = jnp.where(lane == c, col, route)
        route_ref[r0:r0 + MXU, :] = route
        routet_ref[:, r0:r0 + MXU] = route.T[0:8, :]
    carry_ref[...] = jnp.broadcast_to(total, carry_ref.shape)
    cnt_ref[...] = jnp.broadcast_to(total, cnt_ref.shape)


def _post_attn(x, om, of, lw, consts, tm):
    rows = x.shape[0]
    assert tm % MXU == 0
    row = lambda w: pl.BlockSpec((tm, w), lambda i: (i, 0))
    nm = N_MLA * V_DIM
    sds = jax.ShapeDtypeStruct
    return pl.pallas_call(
        _post_attn_body, grid=(rows // tm,),
        in_specs=[row(D_MODEL), row(nm), row(nm), _full((2 * nm, D_MODEL)), _full((1, D_MODEL)),
                  _full((D_MODEL, LANE)), _full((1, LANE)), _full((MXU, MXU))],
        out_specs=(row(D_MODEL), row(D_MODEL), row(LANE), pl.BlockSpec((8, tm), lambda i: (0, i)),
                   _full((8, LANE))),
        out_shape=(sds((rows, D_MODEL), F32), sds((rows, D_MODEL), F32), sds((rows, LANE), F32),
                   sds((8, rows), F32), sds((8, LANE), F32)),
        scratch_shapes=[pltpu.VMEM((8, LANE), F32)],
        compiler_params=_params("arbitrary"), name="post_attn_moe",
    )(x, om, of, lw["w_out"], lw["g_ffn"], lw["w_router"], lw["b_router"], consts["tri256"])


def _swiglu_acc(xb, wg_ref, wu_ref, wd_ref, d_ff, lo=0, hi=None):
    acc = None
    for c in range(lo * MXU, d_ff if hi is None else hi * MXU, MXU):
        g = _dot(xb, wg_ref[:, c:c + MXU])
        u = _dot(xb, wu_ref[:, c:c + MXU])
        h = (g * jax.nn.sigmoid(g) * u).astype(BF16)
        d = _dot(h, wd_ref[c:c + MXU, :])
        acc = d if acc is None else acc + d
    return acc


def _post_ffn_body(x_ref, om_ref, of_ref, wo_ref, gffn_ref, wg_ref, wu_ref, wd_ref, o_ref):
    nm = N_MLA * V_DIM
    x2 = x_ref[...] + _dot(om_ref[...], wo_ref[0:nm, :]) + _dot(of_ref[...], wo_ref[nm:, :])
    xn = _rms(x2, gffn_ref[...], D_MODEL).astype(BF16)
    o_ref[...] = x2 + _swiglu_acc(xn, wg_ref.at[0], wu_ref.at[0], wd_ref.at[0], D_FF)


def _post_ffn(x, om, of, lw, tm):
    rows = x.shape[0]
    row = lambda w: pl.BlockSpec((tm, w), lambda i: (i, 0))
    nm = N_MLA * V_DIM
    jl = lw["mixer_idx"]
    wspec = lambda a, b: pl.BlockSpec((1, a, b), lambda i: (jl, 0, 0))
    return pl.pallas_call(
        _post_ffn_body, grid=(rows // tm,),
        in_specs=[row(D_MODEL), row(nm), row(nm), _full((2 * nm, D_MODEL)), _full((1, D_MODEL)),
                  wspec(D_MODEL, D_FF), wspec(D_MODEL, D_FF), wspec(D_FF, D_MODEL)],
        out_specs=row(D_MODEL), out_shape=jax.ShapeDtypeStruct((rows, D_MODEL), F32),
        compiler_params=_params("arbitrary"), name="post_ffn",
    )(x, om, of, lw["w_out"], lw["g_ffn"], lw["w_gate"], lw["w_up"], lw["w_down"])


SCATTER_UNROLL = 8
SPLIT_CHUNK = 5


def _inv_body(pos_ref, base_hbm, inv_ref, sem, *, tmg):
    fill = pltpu.make_async_copy(base_hbm, inv_ref, sem.at[0])
    fill.start()
    fill.wait()

    def put(i, c):
        for u in range(SCATTER_UNROLL):
            a = i * SCATTER_UNROLL + u
            inv_ref[tmg + pos_ref[a]] = a
        return c

    lax.fori_loop(0, pos_ref.shape[0] // SCATTER_UNROLL, put, 0)


def _route_inverse(pos_flat, base, tmg):
    return pl.pallas_call(
        functools.partial(_inv_body, tmg=tmg),
        in_specs=[pl.BlockSpec(memory_space=pltpu.SMEM), pl.BlockSpec(memory_space=pl.ANY)],
        out_specs=pl.BlockSpec(memory_space=pltpu.SMEM),
        out_shape=jax.ShapeDtypeStruct(base.shape, jnp.int32),
        scratch_shapes=[pltpu.SemaphoreType.DMA((1,))], name="route_inverse",
    )(pos_flat, base)


GATHER, OUT, STAGE = 0, 2, 4


def _moe_routed_body(te_ref, stage_ref, src_ref, dst_ref, xn_hbm, wg_ref, wu_ref, wd_ref, z_hbm,
                     buf, gsem, ssem, *, tmg, n_tiles):
    del te_ref
    j = pl.program_id(0)

    def row_in(entry, slot, r):
        return pltpu.make_async_copy(xn_hbm.at[pl.ds(src_ref[entry], 1), :],
                                     buf.at[GATHER + slot, pl.ds(r, 1), :], gsem.at[slot])

    def row_out(entry, slot, r):
        return pltpu.make_async_copy(buf.at[OUT + slot, pl.ds(r, 1), :],
                                     z_hbm.at[pl.ds(dst_ref[entry], 1), :], ssem.at[0])

    def tile_in(slot):
        return pltpu.make_async_copy(xn_hbm.at[pl.ds(0, tmg), :], buf.at[GATHER + slot], gsem.at[slot])

    def tile_out(slot):
        return pltpu.make_async_copy(buf.at[OUT + slot], z_hbm.at[pl.ds(0, tmg), :], ssem.at[0])

    @pl.when(j == 0)
    def _():
        buf[OUT + 1] = jnp.zeros(buf.shape[1:], F32)
        for r in range(tmg):
            row_in(tmg + r, 0, r).start()

    def step(cur):
        nxt = 1 - cur

        @pl.when(j >= 1)
        def _():
            tile_out(cur).wait()

        tile_in(cur).wait()
        w = (wg_ref.at[0, 0], wu_ref.at[0, 0], wd_ref.at[0, 0])
        head = _swiglu_acc(buf[GATHER + cur].astype(BF16), *w, E_FF, 0, SPLIT_CHUNK)
        for r in range(tmg):
            row_in((j + 2) * tmg + r, nxt, r).start()
            row_out(j * tmg + r, nxt, r).start()
        stage = stage_ref[0]
        buf[stage] = buf[GATHER + cur]
        tail = _swiglu_acc(buf[stage].astype(BF16), *w, E_FF, SPLIT_CHUNK)
        buf[OUT + cur] = head + tail

    for parity in range(2):
        pl.when(j % 2 == parity)(functools.partial(step, parity))

    assert n_tiles % 2 == 0

    @pl.when(j == n_tiles)
    def _():
        tile_out(1).wait()
        tile_in(1).wait()


def _moe_routed(xn, tile_expert, src, dst, lw, tmg, n_tiles):
    n = xn.shape[0]
    jl = lw["mixer_idx"]
    wspec = lambda a, b: pl.BlockSpec((1, 1, a, b), lambda j, te, st, s, d: (jl, te[j], 0, 0))
    grid_spec = pltpu.PrefetchScalarGridSpec(
        num_scalar_prefetch=4, grid=(n_tiles + 1,),
        in_specs=[pl.BlockSpec(memory_space=pl.ANY), wspec(D_MODEL, E_FF), wspec(D_MODEL, E_FF),
                  wspec(E_FF, D_MODEL)],
        out_specs=pl.BlockSpec(memory_space=pl.ANY),
        scratch_shapes=[pltpu.VMEM((STAGE + 1, tmg, D_MODEL), F32),
                        pltpu.SemaphoreType.DMA((2,)), pltpu.SemaphoreType.DMA((1,))])
    return pl.pallas_call(
        functools.partial(_moe_routed_body, tmg=tmg, n_tiles=n_tiles),
        grid_spec=grid_spec, out_shape=jax.ShapeDtypeStruct((2 * n + tmg, D_MODEL), F32),
        compiler_params=_params("arbitrary"), name="moe_routed",
    )(tile_expert, jnp.full((1,), STAGE, jnp.int32), src, dst, xn,
      lw["we_gate"], lw["we_up"], lw["we_down"])


def _combine_body(x2_ref, route_ref, z0_ref, z1_ref, o_ref):
    o_ref[...] = _combine(x2_ref, route_ref, z0_ref, z1_ref)


def _moe_combine(x2, route, z, tm):
    n = x2.shape[0]
    row = lambda w: pl.BlockSpec((tm, w), lambda i: (i, 0))
    return pl.pallas_call(
        _combine_body, grid=(n // tm,),
        in_specs=[row(D_MODEL), row(LANE), row(D_MODEL),
                  pl.BlockSpec((tm, D_MODEL), lambda i: (i + n // tm, 0))],
        out_specs=row(D_MODEL), out_shape=jax.ShapeDtypeStruct((n, D_MODEL), F32),
        compiler_params=_params("arbitrary"), name="moe_combine",
    )(x2, route, z, z)


def _moe(x2, xn, route, route_t, counts, lw, tmg, defer_combine=False):
    n = x2.shape[0]
    assert n & (n - 1) == 0
    n_tiles = 2 * n // tmg + N_EXPERTS
    ext = (n_tiles + 3) * tmg
    e = route_t[0:2].astype(jnp.int32)
    rank = route_t[2:4].astype(jnp.int32)
    cnt = counts[0, :N_EXPERTS].astype(jnp.int32)
    tiles = (cnt + tmg - 1) // tmg
    tile_end = jnp.cumsum(tiles)
    row_start = (tile_end - tiles) * tmg
    start_of = jnp.sum(jnp.where(e[..., None] == jnp.arange(N_EXPERTS), row_start, 0), axis=-1)
    pos_flat = (start_of + rank).reshape(-1)
    steps = jnp.minimum(jnp.arange(n_tiles + 1, dtype=jnp.int32), tile_end[-1] - 1)
    tile_expert = jnp.minimum(jnp.sum((steps[:, None] >= tile_end[None, :]).astype(jnp.int32), axis=1),
                              N_EXPERTS - 1)
    idx = jnp.arange(ext, dtype=jnp.int32)
    dst = _route_inverse(pos_flat, 2 * n + (idx & (tmg - 1)), tmg)
    src = jnp.where(dst < 2 * n, dst, idx) & (n - 1)
    z = _moe_routed(xn, tile_expert, src, dst, lw, tmg, n_tiles)
    return (x2, route, z) if defer_combine else _moe_combine(x2, route, z, tmg)


def _consts():
    i = jnp.arange(MXU)
    blockdiag = lambda w: ((i[:, None] // w) == (i[None, :] // w)).astype(BF16)
    tri = lambda n: (jnp.arange(n)[None, :] <= jnp.arange(n)[:, None]).astype(BF16)
    r = jnp.arange(LANE)
    place = jnp.stack([((r[:, None] < N_FOX) & (r[None, :] == AUG_W * r[:, None] + k)).astype(BF16)
                       for k in range(6)])
    within = r % AUG_W
    headed = r < AUG_W * N_FOX
    ones_q = (headed & (within >= 3) & (within < 6)).astype(F32)
    ones_k = (headed & (within < 3)).astype(F32)
    aug_ones = jnp.zeros((8, LANE), F32).at[0].set(ones_q).at[1].set(ones_k)
    return {"s64": blockdiag(FOX_DIM), "s128": blockdiag(HEAD_PAD), "tri256": tri(MXU),
            "tri128": tri(LANE), "triu256": tri(MXU).T, "triu128": tri(LANE).T,
            "place": place, "aug_ones": aug_ones}


def _rope_tables(pos):
    half = ROPE_DIM // 2
    inv = ROPE_BASE ** (-jnp.arange(half, dtype=F32) / half)
    ang = pos.astype(F32)[:, None] * inv[None, :]
    cos, sin = jnp.cos(ang), jnp.sin(ang)
    t = pos.shape[0]
    ct = jnp.concatenate([jnp.ones((t, ROPE_LO), F32), cos, cos, jnp.zeros((t, LANE - ROPE_HI), F32)], 1)
    st = jnp.concatenate([jnp.zeros((t, ROPE_LO), F32), -sin, sin, jnp.zeros((t, LANE - ROPE_HI), F32)], 1)
    return ct, st


def _pad_lanes(v, width=LANE):
    return jnp.pad(v, [(0, 0)] * (v.ndim - 1) + [(0, width - v.shape[-1])])


def _prep_layer(i, p):
    w_in = p["w_in"][i]
    s = [0, Q_LORA, Q_LORA + KV_LORA, Q_LORA + KV_LORA + ROPE_DIM]
    nfx = N_FOX * FOX_DIM
    s += [s[3] + nfx, s[3] + 2 * nfx, s[3] + 3 * nfx, s[3] + 3 * nfx + N_FOX]
    c_q, c_kv, k_rope = w_in[:, s[0]:s[1]], w_in[:, s[1]:s[2]], w_in[:, s[2]:s[3]]
    fq, fk, fv, f_logit = w_in[:, s[3]:s[4]], w_in[:, s[4]:s[5]], w_in[:, s[5]:s[6]], w_in[:, s[6]:s[7]]
    zc = lambda n: jnp.zeros((D_MODEL, n), F32)
    misc = jnp.concatenate([f_logit, zc(ROPE_LO - N_FOX), k_rope, zc(LANE - ROPE_HI)], axis=1)
    w_in_p = jnp.concatenate([c_q, c_kv, fq, fk, fv, misc], axis=1).astype(BF16)

    w_uq = p["w_uq"][i].reshape(Q_LORA, N_MLA, QK_DIM)
    w_uq_p = _pad_lanes(w_uq, HEAD_PAD).reshape(Q_LORA, -1).astype(BF16)

    w_ukv = p["w_ukv"][i].reshape(KV_LORA, N_MLA, NOPE_DIM + V_DIM)
    w_k = _pad_lanes(w_ukv[..., :NOPE_DIM], HEAD_PAD).reshape(KV_LORA, -1).astype(BF16)
    w_v = w_ukv[..., NOPE_DIM:].reshape(KV_LORA, -1).astype(BF16)

    gq = _pad_lanes(p["g_qn_mla"][i] * (QK_DIM ** -0.5 * LOG2E))[None]
    gk = _pad_lanes(p["g_kn_mla"][i])[None]
    lw = {
        "g_mix": p["g_mix"][i][None], "w_in": w_in_p, "b_f": _pad_lanes(p["b_f"][i])[None],
        "g_cq": p["g_cq"][i][None], "g_ckv": p["g_ckv"][i][None], "w_uq": w_uq_p,
        "gq": gq, "gk": gk,
        "gfq": jnp.tile(p["g_qn_fox"][i] * (FOX_DIM ** -0.5 * LOG2E), N_FOX)[None],
        "gfk": jnp.tile(p["g_kn_fox"][i], N_FOX)[None],
        "w_k": w_k, "w_v": w_v,
        "w_out": p["w_out"][i].astype(BF16), "g_ffn": p["g_ffn"][i][None],
    }
    j = i // 2
    lw["mixer_idx"] = j
    if i % 2 == 0:
        lw.update(w_gate=p["w_gate_b"], w_up=p["w_up_b"], w_down=p["w_down_b"])
    else:
        wr = p["w_router"][j]
        wr_hi = wr.astype(BF16)
        wr_lo = (wr - wr_hi.astype(F32)).astype(BF16)
        lw.update(w_router=_pad_lanes(jnp.concatenate([wr_hi, wr_lo], axis=1)),
                  b_router=_pad_lanes(p["b_router"][j])[None],
                  we_gate=p["we_gate_b"], we_up=p["we_up_b"], we_down=p["we_down_b"])
    return lw


def _channel_mixer(i, x, om, of, lw, consts, tm_ffn, defer_combine=False):
    if i % 2 == 0:
        return _post_ffn(x, om, of, lw, tm_ffn)
    x2, xn, route, route_t, counts = _post_attn(x, om, of, lw, consts, min(tm_ffn, x.shape[0]))
    return _moe(x2, xn, route, route_t, counts, lw, MXU, defer_combine)


def _trunk_prompt(x, layers, consts):
    b, t, _ = x.shape
    rows = b * t
    x = x.reshape(rows, D_MODEL)
    ct, st = _rope_tables(jnp.arange(t, dtype=jnp.int32))
    nfx = N_FOX * FOX_DIM
    stack = ()
    pending = None
    for i, lw in enumerate(layers):
        outs = _mixer_in_prompt(x, lw, ct, st, consts, stack, i, b, combine=pending)
        stack = outs[:N_STACKED]
        q, fq, fkb, fvb, k, v, qaug, kaug = outs[N_STACKED:N_STACKED + 8]
        if pending is not None:
            x = outs[-1]
        om = _attn_prompt(q.reshape(b, t, -1), k.reshape(b, t, -1), v.reshape(b, t, -1), None, 256)
        of = _attn_prompt(fq.reshape(b, t, nfx), fkb.reshape(b, t, nfx), fvb.reshape(b, t, nfx),
                          (qaug.reshape(b, t, LANE), kaug.reshape(b, t, LANE)), 256)
        defer = i % 2 == 1 and i + 1 < len(layers)
        res = _channel_mixer(i, x, om.reshape(rows, -1), of.reshape(rows, -1), lw, consts, 512, defer)
        x, pending = (None, res) if defer else (res, None)
    ckv_s, krt_s, fkt_s, fvt_s, lft_s = stack
    heads = lambda a: jnp.transpose(a.reshape(DEPTH, b, N_FOX, FOX_DIM, t), (0, 1, 4, 2, 3))
    state = (ckv_s.reshape(DEPTH, b, t, KV_LORA), jnp.swapaxes(krt_s, 2, 3), heads(fkt_s), heads(fvt_s),
             jnp.swapaxes(lft_s, 2, 3))
    return x.reshape(b, t, D_MODEL), state


def _trunk_sample(x, past, layers, consts):
    b, t, _ = x.shape
    rows = b * t
    start = PAST_LEN
    x = x.reshape(rows, D_MODEL)
    ct, st = _rope_tables(start + jnp.arange(t, dtype=jnp.int32))
    ct, st = jnp.tile(ct, (b, 1)), jnp.tile(st, (b, 1))
    p_ckv, p_krope, p_fk, p_fv, p_logf = past
    nfx = N_FOX * FOX_DIM
    krope_t = jnp.swapaxes(p_krope, 2, 3)
    fk_t = jnp.transpose(p_fk, (0, 1, 3, 4, 2)).reshape(DEPTH, b, nfx, start)
    fv_t = jnp.transpose(p_fv, (0, 1, 3, 4, 2)).reshape(DEPTH, b, nfx, start)
    lf_t = jnp.swapaxes(p_logf, 2, 3)
    state = ([], [], [], [], [])
    for i, lw in enumerate(layers):
        ckv, kr, kr128, fk, fv, lf128, lf, q, fq, fkb, fvb = _mixer_in(x, lw, ct, st, consts, rows)
        k_c, v_c = _kv_expand_cache(p_ckv, krope_t, i, lw, consts, 512)
        k_n, v_n = _kv_expand(ckv, kr128, lw, consts, rows)
        om = _attn_mla_step(q.reshape(b, t, -1), k_c.reshape(b, start, -1), v_c.reshape(b, start, -1),
                            k_n.reshape(b, t, -1), v_n.reshape(b, t, -1), start)
        of = _attn_fox_step(fq.reshape(b, t, nfx), fk_t, fv_t, fkb.reshape(b, t, nfx),
                            fvb.reshape(b, t, nfx), lf_t, lf128.reshape(b, t, LANE), i, consts)
        x = _channel_mixer(i, x, om.reshape(rows, -1), of.reshape(rows, -1), lw, consts, rows)
        for lst, s_ in zip(state, (ckv.reshape(b, t, KV_LORA), kr.reshape(b, t, ROPE_DIM),
                                   fk.reshape(b, t, N_FOX, FOX_DIM), fv.reshape(b, t, N_FOX, FOX_DIM),
                                   lf.reshape(b, t, N_FOX))):
            lst.append(s_)
    return x.reshape(b, t, D_MODEL), tuple(jnp.stack(s_) for s_ in state)


def kernel(x_prompt, x_sample, cache_mla_ckv, cache_mla_krope, cache_fox_k, cache_fox_v, cache_fox_logf, g_mix, w_in, b_f, g_cq, g_ckv, w_uq, w_ukv, g_qn_mla, g_kn_mla, g_qn_fox, g_kn_fox, w_out, g_ffn, w_gate, w_up, w_down, w_router, b_router, we_gate, we_up, we_down):
    p = dict(g_mix=g_mix, w_in=w_in, b_f=b_f, g_cq=g_cq, g_ckv=g_ckv, w_uq=w_uq, w_ukv=w_ukv,
             g_qn_mla=g_qn_mla, g_kn_mla=g_kn_mla, g_qn_fox=g_qn_fox, g_kn_fox=g_kn_fox,
             w_out=w_out, g_ffn=g_ffn, w_gate=w_gate, w_up=w_up, w_down=w_down,
             w_router=w_router, b_router=b_router, we_gate=we_gate, we_up=we_up, we_down=we_down)
    for name in ("w_gate", "w_up", "w_down", "we_gate", "we_up", "we_down"):
        p[name + "_b"] = p[name].astype(BF16)
    layers = [_prep_layer(i, p) for i in range(DEPTH)]
    consts = _consts()
    y_p, st_p = _trunk_prompt(x_prompt, layers, consts)
    past = (cache_mla_ckv, cache_mla_krope, cache_fox_k, cache_fox_v, cache_fox_logf)
    y_s, st_s = _trunk_sample(x_sample, past, layers, consts)
    return (y_p, y_s) + st_p + st_s
```

```python
import functools

import jax
import jax.numpy as jnp
from jax import lax
from jax.experimental import pallas as pl
from jax.experimental.pallas import tpu as pltpu

F32 = jnp.float32
BF16 = jnp.bfloat16

D_MODEL = 1024
DEPTH = 4
PAST_LEN = 2048
CHUNK_SHIFT = 6
EPS = 1e-6
ROPE_BASE = 10000.0

N_MLA = 8
Q_LORA = 384
KV_LORA = 256
NOPE_DIM = 64
ROPE_DIM = 32
QK_DIM = NOPE_DIM + ROPE_DIM
V_DIM = 64
N_FOX = 8
FOX_DIM = 64
D_FF = 2816
N_EXPERTS = 8
E_FF = 1792

LOG2E = 1.4426950408889634
LANE = 128
MXU = 256
HEAD_PAD = 128
ROPE_LO = NOPE_DIM
ROPE_HI = NOPE_DIM + ROPE_DIM
AUG_W = 16
GROUP = 2

CQ_LO, CQ_HI = 0, Q_LORA
CKV_LO, CKV_HI = CQ_HI, CQ_HI + KV_LORA
FQ_LO, FQ_HI = CKV_HI, CKV_HI + N_FOX * FOX_DIM
FK_LO, FK_HI = FQ_HI, FQ_HI + N_FOX * FOX_DIM
FV_LO, FV_HI = FK_HI, FK_HI + N_FOX * FOX_DIM
MISC_LO, MISC_HI = FV_HI, FV_HI + LANE
D_IN_P = MISC_HI

VMEM_LIMIT = 56 * 1024 * 1024


def _params(*sem):
    return pltpu.CompilerParams(dimension_semantics=sem, vmem_limit_bytes=VMEM_LIMIT)


def _dot(a, b):
    return jnp.dot(a, b, preferred_element_type=F32)


def _dot_nt(a, b):
    return lax.dot_general(a, b, (((1,), (1,)), ((), ())), preferred_element_type=F32)


def _rms(x, g, n):
    return x * lax.rsqrt(jnp.sum(x * x, axis=-1, keepdims=True) * (1.0 / n) + EPS) * g


def _head_norm(x, smat, dim):
    outs = []
    for c in range(0, x.shape[1], MXU):
        xc = x[:, c:c + MXU]
        ss = _dot((xc * xc).astype(BF16), smat)
        outs.append(xc * lax.rsqrt(ss * (1.0 / dim) + EPS))
    return jnp.concatenate(outs, axis=1)


def _full(shape):
    return pl.BlockSpec(shape, lambda *_: (0,) * len(shape))


def _split3(c):
    hi = c.astype(BF16)
    r1 = c - hi.astype(F32)
    mid = r1.astype(BF16)
    lo = (r1 - mid.astype(F32)).astype(BF16)
    return hi, mid, lo


def _combine(x2_ref, route_ref, z0_ref, z1_ref):
    route = route_ref[...]
    lane = lax.broadcasted_iota(jnp.int32, route.shape, 1)
    g1 = jnp.sum(jnp.where(lane == 4, route, 0.0), axis=1, keepdims=True)
    g2 = jnp.sum(jnp.where(lane == 5, route, 0.0), axis=1, keepdims=True)
    return x2_ref[...] + g1 * z0_ref[...] + g2 * z1_ref[...]


N_MIXER_W = 13


def _mixer_core(x, gmix_ref, win_ref, bf_ref, gcq_ref, gckv_ref, wuq_ref, gq_ref,
                gfq_ref, gfk_ref, ct_ref, st_ref, s64_ref, s128_ref):
    tm = x.shape[0]
    xb = _rms(x, gmix_ref[...], D_MODEL).astype(BF16)

    def proj(lo, hi):
        return _dot(xb, win_ref[:, lo:hi])

    ct = ct_ref[...]
    st = st_ref[...]
    lane = lax.broadcasted_iota(jnp.int32, (tm, LANE), 1)

    cq_raw = proj(CQ_LO, CQ_HI)
    a = proj(MISC_LO, MISC_HI)
    ckv_raw = proj(CKV_LO, CKV_HI)
    fk_raw = proj(FK_LO, FK_HI)
    fv = proj(FV_LO, FV_HI)
    fq_raw = proj(FQ_LO, FQ_HI)

    cq = _rms(cq_raw, gcq_ref[...], Q_LORA).astype(BF16)
    qa = _dot(cq, wuq_ref[...])
    half = ROPE_DIM // 2

    def rotary(blk):
        sw = jnp.where(lane < ROPE_LO + half, pltpu.roll(blk, LANE - half, 1), pltpu.roll(blk, half, 1))
        return blk * ct + sw * st

    z = a + bf_ref[...]
    lf = jnp.minimum(z, 0.0) - jnp.log1p(jnp.exp(-jnp.abs(z)))
    lf128 = jnp.where(lane < N_FOX, lf, 0.0)
    kr128 = jnp.where((lane >= ROPE_LO) & (lane < ROPE_HI), rotary(a), 0.0)
    ckv = _rms(ckv_raw, gckv_ref[...], KV_LORA)

    qr = jnp.concatenate(
        [rotary(qa[:, h * HEAD_PAD:(h + 1) * HEAD_PAD]) for h in range(N_MLA)], axis=1)
    qn = _head_norm(qr, s128_ref[...], QK_DIM)
    gq = gq_ref[...]
    q = jnp.concatenate(
        [qn[:, h * HEAD_PAD:(h + 1) * HEAD_PAD] * gq for h in range(N_MLA)], axis=1).astype(BF16)

    s64 = s64_ref[...]
    fk = _head_norm(fk_raw, s64, FOX_DIM) * gfk_ref[...]
    fq = (_head_norm(fq_raw, s64, FOX_DIM) * gfq_ref[...]).astype(BF16)
    return ckv, kr128, lf128, fk, fv, fq, q


def _mixer_in_body(x_ref, *refs):
    (ckv_ref, kr_ref, kr128_ref, fk_ref, fv_ref, lf128_ref, lf_ref,
     q_ref, fq_ref, fkb_ref, fvb_ref) = refs[N_MIXER_W:]
    ckv, kr128, lf128, fk, fv, fq, q = _mixer_core(x_ref[...], *refs[:N_MIXER_W])
    ckv_ref[...] = ckv
    kr128_ref[...] = kr128
    kr_ref[...] = kr128[:, ROPE_LO:ROPE_HI]
    lf128_ref[...] = lf128
    lf_ref[...] = lf128[:, 0:N_FOX]
    fk_ref[...] = fk
    fv_ref[...] = fv
    fkb_ref[...] = fk.astype(BF16)
    fvb_ref[...] = fv.astype(BF16)
    fq_ref[...] = fq
    q_ref[...] = q


def _fox_aug(c, place_ref, ones_ref):
    hi, mid, lo = _split3(c * LOG2E)
    qa = _dot(hi, place_ref[0]) + _dot(mid, place_ref[1]) + _dot(lo, place_ref[2]) + ones_ref[0:1, :]
    ka = ones_ref[1:2, :] - (_dot(hi, place_ref[3]) + _dot(mid, place_ref[4]) + _dot(lo, place_ref[5]))
    return qa.astype(BF16), ka.astype(BF16)


N_STACKED = 5


def _mixer_prompt_body(*refs, first, combine, n_alias, tiles_per_seq):
    n_x = 4 if combine else 1
    n_in = n_x + N_MIXER_W + 6 + n_alias
    weights = refs[n_x:n_x + N_MIXER_W]
    wk_ref, wv_ref, gk_ref, tri_ref, place_ref, ones_ref = refs[n_x + N_MIXER_W:n_x + N_MIXER_W + 6]
    outs = list(refs[n_in:])
    carry_ref = outs.pop()
    (ckv_ref, kr_ref, fk_ref, fv_ref, lf_ref, q_ref, fq_ref, fkb_ref, fvb_ref,
     k_ref, v_ref, qa_ref, ka_ref) = outs[:13]
    if combine:
        x = _combine(*refs[:4])
        outs[13][...] = x
    else:
        x = refs[0][...]
    ckv, kr128, lf128, fk, fv, fq, q = _mixer_core(x, *weights)
    q_ref[...] = q
    fq_ref[...] = fq
    fkb_ref[...] = fk.astype(BF16)
    fvb_ref[...] = fv.astype(BF16)
    ckv_ref[0] = ckv
    kr_ref[0, 0] = kr128.T[ROPE_LO:ROPE_HI, :]
    lf_ref[0, 0] = lf128.T[0:N_FOX, :]
    fk_ref[0, 0] = fk.T
    fv_ref[0, 0] = fv.T
    if first:
        for ref in (ckv_ref, kr_ref, lf_ref, fk_ref, fv_ref):
            ref[1:] = jnp.zeros((ref.shape[0] - 1,) + ref.shape[1:], F32)

    _kv_emit(ckv, kr128, wk_ref, wv_ref, gk_ref, weights[-1], k_ref, v_ref)

    @pl.when(pl.program_id(0) % tiles_per_seq == 0)
    def _():
        carry_ref[...] = jnp.zeros(carry_ref.shape, F32)

    tri = tri_ref[...]
    carry = carry_ref[0:1, :]
    for r0 in range(0, lf128.shape[0], MXU):
        hi, mid, lo = _split3(lf128[r0:r0 + MXU, :])
        c = _dot(tri, hi) + _dot(tri, mid) + _dot(tri, lo) + carry
        carry = c[MXU - 1:, :]
        qa, ka = _fox_aug(c, place_ref, ones_ref)
        qa_ref[r0:r0 + MXU, :] = qa
        ka_ref[r0:r0 + MXU, :] = ka
    carry_ref[...] = jnp.broadcast_to(carry, carry_ref.shape)


def _mixer_specs(tm, nt):
    nfx = N_FOX * FOX_DIM
    tab = pl.BlockSpec((tm, LANE), lambda i: (i % nt, 0))
    return [_full((1, D_MODEL)), _full((D_MODEL, D_IN_P)), _full((1, LANE)),
            _full((1, Q_LORA)), _full((1, KV_LORA)), _full((Q_LORA, N_MLA * HEAD_PAD)),
            _full((1, LANE)), _full((1, nfx)), _full((1, nfx)), tab, tab,
            _full((MXU, MXU)), _full((MXU, MXU))]


def _mixer_args(lw, ct, st, consts):
    return [lw["g_mix"], lw["w_in"], lw["b_f"], lw["g_cq"], lw["g_ckv"], lw["w_uq"], lw["gq"],
            lw["gfq"], lw["gfk"], ct, st, consts["s64"], consts["s128"]]


def _mixer_in(x, lw, ct, st, consts, tm):
    rows = x.shape[0]
    nt = ct.shape[0] // tm
    row = lambda w: pl.BlockSpec((tm, w), lambda i: (i, 0))
    nfx = N_FOX * FOX_DIM
    sds = jax.ShapeDtypeStruct
    outs = [(sds((rows, KV_LORA), F32), row(KV_LORA)), (sds((rows, ROPE_DIM), F32), row(ROPE_DIM)),
            (sds((rows, LANE), F32), row(LANE)), (sds((rows, nfx), F32), row(nfx)),
            (sds((rows, nfx), F32), row(nfx)), (sds((rows, LANE), F32), row(LANE)),
            (sds((rows, N_FOX), F32), row(N_FOX)),
            (sds((rows, N_MLA * HEAD_PAD), BF16), row(N_MLA * HEAD_PAD)),
            (sds((rows, nfx), BF16), row(nfx)), (sds((rows, nfx), BF16), row(nfx)),
            (sds((rows, nfx), BF16), row(nfx))]
    args = _mixer_args(lw, ct, st, consts)
    assert len(args) == N_MIXER_W
    return pl.pallas_call(
        _mixer_in_body, grid=(rows // tm,), in_specs=[row(D_MODEL)] + _mixer_specs(tm, nt),
        out_specs=tuple(o[1] for o in outs), out_shape=tuple(o[0] for o in outs),
        compiler_params=_params("arbitrary"), name="mixer_in",
    )(x, *args)


def _mixer_in_prompt(x, lw, ct, st, consts, stack, layer, batch, combine=None):
    tm = 2 * MXU
    rows = (x if combine is None else combine[0]).shape[0]
    t = rows // batch
    nt = t // tm
    first = not stack
    assert first == (layer == 0)
    row = lambda w: pl.BlockSpec((tm, w), lambda i: (i, 0))
    nfx = N_FOX * FOX_DIM
    nk = N_MLA * HEAD_PAD
    nv = N_MLA * V_DIM
    sds = jax.ShapeDtypeStruct
    nd = DEPTH if first else 1
    tmaj = lambda w: pl.BlockSpec((nd, 1, w, tm), lambda i: (layer, i // nt, 0, i % nt))
    outs = [(sds((DEPTH, rows, KV_LORA), F32), pl.BlockSpec((nd, tm, KV_LORA), lambda i: (layer, i, 0))),
            (sds((DEPTH, batch, ROPE_DIM, t), F32), tmaj(ROPE_DIM)),
            (sds((DEPTH, batch, nfx, t), F32), tmaj(nfx)),
            (sds((DEPTH, batch, nfx, t), F32), tmaj(nfx)),
            (sds((DEPTH, batch, N_FOX, t), F32), tmaj(N_FOX)),
            (sds((rows, nk), BF16), row(nk)), (sds((rows, nfx), BF16), row(nfx)),
            (sds((rows, nfx), BF16), row(nfx)), (sds((rows, nfx), BF16), row(nfx)),
            (sds((rows, nk), BF16), row(nk)), (sds((rows, nv), BF16), row(nv)),
            (sds((rows, LANE), BF16), row(LANE)), (sds((rows, LANE), BF16), row(LANE))]
    if combine is None:
        in_specs, args = [row(D_MODEL)], [x]
    else:
        x2, route, z = combine
        in_specs = [row(D_MODEL), row(LANE), row(D_MODEL),
                    pl.BlockSpec((tm, D_MODEL), lambda i: (i + rows // tm, 0))]
        args = [x2, route, z, z]
        outs.append((sds((rows, D_MODEL), F32), row(D_MODEL)))
    in_specs += _mixer_specs(tm, nt) + [_full((KV_LORA, nk)), _full((KV_LORA, nv)), _full((1, LANE)),
                                        _full((MXU, MXU)), _full((6, LANE, LANE)), _full((8, LANE))]
    args += _mixer_args(lw, ct, st, consts) + [lw["w_k"], lw["w_v"], lw["gk"], consts["tri256"],
                                                consts["place"], consts["aug_ones"]]
    aliases = {}
    if stack:
        aliases = {len(args) + n: n for n in range(N_STACKED)}
        in_specs += [pl.BlockSpec(memory_space=pl.ANY)] * N_STACKED
        args += list(stack)
    return pl.pallas_call(
        functools.partial(_mixer_prompt_body, first=first, combine=combine is not None,
                          n_alias=len(stack), tiles_per_seq=nt),
        grid=(rows // tm,), in_specs=in_specs,
        out_specs=tuple(o[1] for o in outs), out_shape=tuple(o[0] for o in outs),
        scratch_shapes=[pltpu.VMEM((8, LANE), F32)],
        input_output_aliases=aliases, compiler_params=_params("arbitrary"), name="mixer_in_prompt",
    )(*args)


def _kv_emit(ckv, kr, wk_ref, wv_ref, gk_ref, s128_ref, k_ref, v_ref):
    cb = ckv.astype(BF16)
    kn = _dot(cb, wk_ref[...])
    kk = jnp.concatenate(
        [kn[:, h * HEAD_PAD:(h + 1) * HEAD_PAD] + kr for h in range(N_MLA)], axis=1)
    kk = _head_norm(kk, s128_ref[...], QK_DIM)
    gk = gk_ref[...]
    k_ref[...] = jnp.concatenate(
        [kk[:, h * HEAD_PAD:(h + 1) * HEAD_PAD] * gk for h in range(N_MLA)], axis=1).astype(BF16)
    v_ref[...] = _dot(cb, wv_ref[...]).astype(BF16)


def _kv_body(ckv_ref, kr128_ref, *rest):
    _kv_emit(ckv_ref[...], kr128_ref[...], *rest)


def _kv_cache_body(ckv_ref, krt_ref, *rest):
    tm = krt_ref.shape[-1]
    pad = lambda n: jnp.zeros((n, tm), F32)
    kr128 = jnp.concatenate([pad(ROPE_LO), krt_ref[0, 0], pad(LANE - ROPE_HI)], axis=0).T
    _kv_emit(ckv_ref[0, 0], kr128, *rest)


def _kv_expand_cache(ckv_all, krope_t, layer, lw, consts, tm):
    _, b, t, _ = ckv_all.shape
    nt = t // tm
    row = lambda w: pl.BlockSpec((tm, w), lambda bi, ti: (bi * nt + ti, 0))
    const = lambda shape: pl.BlockSpec(shape, lambda bi, ti: (0,) * len(shape))
    nk = N_MLA * HEAD_PAD
    nv = N_MLA * V_DIM
    return pl.pallas_call(
        _kv_cache_body, grid=(b, nt),
        in_specs=[pl.BlockSpec((1, 1, tm, KV_LORA), lambda bi, ti: (layer, bi, ti, 0)),
                  pl.BlockSpec((1, 1, ROPE_DIM, tm), lambda bi, ti: (layer, bi, 0, ti)),
                  const((KV_LORA, nk)), const((KV_LORA, nv)), const((1, LANE)), const((MXU, MXU))],
        out_specs=(row(nk), row(nv)),
        out_shape=(jax.ShapeDtypeStruct((b * t, nk), BF16), jax.ShapeDtypeStruct((b * t, nv), BF16)),
        compiler_params=_params("arbitrary", "arbitrary"), name="kv_expand_cache",
    )(ckv_all, krope_t, lw["w_k"], lw["w_v"], lw["gk"], consts["s128"])


def _kv_expand(ckv, kr128, lw, consts, tm):
    rows = kr128.shape[0]
    row = lambda w: pl.BlockSpec((tm, w), lambda i: (i, 0))
    nk = N_MLA * HEAD_PAD
    nv = N_MLA * V_DIM
    return pl.pallas_call(
        _kv_body, grid=(rows // tm,),
        in_specs=[row(KV_LORA), row(LANE), _full((KV_LORA, nk)), _full((KV_LORA, nv)),
                  _full((1, LANE)), _full((MXU, MXU))],
        out_specs=(row(nk), row(nv)),
        out_shape=(jax.ShapeDtypeStruct((rows, nk), BF16), jax.ShapeDtypeStruct((rows, nv), BF16)),
        compiler_params=_params("arbitrary"), name="kv_expand",
    )(ckv, kr128, lw["w_k"], lw["w_v"], lw["gk"], consts["s128"])


def _attn_prompt_body(*refs, fox, tq):
    n_in = 5 if fox else 3
    ins, (o_ref, o_scr) = refs[:n_in], refs[n_in:]
    q_ref, k_ref, v_ref = ins[:3]
    pair = pl.program_id(1)
    t = k_ref.shape[1]
    lane = lax.broadcasted_iota(jnp.int32, (tq, LANE), 1)
    r = lax.broadcasted_iota(jnp.int32, (tq, tq), 0)
    c = lax.broadcasted_iota(jnp.int32, (tq, tq), 1)
    allowed = (c <= r) if fox else ((c >> CHUNK_SHIFT) <= (r >> CHUNK_SHIFT))

    def scores(hh, i):
        lo, hi = i * tq, (i + 1) * tq
        if fox:
            qa_ref, ka_ref = ins[3:]
            qm = jnp.where((lane >> 6) == hh, q_ref[0, lo:hi, :].astype(F32), 0.0)
            qa = jnp.where((lane >> 4) == 2 * pair + hh, qa_ref[0, lo:hi, :].astype(F32), 0.0)
            q = jnp.concatenate([qm, qa], axis=1).astype(BF16)
            k = jnp.concatenate([k_ref[0, 0:hi, :], ka_ref[0, 0:hi, :]], axis=1)
        else:
            head = slice(HEAD_PAD * hh, HEAD_PAD * (hh + 1))
            q = q_ref[0, lo:hi, head]
            k = k_ref[0, 0:hi, head]
        s = _dot_nt(q, k)
        sd = jnp.where(allowed, s[:, lo:hi], -jnp.inf)
        return sd if i == 0 else jnp.concatenate([s[:, 0:lo], sd], axis=1)

    def softmax(s):
        p = jnp.exp2(s - jnp.max(s, axis=1, keepdims=True))
        return p.astype(BF16), jnp.sum(p, axis=1, keepdims=True)

    def values(hh, i, p, l):
        lo, hi = i * tq, (i + 1) * tq
        o_scr[hh, lo:hi, :] = _dot(p, v_ref[0, 0:hi, :]) / l

    order = list(reversed(range(t // tq)))
    for g in range(0, len(order), GROUP):
        units = [(hh, i) for i in order[g:g + GROUP] for hh in range(2)]
        ss = [scores(hh, i) for hh, i in units]
        pl_ = [softmax(s) for s in ss]
        for (hh, i), (p, l) in zip(units, pl_):
            values(hh, i, p, l)

    lane_t = lax.broadcasted_iota(jnp.int32, (t, LANE), 1)
    o_ref[0] = jnp.where(lane_t < V_DIM, o_scr[0], o_scr[1]).astype(BF16)


def _attn_prompt(q, k, v, aug, tq):
    fox = aug is not None
    b, t, _ = q.shape
    qw = LANE if fox else 2 * HEAD_PAD
    blk = lambda w, f: pl.BlockSpec((1, t, w), f)
    per_pair = lambda bi, p: (bi, 0, p)
    shared = lambda bi, p: (bi, 0, 0)
    in_specs = [blk(qw, per_pair), blk(qw, per_pair), blk(LANE, per_pair)]
    args = [q, k, v]
    if fox:
        in_specs += [blk(LANE, shared), blk(LANE, shared)]
        args += list(aug)
    return pl.pallas_call(
        functools.partial(_attn_prompt_body, fox=fox, tq=tq),
        grid=(b, N_MLA // 2), in_specs=in_specs, out_specs=blk(LANE, per_pair),
        out_shape=jax.ShapeDtypeStruct((b, t, N_MLA * V_DIM), BF16),
        scratch_shapes=[pltpu.VMEM((2, t, LANE), F32)],
        compiler_params=_params("arbitrary", "arbitrary"),
        name="attn_fox" if fox else "attn_mla",
    )(*args)


def _pad_rows(x, rows):
    return jnp.concatenate([x, jnp.zeros((rows - x.shape[0],) + x.shape[1:], x.dtype)], axis=0)


def _step_softmax(s_c, s_n):
    m = jnp.maximum(jnp.max(s_c, axis=1, keepdims=True), jnp.max(s_n, axis=1, keepdims=True))
    p_c = jnp.exp2(s_c - m)
    p_n = jnp.exp2(s_n - m)
    l = jnp.sum(p_c, axis=1, keepdims=True) + jnp.sum(p_n, axis=1, keepdims=True)
    return p_c.astype(BF16), p_n.astype(BF16), l


def _attn_mla_step_body(q_ref, kc_ref, vc_ref, kn_ref, vn_ref, o_ref, *, start):
    tq = q_ref.shape[1]
    tc = kc_ref.shape[1]
    lane = lax.broadcasted_iota(jnp.int32, (tq, LANE), 1)
    chunk = lambda pos: pos >> CHUNK_SHIFT
    qc = chunk(start + lax.broadcasted_iota(jnp.int32, (tq, 1), 0))
    ok_c = chunk(lax.broadcasted_iota(jnp.int32, (tq, tc), 1)) <= qc
    ok_n = (lane < tq) & (chunk(start + lane) <= qc)
    for p in range(N_MLA // 2):
        pair = slice(LANE * p, LANE * (p + 1))
        vc = vc_ref[0, :, pair]
        vn = _pad_rows(vn_ref[0, :, pair], LANE)
        outs = []
        for h in (2 * p, 2 * p + 1):
            head = slice(HEAD_PAD * h, HEAD_PAD * (h + 1))
            q = q_ref[0, :, head]
            s_c = jnp.where(ok_c, _dot_nt(q, kc_ref[0, :, head]), -jnp.inf)
            s_n = jnp.where(ok_n, _dot_nt(q, _pad_rows(kn_ref[0, :, head], LANE)), -jnp.inf)
            p_c, p_n, l = _step_softmax(s_c, s_n)
            outs.append((_dot(p_c, vc) + _dot(p_n, vn)) / l)
        o_ref[0, :, pair] = jnp.where(lane < V_DIM, outs[0], outs[1]).astype(BF16)


def _attn_mla_step(q, kc, vc, kn, vn, start):
    b, tq, _ = q.shape
    blk = lambda a: pl.BlockSpec((1,) + a.shape[1:], lambda bi: (bi, 0, 0))
    return pl.pallas_call(
        functools.partial(_attn_mla_step_body, start=start), grid=(b,),
        in_specs=[blk(q), blk(kc), blk(vc), blk(kn), blk(vn)],
        out_specs=pl.BlockSpec((1, tq, N_MLA * V_DIM), lambda bi: (bi, 0, 0)),
        out_shape=jax.ShapeDtypeStruct((b, tq, N_MLA * V_DIM), BF16),
        compiler_params=_params("arbitrary"), name="attn_mla_step",
    )(q, kc, vc, kn, vn)


def _lane_cumsum(x, triu):
    rows, blk = x.shape[0], triu.shape[0]
    carry = jnp.zeros((rows, 1), F32)
    out = []
    for b0 in range(0, x.shape[1], blk):
        hi, mid, lo = _split3(_pad_rows(x[:, b0:b0 + blk], 16))
        c = (_dot(hi, triu) + _dot(mid, triu) + _dot(lo, triu))[0:rows] + carry
        carry = c[:, blk - 1:blk]
        out.append(c)
    return jnp.concatenate(out, axis=1), carry


def _attn_fox_step_body(q_ref, kt_ref, vt_ref, kn_ref, vn_ref, lft_ref, lfn_ref,
                        triu_ref, tri_ref, triu_s_ref, o_ref):
    tq = q_ref.shape[1]
    lane = lax.broadcasted_iota(jnp.int32, (tq, LANE), 1)
    causal = lane <= lax.broadcasted_iota(jnp.int32, (tq, LANE), 0)
    cc, total = _lane_cumsum(lft_ref[0, 0], triu_ref[...])
    lfn = _pad_rows(lfn_ref[0], LANE)
    hi, mid, lo = _split3(lfn)
    tri = tri_ref[...]
    cn = (_dot(tri, hi) + _dot(tri, mid) + _dot(tri, lo))[0:tq]
    cnt, _ = _lane_cumsum(lfn.T[0:N_FOX, :], triu_s_ref[...])
    for p in range(N_FOX // 2):
        pair = slice(LANE * p, LANE * (p + 1))
        kt = kt_ref[0, 0, pair, :].astype(BF16)
        vt = vt_ref[0, 0, pair, :].astype(BF16)
        kn = _pad_rows(kn_ref[0, :, pair], LANE)
        vn = _pad_rows(vn_ref[0, :, pair], LANE)
        qf = q_ref[0, :, pair].astype(F32)
        outs = []
        for hh in range(2):
            h = 2 * p + hh
            q = jnp.where((lane >> 6) == hh, qf, 0.0).astype(BF16)
            col = jnp.sum(jnp.where(lane == h, cn, 0.0), axis=1, keepdims=True)
            bias_c = ((total[h:h + 1, :] + col) - cc[h:h + 1, :]) * LOG2E
            bias_n = (col - cnt[h:h + 1, :]) * LOG2E
            s_c = _dot(q, kt) + bias_c
            s_n = jnp.where(causal, _dot_nt(q, kn) + bias_n, -jnp.inf)
            p_c, p_n, l = _step_softmax(s_c, s_n)
            outs.append((_dot_nt(p_c, vt) + _dot(p_n, vn)) / l)
        o_ref[0, :, pair] = jnp.where(lane < V_DIM, outs[0], outs[1]).astype(BF16)


def _attn_fox_step(q, kt_all, vt_all, kn, vn, lft_all, lfn, layer, consts):
    b, tq, _ = q.shape
    new = lambda a: pl.BlockSpec((1,) + a.shape[1:], lambda bi: (bi, 0, 0))
    old = lambda a: pl.BlockSpec((1, 1) + a.shape[2:], lambda bi: (layer, bi, 0, 0))
    return pl.pallas_call(
        _attn_fox_step_body, grid=(b,),
        in_specs=[new(q), old(kt_all), old(vt_all), new(kn), new(vn), old(lft_all), new(lfn),
                  _full((MXU, MXU)), _full((LANE, LANE)), _full((LANE, LANE))],
        out_specs=pl.BlockSpec((1, tq, N_FOX * FOX_DIM), lambda bi: (bi, 0, 0)),
        out_shape=jax.ShapeDtypeStruct((b, tq, N_FOX * FOX_DIM), BF16),
        compiler_params=_params("arbitrary"), name="attn_fox_step",
    )(q, kt_all, vt_all, kn, vn, lft_all, lfn, consts["triu256"], consts["tri128"], consts["triu128"])


def _post_attn_body(x_ref, om_ref, of_ref, wo_ref, gffn_ref, wr_ref, br_ref, tri_ref,
                    x2_ref, xn_ref, route_ref, routet_ref, cnt_ref, carry_ref):
    nm = N_MLA * V_DIM
    x2 = x_ref[...] + _dot(om_ref[...], wo_ref[0:nm, :]) + _dot(of_ref[...], wo_ref[nm:, :])
    x2_ref[...] = x2
    xn = _rms(x2, gffn_ref[...], D_MODEL)
    xn_ref[...] = xn
    blocks = range(0, xn.shape[0], MXU)
    each = lambda f, *cols: [f(*v) for v in zip(*cols)]
    xs = [xn[r0:r0 + MXU] for r0 in blocks]
    xh = each(lambda v: v.astype(BF16), xs)
    xl = each(lambda v, h: (v - h.astype(F32)).astype(BF16), xs, xh)
    wr = wr_ref[...]
    r = each(lambda h, l: _dot(h, wr) + _dot(l, wr), xh, xl)
    bias = br_ref[...]
    lane = lax.broadcasted_iota(jnp.int32, (MXU, LANE), 1).astype(F32)
    valid = lane < N_EXPERTS
    lg = each(lambda v: jnp.where(valid, v + pltpu.roll(v, LANE - N_EXPERTS, 1) + bias, -jnp.inf), r)
    e = each(lambda v: jnp.exp(v - jnp.max(v, axis=1, keepdims=True)), lg)
    probs = each(lambda v: v / jnp.sum(v, axis=1, keepdims=True), e)
    first_at = lambda p, m: jnp.min(jnp.where(p == m, lane, float(LANE)), axis=1, keepdims=True)
    p1 = each(lambda p: jnp.where(valid, p, -1.0), probs)
    m1 = each(lambda p: jnp.max(p, axis=1, keepdims=True), p1)
    i1 = each(first_at, p1, m1)
    p2 = each(lambda p, i: jnp.where(lane == i, -1.0, p), p1, i1)
    m2 = each(lambda p: jnp.max(p, axis=1, keepdims=True), p2)
    i2 = each(first_at, p2, m2)

    @pl.when(pl.program_id(0) == 0)
    def _():
        carry_ref[...] = jnp.zeros(carry_ref.shape, F32)

    sel = each(lambda a, b: jnp.where((lane == a) | (lane == b), 1.0, 0.0), i1, i2)
    tri = tri_ref[...]
    counts = each(lambda v: _dot(tri, v.astype(BF16)), sel)
    total = carry_ref[0:1, :]
    for n, r0 in enumerate(blocks):
        incl = counts[n] + total
        total = incl[MXU - 1:, :]
        excl = incl - sel[n]
        rank1 = jnp.sum(jnp.where(lane == i1[n], excl, 0.0), axis=1, keepdims=True)
        rank2 = jnp.sum(jnp.where(lane == i2[n], excl, 0.0), axis=1, keepdims=True)
        den = m1[n] + m2[n]
        route = jnp.zeros((MXU, LANE), F32)
        for c, col in enumerate((i1[n], i2[n], rank1, rank2, m1[n] / den, m2[n] / den)):
            route = jnp.where(lane == c, col, route)
        route_ref[r0:r0 + MXU, :] = route
        routet_ref[:, r0:r0 + MXU] = route.T[0:8, :]
    carry_ref[...] = jnp.broadcast_to(total, carry_ref.shape)
    cnt_ref[...] = jnp.broadcast_to(total, cnt_ref.shape)


def _post_attn(x, om, of, lw, consts, tm):
    rows = x.shape[0]
    assert tm % MXU == 0
    row = lambda w: pl.BlockSpec((tm, w), lambda i: (i, 0))
    nm = N_MLA * V_DIM
    sds = jax.ShapeDtypeStruct
    return pl.pallas_call(
        _post_attn_body, grid=(rows // tm,),
        in_specs=[row(D_MODEL), row(nm), row(nm), _full((2 * nm, D_MODEL)), _full((1, D_MODEL)),
                  _full((D_MODEL, LANE)), _full((1, LANE)), _full((MXU, MXU))],
        out_specs=(row(D_MODEL), row(D_MODEL), row(LANE), pl.BlockSpec((8, tm), lambda i: (0, i)),
                   _full((8, LANE))),
        out_shape=(sds((rows, D_MODEL), F32), sds((rows, D_MODEL), F32), sds((rows, LANE), F32),
                   sds((8, rows), F32), sds((8, LANE), F32)),
        scratch_shapes=[pltpu.VMEM((8, LANE), F32)],
        compiler_params=_params("arbitrary"), name="post_attn_moe",
    )(x, om, of, lw["w_out"], lw["g_ffn"], lw["w_router"], lw["b_router"], consts["tri256"])


def _swiglu_acc(xb, wg_ref, wu_ref, wd_ref, d_ff, lo=0, hi=None):
    acc = None
    for c in range(lo * MXU, d_ff if hi is None else hi * MXU, MXU):
        g = _dot(xb, wg_ref[:, c:c + MXU])
        u = _dot(xb, wu_ref[:, c:c + MXU])
        h = (g * jax.nn.sigmoid(g) * u).astype(BF16)
        d = _dot(h, wd_ref[c:c + MXU, :])
        acc = d if acc is None else acc + d
    return acc


def _post_ffn_body(x_ref, om_ref, of_ref, wo_ref, gffn_ref, wg_ref, wu_ref, wd_ref, o_ref):
    nm = N_MLA * V_DIM
    x2 = x_ref[...] + _dot(om_ref[...], wo_ref[0:nm, :]) + _dot(of_ref[...], wo_ref[nm:, :])
    xn = _rms(x2, gffn_ref[...], D_MODEL).astype(BF16)
    o_ref[...] = x2 + _swiglu_acc(xn, wg_ref.at[0], wu_ref.at[0], wd_ref.at[0], D_FF)


def _post_ffn(x, om, of, lw, tm):
    rows = x.shape[0]
    row = lambda w: pl.BlockSpec((tm, w), lambda i: (i, 0))
    nm = N_MLA * V_DIM
    jl = lw["mixer_idx"]
    wspec = lambda a, b: pl.BlockSpec((1, a, b), lambda i: (jl, 0, 0))
    return pl.pallas_call(
        _post_ffn_body, grid=(rows // tm,),
        in_specs=[row(D_MODEL), row(nm), row(nm), _full((2 * nm, D_MODEL)), _full((1, D_MODEL)),
                  wspec(D_MODEL, D_FF), wspec(D_MODEL, D_FF), wspec(D_FF, D_MODEL)],
        out_specs=row(D_MODEL), out_shape=jax.ShapeDtypeStruct((rows, D_MODEL), F32),
        compiler_params=_params("arbitrary"), name="post_ffn",
    )(x, om, of, lw["w_out"], lw["g_ffn"], lw["w_gate"], lw["w_up"], lw["w_down"])


SCATTER_UNROLL = 8
SPLIT_CHUNK = 5


def _inv_body(pos_ref, base_hbm, inv_ref, sem, *, tmg):
    fill = pltpu.make_async_copy(base_hbm, inv_ref, sem.at[0])
    fill.start()
    fill.wait()

    def put(i, c):
        for u in range(SCATTER_UNROLL):
            a = i * SCATTER_UNROLL + u
            inv_ref[tmg + pos_ref[a]] = a
        return c

    lax.fori_loop(0, pos_ref.shape[0] // SCATTER_UNROLL, put, 0)


def _route_inverse(pos_flat, base, tmg):
    return pl.pallas_call(
        functools.partial(_inv_body, tmg=tmg),
        in_specs=[pl.BlockSpec(memory_space=pltpu.SMEM), pl.BlockSpec(memory_space=pl.ANY)],
        out_specs=pl.BlockSpec(memory_space=pltpu.SMEM),
        out_shape=jax.ShapeDtypeStruct(base.shape, jnp.int32),
        scratch_shapes=[pltpu.SemaphoreType.DMA((1,))], name="route_inverse",
    )(pos_flat, base)


GATHER, OUT, STAGE = 0, 2, 4


def _moe_routed_body(te_ref, stage_ref, src_ref, dst_ref, xn_hbm, wg_ref, wu_ref, wd_ref, z_hbm,
                     buf, gsem, ssem, *, tmg, n_tiles):
    del te_ref
    j = pl.program_id(0)

    def row_in(entry, slot, r):
        return pltpu.make_async_copy(xn_hbm.at[pl.ds(src_ref[entry], 1), :],
                                     buf.at[GATHER + slot, pl.ds(r, 1), :], gsem.at[slot])

    def row_out(entry, slot, r):
        return pltpu.make_async_copy(buf.at[OUT + slot, pl.ds(r, 1), :],
                                     z_hbm.at[pl.ds(dst_ref[entry], 1), :], ssem.at[0])

    def tile_in(slot):
        return pltpu.make_async_copy(xn_hbm.at[pl.ds(0, tmg), :], buf.at[GATHER + slot], gsem.at[slot])

    def tile_out(slot):
        return pltpu.make_async_copy(buf.at[OUT + slot], z_hbm.at[pl.ds(0, tmg), :], ssem.at[0])

    @pl.when(j == 0)
    def _():
        buf[OUT + 1] = jnp.zeros(buf.shape[1:], F32)
        for r in range(tmg):
            row_in(tmg + r, 0, r).start()

    def step(cur):
        nxt = 1 - cur

        @pl.when(j >= 1)
        def _():
            tile_out(cur).wait()

        tile_in(cur).wait()
        w = (wg_ref.at[0, 0], wu_ref.at[0, 0], wd_ref.at[0, 0])
        head = _swiglu_acc(buf[GATHER + cur].astype(BF16), *w, E_FF, 0, SPLIT_CHUNK)
        for r in range(tmg):
            row_in((j + 2) * tmg + r, nxt, r).start()
            row_out(j * tmg + r, nxt, r).start()
        stage = stage_ref[0]
        buf[stage] = buf[GATHER + cur]
        tail = _swiglu_acc(buf[stage].astype(BF16), *w, E_FF, SPLIT_CHUNK)
        buf[OUT + cur] = head + tail

    for parity in range(2):
        pl.when(j % 2 == parity)(functools.partial(step, parity))

    assert n_tiles % 2 == 0

    @pl.when(j == n_tiles)
    def _():
        tile_out(1).wait()
        tile_in(1).wait()


def _moe_routed(xn, tile_expert, src, dst, lw, tmg, n_tiles):
    n = xn.shape[0]
    jl = lw["mixer_idx"]
    wspec = lambda a, b: pl.BlockSpec((1, 1, a, b), lambda j, te, st, s, d: (jl, te[j], 0, 0))
    grid_spec = pltpu.PrefetchScalarGridSpec(
        num_scalar_prefetch=4, grid=(n_tiles + 1,),
        in_specs=[pl.BlockSpec(memory_space=pl.ANY), wspec(D_MODEL, E_FF), wspec(D_MODEL, E_FF),
                  wspec(E_FF, D_MODEL)],
        out_specs=pl.BlockSpec(memory_space=pl.ANY),
        scratch_shapes=[pltpu.VMEM((STAGE + 1, tmg, D_MODEL), F32),
                        pltpu.SemaphoreType.DMA((2,)), pltpu.SemaphoreType.DMA((1,))])
    return pl.pallas_call(
        functools.partial(_moe_routed_body, tmg=tmg, n_tiles=n_tiles),
        grid_spec=grid_spec, out_shape=jax.ShapeDtypeStruct((2 * n + tmg, D_MODEL), F32),
        compiler_params=_params("arbitrary"), name="moe_routed",
    )(tile_expert, jnp.full((1,), STAGE, jnp.int32), src, dst, xn,
      lw["we_gate"], lw["we_up"], lw["we_down"])


def _combine_body(x2_ref, route_ref, z0_ref, z1_ref, o_ref):
    o_ref[...] = _combine(x2_ref, route_ref, z0_ref, z1_ref)


def _moe_combine(x2, route, z, tm):
    n = x2.shape[0]
    row = lambda w: pl.BlockSpec((tm, w), lambda i: (i, 0))
    return pl.pallas_call(
        _combine_body, grid=(n // tm,),
        in_specs=[row(D_MODEL), row(LANE), row(D_MODEL),
                  pl.BlockSpec((tm, D_MODEL), lambda i: (i + n // tm, 0))],
        out_specs=row(D_MODEL), out_shape=jax.ShapeDtypeStruct((n, D_MODEL), F32),
        compiler_params=_params("arbitrary"), name="moe_combine",
    )(x2, route, z, z)


def _moe(x2, xn, route, route_t, counts, lw, tmg, defer_combine=False):
    n = x2.shape[0]
    assert n & (n - 1) == 0
    n_tiles = 2 * n // tmg + N_EXPERTS
    ext = (n_tiles + 3) * tmg
    e = route_t[0:2].astype(jnp.int32)
    rank = route_t[2:4].astype(jnp.int32)
    cnt = counts[0, :N_EXPERTS].astype(jnp.int32)
    tiles = (cnt + tmg - 1) // tmg
    tile_end = jnp.cumsum(tiles)
    row_start = (tile_end - tiles) * tmg
    start_of = jnp.sum(jnp.where(e[..., None] == jnp.arange(N_EXPERTS), row_start, 0), axis=-1)
    pos_flat = (start_of + rank).reshape(-1)
    steps = jnp.minimum(jnp.arange(n_tiles + 1, dtype=jnp.int32), tile_end[-1] - 1)
    tile_expert = jnp.minimum(jnp.sum((steps[:, None] >= tile_end[None, :]).astype(jnp.int32), axis=1),
                              N_EXPERTS - 1)
    idx = jnp.arange(ext, dtype=jnp.int32)
    dst = _route_inverse(pos_flat, 2 * n + (idx & (tmg - 1)), tmg)
    src = jnp.where(dst < 2 * n, dst, idx) & (n - 1)
    z = _moe_routed(xn, tile_expert, src, dst, lw, tmg, n_tiles)
    return (x2, route, z) if defer_combine else _moe_combine(x2, route, z, min(n, 2 * tmg))


def _consts():
    i = jnp.arange(MXU)
    blockdiag = lambda w: ((i[:, None] // w) == (i[None, :] // w)).astype(BF16)
    tri = lambda n: (jnp.arange(n)[None, :] <= jnp.arange(n)[:, None]).astype(BF16)
    r = jnp.arange(LANE)
    place = jnp.stack([((r[:, None] < N_FOX) & (r[None, :] == AUG_W * r[:, None] + k)).astype(BF16)
                       for k in range(6)])
    within = r % AUG_W
    headed = r < AUG_W * N_FOX
    ones_q = (headed & (within >= 3) & (within < 6)).astype(F32)
    ones_k = (headed & (within < 3)).astype(F32)
    aug_ones = jnp.zeros((8, LANE), F32).at[0].set(ones_q).at[1].set(ones_k)
    return {"s64": blockdiag(FOX_DIM), "s128": blockdiag(HEAD_PAD), "tri256": tri(MXU),
            "tri128": tri(LANE), "triu256": tri(MXU).T, "triu128": tri(LANE).T,
            "place": place, "aug_ones": aug_ones}


def _rope_tables(pos):
    half = ROPE_DIM // 2
    inv = ROPE_BASE ** (-jnp.arange(half, dtype=F32) / half)
    ang = pos.astype(F32)[:, None] * inv[None, :]
    cos, sin = jnp.cos(ang), jnp.sin(ang)
    t = pos.shape[0]
    ct = jnp.concatenate([jnp.ones((t, ROPE_LO), F32), cos, cos, jnp.zeros((t, LANE - ROPE_HI), F32)], 1)
    st = jnp.concatenate([jnp.zeros((t, ROPE_LO), F32), -sin, sin, jnp.zeros((t, LANE - ROPE_HI), F32)], 1)
    return ct, st


def _pad_lanes(v, width=LANE):
    return jnp.pad(v, [(0, 0)] * (v.ndim - 1) + [(0, width - v.shape[-1])])


def _prep_layer(i, p):
    w_in = p["w_in"][i]
    s = [0, Q_LORA, Q_LORA + KV_LORA, Q_LORA + KV_LORA + ROPE_DIM]
    nfx = N_FOX * FOX_DIM
    s += [s[3] + nfx, s[3] + 2 * nfx, s[3] + 3 * nfx, s[3] + 3 * nfx + N_FOX]
    c_q, c_kv, k_rope = w_in[:, s[0]:s[1]], w_in[:, s[1]:s[2]], w_in[:, s[2]:s[3]]
    fq, fk, fv, f_logit = w_in[:, s[3]:s[4]], w_in[:, s[4]:s[5]], w_in[:, s[5]:s[6]], w_in[:, s[6]:s[7]]
    zc = lambda n: jnp.zeros((D_MODEL, n), F32)
    misc = jnp.concatenate([f_logit, zc(ROPE_LO - N_FOX), k_rope, zc(LANE - ROPE_HI)], axis=1)
    w_in_p = jnp.concatenate([c_q, c_kv, fq, fk, fv, misc], axis=1).astype(BF16)

    w_uq = p["w_uq"][i].reshape(Q_LORA, N_MLA, QK_DIM)
    w_uq_p = _pad_lanes(w_uq, HEAD_PAD).reshape(Q_LORA, -1).astype(BF16)

    w_ukv = p["w_ukv"][i].reshape(KV_LORA, N_MLA, NOPE_DIM + V_DIM)
    w_k = _pad_lanes(w_ukv[..., :NOPE_DIM], HEAD_PAD).reshape(KV_LORA, -1).astype(BF16)
    w_v = w_ukv[..., NOPE_DIM:].reshape(KV_LORA, -1).astype(BF16)

    gq = _pad_lanes(p["g_qn_mla"][i] * (QK_DIM ** -0.5 * LOG2E))[None]
    gk = _pad_lanes(p["g_kn_mla"][i])[None]
    lw = {
        "g_mix": p["g_mix"][i][None], "w_in": w_in_p, "b_f": _pad_lanes(p["b_f"][i])[None],
        "g_cq": p["g_cq"][i][None], "g_ckv": p["g_ckv"][i][None], "w_uq": w_uq_p,
        "gq": gq, "gk": gk,
        "gfq": jnp.tile(p["g_qn_fox"][i] * (FOX_DIM ** -0.5 * LOG2E), N_FOX)[None],
        "gfk": jnp.tile(p["g_kn_fox"][i], N_FOX)[None],
        "w_k": w_k, "w_v": w_v,
        "w_out": p["w_out"][i].astype(BF16), "g_ffn": p["g_ffn"][i][None],
    }
    j = i // 2
    lw["mixer_idx"] = j
    if i % 2 == 0:
        lw.update(w_gate=p["w_gate_b"], w_up=p["w_up_b"], w_down=p["w_down_b"])
    else:
        wr = p["w_router"][j]
        wr_hi = wr.astype(BF16)
        wr_lo = (wr - wr_hi.astype(F32)).astype(BF16)
        lw.update(w_router=_pad_lanes(jnp.concatenate([wr_hi, wr_lo], axis=1)),
                  b_router=_pad_lanes(p["b_router"][j])[None],
                  we_gate=p["we_gate_b"], we_up=p["we_up_b"], we_down=p["we_down_b"])
    return lw


def _channel_mixer(i, x, om, of, lw, consts, tm_ffn, defer_combine=False):
    if i % 2 == 0:
        return _post_ffn(x, om, of, lw, tm_ffn)
    x2, xn, route, route_t, counts = _post_attn(x, om, of, lw, consts, min(tm_ffn, x.shape[0]))
    return _moe(x2, xn, route, route_t, counts, lw, MXU, defer_combine)


def _trunk_prompt(x, layers, consts):
    b, t, _ = x.shape
    rows = b * t
    x = x.reshape(rows, D_MODEL)
    ct, st = _rope_tables(jnp.arange(t, dtype=jnp.int32))
    nfx = N_FOX * FOX_DIM
    stack = ()
    pending = None
    for i, lw in enumerate(layers):
        outs = _mixer_in_prompt(x, lw, ct, st, consts, stack, i, b, combine=pending)
        stack = outs[:N_STACKED]
        q, fq, fkb, fvb, k, v, qaug, kaug = outs[N_STACKED:N_STACKED + 8]
        if pending is not None:
            x = outs[-1]
        om = _attn_prompt(q.reshape(b, t, -1), k.reshape(b, t, -1), v.reshape(b, t, -1), None, 256)
        of = _attn_prompt(fq.reshape(b, t, nfx), fkb.reshape(b, t, nfx), fvb.reshape(b, t, nfx),
                          (qaug.reshape(b, t, LANE), kaug.reshape(b, t, LANE)), 256)
        defer = i % 2 == 1 and i + 1 < len(layers)
        res = _channel_mixer(i, x, om.reshape(rows, -1), of.reshape(rows, -1), lw, consts, 512, defer)
        x, pending = (None, res) if defer else (res, None)
    ckv_s, krt_s, fkt_s, fvt_s, lft_s = stack
    heads = lambda a: jnp.transpose(a.reshape(DEPTH, b, N_FOX, FOX_DIM, t), (0, 1, 4, 2, 3))
    state = (ckv_s.reshape(DEPTH, b, t, KV_LORA), jnp.swapaxes(krt_s, 2, 3), heads(fkt_s), heads(fvt_s),
             jnp.swapaxes(lft_s, 2, 3))
    return x.reshape(b, t, D_MODEL), state


def _trunk_sample(x, past, layers, consts):
    b, t, _ = x.shape
    rows = b * t
    start = PAST_LEN
    x = x.reshape(rows, D_MODEL)
    ct, st = _rope_tables(start + jnp.arange(t, dtype=jnp.int32))
    ct, st = jnp.tile(ct, (b, 1)), jnp.tile(st, (b, 1))
    p_ckv, p_krope, p_fk, p_fv, p_logf = past
    nfx = N_FOX * FOX_DIM
    krope_t = jnp.swapaxes(p_krope, 2, 3)
    fk_t = jnp.transpose(p_fk, (0, 1, 3, 4, 2)).reshape(DEPTH, b, nfx, start)
    fv_t = jnp.transpose(p_fv, (0, 1, 3, 4, 2)).reshape(DEPTH, b, nfx, start)
    lf_t = jnp.swapaxes(p_logf, 2, 3)
    state = ([], [], [], [], [])
    for i, lw in enumerate(layers):
        ckv, kr, kr128, fk, fv, lf128, lf, q, fq, fkb, fvb = _mixer_in(x, lw, ct, st, consts, rows)
        k_c, v_c = _kv_expand_cache(p_ckv, krope_t, i, lw, consts, 1024)
        k_n, v_n = _kv_expand(ckv, kr128, lw, consts, rows)
        om = _attn_mla_step(q.reshape(b, t, -1), k_c.reshape(b, start, -1), v_c.reshape(b, start, -1),
                            k_n.reshape(b, t, -1), v_n.reshape(b, t, -1), start)
        of = _attn_fox_step(fq.reshape(b, t, nfx), fk_t, fv_t, fkb.reshape(b, t, nfx),
                            fvb.reshape(b, t, nfx), lf_t, lf128.reshape(b, t, LANE), i, consts)
        x = _channel_mixer(i, x, om.reshape(rows, -1), of.reshape(rows, -1), lw, consts, rows)
        for lst, s_ in zip(state, (ckv.reshape(b, t, KV_LORA), kr.reshape(b, t, ROPE_DIM),
                                   fk.reshape(b, t, N_FOX, FOX_DIM), fv.reshape(b, t, N_FOX, FOX_DIM),
                                   lf.reshape(b, t, N_FOX))):
            lst.append(s_)
    return x.reshape(b, t, D_MODEL), tuple(jnp.stack(s_) for s_ in state)


def kernel(x_prompt, x_sample, cache_mla_ckv, cache_mla_krope, cache_fox_k, cache_fox_v, cache_fox_logf, g_mix, w_in, b_f, g_cq, g_ckv, w_uq, w_ukv, g_qn_mla, g_kn_mla, g_qn_fox, g_kn_fox, w_out, g_ffn, w_gate, w_up, w_down, w_router, b_router, we_gate, we_up, we_down):
    p = dict(g_mix=g_mix, w_in=w_in, b_f=b_f, g_cq=g_cq, g_ckv=g_ckv, w_uq=w_uq, w_ukv=w_ukv,
             g_qn_mla=g_qn_mla, g_kn_mla=g_kn_mla, g_qn_fox=g_qn_fox, g_kn_fox=g_kn_fox,
             w_out=w_out, g_ffn=g_ffn, w_gate=w_gate, w_up=w_up, w_down=w_down,
             w_router=w_router, b_router=b_router, we_gate=we_gate, we_up=we_up, we_down=we_down)
    for name in ("w_gate", "w_up", "w_down", "we_gate", "we_up", "we_down"):
        p[name + "_b"] = p[name].astype(BF16)
    layers = [_prep_layer(i, p) for i in range(DEPTH)]
    consts = _consts()
    y_p, st_p = _trunk_prompt(x_prompt, layers, consts)
    past = (cache_mla_ckv, cache_mla_krope, cache_fox_k, cache_fox_v, cache_fox_logf)
    y_s, st_s = _trunk_sample(x_sample, past, layers, consts)
    return (y_p, y_s) + st_p + st_s
```

```python
import functools

import jax
import jax.numpy as jnp
from jax import lax
from jax.experimental import pallas as pl
from jax.experimental.pallas import tpu as pltpu

F32 = jnp.float32
BF16 = jnp.bfloat16

D_MODEL = 1024
DEPTH = 4
PAST_LEN = 2048
CHUNK_SHIFT = 6
EPS = 1e-6
ROPE_BASE = 10000.0

N_MLA = 8
Q_LORA = 384
KV_LORA = 256
NOPE_DIM = 64
ROPE_DIM = 32
QK_DIM = NOPE_DIM + ROPE_DIM
V_DIM = 64
N_FOX = 8
FOX_DIM = 64
D_FF = 2816
N_EXPERTS = 8
E_FF = 1792

LOG2E = 1.4426950408889634
LANE = 128
MXU = 256
HEAD_PAD = 128
ROPE_LO = NOPE_DIM
ROPE_HI = NOPE_DIM + ROPE_DIM
AUG_W = 16
GROUP = 2

CQ_LO, CQ_HI = 0, Q_LORA
CKV_LO, CKV_HI = CQ_HI, CQ_HI + KV_LORA
FQ_LO, FQ_HI = CKV_HI, CKV_HI + N_FOX * FOX_DIM
FK_LO, FK_HI = FQ_HI, FQ_HI + N_FOX * FOX_DIM
FV_LO, FV_HI = FK_HI, FK_HI + N_FOX * FOX_DIM
MISC_LO, MISC_HI = FV_HI, FV_HI + LANE
D_IN_P = MISC_HI

VMEM_LIMIT = 56 * 1024 * 1024


def _params(*sem):
    return pltpu.CompilerParams(dimension_semantics=sem, vmem_limit_bytes=VMEM_LIMIT)


def _dot(a, b):
    return jnp.dot(a, b, preferred_element_type=F32)


def _dot_nt(a, b):
    return lax.dot_general(a, b, (((1,), (1,)), ((), ())), preferred_element_type=F32)


def _rms(x, g, n):
    return x * lax.rsqrt(jnp.sum(x * x, axis=-1, keepdims=True) * (1.0 / n) + EPS) * g


def _head_norm(x, smat, dim):
    outs = []
    for c in range(0, x.shape[1], MXU):
        xc = x[:, c:c + MXU]
        ss = _dot((xc * xc).astype(BF16), smat)
        outs.append(xc * lax.rsqrt(ss * (1.0 / dim) + EPS))
    return jnp.concatenate(outs, axis=1)


def _full(shape):
    return pl.BlockSpec(shape, lambda *_: (0,) * len(shape))


def _split3(c):
    hi = c.astype(BF16)
    r1 = c - hi.astype(F32)
    mid = r1.astype(BF16)
    lo = (r1 - mid.astype(F32)).astype(BF16)
    return hi, mid, lo


def _combine(x2_ref, route_ref, z0_ref, z1_ref):
    route = route_ref[...]
    lane = lax.broadcasted_iota(jnp.int32, route.shape, 1)
    g1 = jnp.sum(jnp.where(lane == 4, route, 0.0), axis=1, keepdims=True)
    g2 = jnp.sum(jnp.where(lane == 5, route, 0.0), axis=1, keepdims=True)
    return x2_ref[...] + g1 * z0_ref[...] + g2 * z1_ref[...]


N_MIXER_W = 13


def _mixer_core(x, gmix_ref, win_ref, bf_ref, gcq_ref, gckv_ref, wuq_ref, gq_ref,
                gfq_ref, gfk_ref, ct_ref, st_ref, s64_ref, s128_ref):
    tm = x.shape[0]
    xb = _rms(x, gmix_ref[...], D_MODEL).astype(BF16)

    def proj(lo, hi):
        return _dot(xb, win_ref[:, lo:hi])

    ct = ct_ref[...]
    st = st_ref[...]
    lane = lax.broadcasted_iota(jnp.int32, (tm, LANE), 1)

    cq_raw = proj(CQ_LO, CQ_HI)
    a = proj(MISC_LO, MISC_HI)
    ckv_raw = proj(CKV_LO, CKV_HI)
    fk_raw = proj(FK_LO, FK_HI)
    fv = proj(FV_LO, FV_HI)
    fq_raw = proj(FQ_LO, FQ_HI)

    cq = _rms(cq_raw, gcq_ref[...], Q_LORA).astype(BF16)
    qa = _dot(cq, wuq_ref[...])
    half = ROPE_DIM // 2

    def rotary(blk):
        sw = jnp.where(lane < ROPE_LO + half, pltpu.roll(blk, LANE - half, 1), pltpu.roll(blk, half, 1))
        return blk * ct + sw * st

    z = a + bf_ref[...]
    lf = jnp.minimum(z, 0.0) - jnp.log1p(jnp.exp(-jnp.abs(z)))
    lf128 = jnp.where(lane < N_FOX, lf, 0.0)
    kr128 = jnp.where((lane >= ROPE_LO) & (lane < ROPE_HI), rotary(a), 0.0)
    ckv = _rms(ckv_raw, gckv_ref[...], KV_LORA)

    qr = jnp.concatenate(
        [rotary(qa[:, h * HEAD_PAD:(h + 1) * HEAD_PAD]) for h in range(N_MLA)], axis=1)
    qn = _head_norm(qr, s128_ref[...], QK_DIM)
    gq = gq_ref[...]
    q = jnp.concatenate(
        [qn[:, h * HEAD_PAD:(h + 1) * HEAD_PAD] * gq for h in range(N_MLA)], axis=1).astype(BF16)

    s64 = s64_ref[...]
    fk = _head_norm(fk_raw, s64, FOX_DIM) * gfk_ref[...]
    fq = (_head_norm(fq_raw, s64, FOX_DIM) * gfq_ref[...]).astype(BF16)
    return ckv, kr128, lf128, fk, fv, fq, q


def _mixer_in_body(x_ref, *refs):
    (ckv_ref, kr_ref, kr128_ref, fk_ref, fv_ref, lf128_ref, lf_ref,
     q_ref, fq_ref, fkb_ref, fvb_ref) = refs[N_MIXER_W:]
    ckv, kr128, lf128, fk, fv, fq, q = _mixer_core(x_ref[...], *refs[:N_MIXER_W])
    ckv_ref[...] = ckv
    kr128_ref[...] = kr128
    kr_ref[...] = kr128[:, ROPE_LO:ROPE_HI]
    lf128_ref[...] = lf128
    lf_ref[...] = lf128[:, 0:N_FOX]
    fk_ref[...] = fk
    fv_ref[...] = fv
    fkb_ref[...] = fk.astype(BF16)
    fvb_ref[...] = fv.astype(BF16)
    fq_ref[...] = fq
    q_ref[...] = q


def _fox_aug(c, place_ref, ones_ref):
    hi, mid, lo = _split3(c * LOG2E)
    qa = _dot(hi, place_ref[0]) + _dot(mid, place_ref[1]) + _dot(lo, place_ref[2]) + ones_ref[0:1, :]
    ka = ones_ref[1:2, :] - (_dot(hi, place_ref[3]) + _dot(mid, place_ref[4]) + _dot(lo, place_ref[5]))
    return qa.astype(BF16), ka.astype(BF16)


N_STACKED = 5


def _mixer_prompt_body(*refs, first, combine, n_alias, tiles_per_seq):
    n_x = 4 if combine else 1
    n_in = n_x + N_MIXER_W + 6 + n_alias
    weights = refs[n_x:n_x + N_MIXER_W]
    wk_ref, wv_ref, gk_ref, tri_ref, place_ref, ones_ref = refs[n_x + N_MIXER_W:n_x + N_MIXER_W + 6]
    outs = list(refs[n_in:])
    carry_ref = outs.pop()
    (ckv_ref, kr_ref, fk_ref, fv_ref, lf_ref, q_ref, fq_ref, fkb_ref, fvb_ref,
     k_ref, v_ref, qa_ref, ka_ref) = outs[:13]
    if combine:
        x = _combine(*refs[:4])
        outs[13][...] = x
    else:
        x = refs[0][...]
    ckv, kr128, lf128, fk, fv, fq, q = _mixer_core(x, *weights)
    q_ref[...] = q
    fq_ref[...] = fq
    fkb_ref[...] = fk.astype(BF16)
    fvb_ref[...] = fv.astype(BF16)
    ckv_ref[0] = ckv
    kr_ref[0, 0] = kr128.T[ROPE_LO:ROPE_HI, :]
    lf_ref[0, 0] = lf128.T[0:N_FOX, :]
    fk_ref[0, 0] = fk.T
    fv_ref[0, 0] = fv.T
    if first:
        for ref in (ckv_ref, kr_ref, lf_ref, fk_ref, fv_ref):
            ref[1:] = jnp.zeros((ref.shape[0] - 1,) + ref.shape[1:], F32)

    _kv_emit(ckv, kr128, wk_ref, wv_ref, gk_ref, weights[-1], k_ref, v_ref)

    @pl.when(pl.program_id(0) % tiles_per_seq == 0)
    def _():
        carry_ref[...] = jnp.zeros(carry_ref.shape, F32)

    tri = tri_ref[...]
    carry = carry_ref[0:1, :]
    for r0 in range(0, lf128.shape[0], MXU):
        hi, mid, lo = _split3(lf128[r0:r0 + MXU, :])
        c = _dot(tri, hi) + _dot(tri, mid) + _dot(tri, lo) + carry
        carry = c[MXU - 1:, :]
        qa, ka = _fox_aug(c, place_ref, ones_ref)
        qa_ref[r0:r0 + MXU, :] = qa
        ka_ref[r0:r0 + MXU, :] = ka
    carry_ref[...] = jnp.broadcast_to(carry, carry_ref.shape)


def _mixer_specs(tm, nt):
    nfx = N_FOX * FOX_DIM
    tab = pl.BlockSpec((tm, LANE), lambda i: (i % nt, 0))
    return [_full((1, D_MODEL)), _full((D_MODEL, D_IN_P)), _full((1, LANE)),
            _full((1, Q_LORA)), _full((1, KV_LORA)), _full((Q_LORA, N_MLA * HEAD_PAD)),
            _full((1, LANE)), _full((1, nfx)), _full((1, nfx)), tab, tab,
            _full((MXU, MXU)), _full((MXU, MXU))]


def _mixer_args(lw, ct, st, consts):
    return [lw["g_mix"], lw["w_in"], lw["b_f"], lw["g_cq"], lw["g_ckv"], lw["w_uq"], lw["gq"],
            lw["gfq"], lw["gfk"], ct, st, consts["s64"], consts["s128"]]


def _mixer_in(x, lw, ct, st, consts, tm):
    rows = x.shape[0]
    nt = ct.shape[0] // tm
    row = lambda w: pl.BlockSpec((tm, w), lambda i: (i, 0))
    nfx = N_FOX * FOX_DIM
    sds = jax.ShapeDtypeStruct
    outs = [(sds((rows, KV_LORA), F32), row(KV_LORA)), (sds((rows, ROPE_DIM), F32), row(ROPE_DIM)),
            (sds((rows, LANE), F32), row(LANE)), (sds((rows, nfx), F32), row(nfx)),
            (sds((rows, nfx), F32), row(nfx)), (sds((rows, LANE), F32), row(LANE)),
            (sds((rows, N_FOX), F32), row(N_FOX)),
            (sds((rows, N_MLA * HEAD_PAD), BF16), row(N_MLA * HEAD_PAD)),
            (sds((rows, nfx), BF16), row(nfx)), (sds((rows, nfx), BF16), row(nfx)),
            (sds((rows, nfx), BF16), row(nfx))]
    args = _mixer_args(lw, ct, st, consts)
    assert len(args) == N_MIXER_W
    return pl.pallas_call(
        _mixer_in_body, grid=(rows // tm,), in_specs=[row(D_MODEL)] + _mixer_specs(tm, nt),
        out_specs=tuple(o[1] for o in outs), out_shape=tuple(o[0] for o in outs),
        compiler_params=_params("arbitrary"), name="mixer_in",
    )(x, *args)


def _mixer_in_prompt(x, lw, ct, st, consts, stack, layer, batch, combine=None):
    tm = 2 * MXU
    rows = (x if combine is None else combine[0]).shape[0]
    t = rows // batch
    nt = t // tm
    first = not stack
    assert first == (layer == 0)
    row = lambda w: pl.BlockSpec((tm, w), lambda i: (i, 0))
    nfx = N_FOX * FOX_DIM
    nk = N_MLA * HEAD_PAD
    nv = N_MLA * V_DIM
    sds = jax.ShapeDtypeStruct
    nd = DEPTH if first else 1
    tmaj = lambda w: pl.BlockSpec((nd, 1, w, tm), lambda i: (layer, i // nt, 0, i % nt))
    outs = [(sds((DEPTH, rows, KV_LORA), F32), pl.BlockSpec((nd, tm, KV_LORA), lambda i: (layer, i, 0))),
            (sds((DEPTH, batch, ROPE_DIM, t), F32), tmaj(ROPE_DIM)),
            (sds((DEPTH, batch, nfx, t), F32), tmaj(nfx)),
            (sds((DEPTH, batch, nfx, t), F32), tmaj(nfx)),
            (sds((DEPTH, batch, N_FOX, t), F32), tmaj(N_FOX)),
            (sds((rows, nk), BF16), row(nk)), (sds((rows, nfx), BF16), row(nfx)),
            (sds((rows, nfx), BF16), row(nfx)), (sds((rows, nfx), BF16), row(nfx)),
            (sds((rows, nk), BF16), row(nk)), (sds((rows, nv), BF16), row(nv)),
            (sds((rows, LANE), BF16), row(LANE)), (sds((rows, LANE), BF16), row(LANE))]
    if combine is None:
        in_specs, args = [row(D_MODEL)], [x]
    else:
        x2, route, z = combine
        in_specs = [row(D_MODEL), row(LANE), row(D_MODEL),
                    pl.BlockSpec((tm, D_MODEL), lambda i: (i + rows // tm, 0))]
        args = [x2, route, z, z]
        outs.append((sds((rows, D_MODEL), F32), row(D_MODEL)))
    in_specs += _mixer_specs(tm, nt) + [_full((KV_LORA, nk)), _full((KV_LORA, nv)), _full((1, LANE)),
                                        _full((MXU, MXU)), _full((6, LANE, LANE)), _full((8, LANE))]
    args += _mixer_args(lw, ct, st, consts) + [lw["w_k"], lw["w_v"], lw["gk"], consts["tri256"],
                                                consts["place"], consts["aug_ones"]]
    aliases = {}
    if stack:
        aliases = {len(args) + n: n for n in range(N_STACKED)}
        in_specs += [pl.BlockSpec(memory_space=pl.ANY)] * N_STACKED
        args += list(stack)
    return pl.pallas_call(
        functools.partial(_mixer_prompt_body, first=first, combine=combine is not None,
                          n_alias=len(stack), tiles_per_seq=nt),
        grid=(rows // tm,), in_specs=in_specs,
        out_specs=tuple(o[1] for o in outs), out_shape=tuple(o[0] for o in outs),
        scratch_shapes=[pltpu.VMEM((8, LANE), F32)],
        input_output_aliases=aliases, compiler_params=_params("arbitrary"), name="mixer_in_prompt",
    )(*args)


def _kv_emit(ckv, kr, wk_ref, wv_ref, gk_ref, s128_ref, k_ref, v_ref):
    cb = ckv.astype(BF16)
    kn = _dot(cb, wk_ref[...])
    kk = jnp.concatenate(
        [kn[:, h * HEAD_PAD:(h + 1) * HEAD_PAD] + kr for h in range(N_MLA)], axis=1)
    kk = _head_norm(kk, s128_ref[...], QK_DIM)
    gk = gk_ref[...]
    k_ref[...] = jnp.concatenate(
        [kk[:, h * HEAD_PAD:(h + 1) * HEAD_PAD] * gk for h in range(N_MLA)], axis=1).astype(BF16)
    v_ref[...] = _dot(cb, wv_ref[...]).astype(BF16)


def _kv_body(ckv_ref, kr128_ref, *rest):
    _kv_emit(ckv_ref[...], kr128_ref[...], *rest)


def _kv_cache_body(ckv_ref, krt_ref, *rest):
    tm = krt_ref.shape[-1]
    pad = lambda n: jnp.zeros((n, tm), F32)
    kr128 = jnp.concatenate([pad(ROPE_LO), krt_ref[0, 0], pad(LANE - ROPE_HI)], axis=0).T
    _kv_emit(ckv_ref[0, 0], kr128, *rest)


def _kv_expand_cache(ckv_all, krope_t, layer, lw, consts, tm):
    _, b, t, _ = ckv_all.shape
    nt = t // tm
    row = lambda w: pl.BlockSpec((tm, w), lambda bi, ti: (bi * nt + ti, 0))
    const = lambda shape: pl.BlockSpec(shape, lambda bi, ti: (0,) * len(shape))
    nk = N_MLA * HEAD_PAD
    nv = N_MLA * V_DIM
    return pl.pallas_call(
        _kv_cache_body, grid=(b, nt),
        in_specs=[pl.BlockSpec((1, 1, tm, KV_LORA), lambda bi, ti: (layer, bi, ti, 0)),
                  pl.BlockSpec((1, 1, ROPE_DIM, tm), lambda bi, ti: (layer, bi, 0, ti)),
                  const((KV_LORA, nk)), const((KV_LORA, nv)), const((1, LANE)), const((MXU, MXU))],
        out_specs=(row(nk), row(nv)),
        out_shape=(jax.ShapeDtypeStruct((b * t, nk), BF16), jax.ShapeDtypeStruct((b * t, nv), BF16)),
        compiler_params=_params("arbitrary", "arbitrary"), name="kv_expand_cache",
    )(ckv_all, krope_t, lw["w_k"], lw["w_v"], lw["gk"], consts["s128"])


def _kv_expand(ckv, kr128, lw, consts, tm):
    rows = kr128.shape[0]
    row = lambda w: pl.BlockSpec((tm, w), lambda i: (i, 0))
    nk = N_MLA * HEAD_PAD
    nv = N_MLA * V_DIM
    return pl.pallas_call(
        _kv_body, grid=(rows // tm,),
        in_specs=[row(KV_LORA), row(LANE), _full((KV_LORA, nk)), _full((KV_LORA, nv)),
                  _full((1, LANE)), _full((MXU, MXU))],
        out_specs=(row(nk), row(nv)),
        out_shape=(jax.ShapeDtypeStruct((rows, nk), BF16), jax.ShapeDtypeStruct((rows, nv), BF16)),
        compiler_params=_params("arbitrary"), name="kv_expand",
    )(ckv, kr128, lw["w_k"], lw["w_v"], lw["gk"], consts["s128"])


def _attn_prompt_body(*refs, fox, tq):
    n_in = 5 if fox else 3
    ins, (o_ref, o_scr) = refs[:n_in], refs[n_in:]
    q_ref, k_ref, v_ref = ins[:3]
    pair = pl.program_id(1)
    t = k_ref.shape[1]
    lane = lax.broadcasted_iota(jnp.int32, (tq, LANE), 1)
    r = lax.broadcasted_iota(jnp.int32, (tq, tq), 0)
    c = lax.broadcasted_iota(jnp.int32, (tq, tq), 1)
    allowed = (c <= r) if fox else ((c >> CHUNK_SHIFT) <= (r >> CHUNK_SHIFT))

    def scores(hh, i):
        lo, hi = i * tq, (i + 1) * tq
        if fox:
            qa_ref, ka_ref = ins[3:]
            qm = jnp.where((lane >> 6) == hh, q_ref[0, lo:hi, :].astype(F32), 0.0)
            qa = jnp.where((lane >> 4) == 2 * pair + hh, qa_ref[0, lo:hi, :].astype(F32), 0.0)
            q = jnp.concatenate([qm, qa], axis=1).astype(BF16)
            k = jnp.concatenate([k_ref[0, 0:hi, :], ka_ref[0, 0:hi, :]], axis=1)
        else:
            head = slice(HEAD_PAD * hh, HEAD_PAD * (hh + 1))
            q = q_ref[0, lo:hi, head]
            k = k_ref[0, 0:hi, head]
        s = _dot_nt(q, k)
        sd = jnp.where(allowed, s[:, lo:hi], -jnp.inf)
        return sd if i == 0 else jnp.concatenate([s[:, 0:lo], sd], axis=1)

    def softmax(s):
        p = jnp.exp2(s - jnp.max(s, axis=1, keepdims=True))
        return p.astype(BF16), jnp.sum(p, axis=1, keepdims=True)

    def values(hh, i, p, l):
        lo, hi = i * tq, (i + 1) * tq
        o_scr[hh, lo:hi, :] = _dot(p, v_ref[0, 0:hi, :]) / l

    order = list(reversed(range(t // tq)))
    for g in range(0, len(order), GROUP):
        units = [(hh, i) for i in order[g:g + GROUP] for hh in range(2)]
        ss = [scores(hh, i) for hh, i in units]
        pl_ = [softmax(s) for s in ss]
        for (hh, i), (p, l) in zip(units, pl_):
            values(hh, i, p, l)

    lane_t = lax.broadcasted_iota(jnp.int32, (t, LANE), 1)
    o_ref[0] = jnp.where(lane_t < V_DIM, o_scr[0], o_scr[1]).astype(BF16)


def _attn_prompt(q, k, v, aug, tq):
    fox = aug is not None
    b, t, _ = q.shape
    qw = LANE if fox else 2 * HEAD_PAD
    blk = lambda w, f: pl.BlockSpec((1, t, w), f)
    per_pair = lambda bi, p: (bi, 0, p)
    shared = lambda bi, p: (bi, 0, 0)
    in_specs = [blk(qw, per_pair), blk(qw, per_pair), blk(LANE, per_pair)]
    args = [q, k, v]
    if fox:
        in_specs += [blk(LANE, shared), blk(LANE, shared)]
        args += list(aug)
    return pl.pallas_call(
        functools.partial(_attn_prompt_body, fox=fox, tq=tq),
        grid=(b, N_MLA // 2), in_specs=in_specs, out_specs=blk(LANE, per_pair),
        out_shape=jax.ShapeDtypeStruct((b, t, N_MLA * V_DIM), BF16),
        scratch_shapes=[pltpu.VMEM((2, t, LANE), F32)],
        compiler_params=_params("arbitrary", "arbitrary"),
        name="attn_fox" if fox else "attn_mla",
    )(*args)


def _pad_rows(x, rows):
    return jnp.concatenate([x, jnp.zeros((rows - x.shape[0],) + x.shape[1:], x.dtype)], axis=0)


def _step_softmax(s_c, s_n):
    m = jnp.maximum(jnp.max(s_c, axis=1, keepdims=True), jnp.max(s_n, axis=1, keepdims=True))
    p_c = jnp.exp2(s_c - m)
    p_n = jnp.exp2(s_n - m)
    l = jnp.sum(p_c, axis=1, keepdims=True) + jnp.sum(p_n, axis=1, keepdims=True)
    return p_c.astype(BF16), p_n.astype(BF16), l


def _attn_mla_step_body(q_ref, kc_ref, vc_ref, kn_ref, vn_ref, o_ref, *, start):
    tq = q_ref.shape[1]
    tc = kc_ref.shape[1]
    lane = lax.broadcasted_iota(jnp.int32, (tq, LANE), 1)
    chunk = lambda pos: pos >> CHUNK_SHIFT
    qc = chunk(start + lax.broadcasted_iota(jnp.int32, (tq, 1), 0))
    ok_c = chunk(lax.broadcasted_iota(jnp.int32, (tq, tc), 1)) <= qc
    ok_n = (lane < tq) & (chunk(start + lane) <= qc)
    for p in range(N_MLA // 2):
        pair = slice(LANE * p, LANE * (p + 1))
        vc = vc_ref[0, :, pair]
        vn = _pad_rows(vn_ref[0, :, pair], LANE)
        outs = []
        for h in (2 * p, 2 * p + 1):
            head = slice(HEAD_PAD * h, HEAD_PAD * (h + 1))
            q = q_ref[0, :, head]
            s_c = jnp.where(ok_c, _dot_nt(q, kc_ref[0, :, head]), -jnp.inf)
            s_n = jnp.where(ok_n, _dot_nt(q, _pad_rows(kn_ref[0, :, head], LANE)), -jnp.inf)
            p_c, p_n, l = _step_softmax(s_c, s_n)
            outs.append((_dot(p_c, vc) + _dot(p_n, vn)) / l)
        o_ref[0, :, pair] = jnp.where(lane < V_DIM, outs[0], outs[1]).astype(BF16)


def _attn_mla_step(q, kc, vc, kn, vn, start):
    b, tq, _ = q.shape
    blk = lambda a: pl.BlockSpec((1,) + a.shape[1:], lambda bi: (bi, 0, 0))
    return pl.pallas_call(
        functools.partial(_attn_mla_step_body, start=start), grid=(b,),
        in_specs=[blk(q), blk(kc), blk(vc), blk(kn), blk(vn)],
        out_specs=pl.BlockSpec((1, tq, N_MLA * V_DIM), lambda bi: (bi, 0, 0)),
        out_shape=jax.ShapeDtypeStruct((b, tq, N_MLA * V_DIM), BF16),
        compiler_params=_params("arbitrary"), name="attn_mla_step",
    )(q, kc, vc, kn, vn)


def _lane_cumsum(x, triu):
    rows, blk = x.shape[0], triu.shape[0]
    carry = jnp.zeros((rows, 1), F32)
    out = []
    for b0 in range(0, x.shape[1], blk):
        hi, mid, lo = _split3(_pad_rows(x[:, b0:b0 + blk], 16))
        c = (_dot(hi, triu) + _dot(mid, triu) + _dot(lo, triu))[0:rows] + carry
        carry = c[:, blk - 1:blk]
        out.append(c)
    return jnp.concatenate(out, axis=1), carry


def _attn_fox_step_body(q_ref, kt_ref, vt_ref, kn_ref, vn_ref, lft_ref, lfn_ref,
                        triu_ref, tri_ref, triu_s_ref, o_ref):
    tq = q_ref.shape[1]
    lane = lax.broadcasted_iota(jnp.int32, (tq, LANE), 1)
    causal = lane <= lax.broadcasted_iota(jnp.int32, (tq, LANE), 0)
    cc, total = _lane_cumsum(lft_ref[0, 0], triu_ref[...])
    lfn = _pad_rows(lfn_ref[0], LANE)
    hi, mid, lo = _split3(lfn)
    tri = tri_ref[...]
    cn = (_dot(tri, hi) + _dot(tri, mid) + _dot(tri, lo))[0:tq]
    cnt, _ = _lane_cumsum(lfn.T[0:N_FOX, :], triu_s_ref[...])
    for p in range(N_FOX // 2):
        pair = slice(LANE * p, LANE * (p + 1))
        kt = kt_ref[0, 0, pair, :].astype(BF16)
        vt = vt_ref[0, 0, pair, :].astype(BF16)
        kn = _pad_rows(kn_ref[0, :, pair], LANE)
        vn = _pad_rows(vn_ref[0, :, pair], LANE)
        qf = q_ref[0, :, pair].astype(F32)
        outs = []
        for hh in range(2):
            h = 2 * p + hh
            q = jnp.where((lane >> 6) == hh, qf, 0.0).astype(BF16)
            col = jnp.sum(jnp.where(lane == h, cn, 0.0), axis=1, keepdims=True)
            bias_c = ((total[h:h + 1, :] + col) - cc[h:h + 1, :]) * LOG2E
            bias_n = (col - cnt[h:h + 1, :]) * LOG2E
            s_c = _dot(q, kt) + bias_c
            s_n = jnp.where(causal, _dot_nt(q, kn) + bias_n, -jnp.inf)
            p_c, p_n, l = _step_softmax(s_c, s_n)
            outs.append((_dot_nt(p_c, vt) + _dot(p_n, vn)) / l)
        o_ref[0, :, pair] = jnp.where(lane < V_DIM, outs[0], outs[1]).astype(BF16)


def _attn_fox_step(q, kt_all, vt_all, kn, vn, lft_all, lfn, layer, consts):
    b, tq, _ = q.shape
    new = lambda a: pl.BlockSpec((1,) + a.shape[1:], lambda bi: (bi, 0, 0))
    old = lambda a: pl.BlockSpec((1, 1) + a.shape[2:], lambda bi: (layer, bi, 0, 0))
    return pl.pallas_call(
        _attn_fox_step_body, grid=(b,),
        in_specs=[new(q), old(kt_all), old(vt_all), new(kn), new(vn), old(lft_all), new(lfn),
                  _full((MXU, MXU)), _full((LANE, LANE)), _full((LANE, LANE))],
        out_specs=pl.BlockSpec((1, tq, N_FOX * FOX_DIM), lambda bi: (bi, 0, 0)),
        out_shape=jax.ShapeDtypeStruct((b, tq, N_FOX * FOX_DIM), BF16),
        compiler_params=_params("arbitrary"), name="attn_fox_step",
    )(q, kt_all, vt_all, kn, vn, lft_all, lfn, consts["triu256"], consts["tri128"], consts["triu128"])


def _post_attn_body(x_ref, om_ref, of_ref, wo_ref, gffn_ref, wr_ref, br_ref, tri_ref,
                    x2_ref, xn_ref, route_ref, routet_ref, cnt_ref, carry_ref):
    nm = N_MLA * V_DIM
    x2 = x_ref[...] + _dot(om_ref[...], wo_ref[0:nm, :]) + _dot(of_ref[...], wo_ref[nm:, :])
    x2_ref[...] = x2
    xn = _rms(x2, gffn_ref[...], D_MODEL)
    xn_ref[...] = xn
    blocks = range(0, xn.shape[0], MXU)
    each = lambda f, *cols: [f(*v) for v in zip(*cols)]
    xs = [xn[r0:r0 + MXU] for r0 in blocks]
    xh = each(lambda v: v.astype(BF16), xs)
    xl = each(lambda v, h: (v - h.astype(F32)).astype(BF16), xs, xh)
    wr = wr_ref[...]
    r = each(lambda h, l: _dot(h, wr) + _dot(l, wr), xh, xl)
    bias = br_ref[...]
    lane = lax.broadcasted_iota(jnp.int32, (MXU, LANE), 1).astype(F32)
    valid = lane < N_EXPERTS
    lg = each(lambda v: jnp.where(valid, v + pltpu.roll(v, LANE - N_EXPERTS, 1) + bias, -jnp.inf), r)
    e = each(lambda v: jnp.exp(v - jnp.max(v, axis=1, keepdims=True)), lg)
    probs = each(lambda v: v / jnp.sum(v, axis=1, keepdims=True), e)
    first_at = lambda p, m: jnp.min(jnp.where(p == m, lane, float(LANE)), axis=1, keepdims=True)
    p1 = each(lambda p: jnp.where(valid, p, -1.0), probs)
    m1 = each(lambda p: jnp.max(p, axis=1, keepdims=True), p1)
    i1 = each(first_at, p1, m1)
    p2 = each(lambda p, i: jnp.where(lane == i, -1.0, p), p1, i1)
    m2 = each(lambda p: jnp.max(p, axis=1, keepdims=True), p2)
    i2 = each(first_at, p2, m2)

    @pl.when(pl.program_id(0) == 0)
    def _():
        carry_ref[...] = jnp.zeros(carry_ref.shape, F32)

    sel = each(lambda a, b: jnp.where((lane == a) | (lane == b), 1.0, 0.0), i1, i2)
    tri = tri_ref[...]
    counts = each(lambda v: _dot(tri, v.astype(BF16)), sel)
    total = carry_ref[0:1, :]
    for n, r0 in enumerate(blocks):
        incl = counts[n] + total
        total = incl[MXU - 1:, :]
        excl = incl - sel[n]
        rank1 = jnp.sum(jnp.where(lane == i1[n], excl, 0.0), axis=1, keepdims=True)
        rank2 = jnp.sum(jnp.where(lane == i2[n], excl, 0.0), axis=1, keepdims=True)
        den = m1[n] + m2[n]
        route = jnp.zeros((MXU, LANE), F32)
        for c, col in enumerate((i1[n], i2[n], rank1, rank2, m1[n] / den, m2[n] / den)):
            route = jnp.where(lane == c, col, route)
        route_ref[r0:r0 + MXU, :] = route
        routet_ref[:, r0:r0 + MXU] = route.T[0:8, :]
    carry_ref[...] = jnp.broadcast_to(total, carry_ref.shape)
    cnt_ref[...] = jnp.broadcast_to(total, cnt_ref.shape)


def _post_attn(x, om, of, lw, consts, tm):
    rows = x.shape[0]
    assert tm % MXU == 0
    row = lambda w: pl.BlockSpec((tm, w), lambda i: (i, 0))
    nm = N_MLA * V_DIM
    sds = jax.ShapeDtypeStruct
    return pl.pallas_call(
        _post_attn_body, grid=(rows // tm,),
        in_specs=[row(D_MODEL), row(nm), row(nm), _full((2 * nm, D_MODEL)), _full((1, D_MODEL)),
                  _full((D_MODEL, LANE)), _full((1, LANE)), _full((MXU, MXU))],
        out_specs=(row(D_MODEL), row(D_MODEL), row(LANE), pl.BlockSpec((8, tm), lambda i: (0, i)),
                   _full((8, LANE))),
        out_shape=(sds((rows, D_MODEL), F32), sds((rows, D_MODEL), F32), sds((rows, LANE), F32),
                   sds((8, rows), F32), sds((8, LANE), F32)),
        scratch_shapes=[pltpu.VMEM((8, LANE), F32)],
        compiler_params=_params("arbitrary"), name="post_attn_moe",
    )(x, om, of, lw["w_out"], lw["g_ffn"], lw["w_router"], lw["b_router"], consts["tri256"])


def _swiglu_acc(xb, wg_ref, wu_ref, wd_ref, d_ff, lo=0, hi=None):
    acc = None
    for c in range(lo * MXU, d_ff if hi is None else hi * MXU, MXU):
        g = _dot(xb, wg_ref[:, c:c + MXU])
        u = _dot(xb, wu_ref[:, c:c + MXU])
        h = (g * jax.nn.sigmoid(g) * u).astype(BF16)
        d = _dot(h, wd_ref[c:c + MXU, :])
        acc = d if acc is None else acc + d
    return acc


def _post_ffn_body(x_ref, om_ref, of_ref, wo_ref, gffn_ref, wg_ref, wu_ref, wd_ref, o_ref):
    nm = N_MLA * V_DIM
    x2 = x_ref[...] + _dot(om_ref[...], wo_ref[0:nm, :]) + _dot(of_ref[...], wo_ref[nm:, :])
    xn = _rms(x2, gffn_ref[...], D_MODEL).astype(BF16)
    o_ref[...] = x2 + _swiglu_acc(xn, wg_ref.at[0], wu_ref.at[0], wd_ref.at[0], D_FF)


def _post_ffn(x, om, of, lw, tm):
    rows = x.shape[0]
    row = lambda w: pl.BlockSpec((tm, w), lambda i: (i, 0))
    nm = N_MLA * V_DIM
    jl = lw["mixer_idx"]
    wspec = lambda a, b: pl.BlockSpec((1, a, b), lambda i: (jl, 0, 0))
    return pl.pallas_call(
        _post_ffn_body, grid=(rows // tm,),
        in_specs=[row(D_MODEL), row(nm), row(nm), _full((2 * nm, D_MODEL)), _full((1, D_MODEL)),
                  wspec(D_MODEL, D_FF), wspec(D_MODEL, D_FF), wspec(D_FF, D_MODEL)],
        out_specs=row(D_MODEL), out_shape=jax.ShapeDtypeStruct((rows, D_MODEL), F32),
        compiler_params=_params("arbitrary"), name="post_ffn",
    )(x, om, of, lw["w_out"], lw["g_ffn"], lw["w_gate"], lw["w_up"], lw["w_down"])


SCATTER_UNROLL = 8
SPLIT_CHUNK = 5


def _inv_body(pos_ref, base_hbm, inv_ref, sem, *, tmg):
    fill = pltpu.make_async_copy(base_hbm, inv_ref, sem.at[0])
    fill.start()
    fill.wait()

    def put(i, c):
        for u in range(SCATTER_UNROLL):
            a = i * SCATTER_UNROLL + u
            inv_ref[tmg + pos_ref[a]] = a
        return c

    lax.fori_loop(0, pos_ref.shape[0] // SCATTER_UNROLL, put, 0)


def _route_inverse(pos_flat, base, tmg):
    return pl.pallas_call(
        functools.partial(_inv_body, tmg=tmg),
        in_specs=[pl.BlockSpec(memory_space=pltpu.SMEM), pl.BlockSpec(memory_space=pl.ANY)],
        out_specs=pl.BlockSpec(memory_space=pltpu.SMEM),
        out_shape=jax.ShapeDtypeStruct(base.shape, jnp.int32),
        scratch_shapes=[pltpu.SemaphoreType.DMA((1,))], name="route_inverse",
    )(pos_flat, base)


GATHER, OUT, STAGE = 0, 2, 4


def _moe_routed_body(te_ref, stage_ref, src_ref, dst_ref, xn_hbm, wg_ref, wu_ref, wd_ref, z_hbm,
                     buf, gsem, ssem, *, tmg, n_tiles):
    del te_ref
    j = pl.program_id(0)

    def row_in(entry, slot, r):
        return pltpu.make_async_copy(xn_hbm.at[pl.ds(src_ref[entry], 1), :],
                                     buf.at[GATHER + slot, pl.ds(r, 1), :], gsem.at[slot])

    def row_out(entry, slot, r):
        return pltpu.make_async_copy(buf.at[OUT + slot, pl.ds(r, 1), :],
                                     z_hbm.at[pl.ds(dst_ref[entry], 1), :], ssem.at[0])

    def tile_in(slot):
        return pltpu.make_async_copy(xn_hbm.at[pl.ds(0, tmg), :], buf.at[GATHER + slot], gsem.at[slot])

    def tile_out(slot):
        return pltpu.make_async_copy(buf.at[OUT + slot], z_hbm.at[pl.ds(0, tmg), :], ssem.at[0])

    @pl.when(j == 0)
    def _():
        buf[OUT + 1] = jnp.zeros(buf.shape[1:], F32)
        for r in range(tmg):
            row_in(tmg + r, 0, r).start()

    def step(cur):
        nxt = 1 - cur

        @pl.when(j >= 1)
        def _():
            tile_out(cur).wait()

        tile_in(cur).wait()
        w = (wg_ref.at[0, 0], wu_ref.at[0, 0], wd_ref.at[0, 0])
        n_active = stage_ref[1]

        def issue():
            for r in range(tmg):
                row_in((j + 2) * tmg + r, nxt, r).start()
                row_out(j * tmg + r, nxt, r).start()

        @pl.when(j < n_active)
        def _():
            head = _swiglu_acc(buf[GATHER + cur].astype(BF16), *w, E_FF, 0, SPLIT_CHUNK)
            issue()
            stage = stage_ref[0]
            buf[stage] = buf[GATHER + cur]
            tail = _swiglu_acc(buf[stage].astype(BF16), *w, E_FF, SPLIT_CHUNK)
            buf[OUT + cur] = head + tail

        @pl.when(j >= n_active)
        def _():
            issue()
            buf[OUT + cur] = jnp.zeros(buf.shape[1:], F32)

    for parity in range(2):
        pl.when(j % 2 == parity)(functools.partial(step, parity))

    assert n_tiles % 2 == 0

    @pl.when(j == n_tiles)
    def _():
        tile_out(1).wait()
        tile_in(1).wait()


def _moe_routed(xn, tile_expert, n_active, src, dst, lw, tmg, n_tiles):
    n = xn.shape[0]
    jl = lw["mixer_idx"]
    wspec = lambda a, b: pl.BlockSpec((1, 1, a, b), lambda j, te, st, s, d: (jl, te[j], 0, 0))
    grid_spec = pltpu.PrefetchScalarGridSpec(
        num_scalar_prefetch=4, grid=(n_tiles + 1,),
        in_specs=[pl.BlockSpec(memory_space=pl.ANY), wspec(D_MODEL, E_FF), wspec(D_MODEL, E_FF),
                  wspec(E_FF, D_MODEL)],
        out_specs=pl.BlockSpec(memory_space=pl.ANY),
        scratch_shapes=[pltpu.VMEM((STAGE + 1, tmg, D_MODEL), F32),
                        pltpu.SemaphoreType.DMA((2,)), pltpu.SemaphoreType.DMA((1,))])
    return pl.pallas_call(
        functools.partial(_moe_routed_body, tmg=tmg, n_tiles=n_tiles),
        grid_spec=grid_spec, out_shape=jax.ShapeDtypeStruct((2 * n + tmg, D_MODEL), F32),
        compiler_params=_params("arbitrary"), name="moe_routed",
    )(tile_expert, jnp.stack([jnp.int32(STAGE), n_active.astype(jnp.int32)]), src, dst, xn,
      lw["we_gate"], lw["we_up"], lw["we_down"])


def _combine_body(x2_ref, route_ref, z0_ref, z1_ref, o_ref):
    o_ref[...] = _combine(x2_ref, route_ref, z0_ref, z1_ref)


def _moe_combine(x2, route, z, tm):
    n = x2.shape[0]
    row = lambda w: pl.BlockSpec((tm, w), lambda i: (i, 0))
    return pl.pallas_call(
        _combine_body, grid=(n // tm,),
        in_specs=[row(D_MODEL), row(LANE), row(D_MODEL),
                  pl.BlockSpec((tm, D_MODEL), lambda i: (i + n // tm, 0))],
        out_specs=row(D_MODEL), out_shape=jax.ShapeDtypeStruct((n, D_MODEL), F32),
        compiler_params=_params("arbitrary"), name="moe_combine",
    )(x2, route, z, z)


def _moe(x2, xn, route, route_t, counts, lw, tmg, defer_combine=False):
    n = x2.shape[0]
    assert n & (n - 1) == 0
    n_tiles = 2 * n // tmg + N_EXPERTS
    ext = (n_tiles + 3) * tmg
    e = route_t[0:2].astype(jnp.int32)
    rank = route_t[2:4].astype(jnp.int32)
    cnt = counts[0, :N_EXPERTS].astype(jnp.int32)
    tiles = (cnt + tmg - 1) // tmg
    tile_end = jnp.cumsum(tiles)
    row_start = (tile_end - tiles) * tmg
    start_of = jnp.sum(jnp.where(e[..., None] == jnp.arange(N_EXPERTS), row_start, 0), axis=-1)
    pos_flat = (start_of + rank).reshape(-1)
    steps = jnp.minimum(jnp.arange(n_tiles + 1, dtype=jnp.int32), tile_end[-1] - 1)
    tile_expert = jnp.minimum(jnp.sum((steps[:, None] >= tile_end[None, :]).astype(jnp.int32), axis=1),
                              N_EXPERTS - 1)
    idx = jnp.arange(ext, dtype=jnp.int32)
    dst = _route_inverse(pos_flat, 2 * n + (idx & (tmg - 1)), tmg)
    src = jnp.where(dst < 2 * n, dst, idx) & (n - 1)
    z = _moe_routed(xn, tile_expert, tile_end[-1], src, dst, lw, tmg, n_tiles)
    return (x2, route, z) if defer_combine else _moe_combine(x2, route, z, min(n, 2 * tmg))


def _consts():
    i = jnp.arange(MXU)
    blockdiag = lambda w: ((i[:, None] // w) == (i[None, :] // w)).astype(BF16)
    tri = lambda n: (jnp.arange(n)[None, :] <= jnp.arange(n)[:, None]).astype(BF16)
    r = jnp.arange(LANE)
    place = jnp.stack([((r[:, None] < N_FOX) & (r[None, :] == AUG_W * r[:, None] + k)).astype(BF16)
                       for k in range(6)])
    within = r % AUG_W
    headed = r < AUG_W * N_FOX
    ones_q = (headed & (within >= 3) & (within < 6)).astype(F32)
    ones_k = (headed & (within < 3)).astype(F32)
    aug_ones = jnp.zeros((8, LANE), F32).at[0].set(ones_q).at[1].set(ones_k)
    return {"s64": blockdiag(FOX_DIM), "s128": blockdiag(HEAD_PAD), "tri256": tri(MXU),
            "tri128": tri(LANE), "triu256": tri(MXU).T, "triu128": tri(LANE).T,
            "place": place, "aug_ones": aug_ones}


def _rope_tables(pos):
    half = ROPE_DIM // 2
    inv = ROPE_BASE ** (-jnp.arange(half, dtype=F32) / half)
    ang = pos.astype(F32)[:, None] * inv[None, :]
    cos, sin = jnp.cos(ang), jnp.sin(ang)
    t = pos.shape[0]
    ct = jnp.concatenate([jnp.ones((t, ROPE_LO), F32), cos, cos, jnp.zeros((t, LANE - ROPE_HI), F32)], 1)
    st = jnp.concatenate([jnp.zeros((t, ROPE_LO), F32), -sin, sin, jnp.zeros((t, LANE - ROPE_HI), F32)], 1)
    return ct, st


def _pad_lanes(v, width=LANE):
    return jnp.pad(v, [(0, 0)] * (v.ndim - 1) + [(0, width - v.shape[-1])])


def _prep_layer(i, p):
    w_in = p["w_in"][i]
    s = [0, Q_LORA, Q_LORA + KV_LORA, Q_LORA + KV_LORA + ROPE_DIM]
    nfx = N_FOX * FOX_DIM
    s += [s[3] + nfx, s[3] + 2 * nfx, s[3] + 3 * nfx, s[3] + 3 * nfx + N_FOX]
    c_q, c_kv, k_rope = w_in[:, s[0]:s[1]], w_in[:, s[1]:s[2]], w_in[:, s[2]:s[3]]
    fq, fk, fv, f_logit = w_in[:, s[3]:s[4]], w_in[:, s[4]:s[5]], w_in[:, s[5]:s[6]], w_in[:, s[6]:s[7]]
    zc = lambda n: jnp.zeros((D_MODEL, n), F32)
    misc = jnp.concatenate([f_logit, zc(ROPE_LO - N_FOX), k_rope, zc(LANE - ROPE_HI)], axis=1)
    w_in_p = jnp.concatenate([c_q, c_kv, fq, fk, fv, misc], axis=1).astype(BF16)

    w_uq = p["w_uq"][i].reshape(Q_LORA, N_MLA, QK_DIM)
    w_uq_p = _pad_lanes(w_uq, HEAD_PAD).reshape(Q_LORA, -1).astype(BF16)

    w_ukv = p["w_ukv"][i].reshape(KV_LORA, N_MLA, NOPE_DIM + V_DIM)
    w_k = _pad_lanes(w_ukv[..., :NOPE_DIM], HEAD_PAD).reshape(KV_LORA, -1).astype(BF16)
    w_v = w_ukv[..., NOPE_DIM:].reshape(KV_LORA, -1).astype(BF16)

    gq = _pad_lanes(p["g_qn_mla"][i] * (QK_DIM ** -0.5 * LOG2E))[None]
    gk = _pad_lanes(p["g_kn_mla"][i])[None]
    lw = {
        "g_mix": p["g_mix"][i][None], "w_in": w_in_p, "b_f": _pad_lanes(p["b_f"][i])[None],
        "g_cq": p["g_cq"][i][None], "g_ckv": p["g_ckv"][i][None], "w_uq": w_uq_p,
        "gq": gq, "gk": gk,
        "gfq": jnp.tile(p["g_qn_fox"][i] * (FOX_DIM ** -0.5 * LOG2E), N_FOX)[None],
        "gfk": jnp.tile(p["g_kn_fox"][i], N_FOX)[None],
        "w_k": w_k, "w_v": w_v,
        "w_out": p["w_out"][i].astype(BF16), "g_ffn": p["g_ffn"][i][None],
    }
    j = i // 2
    lw["mixer_idx"] = j
    if i % 2 == 0:
        lw.update(w_gate=p["w_gate_b"], w_up=p["w_up_b"], w_down=p["w_down_b"])
    else:
        wr = p["w_router"][j]
        wr_hi = wr.astype(BF16)
        wr_lo = (wr - wr_hi.astype(F32)).astype(BF16)
        lw.update(w_router=_pad_lanes(jnp.concatenate([wr_hi, wr_lo], axis=1)),
                  b_router=_pad_lanes(p["b_router"][j])[None],
                  we_gate=p["we_gate_b"], we_up=p["we_up_b"], we_down=p["we_down_b"])
    return lw


def _channel_mixer(i, x, om, of, lw, consts, tm_ffn, defer_combine=False):
    if i % 2 == 0:
        return _post_ffn(x, om, of, lw, tm_ffn)
    x2, xn, route, route_t, counts = _post_attn(x, om, of, lw, consts, min(tm_ffn, x.shape[0]))
    return _moe(x2, xn, route, route_t, counts, lw, MXU, defer_combine)


def _trunk_prompt(x, layers, consts):
    b, t, _ = x.shape
    rows = b * t
    x = x.reshape(rows, D_MODEL)
    ct, st = _rope_tables(jnp.arange(t, dtype=jnp.int32))
    nfx = N_FOX * FOX_DIM
    stack = ()
    pending = None
    for i, lw in enumerate(layers):
        outs = _mixer_in_prompt(x, lw, ct, st, consts, stack, i, b, combine=pending)
        stack = outs[:N_STACKED]
        q, fq, fkb, fvb, k, v, qaug, kaug = outs[N_STACKED:N_STACKED + 8]
        if pending is not None:
            x = outs[-1]
        om = _attn_prompt(q.reshape(b, t, -1), k.reshape(b, t, -1), v.reshape(b, t, -1), None, 256)
        of = _attn_prompt(fq.reshape(b, t, nfx), fkb.reshape(b, t, nfx), fvb.reshape(b, t, nfx),
                          (qaug.reshape(b, t, LANE), kaug.reshape(b, t, LANE)), 256)
        defer = i % 2 == 1 and i + 1 < len(layers)
        res = _channel_mixer(i, x, om.reshape(rows, -1), of.reshape(rows, -1), lw, consts, 512, defer)
        x, pending = (None, res) if defer else (res, None)
    ckv_s, krt_s, fkt_s, fvt_s, lft_s = stack
    heads = lambda a: jnp.transpose(a.reshape(DEPTH, b, N_FOX, FOX_DIM, t), (0, 1, 4, 2, 3))
    state = (ckv_s.reshape(DEPTH, b, t, KV_LORA), jnp.swapaxes(krt_s, 2, 3), heads(fkt_s), heads(fvt_s),
             jnp.swapaxes(lft_s, 2, 3))
    return x.reshape(b, t, D_MODEL), state


def _trunk_sample(x, past, layers, consts):
    b, t, _ = x.shape
    rows = b * t
    start = PAST_LEN
    x = x.reshape(rows, D_MODEL)
    ct, st = _rope_tables(start + jnp.arange(t, dtype=jnp.int32))
    ct, st = jnp.tile(ct, (b, 1)), jnp.tile(st, (b, 1))
    p_ckv, p_krope, p_fk, p_fv, p_logf = past
    nfx = N_FOX * FOX_DIM
    krope_t = jnp.swapaxes(p_krope, 2, 3)
    fk_t = jnp.transpose(p_fk, (0, 1, 3, 4, 2)).reshape(DEPTH, b, nfx, start)
    fv_t = jnp.transpose(p_fv, (0, 1, 3, 4, 2)).reshape(DEPTH, b, nfx, start)
    lf_t = jnp.swapaxes(p_logf, 2, 3)
    state = ([], [], [], [], [])
    for i, lw in enumerate(layers):
        ckv, kr, kr128, fk, fv, lf128, lf, q, fq, fkb, fvb = _mixer_in(x, lw, ct, st, consts, rows)
        k_c, v_c = _kv_expand_cache(p_ckv, krope_t, i, lw, consts, 1024)
        k_n, v_n = _kv_expand(ckv, kr128, lw, consts, rows)
        om = _attn_mla_step(q.reshape(b, t, -1), k_c.reshape(b, start, -1), v_c.reshape(b, start, -1),
                            k_n.reshape(b, t, -1), v_n.reshape(b, t, -1), start)
        of = _attn_fox_step(fq.reshape(b, t, nfx), fk_t, fv_t, fkb.reshape(b, t, nfx),
                            fvb.reshape(b, t, nfx), lf_t, lf128.reshape(b, t, LANE), i, consts)
        x = _channel_mixer(i, x, om.reshape(rows, -1), of.reshape(rows, -1), lw, consts, rows)
        for lst, s_ in zip(state, (ckv.reshape(b, t, KV_LORA), kr.reshape(b, t, ROPE_DIM),
                                   fk.reshape(b, t, N_FOX, FOX_DIM), fv.reshape(b, t, N_FOX, FOX_DIM),
                                   lf.reshape(b, t, N_FOX))):
            lst.append(s_)
    return x.reshape(b, t, D_MODEL), tuple(jnp.stack(s_) for s_ in state)


def kernel(x_prompt, x_sample, cache_mla_ckv, cache_mla_krope, cache_fox_k, cache_fox_v, cache_fox_logf, g_mix, w_in, b_f, g_cq, g_ckv, w_uq, w_ukv, g_qn_mla, g_kn_mla, g_qn_fox, g_kn_fox, w_out, g_ffn, w_gate, w_up, w_down, w_router, b_router, we_gate, we_up, we_down):
    p = dict(g_mix=g_mix, w_in=w_in, b_f=b_f, g_cq=g_cq, g_ckv=g_ckv, w_uq=w_uq, w_ukv=w_ukv,
             g_qn_mla=g_qn_mla, g_kn_mla=g_kn_mla, g_qn_fox=g_qn_fox, g_kn_fox=g_kn_fox,
             w_out=w_out, g_ffn=g_ffn, w_gate=w_gate, w_up=w_up, w_down=w_down,
             w_router=w_router, b_router=b_router, we_gate=we_gate, we_up=we_up, we_down=we_down)
    for name in ("w_gate", "w_up", "w_down", "we_gate", "we_up", "we_down"):
        p[name + "_b"] = p[name].astype(BF16)
    layers = [_prep_layer(i, p) for i in range(DEPTH)]
    consts = _consts()
    y_p, st_p = _trunk_prompt(x_prompt, layers, consts)
    past = (cache_mla_ckv, cache_mla_krope, cache_fox_k, cache_fox_v, cache_fox_logf)
    y_s, st_s = _trunk_sample(x_sample, past, layers, consts)
    return (y_p, y_s) + st_p + st_s
```
